```python
import math
import jax
import jax.numpy as jnp
from jax import lax
import numpy as np

D_MODEL = 1024
BATCH = 16
SEQ = 256
DEPTH = 4
DEC_BATCH = 2
DEC_SEQ = 4096
PAST_LEN = 256

GRID_W = 64
HEAD_DIM = 64
A_HEADS = 6
A_KV_HEADS = 2
B_HEADS = 4
B_Q_RANK = 256
B_KV_RANK = 128
B_NOPE = 64
B_ROPE = 32
B_V = 64
C_HEADS = 6
C_HEAD_DIM = 64
C_INNER = C_HEADS * C_HEAD_DIM
C_GROUPS = 2
C_STATE = 64
C_CONV = 5
C_CHUNK = 128
C_CONV_CH = C_INNER + 2 * C_GROUPS * C_STATE
D_FF = 4 * D_MODEL
A_Q = A_HEADS * HEAD_DIM
A_KV = A_KV_HEADS * HEAD_DIM
IN_SIZES = (A_Q, A_KV, A_KV, B_Q_RANK, B_KV_RANK, B_ROPE, C_INNER, C_CONV_CH, 2 * C_HEADS)
IN_COLS = A_Q + 2 * A_KV + B_Q_RANK + B_KV_RANK + B_ROPE + C_INNER + C_CONV_CH + 2 * C_HEADS
MIX_W = A_Q + B_HEADS * B_V + C_INNER
Q_BLOCK = 128
ROPE_THETA = 10000.0
EPS = 1e-6
F32 = jnp.float32

kernel_name = 'hybrid_dit_prefix_ctx_step'


def rmsnorm(x, g):
    xf = x.astype(F32)
    y = xf * lax.rsqrt(jnp.mean(xf * xf, axis=-1, keepdims=True) + EPS)
    return (y * g.astype(F32)).astype(x.dtype)


def axial_rope_tables(t, rot_dim):
    rows = t // GRID_W
    row = jnp.repeat(jnp.arange(rows), GRID_W).astype(F32)
    col = jnp.tile(jnp.arange(GRID_W), rows).astype(F32)
    half = rot_dim // 2
    inv = 1.0 / (ROPE_THETA ** (jnp.arange(0, half, 2, dtype=F32) / half))
    ang = jnp.concatenate([row[:, None] * inv, col[:, None] * inv], axis=-1)
    return jnp.cos(ang), jnp.sin(ang)


def apply_axial_rope(x, cos, sin):
    half = x.shape[-1] // 2
    q = half // 2
    cos = cos[None, :, None, :].astype(x.dtype)
    sin = sin[None, :, None, :].astype(x.dtype)

    def rot(xa, cs, sn):
        x1, x2 = xa[..., :q], xa[..., q:]
        return jnp.concatenate([x1 * cs - x2 * sn, x1 * sn + x2 * cs], axis=-1)

    return jnp.concatenate([rot(x[..., :half], cos[..., :q], sin[..., :q]),
                            rot(x[..., half:], cos[..., q:], sin[..., q:])], axis=-1)


def block_attention(q, k, v, scale):
    b, t, h, dk = q.shape
    g = k.shape[2]
    rep = h // g
    dv = v.shape[-1]
    nb = t // Q_BLOCK
    qb = q.reshape(b, nb, Q_BLOCK, g, rep, dk).transpose(1, 0, 2, 3, 4, 5)

    def one_block(qblk):
        s = jnp.einsum('bqgrd,bsgd->bgrqs', qblk, k).astype(F32) * scale
        pr = jax.nn.softmax(s, axis=-1).astype(v.dtype)
        return jnp.einsum('bgrqs,bsgd->bqgrd', pr, v)

    o = lax.map(one_block, qb)
    return o.transpose(1, 0, 2, 3, 4, 5).reshape(b, t, h, dv)


def dwconv_centred(u, w, bias):
    rhs = w.T[:, None, :].astype(u.dtype)
    out = lax.conv_general_dilated(u, rhs, window_strides=(1,),
                                   padding=[(C_CONV // 2, C_CONV // 2)],
                                   dimension_numbers=('NWC', 'WIO', 'NWC'),
                                   feature_group_count=u.shape[-1])
    return out + bias.astype(u.dtype)


def ssd_scan(x, dt, a, bm, cm, h0):
    b, t, nh, hp = x.shape
    nc = t // C_CHUNK
    rep = nh // bm.shape[2]
    bm = jnp.repeat(bm, rep, axis=2)
    cm = jnp.repeat(cm, rep, axis=2)
    xc = x.reshape(b, nc, C_CHUNK, nh, hp)
    bc = bm.reshape(b, nc, C_CHUNK, nh, C_STATE)
    cc = cm.reshape(b, nc, C_CHUNK, nh, C_STATE)
    dtc = dt.reshape(b, nc, C_CHUNK, nh)
    acum = jnp.cumsum(dtc * a.astype(F32), axis=2)
    acum_t = acum.transpose(0, 1, 3, 2)
    seg = acum_t[..., :, None] - acum_t[..., None, :]
    causal = jnp.tril(jnp.ones((C_CHUNK, C_CHUNK), dtype=bool))
    decay = jnp.exp(jnp.where(causal, seg, -jnp.inf))
    scores = jnp.einsum('bclhn,bcshn->bchls', cc, bc) * decay * dtc.transpose(0, 1, 3, 2)[..., None, :]
    y_diag = jnp.einsum('bchls,bcshp->bclhp', scores, xc)
    w_state = jnp.exp(acum[:, :, -1:, :] - acum) * dtc
    states = jnp.einsum('bclhn,bclh,bclhp->bchpn', bc, w_state, xc)
    chunk_decay = jnp.exp(acum[:, :, -1, :])

    def step(hc, inp):
        st, dec = inp
        return hc * dec[..., None, None] + st, hc

    h_final, h_start = lax.scan(step, h0.astype(F32),
                                (states.transpose(1, 0, 2, 3, 4), chunk_decay.transpose(1, 0, 2)))
    h_start = h_start.transpose(1, 0, 2, 3, 4)
    y_off = jnp.einsum('bclhn,bchpn->bclhp', cc, h_start) * jnp.exp(acum)[..., None]
    y = (y_diag + y_off).reshape(b, t, nh, hp).astype(x.dtype)
    return y, h_final.astype(x.dtype)


def ssd_mixer(z, xbc, dt_raw, p, h0):
    b, t, _ = z.shape
    xbc = jax.nn.silu(dwconv_centred(xbc, p['ssm_conv_w'], p['ssm_conv_b']))
    xs, bm, cm = jnp.split(xbc, [C_INNER, C_INNER + C_GROUPS * C_STATE], axis=-1)
    xs = xs.reshape(b, t, C_HEADS, C_HEAD_DIM)
    bm = bm.reshape(b, t, C_GROUPS, C_STATE)
    cm = cm.reshape(b, t, C_GROUPS, C_STATE)
    dt = jax.nn.softplus(dt_raw.astype(F32).reshape(b, t, 2, C_HEADS) + p['ssm_dt_bias'].astype(F32))
    a = -jnp.exp(p['ssm_a_log'].astype(F32))
    y_f, h_f = ssd_scan(xs, dt[:, :, 0], a[0], bm, cm, h0[:, 0])
    y_b, h_b = ssd_scan(jnp.flip(xs, 1), jnp.flip(dt[:, :, 1], 1), a[1],
                        jnp.flip(bm, 1), jnp.flip(cm, 1), h0[:, 1])
    y = y_f + jnp.flip(y_b, 1) + xs * p['ssm_d'][:, None].astype(xs.dtype)
    y = rmsnorm(y.reshape(b, t, C_INNER) * jax.nn.silu(z), p['ssm_norm'])
    return y, jnp.stack([h_f, h_b], axis=1)


def token_mixers(h, p, ctx):
    b, t, _ = h.shape
    proj = h @ p['w_in']
    qa, ka, va, qc, kvc, kpe, z, xbc, dt_raw = jnp.split(
        proj, np.cumsum(IN_SIZES)[:-1].tolist(), axis=-1)
    qa = rmsnorm(qa.reshape(b, t, A_HEADS, HEAD_DIM), p['attn_q_norm'])
    ka = rmsnorm(ka.reshape(b, t, A_KV_HEADS, HEAD_DIM), p['attn_k_norm'])
    va = va.reshape(b, t, A_KV_HEADS, HEAD_DIM)
    qb = (rmsnorm(qc, p['mla_q_norm']) @ p['mla_w_qb']).reshape(b, t, B_HEADS, B_NOPE + B_ROPE)
    q_nope, q_pe = qb[..., :B_NOPE], qb[..., B_NOPE:]
    ckv = rmsnorm(kvc, p['mla_kv_norm'])
    if ctx is None:
        k_all, v_all, ckv_all, kpe_all = ka, va, ckv, kpe
        h0 = jnp.zeros((b, 2, C_HEADS, C_HEAD_DIM, C_STATE), h.dtype)
    else:
        ctx_k, ctx_v, ctx_ckv, ctx_kpe, h0 = ctx
        cos_a, sin_a = axial_rope_tables(t, HEAD_DIM)
        cos_b, sin_b = axial_rope_tables(t, B_ROPE)
        qa = apply_axial_rope(qa, cos_a, sin_a)
        k_all = jnp.concatenate([ctx_k, apply_axial_rope(ka, cos_a, sin_a)], axis=1)
        v_all = jnp.concatenate([ctx_v, va], axis=1)
        q_pe = apply_axial_rope(q_pe, cos_b, sin_b)
        ckv_all = jnp.concatenate([ctx_ckv, ckv], axis=1)
        kpe_all = jnp.concatenate(
            [ctx_kpe, apply_axial_rope(kpe[:, :, None, :], cos_b, sin_b)[:, :, 0, :]], axis=1)
    s = k_all.shape[1]
    o_a = block_attention(qa, k_all, v_all, HEAD_DIM ** -0.5).reshape(b, t, A_Q)
    kv = (ckv_all @ p['mla_w_kvb']).reshape(b, s, B_HEADS, B_NOPE + B_V)
    k_b = jnp.concatenate(
        [kv[..., :B_NOPE], jnp.broadcast_to(kpe_all[:, :, None, :], (b, s, B_HEADS, B_ROPE))], axis=-1)
    q_b = jnp.concatenate([q_nope, q_pe], axis=-1)
    o_b = block_attention(q_b, k_b, kv[..., B_NOPE:], (B_NOPE + B_ROPE) ** -0.5).reshape(b, t, B_HEADS * B_V)
    o_c, h_fin = ssd_mixer(z, xbc, dt_raw, p, h0)
    out = jnp.concatenate([o_a, o_b, o_c], axis=-1) @ p['w_out']
    return out, (ka, va, ckv, kpe, h_fin)


def trunk_layer(x, mod, p, ctx):
    shift1, scale1, gate1, shift2, scale2, gate2 = jnp.split(mod[:, None, :].astype(x.dtype), 6, axis=-1)
    h = rmsnorm(x, p['norm_mix_pre']) * (1 + scale1) + shift1
    mix, ctx_tensors = token_mixers(h, p, ctx)
    x = x + gate1 * rmsnorm(mix, p['norm_mix_post'])
    h = rmsnorm(x, p['norm_ffn_pre']) * (1 + scale2) + shift2
    f = jnp.square(jax.nn.relu(h @ p['w_ffn1'])) @ p['w_ffn2']
    x = x + gate2 * rmsnorm(f, p['norm_ffn_post'])
    return x, ctx_tensors


def setup_inputs(seed: int = 0) -> dict:
    key = jax.random.key(seed)
    ks = jax.random.split(key, 32)

    def nrm(k, shape, s):
        return jax.random.normal(k, shape, F32) * s

    def gain(k, shape):
        return 1.0 + 0.02 * jax.random.normal(k, shape, F32)

    dt0 = jnp.exp(jax.random.uniform(ks[24], (DEPTH, 2, C_HEADS), F32, math.log(1e-3), math.log(1e-1)))
    return {
        'x_prompt': nrm(ks[0], (BATCH, SEQ, D_MODEL), 1.0),
        'x_sample': nrm(ks[1], (DEC_BATCH, DEC_SEQ, D_MODEL), 1.0),
        'cache_attn_k': nrm(ks[2], (DEC_BATCH, DEPTH, PAST_LEN, A_KV_HEADS, HEAD_DIM), 1.0),
        'cache_attn_v': nrm(ks[3], (DEC_BATCH, DEPTH, PAST_LEN, A_KV_HEADS, HEAD_DIM), 1.0),
        'cache_mla_ckv': nrm(ks[4], (DEC_BATCH, DEPTH, PAST_LEN, B_KV_RANK), 1.0),
        'cache_mla_kpe': nrm(ks[5], (DEC_BATCH, DEPTH, PAST_LEN, B_ROPE), 1.0),
        'state_ssm': nrm(ks[6], (DEC_BATCH, DEPTH, 2, C_HEADS, C_HEAD_DIM, C_STATE), 0.5),
        'c': nrm(ks[7], (DEC_BATCH, D_MODEL), 1.0),
        'c_ctx': nrm(ks[8], (D_MODEL,), 1.0),
        'norm_mix_pre': gain(ks[9], (DEPTH, D_MODEL)),
        'norm_mix_post': gain(ks[10], (DEPTH, D_MODEL)),
        'norm_ffn_pre': gain(ks[11], (DEPTH, D_MODEL)),
        'norm_ffn_post': gain(ks[12], (DEPTH, D_MODEL)),
        'w_mod': nrm(ks[13], (DEPTH, D_MODEL, 6 * D_MODEL), 0.5 * D_MODEL ** -0.5),
        'b_mod': nrm(ks[14], (DEPTH, 6 * D_MODEL), 0.01),
        'w_in': nrm(ks[15], (DEPTH, D_MODEL, IN_COLS), D_MODEL ** -0.5),
        'attn_q_norm': gain(ks[16], (DEPTH, HEAD_DIM)),
        'attn_k_norm': gain(ks[17], (DEPTH, HEAD_DIM)),
        'mla_q_norm': gain(ks[18], (DEPTH, B_Q_RANK)),
        'mla_w_qb': nrm(ks[19], (DEPTH, B_Q_RANK, B_HEADS * (B_NOPE + B_ROPE)), B_Q_RANK ** -0.5),
        'mla_kv_norm': gain(ks[20], (DEPTH, B_KV_RANK)),
        'mla_w_kvb': nrm(ks[21], (DEPTH, B_KV_RANK, B_HEADS * (B_NOPE + B_V)), B_KV_RANK ** -0.5),
        'ssm_conv_w': nrm(ks[22], (DEPTH, C_CONV_CH, C_CONV), C_CONV ** -0.5),
        'ssm_conv_b': nrm(ks[23], (DEPTH, C_CONV_CH), 0.01),
        'ssm_dt_bias': dt0 + jnp.log(-jnp.expm1(-dt0)),
        'ssm_a_log': jnp.log(jax.random.uniform(ks[25], (DEPTH, 2, C_HEADS), F32, 1.0, 16.0)),
        'ssm_d': 1.0 + 0.1 * jax.random.normal(ks[26], (DEPTH, C_HEADS), F32),
        'ssm_norm': gain(ks[27], (DEPTH, C_INNER)),
        'w_out': nrm(ks[28], (DEPTH, MIX_W, D_MODEL), MIX_W ** -0.5),
        'w_ffn1': nrm(ks[29], (DEPTH, D_MODEL, D_FF), D_MODEL ** -0.5),
        'w_ffn2': nrm(ks[30], (DEPTH, D_FF, D_MODEL), D_FF ** -0.5),
    }


def reference(x_prompt, x_sample, cache_attn_k, cache_attn_v, cache_mla_ckv, cache_mla_kpe, state_ssm,
              c, c_ctx, norm_mix_pre, norm_mix_post, norm_ffn_pre, norm_ffn_post, w_mod, b_mod, w_in,
              attn_q_norm, attn_k_norm, mla_q_norm, mla_w_qb, mla_kv_norm, mla_w_kvb,
              ssm_conv_w, ssm_conv_b, ssm_dt_bias, ssm_a_log, ssm_d, ssm_norm, w_out, w_ffn1, w_ffn2):
    xp = x_prompt
    xs = x_sample
    new_k, new_v, new_ckv, new_kpe, new_ssm = [], [], [], [], []
    for l in range(DEPTH):
        p = {
            'norm_mix_pre': norm_mix_pre[l], 'norm_mix_post': norm_mix_post[l],
            'norm_ffn_pre': norm_ffn_pre[l], 'norm_ffn_post': norm_ffn_post[l],
            'w_in': w_in[l], 'attn_q_norm': attn_q_norm[l], 'attn_k_norm': attn_k_norm[l],
            'mla_q_norm': mla_q_norm[l], 'mla_w_qb': mla_w_qb[l],
            'mla_kv_norm': mla_kv_norm[l], 'mla_w_kvb': mla_w_kvb[l],
            'ssm_conv_w': ssm_conv_w[l], 'ssm_conv_b': ssm_conv_b[l], 'ssm_dt_bias': ssm_dt_bias[l],
            'ssm_a_log': ssm_a_log[l], 'ssm_d': ssm_d[l], 'ssm_norm': ssm_norm[l],
            'w_out': w_out[l], 'w_ffn1': w_ffn1[l], 'w_ffn2': w_ffn2[l],
        }
        mod_ctx = (jax.nn.silu(c_ctx) @ w_mod[l] + b_mod[l])[None, :]
        mod_lat = jax.nn.silu(c) @ w_mod[l] + b_mod[l]
        xp, (k_a, v_a, ckv, kpe, h_ssm) = trunk_layer(xp, mod_ctx, p, None)
        new_k.append(k_a)
        new_v.append(v_a)
        new_ckv.append(ckv)
        new_kpe.append(kpe)
        new_ssm.append(h_ssm)
        xs, _ = trunk_layer(xs, mod_lat, p, (cache_attn_k[:, l], cache_attn_v[:, l], cache_mla_ckv[:, l],
                                             cache_mla_kpe[:, l], state_ssm[:, l]))
    return (xp, xs, jnp.stack(new_k, axis=1), jnp.stack(new_v, axis=1), jnp.stack(new_ckv, axis=1),
            jnp.stack(new_kpe, axis=1), jnp.stack(new_ssm, axis=1))
```

```python
import functools
import math

import jax
import jax.numpy as jnp
from jax import lax
from jax.experimental import pallas as pl
from jax.experimental.pallas import tpu as pltpu

F32 = jnp.float32
BF16 = jnp.bfloat16

D_MODEL = 1024
BATCH = 16
SEQ = 256
DEPTH = 4
DEC_BATCH = 2
DEC_SEQ = 4096
PAST_LEN = 256
GRID_W = 64
HEAD_DIM = 64
A_HEADS = 6
A_KV_HEADS = 2
A_REP = A_HEADS // A_KV_HEADS
B_HEADS = 4
B_Q_RANK = 256
B_KV_RANK = 128
B_NOPE = 64
B_ROPE = 32
B_V = 64
C_HEADS = 6
C_HEAD_DIM = 64
C_INNER = C_HEADS * C_HEAD_DIM
C_GROUPS = 2
C_STATE = 64
C_CONV = 5
C_CHUNK = 128
C_CONV_CH = C_INNER + 2 * C_GROUPS * C_STATE
D_FF = 4 * D_MODEL
A_Q = A_HEADS * HEAD_DIM
A_KV = A_KV_HEADS * HEAD_DIM
ROPE_THETA = 10000.0
EPS = 1e-6

LANES = 128
SUBLANES = 8
VMEM_LIMIT = 56 * 1024 * 1024

QA_W = A_HEADS * LANES
OFF_QA = 0
OFF_KA = OFF_QA + QA_W
OFF_VA = OFF_KA + A_KV
OFF_QC = OFF_VA + A_KV
OFF_KVC = OFF_QC + B_Q_RANK
OFF_MISC = OFF_KVC + B_KV_RANK
OFF_Z = OFF_MISC + LANES
OFF_XBC = OFF_Z + C_INNER
IN_COLS_P = OFF_XBC + C_CONV_CH
KPE_LANE = 64
DT_LANE = KPE_LANE + B_ROPE
QB_W = B_HEADS * LANES

TM = 512
TQ = 512
TK = 512


def _cparams(sem):
    return pltpu.CompilerParams(dimension_semantics=sem, vmem_limit_bytes=VMEM_LIMIT)


def _rms(x, g):
    return x * lax.rsqrt(jnp.mean(x * x, axis=-1, keepdims=True) + EPS) * g


def _bdot(a, b):
    return jnp.dot(a.astype(BF16), b.astype(BF16), preferred_element_type=F32)


def _lane_iota(shape):
    return lax.broadcasted_iota(jnp.int32, shape, len(shape) - 1)


def _swap_lanes(x, dist):
    lane = _lane_iota(x.shape)
    fwd = pltpu.roll(x, LANES - dist, axis=1)
    bwd = pltpu.roll(x, dist, axis=1)
    return jnp.where((lane % (2 * dist)) < dist, fwd, bwd)


def _rope(x, cos, sin, dist):
    return x * cos + _swap_lanes(x, dist) * sin


MOD_ROWS = SUBLANES
MOD_TN = 1536


def _mod_kernel(c_ref, w_ref, b_ref, o_ref):
    c = c_ref[...]
    o_ref[...] = _bdot(c * jax.nn.sigmoid(c), w_ref[...]) + b_ref[...]


def _modulation(cvec, w_mod, b_mod):
    return pl.pallas_call(
        _mod_kernel,
        out_shape=jax.ShapeDtypeStruct((DEPTH, MOD_ROWS, 6 * D_MODEL), F32),
        grid=(DEPTH, 6 * D_MODEL // MOD_TN),
        in_specs=[
            pl.BlockSpec((MOD_ROWS, D_MODEL), lambda l, j: (0, 0)),
            pl.BlockSpec((None, D_MODEL, MOD_TN), lambda l, j: (l, 0, j)),
            pl.BlockSpec((None, 1, MOD_TN), lambda l, j: (l, 0, j)),
        ],
        out_specs=pl.BlockSpec((None, MOD_ROWS, MOD_TN), lambda l, j: (l, 0, j)),
        compiler_params=_cparams(("parallel", "parallel")),
        name="modulation",
    )(cvec, w_mod, b_mod.reshape(DEPTH, 1, 6 * D_MODEL))


def _cache_kernel(k_ref, v_ref, ckv_ref, kpe_ref, wk_ref, wv_ref, kt_ref, vo_ref, kbt_ref, vb_ref):
    kt_ref[...] = k_ref[...].T.astype(BF16)
    vo_ref[...] = v_ref[...].astype(BF16)
    ckv = ckv_ref[...]
    kn = _bdot(ckv, wk_ref[...])
    kpe = kpe_ref[...]
    for h in range(B_HEADS):
        kbt_ref[h * LANES:(h + 1) * LANES, :] = (kn[:, h * LANES:(h + 1) * LANES] + kpe).T.astype(BF16)
    vb_ref[...] = _bdot(ckv, wv_ref[...]).astype(BF16)


def _cache_prep(cache_k, cache_v, cache_ckv, cache_kpe_p, wk_p, wv_p):
    spec_tok = lambda w: pl.BlockSpec((None, None, PAST_LEN, w), lambda l, b: (b, l, 0, 0))
    spec_t = lambda w: pl.BlockSpec((None, None, w, PAST_LEN), lambda l, b: (l, b, 0, 0))
    spec_o = lambda w: pl.BlockSpec((None, None, PAST_LEN, w), lambda l, b: (l, b, 0, 0))
    return pl.pallas_call(
        _cache_kernel,
        out_shape=(
            jax.ShapeDtypeStruct((DEPTH, DEC_BATCH, A_KV, PAST_LEN), BF16),
            jax.ShapeDtypeStruct((DEPTH, DEC_BATCH, PAST_LEN, A_KV), BF16),
            jax.ShapeDtypeStruct((DEPTH, DEC_BATCH, QB_W, PAST_LEN), BF16),
            jax.ShapeDtypeStruct((DEPTH, DEC_BATCH, PAST_LEN, B_HEADS * B_V), BF16),
        ),
        grid=(DEPTH, DEC_BATCH),
        in_specs=[
            spec_tok(A_KV), spec_tok(A_KV), spec_tok(B_KV_RANK), spec_tok(LANES),
            pl.BlockSpec((None, B_KV_RANK, QB_W), lambda l, b: (l, 0, 0)),
            pl.BlockSpec((None, B_KV_RANK, B_HEADS * B_V), lambda l, b: (l, 0, 0)),
        ],
        out_specs=(spec_t(A_KV), spec_o(A_KV), spec_t(QB_W), spec_o(B_HEADS * B_V)),
        compiler_params=_cparams(("parallel", "parallel")),
        name="cache_prep",
    )(cache_k, cache_v, cache_ckv, cache_kpe_p, wk_p, wv_p)


def _pre_kernel(is_ctx, x_ref, mod_ref, gpre_ref, win_ref, gq_ref, gk_ref, gqc_ref, wqb_ref, gkv_ref,
                wk_ref, wv_ref, ca_ref, sa_ref, cb_ref, sb_ref, *outs):
    if is_ctx:
        (qa_ref, kat_ref, va_ref, qb_ref, kbt_ref, vb_ref, z_ref, xbc_ref, misc_ref,
         kf_ref, vf_ref, ckvf_ref) = outs
    else:
        qa_ref, kat_ref, va_ref, qb_ref, kbt_ref, vb_ref, z_ref, xbc_ref, misc_ref = outs
    x = x_ref[...]
    shift1 = mod_ref[:, 0:D_MODEL]
    scale1 = mod_ref[:, D_MODEL:2 * D_MODEL]
    h = _rms(x, gpre_ref[...]) * (1.0 + scale1) + shift1
    proj = _bdot(h, win_ref[...])

    ca, sa, cb, sb = ca_ref[...], sa_ref[...], cb_ref[...], sb_ref[...]
    lane = _lane_iota((x.shape[0], LANES))

    gq = gq_ref[...]
    for hd in range(A_HEADS):
        col = proj[:, OFF_QA + hd * LANES:OFF_QA + (hd + 1) * LANES]
        ss = jnp.sum(col * col, axis=-1, keepdims=True)
        qn = col * lax.rsqrt(ss * (1.0 / HEAD_DIM) + EPS) * gq
        qn = _rope(qn, ca, sa, HEAD_DIM // 4) * (HEAD_DIM ** -0.5)
        qa_ref[:, hd * LANES:(hd + 1) * LANES] = qn.astype(BF16)

    kcol = proj[:, OFF_KA:OFF_KA + A_KV]
    k2 = kcol * kcol
    lo = lane < HEAD_DIM
    s_lo = jnp.sum(jnp.where(lo, k2, 0.0), axis=-1, keepdims=True)
    s_hi = jnp.sum(jnp.where(lo, 0.0, k2), axis=-1, keepdims=True)
    kn = kcol * lax.rsqrt(jnp.where(lo, s_lo, s_hi) * (1.0 / HEAD_DIM) + EPS) * gk_ref[...]
    vcol = proj[:, OFF_VA:OFF_VA + A_KV]
    if is_ctx:
        kf_ref[...] = kn
        vf_ref[...] = vcol
    kat_ref[...] = _rope(kn, ca, sa, HEAD_DIM // 4).T.astype(BF16)
    va_ref[...] = vcol.astype(BF16)

    qc = _rms(proj[:, OFF_QC:OFF_QC + B_Q_RANK], gqc_ref[...])
    qb = _bdot(qc, wqb_ref[...])
    for hd in range(B_HEADS):
        col = qb[:, hd * LANES:(hd + 1) * LANES]
        col = _rope(col, cb, sb, B_ROPE // 4) * ((B_NOPE + B_ROPE) ** -0.5)
        qb_ref[:, hd * LANES:(hd + 1) * LANES] = col.astype(BF16)

    ckv = _rms(proj[:, OFF_KVC:OFF_KVC + B_KV_RANK], gkv_ref[...])
    if is_ctx:
        ckvf_ref[...] = ckv
    knope = _bdot(ckv, wk_ref[...])
    misc = proj[:, OFF_MISC:OFF_MISC + LANES]
    misc_ref[...] = misc
    kpe = jnp.where((lane >= KPE_LANE) & (lane < KPE_LANE + B_ROPE), _rope(misc, cb, sb, B_ROPE // 4), 0.0)
    for hd in range(B_HEADS):
        kbt_ref[hd * LANES:(hd + 1) * LANES, :] = (knope[:, hd * LANES:(hd + 1) * LANES] + kpe).T.astype(BF16)
    vb_ref[...] = _bdot(ckv, wv_ref[...]).astype(BF16)

    z_ref[...] = proj[:, OFF_Z:OFF_Z + C_INNER]
    xbc_ref[...] = proj[:, OFF_XBC:OFF_XBC + C_CONV_CH]


def _pre(is_ctx, x2d, mod_l, lw, tabs):
    t = x2d.shape[0]
    nt = t // TM
    per_seq = DEC_SEQ // TM
    if is_ctx:
        mod_map = lambda i: (0, 0, 0)
        tab_map = lambda i: (per_seq, 0)
    else:
        mod_map = lambda i: (1 + i // per_seq, 0, 0)
        tab_map = lambda i: (i % per_seq, 0)
    const = lambda shape: pl.BlockSpec(shape, lambda i: (0,) * len(shape))
    tab_spec = pl.BlockSpec((TM, LANES), tab_map)
    tok = lambda w: pl.BlockSpec((TM, w), lambda i: (i, 0))
    tok_t = lambda w: pl.BlockSpec((w, TM), lambda i: (0, i))
    out_shape = [
        jax.ShapeDtypeStruct((t, QA_W), BF16),
        jax.ShapeDtypeStruct((A_KV, t), BF16),
        jax.ShapeDtypeStruct((t, A_KV), BF16),
        jax.ShapeDtypeStruct((t, QB_W), BF16),
        jax.ShapeDtypeStruct((QB_W, t), BF16),
        jax.ShapeDtypeStruct((t, B_HEADS * B_V), BF16),
        jax.ShapeDtypeStruct((t, C_INNER), F32),
        jax.ShapeDtypeStruct((t, C_CONV_CH), F32),
        jax.ShapeDtypeStruct((t, LANES), F32),
    ]
    out_specs = [tok(QA_W), tok_t(A_KV), tok(A_KV), tok(QB_W), tok_t(QB_W), tok(B_HEADS * B_V),
                 tok(C_INNER), tok(C_CONV_CH), tok(LANES)]
    if is_ctx:
        out_shape += [jax.ShapeDtypeStruct((t, A_KV), F32)] * 2 + [jax.ShapeDtypeStruct((t, B_KV_RANK), F32)]
        out_specs += [tok(A_KV), tok(A_KV), tok(B_KV_RANK)]
    return pl.pallas_call(
        functools.partial(_pre_kernel, is_ctx),
        out_shape=tuple(out_shape),
        grid=(nt,),
        in_specs=[
            tok(D_MODEL),
            pl.BlockSpec((None, 1, 6 * D_MODEL), mod_map),
            const((1, D_MODEL)),
            const((D_MODEL, IN_COLS_P)),
            const((1, LANES)), const((1, LANES)), const((1, B_Q_RANK)),
            const((B_Q_RANK, QB_W)), const((1, B_KV_RANK)),
            const((B_KV_RANK, QB_W)), const((B_KV_RANK, B_HEADS * B_V)),
            tab_spec, tab_spec, tab_spec, tab_spec,
        ],
        out_specs=tuple(out_specs),
        compiler_params=_cparams(("parallel",)),
        name="pre_ctx" if is_ctx else "pre_lat",
    )(x2d, mod_l, lw["g_pre"], lw["w_in"], lw["g_q"], lw["g_k"], lw["g_qc"], lw["w_qb"], lw["g_kv"],
      lw["w_k"], lw["w_v"], *tabs)


def _merge_halves(lo_part, hi_part):
    return jnp.where(_lane_iota(lo_part.shape) < HEAD_DIM, lo_part, hi_part)


def _attn_ctx_kernel(qa_ref, kat_ref, va_ref, qb_ref, kbt_ref, vb_ref, oa_ref, ob_ref):
    def attend(q, kt, v):
        s = jnp.dot(q, kt, preferred_element_type=F32)
        p = jnp.exp(s - jnp.max(s, axis=-1, keepdims=True))
        o = jnp.dot(p.astype(BF16), v, preferred_element_type=F32)
        return o / jnp.sum(p, axis=-1, keepdims=True)

    kat, va = kat_ref[...], va_ref[...]
    oa = [attend(qa_ref[:, hd * LANES:(hd + 1) * LANES], kat, va) for hd in range(A_HEADS)]
    for r in range(A_REP):
        oa_ref[:, r * LANES:(r + 1) * LANES] = _merge_halves(oa[r], oa[A_REP + r]).astype(BF16)
    ob = [attend(qb_ref[:, hd * LANES:(hd + 1) * LANES], kbt_ref[hd * LANES:(hd + 1) * LANES, :],
                 vb_ref[:, (hd // 2) * LANES:(hd // 2 + 1) * LANES]) for hd in range(B_HEADS)]
    for pr in range(B_HEADS // 2):
        ob_ref[:, pr * LANES:(pr + 1) * LANES] = _merge_halves(ob[2 * pr], ob[2 * pr + 1]).astype(BF16)


def _attn_ctx(qa, kat, va, qb, kbt, vb):
    t = qa.shape[0]
    tok = lambda w: pl.BlockSpec((SEQ, w), lambda b: (b, 0))
    tok_t = lambda w: pl.BlockSpec((w, SEQ), lambda b: (0, b))
    return pl.pallas_call(
        _attn_ctx_kernel,
        out_shape=(jax.ShapeDtypeStruct((t, A_Q), BF16), jax.ShapeDtypeStruct((t, B_HEADS * B_V), BF16)),
        grid=(t // SEQ,),
        in_specs=[tok(QA_W), tok_t(A_KV), tok(A_KV), tok(QB_W), tok_t(QB_W), tok(B_HEADS * B_V)],
        out_specs=(tok(A_Q), tok(B_HEADS * B_V)),
        compiler_params=_cparams(("parallel",)),
        name="attn_ctx",
    )(qa, kat, va, qb, kbt, vb)


def _attn_lat_kernel(qa_ref, kat_ref, va_ref, qb_ref, kbt_ref, vb_ref,
                     kct_ref, vc_ref, kbct_ref, vbc_ref, oa_ref, ob_ref):
    n_tiles = DEC_SEQ // TK

    def attend(q, kct, vc, kt_ref, row0, nrows, v_ref, col0):
        s = jnp.dot(q, kct, preferred_element_type=F32)
        m = jnp.max(s, axis=-1, keepdims=True)
        p = jnp.exp(s - m)
        l = jnp.sum(p, axis=-1, keepdims=True)
        acc = jnp.dot(p.astype(BF16), vc, preferred_element_type=F32)

        def body(j, carry):
            m, l, acc = carry
            k0 = pl.multiple_of(j * TK, TK)
            s = jnp.dot(q, kt_ref[row0:row0 + nrows, pl.ds(k0, TK)], preferred_element_type=F32)
            m_new = jnp.maximum(m, jnp.max(s, axis=-1, keepdims=True))
            alpha = jnp.exp(m - m_new)
            p = jnp.exp(s - m_new)
            l = alpha * l + jnp.sum(p, axis=-1, keepdims=True)
            acc = alpha * acc + jnp.dot(p.astype(BF16), v_ref[pl.ds(k0, TK), col0:col0 + LANES],
                                        preferred_element_type=F32)
            return m_new, l, acc

        m, l, acc = lax.fori_loop(0, n_tiles, body, (m, l, acc))
        return acc / l

    kct, vc = kct_ref[...], vc_ref[...]
    oa = [attend(qa_ref[:, hd * LANES:(hd + 1) * LANES], kct, vc, kat_ref, 0, A_KV, va_ref, 0)
          for hd in range(A_HEADS)]
    for r in range(A_REP):
        oa_ref[:, r * LANES:(r + 1) * LANES] = _merge_halves(oa[r], oa[A_REP + r]).astype(BF16)
    ob = []
    for hd in range(B_HEADS):
        pr = hd // 2
        ob.append(attend(qb_ref[:, hd * LANES:(hd + 1) * LANES], kbct_ref[hd * LANES:(hd + 1) * LANES, :],
                         vbc_ref[:, pr * LANES:(pr + 1) * LANES], kbt_ref, hd * LANES, LANES, vb_ref, pr * LANES))
    for pr in range(B_HEADS // 2):
        ob_ref[:, pr * LANES:(pr + 1) * LANES] = _merge_halves(ob[2 * pr], ob[2 * pr + 1]).astype(BF16)


def _attn_lat(qa, kat, va, qb, kbt, vb, kct, vc, kbct, vbc):
    t = qa.shape[0]
    nq = DEC_SEQ // TQ
    tok = lambda w: pl.BlockSpec((TQ, w), lambda b, i: (b * nq + i, 0))
    seq = lambda w: pl.BlockSpec((DEC_SEQ, w), lambda b, i: (b, 0))
    seq_t = lambda w: pl.BlockSpec((w, DEC_SEQ), lambda b, i: (0, b))
    cache = lambda r, c: pl.BlockSpec((None, r, c), lambda b, i: (b, 0, 0))
    return pl.pallas_call(
        _attn_lat_kernel,
        out_shape=(jax.ShapeDtypeStruct((t, A_Q), BF16), jax.ShapeDtypeStruct((t, B_HEADS * B_V), BF16)),
        grid=(DEC_BATCH, nq),
        in_specs=[tok(QA_W), seq_t(A_KV), seq(A_KV), tok(QB_W), seq_t(QB_W), seq(B_HEADS * B_V),
                  cache(A_KV, PAST_LEN), cache(PAST_LEN, A_KV), cache(QB_W, PAST_LEN),
                  cache(PAST_LEN, B_HEADS * B_V)],
        out_specs=(tok(A_Q), tok(B_HEADS * B_V)),
        compiler_params=_cparams(("parallel", "parallel")),
        name="attn_lat",
    )(qa, kat, va, qb, kbt, vb, kct, vc, kbct, vbc)


HALO = SUBLANES
N_PAIRS = C_HEADS // 2
HEADS_PER_GROUP = C_HEADS // C_GROUPS


def _ssd_kernel(n_chunks, has_h0, xbc_ref, prev_ref, next_ref, misc_ref, cw_ref, cb_ref, dtb_ref, alog_ref,
                dvec_ref, *rest):
    if has_h0:
        h0_ref, y_ref, hout_ref, st_ref = rest
    else:
        y_ref, hout_ref, st_ref = rest
    d = pl.program_id(1)
    c = pl.program_id(2)
    cidx = c + d * (n_chunks - 1 - 2 * c)

    @pl.when(c == 0)
    def _():
        if has_h0:
            st_ref[...] = h0_ref[...]
        else:
            st_ref[...] = jnp.zeros_like(st_ref)

    keep_prev = (cidx > 0).astype(F32)
    keep_next = (cidx < n_chunks - 1).astype(F32)
    ue = jnp.concatenate([prev_ref[...] * keep_prev, xbc_ref[...], next_ref[...] * keep_next], axis=0)
    rows = ue.shape[0]
    acc = jnp.broadcast_to(cb_ref[...], (C_CHUNK, C_CONV_CH))
    for k in range(C_CONV):
        sh = (C_CONV // 2 - k) % rows
        r = ue if sh == 0 else pltpu.roll(ue, sh, axis=0)
        acc = acc + r[HALO:HALO + C_CHUNK, :] * cw_ref[k:k + 1, :]
    xc = acc * jax.nn.sigmoid(acc)
    xs = xc[:, 0:C_INNER]
    bm = xc[:, C_INNER:C_INNER + LANES]
    cm = xc[:, C_INNER + LANES:C_INNER + 2 * LANES]
    bmt = bm.T.astype(BF16)

    x_dt = misc_ref[...] + dtb_ref[...]
    dtv = jnp.maximum(x_dt, 0.0) + jnp.log1p(jnp.exp(-jnp.abs(x_dt)))
    dta = dtv * (-jnp.exp(alog_ref[...]))
    row = lax.broadcasted_iota(jnp.int32, (C_CHUNK, LANES), 0)
    lane = _lane_iota((C_CHUNK, LANES))
    cum = dta
    step = 1
    while step < C_CHUNK:
        cum = cum + jnp.where(row >= step, pltpu.roll(cum, step, axis=0), 0.0)
        step *= 2
    tot = cum[C_CHUNK - 1:C_CHUNK, :]
    lo = lane < C_HEAD_DIM
    lane_group = (lane >= C_STATE).astype(jnp.int32)
    row_group = (row >= C_STATE).astype(jnp.int32)

    def run(direction):
        cx = cum if direction == 0 else tot - cum + dta
        cxt = cx.T
        dtt = dtv.T
        ecx = jnp.exp(cx)
        wst = jnp.exp(tot - cx) * dtv
        cdec = jnp.exp(tot)
        causal = (row >= lane) if direction == 0 else (row <= lane)
        g_mats = [jnp.dot(jnp.where(lane_group == g, cm, 0.0).astype(BF16), bmt, preferred_element_type=F32)
                  for g in range(C_GROUPS)]
        cm16 = cm.astype(BF16)

        def col(v, hd):
            j = DT_LANE + C_HEADS * direction + hd
            return v[:, j:j + 1]

        def rowv(vt, hd):
            j = DT_LANE + C_HEADS * direction + hd
            return vt[j:j + 1, :]

        for pr in range(N_PAIRS):
            xs_pair = xs[:, pr * LANES:(pr + 1) * LANES]
            yd = []
            for hd in (2 * pr, 2 * pr + 1):
                seg = jnp.where(causal, col(cx, hd) - rowv(cxt, hd), -jnp.inf)
                sc = g_mats[hd // HEADS_PER_GROUP] * jnp.exp(seg) * rowv(dtt, hd)
                yd.append(jnp.dot(sc.astype(BF16), xs_pair.astype(BF16), preferred_element_type=F32))
            st = st_ref[pr]
            y_off = jnp.dot(cm16, st.astype(BF16), preferred_element_type=F32)
            y = _merge_halves(yd[0], yd[1]) + y_off * jnp.where(lo, col(ecx, 2 * pr), col(ecx, 2 * pr + 1))
            if direction == 0:
                y = y + xs_pair * dvec_ref[:, pr * LANES:(pr + 1) * LANES]
            y_ref[:, pr * LANES:(pr + 1) * LANES] = y
            xw = xs_pair * jnp.where(lo, col(wst, 2 * pr), col(wst, 2 * pr + 1))
            new = jnp.dot(bmt, xw.astype(BF16), preferred_element_type=F32)
            own = row_group == jnp.where(lo, (2 * pr) // HEADS_PER_GROUP, (2 * pr + 1) // HEADS_PER_GROUP)
            dec = jnp.where(lo[0:1, :], col(cdec, 2 * pr), col(cdec, 2 * pr + 1))
            st_new = st * dec + jnp.where(own, new, 0.0)
            st_ref[pr] = st_new
            hout_ref[pr] = st_new

    @pl.when(d == 0)
    def _():
        run(0)

    @pl.when(d == 1)
    def _():
        run(1)


def _ssd(xbc, misc, lw, h0, seq_len):
    t = xbc.shape[0]
    nb = t // seq_len
    nc = seq_len // C_CHUNK
    per_chunk = C_CHUNK // HALO
    last_halo = t // HALO - 1

    def chunk(b, d, c):
        return b * nc + c + d * (nc - 1 - 2 * c)

    has_h0 = h0 is not None
    const = lambda shape: pl.BlockSpec(shape, lambda b, d, c: (0,) * len(shape))
    state_spec = pl.BlockSpec((None, None, N_PAIRS, LANES, LANES), lambda b, d, c: (b, d, 0, 0, 0))
    in_specs = [
        pl.BlockSpec((C_CHUNK, C_CONV_CH), lambda b, d, c: (chunk(b, d, c), 0)),
        pl.BlockSpec((HALO, C_CONV_CH), lambda b, d, c: (jnp.maximum(chunk(b, d, c) * per_chunk - 1, 0), 0)),
        pl.BlockSpec((HALO, C_CONV_CH),
                     lambda b, d, c: (jnp.minimum((chunk(b, d, c) + 1) * per_chunk, last_halo), 0)),
        pl.BlockSpec((C_CHUNK, LANES), lambda b, d, c: (chunk(b, d, c), 0)),
        const((C_CONV, C_CONV_CH)), const((1, C_CONV_CH)), const((1, LANES)), const((1, LANES)),
        const((1, C_INNER)),
    ]
    args = [xbc, xbc, xbc, misc, lw["conv_w"], lw["conv_b"], lw["dt_bias"], lw["a_log"], lw["d_vec"]]
    if has_h0:
        in_specs.append(state_spec)
        args.append(h0)
    return pl.pallas_call(
        functools.partial(_ssd_kernel, nc, has_h0),
        out_shape=(jax.ShapeDtypeStruct((2, t, C_INNER), F32),
                   jax.ShapeDtypeStruct((nb, 2, N_PAIRS, LANES, LANES), F32)),
        grid=(nb, 2, nc),
        in_specs=in_specs,
        out_specs=(pl.BlockSpec((None, C_CHUNK, C_INNER), lambda b, d, c: (d, chunk(b, d, c), 0)), state_spec),
        scratch_shapes=[pltpu.VMEM((N_PAIRS, LANES, LANES), F32)],
        compiler_params=_cparams(("parallel", "parallel", "arbitrary")),
        name="ssd_lat" if has_h0 else "ssd_ctx",
    )(*args)


def _post_kernel(x_ref, mod_ref, oa_ref, ob_ref, y_ref, z_ref, gssm_ref, wout_ref, gmix_ref, gffn_ref,
                 gffo_ref, w1_ref, w2_ref, o_ref):
    x = x_ref[...]
    gate1 = mod_ref[:, 2 * D_MODEL:3 * D_MODEL]
    shift2 = mod_ref[:, 3 * D_MODEL:4 * D_MODEL]
    scale2 = mod_ref[:, 4 * D_MODEL:5 * D_MODEL]
    gate2 = mod_ref[:, 5 * D_MODEL:6 * D_MODEL]
    z = z_ref[...]
    oc = _rms((y_ref[0] + y_ref[1]) * (z * jax.nn.sigmoid(z)), gssm_ref[...])
    mix = jnp.concatenate([oa_ref[...], ob_ref[...], oc.astype(BF16)], axis=1)
    out = jnp.dot(mix, wout_ref[...], preferred_element_type=F32)
    x = x + gate1 * _rms(out, gmix_ref[...])
    h = _rms(x, gffn_ref[...]) * (1.0 + scale2) + shift2
    u = jnp.maximum(_bdot(h, w1_ref[...]), 0.0)
    f = _bdot(u * u, w2_ref[...])
    o_ref[...] = x + gate2 * _rms(f, gffo_ref[...])


def _post(is_ctx, x2d, mod_l, oa, ob, y, z, lw):
    t = x2d.shape[0]
    per_seq = DEC_SEQ // TM
    mod_map = (lambda i: (0, 0, 0)) if is_ctx else (lambda i: (1 + i // per_seq, 0, 0))
    tok = lambda w: pl.BlockSpec((TM, w), lambda i: (i, 0))
    const = lambda shape: pl.BlockSpec(shape, lambda i: (0,) * len(shape), pipeline_mode=pl.Buffered(1))
    return pl.pallas_call(
        _post_kernel,
        out_shape=jax.ShapeDtypeStruct((t, D_MODEL), F32),
        grid=(t // TM,),
        in_specs=[
            tok(D_MODEL),
            pl.BlockSpec((None, 1, 6 * D_MODEL), mod_map),
            tok(A_Q), tok(B_HEADS * B_V),
            pl.BlockSpec((2, TM, C_INNER), lambda i: (0, i, 0)),
            tok(C_INNER),
            const((1, C_INNER)), const((D_MODEL, D_MODEL)), const((1, D_MODEL)), const((1, D_MODEL)),
            const((1, D_MODEL)), const((D_MODEL, D_FF)), const((D_FF, D_MODEL)),
        ],
        out_specs=tok(D_MODEL),
        compiler_params=_cparams(("parallel",)),
        name="post_ctx" if is_ctx else "post_lat",
    )(x2d, mod_l, oa, ob, y, z, lw["g_ssm"], lw["w_out"], lw["g_mix"], lw["g_ffn"], lw["g_ffo"],
      lw["w1"], lw["w2"])


def _rope_tables():
    pos = jnp.arange(DEC_SEQ)
    axis_pos = jnp.stack([(pos // GRID_W).astype(F32), (pos % GRID_W).astype(F32)], axis=0)

    def pattern(rot_dim):
        half = rot_dim // 2
        quarter = half // 2
        inv = 1.0 / (ROPE_THETA ** (jnp.arange(0, half, 2, dtype=F32) / half))
        dd = jnp.arange(rot_dim)
        ang = axis_pos[dd // half][:, :].T * inv[dd % quarter][None, :]
        sign = jnp.where((dd % half) < quarter, -1.0, 1.0).astype(F32)
        return jnp.cos(ang), jnp.sin(ang) * sign

    ca, sa = pattern(HEAD_DIM)
    ca = jnp.tile(ca, (1, LANES // HEAD_DIM))
    sa = jnp.tile(sa, (1, LANES // HEAD_DIM))
    cb32, sb32 = pattern(B_ROPE)
    ones = jnp.ones((DEC_SEQ, KPE_LANE), F32)
    tail = LANES - KPE_LANE - B_ROPE
    cb = jnp.concatenate([ones, cb32, jnp.ones((DEC_SEQ, tail), F32)], axis=1)
    sb = jnp.concatenate([0 * ones, sb32, jnp.zeros((DEC_SEQ, tail), F32)], axis=1)
    ident_c = jnp.ones((TM, LANES), F32)
    ident_s = jnp.zeros((TM, LANES), F32)
    return tuple(jnp.concatenate([tb, idt], axis=0)
                 for tb, idt in ((ca, ident_c), (sa, ident_s), (cb, ident_c), (sb, ident_s)))


def _slot_vec(g):
    return jnp.tile(g, LANES // HEAD_DIM)[None, :]


def _layer_weights(l, p):
    w_in = p["w_in"][l]
    zeros = lambda n: jnp.zeros((D_MODEL, n), F32)
    cols = []
    for hd in range(A_HEADS):
        q = w_in[:, hd * HEAD_DIM:(hd + 1) * HEAD_DIM]
        cols += [q, zeros(HEAD_DIM)] if hd // A_REP == 0 else [zeros(HEAD_DIM), q]
    o = A_Q
    cols.append(w_in[:, o:o + 2 * A_KV + B_Q_RANK + B_KV_RANK])
    o += 2 * A_KV + B_Q_RANK + B_KV_RANK
    kpe_w = w_in[:, o:o + B_ROPE]
    o += B_ROPE
    z_w = w_in[:, o:o + C_INNER]
    o += C_INNER
    xbc_w = w_in[:, o:o + C_CONV_CH]
    o += C_CONV_CH
    dt_w = w_in[:, o:o + 2 * C_HEADS]
    cols += [zeros(KPE_LANE), kpe_w, dt_w, zeros(LANES - DT_LANE - 2 * C_HEADS), z_w, xbc_w]
    w_in_p = jnp.concatenate(cols, axis=1).astype(BF16)

    w_qb = p["mla_w_qb"][l].reshape(B_Q_RANK, B_HEADS, B_NOPE + B_ROPE)
    w_qb_p = jnp.pad(w_qb, ((0, 0), (0, 0), (0, LANES - B_NOPE - B_ROPE))).reshape(B_Q_RANK, QB_W)
    w_kvb = p["mla_w_kvb"][l].reshape(B_KV_RANK, B_HEADS, B_NOPE + B_V)
    w_k = jnp.pad(w_kvb[:, :, :B_NOPE], ((0, 0), (0, 0), (0, LANES - B_NOPE))).reshape(B_KV_RANK, QB_W)
    w_v = w_kvb[:, :, B_NOPE:].reshape(B_KV_RANK, B_HEADS * B_V)

    w_out = p["w_out"][l]
    order = [r + g * A_REP for r in range(A_REP) for g in range(A_KV_HEADS)]
    w_out_a = w_out[:A_Q].reshape(A_HEADS, HEAD_DIM, D_MODEL)[jnp.array(order)].reshape(A_Q, D_MODEL)
    w_out_p = jnp.concatenate([w_out_a, w_out[A_Q:]], axis=0).astype(BF16)

    lane_vec = lambda v: jnp.pad(v.reshape(-1), (DT_LANE, LANES - DT_LANE - 2 * C_HEADS))[None, :]
    return {
        "g_pre": p["norm_mix_pre"][l][None, :],
        "w_in": w_in_p,
        "g_q": _slot_vec(p["attn_q_norm"][l]),
        "g_k": _slot_vec(p["attn_k_norm"][l]),
        "g_qc": p["mla_q_norm"][l][None, :],
        "w_qb": w_qb_p.astype(BF16),
        "g_kv": p["mla_kv_norm"][l][None, :],
        "w_k": w_k.astype(BF16),
        "w_v": w_v.astype(BF16),
        "conv_w": p["ssm_conv_w"][l].T,
        "conv_b": p["ssm_conv_b"][l][None, :],
        "dt_bias": lane_vec(p["ssm_dt_bias"][l]),
        "a_log": lane_vec(p["ssm_a_log"][l]),
        "d_vec": jnp.repeat(p["ssm_d"][l], C_HEAD_DIM)[None, :],
        "g_ssm": p["ssm_norm"][l][None, :],
        "w_out": w_out_p,
        "g_mix": p["norm_mix_post"][l][None, :],
        "g_ffn": p["norm_ffn_pre"][l][None, :],
        "g_ffo": p["norm_ffn_post"][l][None, :],
        "w1": p["w_ffn1"][l].astype(BF16),
        "w2": p["w_ffn2"][l].astype(BF16),
    }


def _state_to_pairs(h):
    lead = h.shape[:-3]
    ht = jnp.swapaxes(h, -1, -2)
    out = jnp.zeros(lead + (N_PAIRS, LANES, LANES), h.dtype)
    for hd in range(C_HEADS):
        g = hd // HEADS_PER_GROUP
        out = out.at[..., hd // 2, g * C_STATE:(g + 1) * C_STATE,
                     (hd % 2) * C_HEAD_DIM:(hd % 2 + 1) * C_HEAD_DIM].set(ht[..., hd, :, :])
    return out


def _pairs_to_state(s):
    heads = []
    for hd in range(C_HEADS):
        g = hd // HEADS_PER_GROUP
        blk = s[..., hd // 2, g * C_STATE:(g + 1) * C_STATE, (hd % 2) * C_HEAD_DIM:(hd % 2 + 1) * C_HEAD_DIM]
        heads.append(jnp.swapaxes(blk, -1, -2))
    return jnp.stack(heads, axis=-3)


def kernel(x_prompt, x_sample, cache_attn_k, cache_attn_v, cache_mla_ckv, cache_mla_kpe, state_ssm, c, c_ctx, norm_mix_pre, norm_mix_post, norm_ffn_pre, norm_ffn_post, w_mod, b_mod, w_in, attn_q_norm, attn_k_norm, mla_q_norm, mla_w_qb, mla_kv_norm, mla_w_kvb, ssm_conv_w, ssm_conv_b, ssm_dt_bias, ssm_a_log, ssm_d, ssm_norm, w_out, w_ffn1, w_ffn2):
    p = dict(norm_mix_pre=norm_mix_pre, norm_mix_post=norm_mix_post, norm_ffn_pre=norm_ffn_pre,
             norm_ffn_post=norm_ffn_post, w_in=w_in, attn_q_norm=attn_q_norm, attn_k_norm=attn_k_norm,
             mla_q_norm=mla_q_norm, mla_w_qb=mla_w_qb, mla_kv_norm=mla_kv_norm, mla_w_kvb=mla_w_kvb,
             ssm_conv_w=ssm_conv_w, ssm_conv_b=ssm_conv_b, ssm_dt_bias=ssm_dt_bias, ssm_a_log=ssm_a_log,
             ssm_d=ssm_d, ssm_norm=ssm_norm, w_out=w_out, w_ffn1=w_ffn1, w_ffn2=w_ffn2)
    lws = [_layer_weights(l, p) for l in range(DEPTH)]
    tabs = _rope_tables()

    cvec = jnp.concatenate([c_ctx[None, :], c, jnp.zeros((MOD_ROWS - 1 - DEC_BATCH, D_MODEL), F32)], axis=0)
    mod = _modulation(cvec, w_mod, b_mod)

    kpe_p = jnp.pad(cache_mla_kpe, ((0, 0),) * 3 + ((KPE_LANE, LANES - KPE_LANE - B_ROPE),))
    kct, vc, kbct, vbc = _cache_prep(
        cache_attn_k.reshape(DEC_BATCH, DEPTH, PAST_LEN, A_KV), cache_attn_v.reshape(DEC_BATCH, DEPTH, PAST_LEN, A_KV),
        cache_mla_ckv, kpe_p, jnp.stack([lw["w_k"] for lw in lws]), jnp.stack([lw["w_v"] for lw in lws]))
    h0_lat = _state_to_pairs(state_ssm)

    xp = x_prompt.reshape(BATCH * SEQ, D_MODEL)
    xs = x_sample.reshape(DEC_BATCH * DEC_SEQ, D_MODEL)
    new_k, new_v, new_ckv, new_kpe, new_ssm = [], [], [], [], []
    for l in range(DEPTH):
        lw = lws[l]
        mod_l = mod[l][:, None, :]
        qa, kat, va, qb, kbt, vb, z, xbc, misc, kf, vf, ckvf = _pre(True, xp, mod_l, lw, tabs)
        oa, ob = _attn_ctx(qa, kat, va, qb, kbt, vb)
        y, hfin = _ssd(xbc, misc, lw, None, SEQ)
        xp = _post(True, xp, mod_l, oa, ob, y, z, lw)
        new_k.append(kf.reshape(BATCH, SEQ, A_KV_HEADS, HEAD_DIM))
        new_v.append(vf.reshape(BATCH, SEQ, A_KV_HEADS, HEAD_DIM))
        new_ckv.append(ckvf.reshape(BATCH, SEQ, B_KV_RANK))
        new_kpe.append(misc[:, KPE_LANE:KPE_LANE + B_ROPE].reshape(BATCH, SEQ, B_ROPE))
        new_ssm.append(_pairs_to_state(hfin))
        qa, kat, va, qb, kbt, vb, z, xbc, misc = _pre(False, xs, mod_l, lw, tabs)
        oa, ob = _attn_lat(qa, kat, va, qb, kbt, vb, kct[l], vc[l], kbct[l], vbc[l])
        y, _ = _ssd(xbc, misc, lw, h0_lat[:, l], DEC_SEQ)
        xs = _post(False, xs, mod_l, oa, ob, y, z, lw)
    return (xp.reshape(BATCH, SEQ, D_MODEL), xs.reshape(DEC_BATCH, DEC_SEQ, D_MODEL),
            jnp.stack(new_k, axis=1), jnp.stack(new_v, axis=1), jnp.stack(new_ckv, axis=1),
            jnp.stack(new_kpe, axis=1), jnp.stack(new_ssm, axis=1))
```

```python
import functools
import math

import numpy as np
import jax
import jax.numpy as jnp
from jax import lax
from jax.experimental import pallas as pl
from jax.experimental.pallas import tpu as pltpu

F32 = jnp.float32
BF16 = jnp.bfloat16

D_MODEL = 1024
BATCH = 16
SEQ = 256
DEPTH = 4
DEC_BATCH = 2
DEC_SEQ = 4096
PAST_LEN = 256
GRID_W = 64
HEAD_DIM = 64
A_HEADS = 6
A_KV_HEADS = 2
A_REP = A_HEADS // A_KV_HEADS
B_HEADS = 4
B_Q_RANK = 256
B_KV_RANK = 128
B_NOPE = 64
B_ROPE = 32
B_V = 64
C_HEADS = 6
C_HEAD_DIM = 64
C_INNER = C_HEADS * C_HEAD_DIM
C_GROUPS = 2
C_STATE = 64
C_CONV = 5
C_CHUNK = 128
C_CONV_CH = C_INNER + 2 * C_GROUPS * C_STATE
D_FF = 4 * D_MODEL
A_Q = A_HEADS * HEAD_DIM
A_KV = A_KV_HEADS * HEAD_DIM
ROPE_THETA = 10000.0
EPS = 1e-6
LOG2E = math.log2(math.e)

LANES = 128
SUBLANES = 8
VMEM_LIMIT = 56 * 1024 * 1024

OFF_QA = 0
OFF_KA = OFF_QA + A_Q
OFF_VA = OFF_KA + A_KV
OFF_QC = OFF_VA + A_KV
OFF_KVC = OFF_QC + B_Q_RANK
OFF_Z = OFF_KVC + B_KV_RANK
OFF_XBC = OFF_Z + C_INNER
OFF_MISC = OFF_XBC + C_CONV_CH
IN_COLS_P = OFF_MISC + LANES
KPE_LANE = 64
DT_LANE = KPE_LANE + B_ROPE
QB_W = B_HEADS * LANES

TM = 512
TQ = 512
TK = 512


def _cparams(sem):
    return pltpu.CompilerParams(dimension_semantics=sem, vmem_limit_bytes=VMEM_LIMIT)


def _rms(x, g):
    return x * lax.rsqrt(jnp.mean(x * x, axis=-1, keepdims=True) + EPS) * g


def _bdot(a, b):
    return jnp.dot(a.astype(BF16), b.astype(BF16), preferred_element_type=F32)


def _lane_iota(shape):
    return lax.broadcasted_iota(jnp.int32, shape, len(shape) - 1)


def _swap_lanes(x, dist):
    lane = _lane_iota(x.shape)
    fwd = pltpu.roll(x, LANES - dist, axis=1)
    bwd = pltpu.roll(x, dist, axis=1)
    return jnp.where((lane % (2 * dist)) < dist, fwd, bwd)


def _rope(x, cos, sin, dist):
    return x * cos + _swap_lanes(x, dist) * sin


def _head_pair_rms(col, gain):
    lo = _lane_iota(col.shape) < HEAD_DIM
    c2 = col * col
    s_lo = jnp.sum(jnp.where(lo, c2, 0.0), axis=-1, keepdims=True)
    s_hi = jnp.sum(jnp.where(lo, 0.0, c2), axis=-1, keepdims=True)
    return col * lax.rsqrt(jnp.where(lo, s_lo, s_hi) * (1.0 / HEAD_DIM) + EPS) * gain


MOD_ROWS = SUBLANES
MOD_TN = 1536


def _mod_kernel(c_ref, w_ref, b_ref, o_ref):
    c = c_ref[...]
    o_ref[...] = _bdot(c * jax.nn.sigmoid(c), w_ref[...]) + b_ref[...]


def _modulation(cvec, w_mod, b_mod):
    return pl.pallas_call(
        _mod_kernel,
        out_shape=jax.ShapeDtypeStruct((DEPTH, MOD_ROWS, 6 * D_MODEL), F32),
        grid=(DEPTH, 6 * D_MODEL // MOD_TN),
        in_specs=[
            pl.BlockSpec((MOD_ROWS, D_MODEL), lambda l, j: (0, 0)),
            pl.BlockSpec((None, D_MODEL, MOD_TN), lambda l, j: (l, 0, j)),
            pl.BlockSpec((None, 1, MOD_TN), lambda l, j: (l, 0, j)),
        ],
        out_specs=pl.BlockSpec((None, MOD_ROWS, MOD_TN), lambda l, j: (l, 0, j)),
        compiler_params=_cparams(("parallel", "parallel")),
        name="modulation",
    )(cvec, w_mod, b_mod.reshape(DEPTH, 1, 6 * D_MODEL))


def _cache_kernel(k_ref, v_ref, ckv_ref, kpe_ref, wk_ref, wv_ref, ko_ref, vt_ref, kb_ref, vbt_ref):
    ko_ref[...] = k_ref[...].astype(BF16)
    vt_ref[...] = v_ref[...].T.astype(BF16)
    ckv = ckv_ref[...]
    kn = _bdot(ckv, wk_ref[...])
    kpe = kpe_ref[...]
    for h in range(B_HEADS):
        kb_ref[:, h * LANES:(h + 1) * LANES] = (kn[:, h * LANES:(h + 1) * LANES] + kpe).astype(BF16)
    vbt_ref[...] = _bdot(ckv, wv_ref[...]).T.astype(BF16)


def _cache_prep(cache_k, cache_v, cache_ckv, cache_kpe_p, wk_p, wv_p):
    spec_tok = lambda w: pl.BlockSpec((None, None, PAST_LEN, w), lambda l, b: (b, l, 0, 0))
    spec_t = lambda w: pl.BlockSpec((None, None, w, PAST_LEN), lambda l, b: (l, b, 0, 0))
    spec_o = lambda w: pl.BlockSpec((None, None, PAST_LEN, w), lambda l, b: (l, b, 0, 0))
    return pl.pallas_call(
        _cache_kernel,
        out_shape=(
            jax.ShapeDtypeStruct((DEPTH, DEC_BATCH, PAST_LEN, A_KV), BF16),
            jax.ShapeDtypeStruct((DEPTH, DEC_BATCH, A_KV, PAST_LEN), BF16),
            jax.ShapeDtypeStruct((DEPTH, DEC_BATCH, PAST_LEN, QB_W), BF16),
            jax.ShapeDtypeStruct((DEPTH, DEC_BATCH, B_HEADS * B_V, PAST_LEN), BF16),
        ),
        grid=(DEPTH, DEC_BATCH),
        in_specs=[
            spec_tok(A_KV), spec_tok(A_KV), spec_tok(B_KV_RANK), spec_tok(LANES),
            pl.BlockSpec((None, B_KV_RANK, QB_W), lambda l, b: (l, 0, 0)),
            pl.BlockSpec((None, B_KV_RANK, B_HEADS * B_V), lambda l, b: (l, 0, 0)),
        ],
        out_specs=(spec_o(A_KV), spec_t(A_KV), spec_o(QB_W), spec_t(B_HEADS * B_V)),
        compiler_params=_cparams(("parallel", "parallel")),
        name="cache_prep",
    )(cache_k, cache_v, cache_ckv, cache_kpe_p, wk_p, wv_p)


def _pre_kernel(is_ctx, x_ref, mod_ref, gpre_ref, win_ref, gq_ref, gk_ref, gqc_ref, wqb_ref, gkv_ref,
                wk_ref, wv_ref, ca_ref, sa_ref, cb_ref, sb_ref, *outs):
    if is_ctx:
        (qat_ref, ka_ref, vat_ref, qbt_ref, kb_ref, vbt_ref, z_ref, xbc_ref, misc_ref,
         kf_ref, vf_ref, ckvf_ref) = outs
    else:
        qat_ref, ka_ref, vat_ref, qbt_ref, kb_ref, vbt_ref, z_ref, xbc_ref, misc_ref = outs
    x = x_ref[...]
    shift1 = mod_ref[:, 0:D_MODEL]
    scale1 = mod_ref[:, D_MODEL:2 * D_MODEL]
    h = _rms(x, gpre_ref[...]) * (1.0 + scale1) + shift1
    proj = _bdot(h, win_ref[...])

    ca, sa, cb, sb = ca_ref[...], sa_ref[...], cb_ref[...], sb_ref[...]
    lane = _lane_iota((x.shape[0], LANES))

    gq = gq_ref[...]
    for cidx in range(A_Q // LANES):
        col = proj[:, OFF_QA + cidx * LANES:OFF_QA + (cidx + 1) * LANES]
        qn = _rope(_head_pair_rms(col, gq), ca, sa, HEAD_DIM // 4) * (HEAD_DIM ** -0.5 * LOG2E)
        qat_ref[cidx * LANES:(cidx + 1) * LANES, :] = qn.T.astype(BF16)

    kn = _head_pair_rms(proj[:, OFF_KA:OFF_KA + A_KV], gk_ref[...])
    vcol = proj[:, OFF_VA:OFF_VA + A_KV]
    if is_ctx:
        kf_ref[...] = kn
        vf_ref[...] = vcol
    ka_ref[...] = _rope(kn, ca, sa, HEAD_DIM // 4).astype(BF16)
    vat_ref[...] = vcol.T.astype(BF16)

    qc = _rms(proj[:, OFF_QC:OFF_QC + B_Q_RANK], gqc_ref[...])
    qb = _bdot(qc, wqb_ref[...])
    for hd in range(B_HEADS):
        col = qb[:, hd * LANES:(hd + 1) * LANES]
        col = _rope(col, cb, sb, B_ROPE // 4) * ((B_NOPE + B_ROPE) ** -0.5 * LOG2E)
        qbt_ref[hd * LANES:(hd + 1) * LANES, :] = col.T.astype(BF16)

    ckv = _rms(proj[:, OFF_KVC:OFF_KVC + B_KV_RANK], gkv_ref[...])
    if is_ctx:
        ckvf_ref[...] = ckv
    knope = _bdot(ckv, wk_ref[...])
    misc = proj[:, OFF_MISC:OFF_MISC + LANES]
    misc_ref[...] = misc
    kpe = jnp.where((lane >= KPE_LANE) & (lane < KPE_LANE + B_ROPE), _rope(misc, cb, sb, B_ROPE // 4), 0.0)
    for hd in range(B_HEADS):
        kb_ref[:, hd * LANES:(hd + 1) * LANES] = (knope[:, hd * LANES:(hd + 1) * LANES] + kpe).astype(BF16)
    vbt_ref[...] = _bdot(ckv, wv_ref[...]).T.astype(BF16)

    z_ref[...] = proj[:, OFF_Z:OFF_Z + C_INNER]
    xbc_ref[...] = proj[:, OFF_XBC:OFF_XBC + C_CONV_CH]


def _pre(is_ctx, x2d, mod_l, lw, tabs):
    t = x2d.shape[0]
    nt = t // TM
    per_seq = DEC_SEQ // TM
    if is_ctx:
        mod_map = lambda i: (0, 0, 0)
        tab_map = lambda i: (per_seq, 0)
    else:
        mod_map = lambda i: (1 + i // per_seq, 0, 0)
        tab_map = lambda i: (i % per_seq, 0)
    const = lambda shape: pl.BlockSpec(shape, lambda i: (0,) * len(shape))
    tab_spec = pl.BlockSpec((TM, LANES), tab_map)
    tok = lambda w: pl.BlockSpec((TM, w), lambda i: (i, 0))
    tok_t = lambda w: pl.BlockSpec((w, TM), lambda i: (0, i))
    out_shape = [
        jax.ShapeDtypeStruct((A_Q, t), BF16),
        jax.ShapeDtypeStruct((t, A_KV), BF16),
        jax.ShapeDtypeStruct((A_KV, t), BF16),
        jax.ShapeDtypeStruct((QB_W, t), BF16),
        jax.ShapeDtypeStruct((t, QB_W), BF16),
        jax.ShapeDtypeStruct((B_HEADS * B_V, t), BF16),
        jax.ShapeDtypeStruct((t, C_INNER), F32),
        jax.ShapeDtypeStruct((t, C_CONV_CH), F32),
        jax.ShapeDtypeStruct((t, LANES), F32),
    ]
    out_specs = [tok_t(A_Q), tok(A_KV), tok_t(A_KV), tok_t(QB_W), tok(QB_W), tok_t(B_HEADS * B_V),
                 tok(C_INNER), tok(C_CONV_CH), tok(LANES)]
    if is_ctx:
        out_shape += [jax.ShapeDtypeStruct((t, A_KV), F32)] * 2 + [jax.ShapeDtypeStruct((t, B_KV_RANK), F32)]
        out_specs += [tok(A_KV), tok(A_KV), tok(B_KV_RANK)]
    return pl.pallas_call(
        functools.partial(_pre_kernel, is_ctx),
        out_shape=tuple(out_shape),
        grid=(nt,),
        in_specs=[
            tok(D_MODEL),
            pl.BlockSpec((None, 1, 6 * D_MODEL), mod_map),
            const((1, D_MODEL)),
            const((D_MODEL, IN_COLS_P)),
            const((1, LANES)), const((1, LANES)), const((1, B_Q_RANK)),
            const((B_Q_RANK, QB_W)), const((1, B_KV_RANK)),
            const((B_KV_RANK, QB_W)), const((B_KV_RANK, B_HEADS * B_V)),
            tab_spec, tab_spec, tab_spec, tab_spec,
        ],
        out_specs=tuple(out_specs),
        compiler_params=_cparams(("parallel",)),
        name="pre_ctx" if is_ctx else "pre_lat",
    )(x2d, mod_l, lw["g_pre"], lw["w_in"], lw["g_q"], lw["g_k"], lw["g_qc"], lw["w_qb"], lw["g_kv"],
      lw["w_k"], lw["w_v"], *tabs)


def _merge_halves(lo_part, hi_part):
    return jnp.where(_lane_iota(lo_part.shape) < HEAD_DIM, lo_part, hi_part)


def _gqa_query_slot(qat_ref, hd):
    q = qat_ref[hd * HEAD_DIM:(hd + 1) * HEAD_DIM, :]
    zero = jnp.zeros_like(q)
    return jnp.concatenate([q, zero] if hd // A_REP == 0 else [zero, q], axis=0)


ONES_ROWS = 16
ACC_ROWS = LANES + ONES_ROWS


def _with_ones(v_t):
    return jnp.concatenate([v_t, jnp.ones((ONES_ROWS, v_t.shape[1]), BF16)], axis=0)


def _first_tile(k, q_t, v_t):
    s = jnp.dot(k, q_t, preferred_element_type=F32)
    m = jnp.max(s, axis=0, keepdims=True)
    p = jnp.exp2(s - m)
    return m, jnp.dot(_with_ones(v_t), p.astype(BF16), preferred_element_type=F32)


def _next_tile(s, m, acc, v_t):
    m_new = jnp.maximum(m, jnp.max(s, axis=0, keepdims=True))
    p = jnp.exp2(s - m_new)
    acc = jnp.exp2(m - m_new) * acc + jnp.dot(_with_ones(v_t), p.astype(BF16), preferred_element_type=F32)
    return m_new, acc


def _normalised(acc):
    return (acc[0:LANES, :] / acc[LANES:LANES + 1, :]).T


def _write_heads(outs_a, outs_b, oa_ref, ob_ref):
    for r in range(A_REP):
        oa_ref[:, r * LANES:(r + 1) * LANES] = _merge_halves(outs_a[r], outs_a[A_REP + r]).astype(BF16)
    for pr in range(B_HEADS // 2):
        ob_ref[:, pr * LANES:(pr + 1) * LANES] = _merge_halves(outs_b[2 * pr], outs_b[2 * pr + 1]).astype(BF16)


def _attn_ctx_kernel(qat_ref, ka_ref, vat_ref, qbt_ref, kb_ref, vbt_ref, oa_ref, ob_ref):
    def attend(k, q_t, v_t):
        return _normalised(_first_tile(k, q_t, v_t)[1])

    ka, vat = ka_ref[...], vat_ref[...]
    outs_a = [attend(ka, _gqa_query_slot(qat_ref, hd), vat) for hd in range(A_HEADS)]
    outs_b = [attend(kb_ref[:, hd * LANES:(hd + 1) * LANES], qbt_ref[hd * LANES:(hd + 1) * LANES, :],
                     vbt_ref[(hd // 2) * LANES:(hd // 2 + 1) * LANES, :]) for hd in range(B_HEADS)]
    _write_heads(outs_a, outs_b, oa_ref, ob_ref)


def _attn_ctx(qat, ka, vat, qbt, kb, vbt):
    t = ka.shape[0]
    tok = lambda w: pl.BlockSpec((SEQ, w), lambda b: (b, 0))
    tok_t = lambda w: pl.BlockSpec((w, SEQ), lambda b: (0, b))
    return pl.pallas_call(
        _attn_ctx_kernel,
        out_shape=(jax.ShapeDtypeStruct((t, A_Q), BF16), jax.ShapeDtypeStruct((t, B_HEADS * B_V), BF16)),
        grid=(t // SEQ,),
        in_specs=[tok_t(A_Q), tok(A_KV), tok_t(A_KV), tok_t(QB_W), tok(QB_W), tok_t(B_HEADS * B_V)],
        out_specs=(tok(A_Q), tok(B_HEADS * B_V)),
        compiler_params=_cparams(("parallel",)),
        name="attn_ctx",
    )(qat, ka, vat, qbt, kb, vbt)


def _attn_lat_kernel(is_gqa, *refs):
    if is_gqa:
        q0_ref, q1_ref, k_ref, vt_ref, kc_ref, vct_ref, o_ref, s_ref, acc_ref = refs
        zero = jnp.zeros((HEAD_DIM, TQ), BF16)
        q_slots = (jnp.concatenate([q0_ref[...], zero], axis=0), jnp.concatenate([zero, q1_ref[...]], axis=0))
        key_cols = (0, 0)
    else:
        q_ref, k_ref, vt_ref, kc_ref, vct_ref, o_ref, s_ref, acc_ref = refs
        q_slots = (q_ref[0:LANES, :], q_ref[LANES:2 * LANES, :])
        key_cols = (0, LANES)
    n_tiles = DEC_SEQ // TK
    vct = vct_ref[...]

    row_max = []
    for hh in range(2):
        c0 = key_cols[hh]
        m, acc = _first_tile(kc_ref[:, c0:c0 + LANES], q_slots[hh], vct)
        acc_ref[hh] = acc
        s_ref[hh, 0] = jnp.dot(k_ref[0:TK, c0:c0 + LANES], q_slots[hh], preferred_element_type=F32)
        row_max.append(m)
    for j in range(n_tiles):
        for hh in range(2):
            c0 = key_cols[hh]
            if j + 1 < n_tiles:
                s_ref[hh, (j + 1) % 2] = jnp.dot(k_ref[(j + 1) * TK:(j + 2) * TK, c0:c0 + LANES], q_slots[hh],
                                                 preferred_element_type=F32)
            row_max[hh], acc = _next_tile(s_ref[hh, j % 2], row_max[hh], acc_ref[hh], vt_ref[:, j * TK:(j + 1) * TK])
            acc_ref[hh] = acc
    o_ref[...] = _merge_halves(_normalised(acc_ref[0]), _normalised(acc_ref[1])).astype(BF16)


def _attn_lat(is_gqa, q_t, k, v_t, kc, vc_t):
    t = k.shape[0]
    nq = DEC_SEQ // TQ
    n_pairs = A_REP if is_gqa else B_HEADS // 2
    kw = A_KV if is_gqa else 2 * LANES
    if is_gqa:
        q_specs = [pl.BlockSpec((HEAD_DIM, TQ), lambda b, r, i: (r, b * nq + i)),
                   pl.BlockSpec((HEAD_DIM, TQ), lambda b, r, i: (A_REP + r, b * nq + i))]
        q_args = [q_t, q_t]
        pair_col = lambda r: 0
    else:
        q_specs = [pl.BlockSpec((2 * LANES, TQ), lambda b, r, i: (r, b * nq + i))]
        q_args = [q_t]
        pair_col = lambda r: r
    return pl.pallas_call(
        functools.partial(_attn_lat_kernel, is_gqa),
        out_shape=jax.ShapeDtypeStruct((t, n_pairs * LANES), BF16),
        grid=(DEC_BATCH, n_pairs, nq),
        in_specs=q_specs + [
            pl.BlockSpec((DEC_SEQ, kw), lambda b, r, i: (b, pair_col(r))),
            pl.BlockSpec((LANES, DEC_SEQ), lambda b, r, i: (pair_col(r), b)),
            pl.BlockSpec((None, PAST_LEN, kw), lambda b, r, i: (b, 0, pair_col(r))),
            pl.BlockSpec((None, LANES, PAST_LEN), lambda b, r, i: (b, pair_col(r), 0)),
        ],
        out_specs=pl.BlockSpec((TQ, LANES), lambda b, r, i: (b * nq + i, r)),
        scratch_shapes=[pltpu.VMEM((2, 2, TK, TQ), F32), pltpu.VMEM((2, ACC_ROWS, TQ), F32)],
        compiler_params=_cparams(("parallel", "parallel", "parallel")),
        name="attn_lat_gqa" if is_gqa else "attn_lat_mla",
    )(*q_args, k, v_t, kc, vc_t)


HALO = SUBLANES
N_PAIRS = C_HEADS // 2
HEADS_PER_GROUP = C_HEADS // C_GROUPS


def _ssd_kernel(n_chunks, has_h0, xbc_ref, prev_ref, next_ref, misc_ref, cw_ref, cb_ref, dtb_ref, alog_ref,
                dvec_ref, *rest):
    if has_h0:
        h0_ref, y_ref, hout_ref, st_ref = rest
    else:
        y_ref, hout_ref, st_ref = rest
    d = pl.program_id(1)
    c = pl.program_id(2)
    cidx = c + d * (n_chunks - 1 - 2 * c)

    @pl.when(c == 0)
    def _():
        if has_h0:
            st_ref[...] = h0_ref[...]
        else:
            st_ref[...] = jnp.zeros_like(st_ref)

    keep_prev = (cidx > 0).astype(F32)
    keep_next = (cidx < n_chunks - 1).astype(F32)
    ue = jnp.concatenate([prev_ref[...] * keep_prev, xbc_ref[...], next_ref[...] * keep_next], axis=0)
    rows = ue.shape[0]
    acc = jnp.broadcast_to(cb_ref[...], (C_CHUNK, C_CONV_CH))
    for k in range(C_CONV):
        sh = (C_CONV // 2 - k) % rows
        r = ue if sh == 0 else pltpu.roll(ue, sh, axis=0)
        acc = acc + r[HALO:HALO + C_CHUNK, :] * cw_ref[k:k + 1, :]
    xc = acc * jax.nn.sigmoid(acc)
    xs = xc[:, 0:C_INNER]
    bm = xc[:, C_INNER:C_INNER + LANES]
    cm = xc[:, C_INNER + LANES:C_INNER + 2 * LANES]
    bmt = bm.T.astype(BF16)

    x_dt = misc_ref[...] + dtb_ref[...]
    dtv = jnp.maximum(x_dt, 0.0) + jnp.log1p(jnp.exp(-jnp.abs(x_dt)))
    dta = dtv * (-jnp.exp(alog_ref[...]))
    row = lax.broadcasted_iota(jnp.int32, (C_CHUNK, LANES), 0)
    lane = _lane_iota((C_CHUNK, LANES))
    cum = dta
    step = 1
    while step < C_CHUNK:
        cum = cum + jnp.where(row >= step, pltpu.roll(cum, step, axis=0), 0.0)
        step *= 2
    tot = cum[C_CHUNK - 1:C_CHUNK, :]
    lo = lane < C_HEAD_DIM
    lane_group = (lane >= C_STATE).astype(jnp.int32)
    row_group = (row >= C_STATE).astype(jnp.int32)

    def run(direction):
        cx = cum if direction == 0 else tot - cum + dta
        cxt = cx.T
        dtt = dtv.T
        ecx = jnp.exp(cx)
        wst = jnp.exp(tot - cx) * dtv
        cdec = jnp.exp(tot)
        causal = (row >= lane) if direction == 0 else (row <= lane)
        g_mats = [jnp.dot(jnp.where(lane_group == g, cm, 0.0).astype(BF16), bmt, preferred_element_type=F32)
                  for g in range(C_GROUPS)]
        cm16 = cm.astype(BF16)

        def col(v, hd):
            j = DT_LANE + C_HEADS * direction + hd
            return v[:, j:j + 1]

        def rowv(vt, hd):
            j = DT_LANE + C_HEADS * direction + hd
            return vt[j:j + 1, :]

        for pr in range(N_PAIRS):
            heads = (2 * pr, 2 * pr + 1)
            groups = tuple(hd // HEADS_PER_GROUP for hd in heads)
            xs_pair = xs[:, pr * LANES:(pr + 1) * LANES]
            yd = []
            for hd in heads:
                seg = jnp.where(causal, col(cx, hd) - rowv(cxt, hd), -jnp.inf)
                sc = g_mats[hd // HEADS_PER_GROUP] * jnp.exp(seg) * rowv(dtt, hd)
                yd.append(jnp.dot(sc.astype(BF16), xs_pair.astype(BF16), preferred_element_type=F32))
            st = st_ref[pr]
            y_off = jnp.dot(cm16, st.astype(BF16), preferred_element_type=F32)
            y = _merge_halves(yd[0], yd[1]) + y_off * jnp.where(lo, col(ecx, heads[0]), col(ecx, heads[1]))
            if direction == 0:
                y = y + xs_pair * dvec_ref[:, pr * LANES:(pr + 1) * LANES]
            y_ref[:, pr * LANES:(pr + 1) * LANES] = y
            xw = xs_pair * jnp.where(lo, col(wst, heads[0]), col(wst, heads[1]))
            new = jnp.dot(bmt, xw.astype(BF16), preferred_element_type=F32)
            own = row_group == jnp.where(lo, groups[0], groups[1])
            dec = jnp.where(lo[0:1, :], col(cdec, heads[0]), col(cdec, heads[1]))
            st_new = st * dec + jnp.where(own, new, 0.0)
            st_ref[pr] = st_new

            @pl.when(c == n_chunks - 1)
            def _():
                st_t = st_new.T
                for hh, hd in enumerate(heads):
                    g = groups[hh]
                    hout_ref[hd] = st_t[hh * C_HEAD_DIM:(hh + 1) * C_HEAD_DIM, g * C_STATE:(g + 1) * C_STATE]

    @pl.when(d == 0)
    def _():
        run(0)

    @pl.when(d == 1)
    def _():
        run(1)


def _ssd(xbc, misc, lw, h0, seq_len):
    t = xbc.shape[0]
    nb = t // seq_len
    nc = seq_len // C_CHUNK
    per_chunk = C_CHUNK // HALO
    last_halo = t // HALO - 1

    def chunk(b, d, c):
        return b * nc + c + d * (nc - 1 - 2 * c)

    has_h0 = h0 is not None
    const = lambda shape: pl.BlockSpec(shape, lambda b, d, c: (0,) * len(shape))
    in_specs = [
        pl.BlockSpec((C_CHUNK, C_CONV_CH), lambda b, d, c: (chunk(b, d, c), 0)),
        pl.BlockSpec((HALO, C_CONV_CH), lambda b, d, c: (jnp.maximum(chunk(b, d, c) * per_chunk - 1, 0), 0)),
        pl.BlockSpec((HALO, C_CONV_CH),
                     lambda b, d, c: (jnp.minimum((chunk(b, d, c) + 1) * per_chunk, last_halo), 0)),
        pl.BlockSpec((C_CHUNK, LANES), lambda b, d, c: (chunk(b, d, c), 0)),
        const((C_CONV, C_CONV_CH)), const((1, C_CONV_CH)), const((1, LANES)), const((1, LANES)),
        const((1, C_INNER)),
    ]
    args = [xbc, xbc, xbc, misc, lw["conv_w"], lw["conv_b"], lw["dt_bias"], lw["a_log"], lw["d_vec"]]
    if has_h0:
        in_specs.append(pl.BlockSpec((None, None, N_PAIRS, LANES, LANES), lambda b, d, c: (b, d, 0, 0, 0)))
        args.append(h0)
    return pl.pallas_call(
        functools.partial(_ssd_kernel, nc, has_h0),
        out_shape=(jax.ShapeDtypeStruct((2, t, C_INNER), F32),
                   jax.ShapeDtypeStruct((nb, 2, C_HEADS, C_HEAD_DIM, C_STATE), F32)),
        grid=(nb, 2, nc),
        in_specs=in_specs,
        out_specs=(pl.BlockSpec((None, C_CHUNK, C_INNER), lambda b, d, c: (d, chunk(b, d, c), 0)),
                   pl.BlockSpec((None, None, C_HEADS, C_HEAD_DIM, C_STATE), lambda b, d, c: (b, d, 0, 0, 0))),
        scratch_shapes=[pltpu.VMEM((N_PAIRS, LANES, LANES), F32)],
        compiler_params=_cparams(("parallel", "parallel", "arbitrary")),
        name="ssd_lat" if has_h0 else "ssd_ctx",
    )(*args)


def _post_kernel(x_ref, mod_ref, oa_ref, ob_ref, y_ref, z_ref, gssm_ref, wout_ref, gmix_ref, gffn_ref,
                 gffo_ref, w1_ref, w2_ref, o_ref):
    x = x_ref[...]
    gate1 = mod_ref[:, 2 * D_MODEL:3 * D_MODEL]
    shift2 = mod_ref[:, 3 * D_MODEL:4 * D_MODEL]
    scale2 = mod_ref[:, 4 * D_MODEL:5 * D_MODEL]
    gate2 = mod_ref[:, 5 * D_MODEL:6 * D_MODEL]
    z = z_ref[...]
    oc = _rms((y_ref[0] + y_ref[1]) * (z * jax.nn.sigmoid(z)), gssm_ref[...])
    mix = jnp.concatenate([oa_ref[...], ob_ref[...], oc.astype(BF16)], axis=1)
    out = jnp.dot(mix, wout_ref[...], preferred_element_type=F32)
    x = x + gate1 * _rms(out, gmix_ref[...])
    h = _rms(x, gffn_ref[...]) * (1.0 + scale2) + shift2
    u = jnp.maximum(_bdot(h, w1_ref[...]), 0.0)
    f = _bdot(u * u, w2_ref[...])
    o_ref[...] = x + gate2 * _rms(f, gffo_ref[...])


def _post(is_ctx, x2d, mod_l, oa, ob, y, z, lw):
    t = x2d.shape[0]
    per_seq = DEC_SEQ // TM
    mod_map = (lambda i: (0, 0, 0)) if is_ctx else (lambda i: (1 + i // per_seq, 0, 0))
    tok = lambda w: pl.BlockSpec((TM, w), lambda i: (i, 0))
    const = lambda shape: pl.BlockSpec(shape, lambda i: (0,) * len(shape), pipeline_mode=pl.Buffered(1))
    return pl.pallas_call(
        _post_kernel,
        out_shape=jax.ShapeDtypeStruct((t, D_MODEL), F32),
        grid=(t // TM,),
        in_specs=[
            tok(D_MODEL),
            pl.BlockSpec((None, 1, 6 * D_MODEL), mod_map),
            tok(A_Q), tok(B_HEADS * B_V),
            pl.BlockSpec((2, TM, C_INNER), lambda i: (0, i, 0)),
            tok(C_INNER),
            const((1, C_INNER)), const((D_MODEL, D_MODEL)), const((1, D_MODEL)), const((1, D_MODEL)),
            const((1, D_MODEL)), const((D_MODEL, D_FF)), const((D_FF, D_MODEL)),
        ],
        out_specs=tok(D_MODEL),
        compiler_params=_cparams(("parallel",)),
        name="post_ctx" if is_ctx else "post_lat",
    )(x2d, mod_l, oa, ob, y, z, lw["g_ssm"], lw["w_out"], lw["g_mix"], lw["g_ffn"], lw["g_ffo"],
      lw["w1"], lw["w2"])


def _rope_tables():
    pos = np.arange(DEC_SEQ)
    axis_pos = np.stack([(pos // GRID_W), (pos % GRID_W)], axis=0).astype(np.float32)

    def pattern(rot_dim):
        half = rot_dim // 2
        quarter = half // 2
        inv = (1.0 / (np.float32(ROPE_THETA) ** (np.arange(0, half, 2, dtype=np.float32) / np.float32(half))))
        inv = inv.astype(np.float32)
        dd = np.arange(rot_dim)
        ang = (axis_pos[dd // half].T * inv[dd % quarter][None, :]).astype(np.float32)
        sign = np.where((dd % half) < quarter, -1.0, 1.0).astype(np.float32)
        return np.cos(ang).astype(np.float32), (np.sin(ang) * sign).astype(np.float32)

    ca, sa = pattern(HEAD_DIM)
    ca = np.tile(ca, (1, LANES // HEAD_DIM))
    sa = np.tile(sa, (1, LANES // HEAD_DIM))
    cb32, sb32 = pattern(B_ROPE)
    tail = LANES - KPE_LANE - B_ROPE
    cb = np.concatenate([np.ones((DEC_SEQ, KPE_LANE), np.float32), cb32, np.ones((DEC_SEQ, tail), np.float32)], 1)
    sb = np.concatenate([np.zeros((DEC_SEQ, KPE_LANE), np.float32), sb32, np.zeros((DEC_SEQ, tail), np.float32)], 1)
    ident_c = np.ones((TM, LANES), np.float32)
    ident_s = np.zeros((TM, LANES), np.float32)
    return tuple(jnp.asarray(np.concatenate([tb, idt], axis=0))
                 for tb, idt in ((ca, ident_c), (sa, ident_s), (cb, ident_c), (sb, ident_s)))


def _slot_vec(g):
    return jnp.tile(g, LANES // HEAD_DIM)[None, :]


def _layer_weights(l, p):
    w_in = p["w_in"][l]
    o_kpe = A_Q + 2 * A_KV + B_Q_RANK + B_KV_RANK
    o_z = o_kpe + B_ROPE
    o_xbc = o_z + C_INNER
    o_dt = o_xbc + C_CONV_CH
    zeros = lambda n: jnp.zeros((D_MODEL, n), F32)
    w_in_p = jnp.concatenate(
        [w_in[:, :o_kpe], w_in[:, o_z:o_dt], zeros(KPE_LANE), w_in[:, o_kpe:o_z], w_in[:, o_dt:],
         zeros(LANES - DT_LANE - 2 * C_HEADS)], axis=1).astype(BF16)

    w_qb = p["mla_w_qb"][l].reshape(B_Q_RANK, B_HEADS, B_NOPE + B_ROPE)
    w_qb_p = jnp.pad(w_qb, ((0, 0), (0, 0), (0, LANES - B_NOPE - B_ROPE))).reshape(B_Q_RANK, QB_W)
    w_kvb = p["mla_w_kvb"][l].reshape(B_KV_RANK, B_HEADS, B_NOPE + B_V)
    w_k = jnp.pad(w_kvb[:, :, :B_NOPE], ((0, 0), (0, 0), (0, LANES - B_NOPE))).reshape(B_KV_RANK, QB_W)
    w_v = w_kvb[:, :, B_NOPE:].reshape(B_KV_RANK, B_HEADS * B_V)

    w_out = p["w_out"][l]
    order = [r + g * A_REP for r in range(A_REP) for g in range(A_KV_HEADS)]
    w_out_a = jnp.concatenate([w_out[hd * HEAD_DIM:(hd + 1) * HEAD_DIM] for hd in order], axis=0)
    w_out_p = jnp.concatenate([w_out_a, w_out[A_Q:]], axis=0).astype(BF16)

    lane_vec = lambda v: jnp.pad(v.reshape(-1), (DT_LANE, LANES - DT_LANE - 2 * C_HEADS))[None, :]
    return {
        "g_pre": p["norm_mix_pre"][l][None, :],
        "w_in": w_in_p,
        "g_q": _slot_vec(p["attn_q_norm"][l]),
        "g_k": _slot_vec(p["attn_k_norm"][l]),
        "g_qc": p["mla_q_norm"][l][None, :],
        "w_qb": w_qb_p.astype(BF16),
        "g_kv": p["mla_kv_norm"][l][None, :],
        "w_k": w_k.astype(BF16),
        "w_v": w_v.astype(BF16),
        "conv_w": p["ssm_conv_w"][l].T,
        "conv_b": p["ssm_conv_b"][l][None, :],
        "dt_bias": lane_vec(p["ssm_dt_bias"][l]),
        "a_log": lane_vec(p["ssm_a_log"][l]),
        "d_vec": jnp.repeat(p["ssm_d"][l], C_HEAD_DIM)[None, :],
        "g_ssm": p["ssm_norm"][l][None, :],
        "w_out": w_out_p,
        "g_mix": p["norm_mix_post"][l][None, :],
        "g_ffn": p["norm_ffn_pre"][l][None, :],
        "g_ffo": p["norm_ffn_post"][l][None, :],
        "w1": p["w_ffn1"][l].astype(BF16),
        "w2": p["w_ffn2"][l].astype(BF16),
    }


def _state_to_pairs(h):
    ht = jnp.swapaxes(h, -1, -2)
    zero = jnp.zeros_like(ht[..., 0, :, :])
    pairs = []
    for pr in range(N_PAIRS):
        heads = (2 * pr, 2 * pr + 1)
        row_blocks = [jnp.concatenate([ht[..., hd, :, :] if hd // HEADS_PER_GROUP == g else zero for hd in heads],
                                      axis=-1) for g in range(C_GROUPS)]
        pairs.append(jnp.concatenate(row_blocks, axis=-2))
    return jnp.stack(pairs, axis=-3)


def kernel(x_prompt, x_sample, cache_attn_k, cache_attn_v, cache_mla_ckv, cache_mla_kpe, state_ssm, c, c_ctx, norm_mix_pre, norm_mix_post, norm_ffn_pre, norm_ffn_post, w_mod, b_mod, w_in, attn_q_norm, attn_k_norm, mla_q_norm, mla_w_qb, mla_kv_norm, mla_w_kvb, ssm_conv_w, ssm_conv_b, ssm_dt_bias, ssm_a_log, ssm_d, ssm_norm, w_out, w_ffn1, w_ffn2):
    p = dict(norm_mix_pre=norm_mix_pre, norm_mix_post=norm_mix_post, norm_ffn_pre=norm_ffn_pre,
             norm_ffn_post=norm_ffn_post, w_in=w_in, attn_q_norm=attn_q_norm, attn_k_norm=attn_k_norm,
             mla_q_norm=mla_q_norm, mla_w_qb=mla_w_qb, mla_kv_norm=mla_kv_norm, mla_w_kvb=mla_w_kvb,
             ssm_conv_w=ssm_conv_w, ssm_conv_b=ssm_conv_b, ssm_dt_bias=ssm_dt_bias, ssm_a_log=ssm_a_log,
             ssm_d=ssm_d, ssm_norm=ssm_norm, w_out=w_out, w_ffn1=w_ffn1, w_ffn2=w_ffn2)
    lws = [_layer_weights(l, p) for l in range(DEPTH)]
    tabs = _rope_tables()

    cvec = jnp.concatenate([c_ctx[None, :], c, jnp.zeros((MOD_ROWS - 1 - DEC_BATCH, D_MODEL), F32)], axis=0)
    mod = _modulation(cvec, w_mod, b_mod)

    kpe_p = jnp.pad(cache_mla_kpe, ((0, 0),) * 3 + ((KPE_LANE, LANES - KPE_LANE - B_ROPE),))
    kc, vct, kbc, vbct = _cache_prep(
        cache_attn_k.reshape(DEC_BATCH, DEPTH, PAST_LEN, A_KV), cache_attn_v.reshape(DEC_BATCH, DEPTH, PAST_LEN, A_KV),
        cache_mla_ckv, kpe_p, jnp.stack([lw["w_k"] for lw in lws]), jnp.stack([lw["w_v"] for lw in lws]))
    h0_lat = _state_to_pairs(state_ssm)

    xp = x_prompt.reshape(BATCH * SEQ, D_MODEL)
    xs = x_sample.reshape(DEC_BATCH * DEC_SEQ, D_MODEL)
    new_k, new_v, new_ckv, new_kpe, new_ssm = [], [], [], [], []
    for l in range(DEPTH):
        lw = lws[l]
        mod_l = mod[l][:, None, :]
        qat, ka, vat, qbt, kb, vbt, z, xbc, misc, kf, vf, ckvf = _pre(True, xp, mod_l, lw, tabs)
        oa, ob = _attn_ctx(qat, ka, vat, qbt, kb, vbt)
        y, hfin = _ssd(xbc, misc, lw, None, SEQ)
        xp = _post(True, xp, mod_l, oa, ob, y, z, lw)
        new_k.append(kf.reshape(BATCH, SEQ, A_KV_HEADS, HEAD_DIM))
        new_v.append(vf.reshape(BATCH, SEQ, A_KV_HEADS, HEAD_DIM))
        new_ckv.append(ckvf.reshape(BATCH, SEQ, B_KV_RANK))
        new_kpe.append(misc[:, KPE_LANE:KPE_LANE + B_ROPE].reshape(BATCH, SEQ, B_ROPE))
        new_ssm.append(hfin)
        qat, ka, vat, qbt, kb, vbt, z, xbc, misc = _pre(False, xs, mod_l, lw, tabs)
        oa = _attn_lat(True, qat, ka, vat, kc[l], vct[l])
        ob = _attn_lat(False, qbt, kb, vbt, kbc[l], vbct[l])
        y, _ = _ssd(xbc, misc, lw, h0_lat[:, l], DEC_SEQ)
        xs = _post(False, xs, mod_l, oa, ob, y, z, lw)
    return (xp.reshape(BATCH, SEQ, D_MODEL), xs.reshape(DEC_BATCH, DEC_SEQ, D_MODEL),
            jnp.stack(new_k, axis=1), jnp.stack(new_v, axis=1), jnp.stack(new_ckv, axis=1),
            jnp.stack(new_kpe, axis=1), jnp.stack(new_ssm, axis=1))
```

```python
import functools
import math

import numpy as np
import jax
import jax.numpy as jnp
from jax import lax
from jax.experimental import pallas as pl
from jax.experimental.pallas import tpu as pltpu

F32 = jnp.float32
BF16 = jnp.bfloat16

D_MODEL = 1024
BATCH = 16
SEQ = 256
DEPTH = 4
DEC_BATCH = 2
DEC_SEQ = 4096
PAST_LEN = 256
GRID_W = 64
HEAD_DIM = 64
A_HEADS = 6
A_KV_HEADS = 2
A_REP = A_HEADS // A_KV_HEADS
B_HEADS = 4
B_Q_RANK = 256
B_KV_RANK = 128
B_NOPE = 64
B_ROPE = 32
B_V = 64
C_HEADS = 6
C_HEAD_DIM = 64
C_INNER = C_HEADS * C_HEAD_DIM
C_GROUPS = 2
C_STATE = 64
C_CONV = 5
C_CHUNK = 128
C_CONV_CH = C_INNER + 2 * C_GROUPS * C_STATE
D_FF = 4 * D_MODEL
A_Q = A_HEADS * HEAD_DIM
A_KV = A_KV_HEADS * HEAD_DIM
ROPE_THETA = 10000.0
EPS = 1e-6
LOG2E = math.log2(math.e)

LANES = 128
SUBLANES = 8
VMEM_LIMIT = 56 * 1024 * 1024

OFF_QA = 0
OFF_KA = OFF_QA + A_Q
OFF_VA = OFF_KA + A_KV
OFF_QC = OFF_VA + A_KV
OFF_KVC = OFF_QC + B_Q_RANK
OFF_Z = OFF_KVC + B_KV_RANK
OFF_XBC = OFF_Z + C_INNER
OFF_MISC = OFF_XBC + C_CONV_CH
IN_COLS_P = OFF_MISC + LANES
KPE_LANE = 64
DT_LANE = KPE_LANE + B_ROPE
QB_W = B_HEADS * LANES

TM = 512
TQ = 512
TK = 1024


def _cparams(sem):
    return pltpu.CompilerParams(dimension_semantics=sem, vmem_limit_bytes=VMEM_LIMIT)


def _rms(x, g):
    return x * lax.rsqrt(jnp.mean(x * x, axis=-1, keepdims=True) + EPS) * g


def _bdot(a, b):
    return jnp.dot(a.astype(BF16), b.astype(BF16), preferred_element_type=F32)


def _lane_iota(shape):
    return lax.broadcasted_iota(jnp.int32, shape, len(shape) - 1)


def _swap_lanes(x, dist):
    lane = _lane_iota(x.shape)
    fwd = pltpu.roll(x, LANES - dist, axis=1)
    bwd = pltpu.roll(x, dist, axis=1)
    return jnp.where((lane % (2 * dist)) < dist, fwd, bwd)


def _rope(x, cos, sin, dist):
    return x * cos + _swap_lanes(x, dist) * sin


def _head_pair_rms(col, gain):
    lo = _lane_iota(col.shape) < HEAD_DIM
    c2 = col * col
    s_lo = jnp.sum(jnp.where(lo, c2, 0.0), axis=-1, keepdims=True)
    s_hi = jnp.sum(jnp.where(lo, 0.0, c2), axis=-1, keepdims=True)
    return col * lax.rsqrt(jnp.where(lo, s_lo, s_hi) * (1.0 / HEAD_DIM) + EPS) * gain


MOD_ROWS = SUBLANES
MOD_TN = 1536


def _mod_kernel(c_ref, w_ref, b_ref, o_ref):
    c = c_ref[...]
    o_ref[...] = _bdot(c * jax.nn.sigmoid(c), w_ref[...]) + b_ref[...]


def _modulation(cvec, w_mod, b_mod):
    return pl.pallas_call(
        _mod_kernel,
        out_shape=jax.ShapeDtypeStruct((DEPTH, MOD_ROWS, 6 * D_MODEL), F32),
        grid=(DEPTH, 6 * D_MODEL // MOD_TN),
        in_specs=[
            pl.BlockSpec((MOD_ROWS, D_MODEL), lambda l, j: (0, 0)),
            pl.BlockSpec((None, D_MODEL, MOD_TN), lambda l, j: (l, 0, j)),
            pl.BlockSpec((None, 1, MOD_TN), lambda l, j: (l, 0, j)),
        ],
        out_specs=pl.BlockSpec((None, MOD_ROWS, MOD_TN), lambda l, j: (l, 0, j)),
        compiler_params=_cparams(("parallel", "parallel")),
        name="modulation",
    )(cvec, w_mod, b_mod.reshape(DEPTH, 1, 6 * D_MODEL))


def _cache_kernel(k_ref, v_ref, ckv_ref, kpe_ref, wk_ref, wv_ref, ko_ref, vt_ref, kb_ref, vbt_ref):
    ko_ref[...] = k_ref[...].astype(BF16)
    vt_ref[...] = v_ref[...].T.astype(BF16)
    ckv = ckv_ref[...]
    kn = _bdot(ckv, wk_ref[...])
    kpe = kpe_ref[...]
    for h in range(B_HEADS):
        kb_ref[:, h * LANES:(h + 1) * LANES] = (kn[:, h * LANES:(h + 1) * LANES] + kpe).astype(BF16)
    vbt_ref[...] = _bdot(ckv, wv_ref[...]).T.astype(BF16)


def _cache_prep(cache_k, cache_v, cache_ckv, cache_kpe_p, wk_p, wv_p):
    spec_tok = lambda w: pl.BlockSpec((None, None, PAST_LEN, w), lambda l, b: (b, l, 0, 0))
    spec_t = lambda w: pl.BlockSpec((None, None, w, PAST_LEN), lambda l, b: (l, b, 0, 0))
    spec_o = lambda w: pl.BlockSpec((None, None, PAST_LEN, w), lambda l, b: (l, b, 0, 0))
    return pl.pallas_call(
        _cache_kernel,
        out_shape=(
            jax.ShapeDtypeStruct((DEPTH, DEC_BATCH, PAST_LEN, A_KV), BF16),
            jax.ShapeDtypeStruct((DEPTH, DEC_BATCH, A_KV, PAST_LEN), BF16),
            jax.ShapeDtypeStruct((DEPTH, DEC_BATCH, PAST_LEN, QB_W), BF16),
            jax.ShapeDtypeStruct((DEPTH, DEC_BATCH, B_HEADS * B_V, PAST_LEN), BF16),
        ),
        grid=(DEPTH, DEC_BATCH),
        in_specs=[
            spec_tok(A_KV), spec_tok(A_KV), spec_tok(B_KV_RANK), spec_tok(LANES),
            pl.BlockSpec((None, B_KV_RANK, QB_W), lambda l, b: (l, 0, 0)),
            pl.BlockSpec((None, B_KV_RANK, B_HEADS * B_V), lambda l, b: (l, 0, 0)),
        ],
        out_specs=(spec_o(A_KV), spec_t(A_KV), spec_o(QB_W), spec_t(B_HEADS * B_V)),
        compiler_params=_cparams(("parallel", "parallel")),
        name="cache_prep",
    )(cache_k, cache_v, cache_ckv, cache_kpe_p, wk_p, wv_p)


def _pre_kernel(is_ctx, x_ref, mod_ref, gpre_ref, win_ref, gq_ref, gk_ref, gqc_ref, wqb_ref, gkv_ref,
                wk_ref, wv_ref, ca_ref, sa_ref, cb_ref, sb_ref, *outs):
    if is_ctx:
        (qat_ref, ka_ref, vat_ref, qbt_ref, kb_ref, vbt_ref, z_ref, xbc_ref, misc_ref,
         kf_ref, vf_ref, ckvf_ref) = outs
    else:
        qat_ref, ka_ref, vat_ref, qbt_ref, kb_ref, vbt_ref, z_ref, xbc_ref, misc_ref = outs
    x = x_ref[...]
    shift1 = mod_ref[:, 0:D_MODEL]
    scale1 = mod_ref[:, D_MODEL:2 * D_MODEL]
    h = _rms(x, gpre_ref[...]) * (1.0 + scale1) + shift1
    proj = _bdot(h, win_ref[...])

    ca, sa, cb, sb = ca_ref[...], sa_ref[...], cb_ref[...], sb_ref[...]
    lane = _lane_iota((x.shape[0], LANES))

    gq = gq_ref[...]
    for cidx in range(A_Q // LANES):
        col = proj[:, OFF_QA + cidx * LANES:OFF_QA + (cidx + 1) * LANES]
        qn = _rope(_head_pair_rms(col, gq), ca, sa, HEAD_DIM // 4) * (HEAD_DIM ** -0.5 * LOG2E)
        qat_ref[cidx * LANES:(cidx + 1) * LANES, :] = qn.T.astype(BF16)

    kn = _head_pair_rms(proj[:, OFF_KA:OFF_KA + A_KV], gk_ref[...])
    vcol = proj[:, OFF_VA:OFF_VA + A_KV]
    if is_ctx:
        kf_ref[...] = kn
        vf_ref[...] = vcol
    ka_ref[...] = _rope(kn, ca, sa, HEAD_DIM // 4).astype(BF16)
    vat_ref[...] = vcol.T.astype(BF16)

    qc = _rms(proj[:, OFF_QC:OFF_QC + B_Q_RANK], gqc_ref[...])
    qb = _bdot(qc, wqb_ref[...])
    for hd in range(B_HEADS):
        col = qb[:, hd * LANES:(hd + 1) * LANES]
        col = _rope(col, cb, sb, B_ROPE // 4) * ((B_NOPE + B_ROPE) ** -0.5 * LOG2E)
        qbt_ref[hd * LANES:(hd + 1) * LANES, :] = col.T.astype(BF16)

    ckv = _rms(proj[:, OFF_KVC:OFF_KVC + B_KV_RANK], gkv_ref[...])
    if is_ctx:
        ckvf_ref[...] = ckv
    knope = _bdot(ckv, wk_ref[...])
    misc = proj[:, OFF_MISC:OFF_MISC + LANES]
    misc_ref[...] = misc
    kpe = jnp.where((lane >= KPE_LANE) & (lane < KPE_LANE + B_ROPE), _rope(misc, cb, sb, B_ROPE // 4), 0.0)
    for hd in range(B_HEADS):
        kb_ref[:, hd * LANES:(hd + 1) * LANES] = (knope[:, hd * LANES:(hd + 1) * LANES] + kpe).astype(BF16)
    vbt_ref[...] = _bdot(ckv, wv_ref[...]).T.astype(BF16)

    z_ref[...] = proj[:, OFF_Z:OFF_Z + C_INNER]
    xbc_ref[...] = proj[:, OFF_XBC:OFF_XBC + C_CONV_CH]


def _pre(is_ctx, x2d, mod_l, lw, tabs):
    t = x2d.shape[0]
    nt = t // TM
    per_seq = DEC_SEQ // TM
    if is_ctx:
        mod_map = lambda i: (0, 0, 0)
        tab_map = lambda i: (per_seq, 0)
    else:
        mod_map = lambda i: (1 + i // per_seq, 0, 0)
        tab_map = lambda i: (i % per_seq, 0)
    const = lambda shape: pl.BlockSpec(shape, lambda i: (0,) * len(shape))
    tab_spec = pl.BlockSpec((TM, LANES), tab_map)
    tok = lambda w: pl.BlockSpec((TM, w), lambda i: (i, 0))
    tok_t = lambda w: pl.BlockSpec((w, TM), lambda i: (0, i))
    out_shape = [
        jax.ShapeDtypeStruct((A_Q, t), BF16),
        jax.ShapeDtypeStruct((t, A_KV), BF16),
        jax.ShapeDtypeStruct((A_KV, t), BF16),
        jax.ShapeDtypeStruct((QB_W, t), BF16),
        jax.ShapeDtypeStruct((t, QB_W), BF16),
        jax.ShapeDtypeStruct((B_HEADS * B_V, t), BF16),
        jax.ShapeDtypeStruct((t, C_INNER), F32),
        jax.ShapeDtypeStruct((t, C_CONV_CH), F32),
        jax.ShapeDtypeStruct((t, LANES), F32),
    ]
    out_specs = [tok_t(A_Q), tok(A_KV), tok_t(A_KV), tok_t(QB_W), tok(QB_W), tok_t(B_HEADS * B_V),
                 tok(C_INNER), tok(C_CONV_CH), tok(LANES)]
    if is_ctx:
        out_shape += [jax.ShapeDtypeStruct((t, A_KV), F32)] * 2 + [jax.ShapeDtypeStruct((t, B_KV_RANK), F32)]
        out_specs += [tok(A_KV), tok(A_KV), tok(B_KV_RANK)]
    return pl.pallas_call(
        functools.partial(_pre_kernel, is_ctx),
        out_shape=tuple(out_shape),
        grid=(nt,),
        in_specs=[
            tok(D_MODEL),
            pl.BlockSpec((None, 1, 6 * D_MODEL), mod_map),
            const((1, D_MODEL)),
            const((D_MODEL, IN_COLS_P)),
            const((1, LANES)), const((1, LANES)), const((1, B_Q_RANK)),
            const((B_Q_RANK, QB_W)), const((1, B_KV_RANK)),
            const((B_KV_RANK, QB_W)), const((B_KV_RANK, B_HEADS * B_V)),
            tab_spec, tab_spec, tab_spec, tab_spec,
        ],
        out_specs=tuple(out_specs),
        compiler_params=_cparams(("parallel",)),
        name="pre_ctx" if is_ctx else "pre_lat",
    )(x2d, mod_l, lw["g_pre"], lw["w_in"], lw["g_q"], lw["g_k"], lw["g_qc"], lw["w_qb"], lw["g_kv"],
      lw["w_k"], lw["w_v"], *tabs)


def _merge_halves(lo_part, hi_part):
    return jnp.where(_lane_iota(lo_part.shape) < HEAD_DIM, lo_part, hi_part)


def _gqa_query_slot(qat_ref, hd):
    q = qat_ref[hd * HEAD_DIM:(hd + 1) * HEAD_DIM, :]
    zero = jnp.zeros_like(q)
    return jnp.concatenate([q, zero] if hd // A_REP == 0 else [zero, q], axis=0)


ONES_ROWS = 16
ACC_ROWS = LANES + ONES_ROWS


def _with_ones(v_t):
    return jnp.concatenate([v_t, jnp.ones((ONES_ROWS, v_t.shape[1]), BF16)], axis=0)


def _first_tile(k, q_t, v_t):
    return _first_scores(jnp.dot(k, q_t, preferred_element_type=F32), v_t)


def _first_scores(s, v_t):
    m = jnp.max(s, axis=0, keepdims=True)
    p = jnp.exp2(s - m)
    return m, jnp.dot(_with_ones(v_t), p.astype(BF16), preferred_element_type=F32)


def _next_tile(s, m, acc, v_t):
    m_new = jnp.maximum(m, jnp.max(s, axis=0, keepdims=True))
    p = jnp.exp2(s - m_new)
    acc = jnp.exp2(m - m_new) * acc + jnp.dot(_with_ones(v_t), p.astype(BF16), preferred_element_type=F32)
    return m_new, acc


def _normalised(acc):
    return (acc[0:LANES, :] / acc[LANES:LANES + 1, :]).T


def _write_heads(outs_a, outs_b, oa_ref, ob_ref):
    for r in range(A_REP):
        oa_ref[:, r * LANES:(r + 1) * LANES] = _merge_halves(outs_a[r], outs_a[A_REP + r]).astype(BF16)
    for pr in range(B_HEADS // 2):
        ob_ref[:, pr * LANES:(pr + 1) * LANES] = _merge_halves(outs_b[2 * pr], outs_b[2 * pr + 1]).astype(BF16)


def _attn_ctx_kernel(qat_ref, ka_ref, vat_ref, qbt_ref, kb_ref, vbt_ref, oa_ref, ob_ref):
    def split_heads(acc, n):
        o = acc[0:LANES, :] / acc[LANES:LANES + 1, :]
        return [o[:, i * SEQ:(i + 1) * SEQ].T for i in range(n)]

    q_all = jnp.concatenate([_gqa_query_slot(qat_ref, hd) for hd in range(A_HEADS)], axis=1)
    outs_a = split_heads(_first_tile(ka_ref[...], q_all, vat_ref[...])[1], A_HEADS)
    outs_b = []
    for pr in range(B_HEADS // 2):
        s = jnp.concatenate([jnp.dot(kb_ref[:, hd * LANES:(hd + 1) * LANES], qbt_ref[hd * LANES:(hd + 1) * LANES, :],
                                     preferred_element_type=F32) for hd in (2 * pr, 2 * pr + 1)], axis=1)
        outs_b += split_heads(_first_scores(s, vbt_ref[pr * LANES:(pr + 1) * LANES, :])[1], 2)
    _write_heads(outs_a, outs_b, oa_ref, ob_ref)


def _attn_ctx(qat, ka, vat, qbt, kb, vbt):
    t = ka.shape[0]
    tok = lambda w: pl.BlockSpec((SEQ, w), lambda b: (b, 0))
    tok_t = lambda w: pl.BlockSpec((w, SEQ), lambda b: (0, b))
    return pl.pallas_call(
        _attn_ctx_kernel,
        out_shape=(jax.ShapeDtypeStruct((t, A_Q), BF16), jax.ShapeDtypeStruct((t, B_HEADS * B_V), BF16)),
        grid=(t // SEQ,),
        in_specs=[tok_t(A_Q), tok(A_KV), tok_t(A_KV), tok_t(QB_W), tok(QB_W), tok_t(B_HEADS * B_V)],
        out_specs=(tok(A_Q), tok(B_HEADS * B_V)),
        compiler_params=_cparams(("parallel",)),
        name="attn_ctx",
    )(qat, ka, vat, qbt, kb, vbt)


def _attn_lat_kernel(is_gqa, *refs):
    if is_gqa:
        q0_ref, q1_ref, k_ref, vt_ref, kc_ref, vct_ref, o_ref, s_ref, acc_ref = refs
        zero = jnp.zeros((HEAD_DIM, TQ), BF16)
        q_slots = (jnp.concatenate([q0_ref[...], zero], axis=0), jnp.concatenate([zero, q1_ref[...]], axis=0))
        key_cols = (0, 0)
    else:
        q_ref, k_ref, vt_ref, kc_ref, vct_ref, o_ref, s_ref, acc_ref = refs
        q_slots = (q_ref[0:LANES, :], q_ref[LANES:2 * LANES, :])
        key_cols = (0, LANES)
    n_tiles = DEC_SEQ // TK
    vct = vct_ref[...]

    row_max = []
    for hh in range(2):
        c0 = key_cols[hh]
        m, acc = _first_tile(kc_ref[:, c0:c0 + LANES], q_slots[hh], vct)
        acc_ref[hh] = acc
        s_ref[hh, 0] = jnp.dot(k_ref[0:TK, c0:c0 + LANES], q_slots[hh], preferred_element_type=F32)
        row_max.append(m)
    for j in range(n_tiles):
        for hh in range(2):
            c0 = key_cols[hh]
            if j + 1 < n_tiles:
                s_ref[hh, (j + 1) % 2] = jnp.dot(k_ref[(j + 1) * TK:(j + 2) * TK, c0:c0 + LANES], q_slots[hh],
                                                 preferred_element_type=F32)
            row_max[hh], acc = _next_tile(s_ref[hh, j % 2], row_max[hh], acc_ref[hh], vt_ref[:, j * TK:(j + 1) * TK])
            acc_ref[hh] = acc
    o_ref[...] = _merge_halves(_normalised(acc_ref[0]), _normalised(acc_ref[1])).astype(BF16)


def _attn_lat(is_gqa, q_t, k, v_t, kc, vc_t):
    t = k.shape[0]
    nq = DEC_SEQ // TQ
    n_pairs = A_REP if is_gqa else B_HEADS // 2
    kw = A_KV if is_gqa else 2 * LANES
    if is_gqa:
        q_specs = [pl.BlockSpec((HEAD_DIM, TQ), lambda b, r, i: (r, b * nq + i)),
                   pl.BlockSpec((HEAD_DIM, TQ), lambda b, r, i: (A_REP + r, b * nq + i))]
        q_args = [q_t, q_t]
        pair_col = lambda r: 0
    else:
        q_specs = [pl.BlockSpec((2 * LANES, TQ), lambda b, r, i: (r, b * nq + i))]
        q_args = [q_t]
        pair_col = lambda r: r
    return pl.pallas_call(
        functools.partial(_attn_lat_kernel, is_gqa),
        out_shape=jax.ShapeDtypeStruct((t, n_pairs * LANES), BF16),
        grid=(DEC_BATCH, n_pairs, nq),
        in_specs=q_specs + [
            pl.BlockSpec((DEC_SEQ, kw), lambda b, r, i: (b, pair_col(r))),
            pl.BlockSpec((LANES, DEC_SEQ), lambda b, r, i: (pair_col(r), b)),
            pl.BlockSpec((None, PAST_LEN, kw), lambda b, r, i: (b, 0, pair_col(r))),
            pl.BlockSpec((None, LANES, PAST_LEN), lambda b, r, i: (b, pair_col(r), 0)),
        ],
        out_specs=pl.BlockSpec((TQ, LANES), lambda b, r, i: (b * nq + i, r)),
        scratch_shapes=[pltpu.VMEM((2, 2, TK, TQ), F32), pltpu.VMEM((2, ACC_ROWS, TQ), F32)],
        compiler_params=_cparams(("parallel", "parallel", "parallel")),
        name="attn_lat_gqa" if is_gqa else "attn_lat_mla",
    )(*q_args, k, v_t, kc, vc_t)


HALO = SUBLANES
N_PAIRS = C_HEADS // 2
HEADS_PER_GROUP = C_HEADS // C_GROUPS


def _ssd_kernel(n_chunks, has_h0, xbc_ref, misc_ref, cw_ref, cb_ref, dtb_ref, alog_ref, dvec_ref, *rest):
    if has_h0:
        h0_ref, y_ref, hout_ref, stf_ref, stb_ref, newb_ref, cm_ref, ecx_ref = rest
        stf_ref[...] = h0_ref[0]
        stb_ref[...] = h0_ref[1]
    else:
        y_ref, hout_ref, stf_ref, stb_ref, newb_ref, cm_ref, ecx_ref = rest
        stf_ref[...] = jnp.zeros_like(stf_ref)
        stb_ref[...] = jnp.zeros_like(stb_ref)
    seq_len = n_chunks * C_CHUNK
    row = lax.broadcasted_iota(jnp.int32, (C_CHUNK, LANES), 0)
    lane = _lane_iota((C_CHUNK, LANES))
    lo = lane < C_HEAD_DIM
    lane_group = (lane >= C_STATE).astype(jnp.int32)
    row_group = (row >= C_STATE).astype(jnp.int32)
    causal = (row >= lane, row <= lane)
    neg_a = -jnp.exp(alog_ref[...])

    def lane_of(direction, hd):
        return DT_LANE + C_HEADS * direction + hd

    def pair_cols(v, direction, heads):
        j0, j1 = lane_of(direction, heads[0]), lane_of(direction, heads[1])
        return jnp.where(lo[0:v.shape[0], :], v[:, j0:j0 + 1], v[:, j1:j1 + 1])

    def pass1(c, carry):
        start = pl.multiple_of(c * C_CHUNK, C_CHUNK)
        prev0 = pl.multiple_of(jnp.maximum(start - HALO, 0), HALO)
        next0 = pl.multiple_of(jnp.minimum(start + C_CHUNK, seq_len - HALO), HALO)
        keep_prev = (c > 0).astype(F32)
        keep_next = (c < n_chunks - 1).astype(F32)
        rows_c = pl.ds(start, C_CHUNK)

        ue = jnp.concatenate([xbc_ref[pl.ds(prev0, HALO), :] * keep_prev, xbc_ref[rows_c, :],
                              xbc_ref[pl.ds(next0, HALO), :] * keep_next], axis=0)
        acc = jnp.broadcast_to(cb_ref[...], (C_CHUNK, C_CONV_CH))
        for k in range(C_CONV):
            sh = (C_CONV // 2 - k) % ue.shape[0]
            r = ue if sh == 0 else pltpu.roll(ue, sh, axis=0)
            acc = acc + r[HALO:HALO + C_CHUNK, :] * cw_ref[k:k + 1, :]
        xc = acc * jax.nn.sigmoid(acc)
        xs = xc[:, 0:C_INNER]
        bm = xc[:, C_INNER:C_INNER + LANES]
        cm = xc[:, C_INNER + LANES:C_INNER + 2 * LANES]
        bmt = bm.T.astype(BF16)
        cm16 = cm.astype(BF16)
        g_mats = [jnp.dot(jnp.where(lane_group == g, cm, 0.0).astype(BF16), bmt, preferred_element_type=F32)
                  for g in range(C_GROUPS)]

        x_dt = misc_ref[rows_c, :] + dtb_ref[...]
        dtv = jnp.maximum(x_dt, 0.0) + jnp.log(1.0 + jnp.exp(-jnp.abs(x_dt)))
        dta = dtv * neg_a
        cum = dta
        step = 1
        while step < C_CHUNK:
            cum = cum + jnp.where(row >= step, pltpu.roll(cum, step, axis=0), 0.0)
            step *= 2
        tot = cum[C_CHUNK - 1:C_CHUNK, :]
        cxs = (cum, tot - cum + dta)
        cxts = (cxs[0].T, cxs[1].T)
        dtt = dtv.T
        wsts = tuple(jnp.exp(tot - cx) * dtv for cx in cxs)
        ecx_f = jnp.exp(cxs[0])
        cdec = jnp.exp(tot)
        cm_ref[c] = cm16
        ecx_ref[c, 0:C_CHUNK, :] = jnp.exp(cxs[1])
        ecx_ref[c, C_CHUNK:C_CHUNK + SUBLANES, :] = jnp.broadcast_to(cdec, (SUBLANES, LANES))

        for pr in range(N_PAIRS):
            heads = (2 * pr, 2 * pr + 1)
            groups = tuple(hd // HEADS_PER_GROUP for hd in heads)
            xs_pair = xs[:, pr * LANES:(pr + 1) * LANES]
            xs16 = xs_pair.astype(BF16)
            own = row_group == jnp.where(lo, groups[0], groups[1])
            y = xs_pair * dvec_ref[:, pr * LANES:(pr + 1) * LANES]
            for direction in range(2):
                yd = []
                for hd in heads:
                    j = lane_of(direction, hd)
                    seg = jnp.where(causal[direction], cxs[direction][:, j:j + 1] - cxts[direction][j:j + 1, :],
                                    -jnp.inf)
                    sc = g_mats[hd // HEADS_PER_GROUP] * jnp.exp(seg) * dtt[j:j + 1, :]
                    yd.append(jnp.dot(sc.astype(BF16), xs16, preferred_element_type=F32))
                y = y + _merge_halves(yd[0], yd[1])
                xw = xs_pair * pair_cols(wsts[direction], direction, heads)
                new = jnp.where(own, jnp.dot(bmt, xw.astype(BF16), preferred_element_type=F32), 0.0)
                if direction == 0:
                    st = stf_ref[pr]
                    y_off = jnp.dot(cm16, st.astype(BF16), preferred_element_type=F32)
                    y = y + y_off * pair_cols(ecx_f, 0, heads)
                    stf_ref[pr] = st * pair_cols(cdec, 0, heads) + new
                else:
                    newb_ref[c, pr] = new
            y_ref[rows_c, pr * LANES:(pr + 1) * LANES] = y
        return carry

    def pass2(i, carry):
        c = n_chunks - 1 - i
        rows_c = pl.ds(pl.multiple_of(c * C_CHUNK, C_CHUNK), C_CHUNK)
        cm16 = cm_ref[c]
        ecx_b = ecx_ref[c, 0:C_CHUNK, :]
        cdec = ecx_ref[c, C_CHUNK:C_CHUNK + 1, :]
        for pr in range(N_PAIRS):
            heads = (2 * pr, 2 * pr + 1)
            st = stb_ref[pr]
            y_off = jnp.dot(cm16, st.astype(BF16), preferred_element_type=F32)
            cols = pl.ds(pr * LANES, LANES)
            y_ref[rows_c, cols] = y_ref[rows_c, cols] + y_off * pair_cols(ecx_b, 1, heads)
            stb_ref[pr] = st * pair_cols(cdec, 1, heads) + newb_ref[c, pr]
        return carry

    unroll = n_chunks <= 2
    lax.fori_loop(0, n_chunks, pass1, 0, unroll=unroll)
    lax.fori_loop(0, n_chunks, pass2, 0, unroll=unroll)

    for direction, st_ref in enumerate((stf_ref, stb_ref)):
        for pr in range(N_PAIRS):
            st_t = st_ref[pr].T
            for hh, hd in enumerate((2 * pr, 2 * pr + 1)):
                g = hd // HEADS_PER_GROUP
                hout_ref[direction, hd] = st_t[hh * C_HEAD_DIM:(hh + 1) * C_HEAD_DIM,
                                               g * C_STATE:(g + 1) * C_STATE]


def _ssd(xbc, misc, lw, h0, seq_len):
    t = xbc.shape[0]
    nb = t // seq_len
    nc = seq_len // C_CHUNK
    has_h0 = h0 is not None
    const = lambda shape: pl.BlockSpec(shape, lambda b: (0,) * len(shape))
    in_specs = [
        pl.BlockSpec((seq_len, C_CONV_CH), lambda b: (b, 0)),
        pl.BlockSpec((seq_len, LANES), lambda b: (b, 0)),
        const((C_CONV, C_CONV_CH)), const((1, C_CONV_CH)), const((1, LANES)), const((1, LANES)),
        const((1, C_INNER)),
    ]
    args = [xbc, misc, lw["conv_w"], lw["conv_b"], lw["dt_bias"], lw["a_log"], lw["d_vec"]]
    if has_h0:
        in_specs.append(pl.BlockSpec((None, 2, N_PAIRS, LANES, LANES), lambda b: (b, 0, 0, 0, 0)))
        args.append(h0)
    return pl.pallas_call(
        functools.partial(_ssd_kernel, nc, has_h0),
        out_shape=(jax.ShapeDtypeStruct((t, C_INNER), F32),
                   jax.ShapeDtypeStruct((nb, 2, C_HEADS, C_HEAD_DIM, C_STATE), F32)),
        grid=(nb,),
        in_specs=in_specs,
        out_specs=(pl.BlockSpec((seq_len, C_INNER), lambda b: (b, 0)),
                   pl.BlockSpec((None, 2, C_HEADS, C_HEAD_DIM, C_STATE), lambda b: (b, 0, 0, 0, 0))),
        scratch_shapes=[pltpu.VMEM((N_PAIRS, LANES, LANES), F32), pltpu.VMEM((N_PAIRS, LANES, LANES), F32),
                        pltpu.VMEM((nc, N_PAIRS, LANES, LANES), F32), pltpu.VMEM((nc, C_CHUNK, LANES), BF16),
                        pltpu.VMEM((nc, C_CHUNK + SUBLANES, LANES), F32)],
        compiler_params=_cparams(("parallel",)),
        name="ssd_lat" if has_h0 else "ssd_ctx",
    )(*args)


def _post_kernel(x_ref, mod_ref, oa_ref, ob_ref, y_ref, z_ref, gssm_ref, wout_ref, gmix_ref, gffn_ref,
                 gffo_ref, w1_ref, w2_ref, o_ref):
    x = x_ref[...]
    gate1 = mod_ref[:, 2 * D_MODEL:3 * D_MODEL]
    shift2 = mod_ref[:, 3 * D_MODEL:4 * D_MODEL]
    scale2 = mod_ref[:, 4 * D_MODEL:5 * D_MODEL]
    gate2 = mod_ref[:, 5 * D_MODEL:6 * D_MODEL]
    z = z_ref[...]
    oc = _rms(y_ref[...] * (z * jax.nn.sigmoid(z)), gssm_ref[...])
    mix = jnp.concatenate([oa_ref[...], ob_ref[...], oc.astype(BF16)], axis=1)
    out = jnp.dot(mix, wout_ref[...], preferred_element_type=F32)
    x = x + gate1 * _rms(out, gmix_ref[...])
    h = _rms(x, gffn_ref[...]) * (1.0 + scale2) + shift2
    u = jnp.maximum(_bdot(h, w1_ref[...]), 0.0)
    f = _bdot(u * u, w2_ref[...])
    o_ref[...] = x + gate2 * _rms(f, gffo_ref[...])


def _post(is_ctx, x2d, mod_l, oa, ob, y, z, lw):
    t = x2d.shape[0]
    per_seq = DEC_SEQ // TM
    mod_map = (lambda i: (0, 0, 0)) if is_ctx else (lambda i: (1 + i // per_seq, 0, 0))
    tok = lambda w: pl.BlockSpec((TM, w), lambda i: (i, 0))
    const = lambda shape: pl.BlockSpec(shape, lambda i: (0,) * len(shape), pipeline_mode=pl.Buffered(1))
    return pl.pallas_call(
        _post_kernel,
        out_shape=jax.ShapeDtypeStruct((t, D_MODEL), F32),
        grid=(t // TM,),
        in_specs=[
            tok(D_MODEL),
            pl.BlockSpec((None, 1, 6 * D_MODEL), mod_map),
            tok(A_Q), tok(B_HEADS * B_V), tok(C_INNER), tok(C_INNER),
            const((1, C_INNER)), const((D_MODEL, D_MODEL)), const((1, D_MODEL)), const((1, D_MODEL)),
            const((1, D_MODEL)), const((D_MODEL, D_FF)), const((D_FF, D_MODEL)),
        ],
        out_specs=tok(D_MODEL),
        compiler_params=_cparams(("parallel",)),
        name="post_ctx" if is_ctx else "post_lat",
    )(x2d, mod_l, oa, ob, y, z, lw["g_ssm"], lw["w_out"], lw["g_mix"], lw["g_ffn"], lw["g_ffo"],
      lw["w1"], lw["w2"])


def _rope_tables():
    pos = np.arange(DEC_SEQ)
    axis_pos = np.stack([(pos // GRID_W), (pos % GRID_W)], axis=0).astype(np.float32)

    def pattern(rot_dim):
        half = rot_dim // 2
        quarter = half // 2
        inv = (1.0 / (np.float32(ROPE_THETA) ** (np.arange(0, half, 2, dtype=np.float32) / np.float32(half))))
        inv = inv.astype(np.float32)
        dd = np.arange(rot_dim)
        ang = (axis_pos[dd // half].T * inv[dd % quarter][None, :]).astype(np.float32)
        sign = np.where((dd % half) < quarter, -1.0, 1.0).astype(np.float32)
        return np.cos(ang).astype(np.float32), (np.sin(ang) * sign).astype(np.float32)

    ca, sa = pattern(HEAD_DIM)
    ca = np.tile(ca, (1, LANES // HEAD_DIM))
    sa = np.tile(sa, (1, LANES // HEAD_DIM))
    cb32, sb32 = pattern(B_ROPE)
    tail = LANES - KPE_LANE - B_ROPE
    cb = np.concatenate([np.ones((DEC_SEQ, KPE_LANE), np.float32), cb32, np.ones((DEC_SEQ, tail), np.float32)], 1)
    sb = np.concatenate([np.zeros((DEC_SEQ, KPE_LANE), np.float32), sb32, np.zeros((DEC_SEQ, tail), np.float32)], 1)
    ident_c = np.ones((TM, LANES), np.float32)
    ident_s = np.zeros((TM, LANES), np.float32)
    return tuple(jnp.asarray(np.concatenate([tb, idt], axis=0))
                 for tb, idt in ((ca, ident_c), (sa, ident_s), (cb, ident_c), (sb, ident_s)))


def _slot_vec(g):
    return jnp.tile(g, LANES // HEAD_DIM)[None, :]


def _layer_weights(l, p):
    w_in = p["w_in"][l]
    o_kpe = A_Q + 2 * A_KV + B_Q_RANK + B_KV_RANK
    o_z = o_kpe + B_ROPE
    o_xbc = o_z + C_INNER
    o_dt = o_xbc + C_CONV_CH
    zeros = lambda n: jnp.zeros((D_MODEL, n), F32)
    w_in_p = jnp.concatenate(
        [w_in[:, :o_kpe], w_in[:, o_z:o_dt], zeros(KPE_LANE), w_in[:, o_kpe:o_z], w_in[:, o_dt:],
         zeros(LANES - DT_LANE - 2 * C_HEADS)], axis=1).astype(BF16)

    w_qb = p["mla_w_qb"][l].reshape(B_Q_RANK, B_HEADS, B_NOPE + B_ROPE)
    w_qb_p = jnp.pad(w_qb, ((0, 0), (0, 0), (0, LANES - B_NOPE - B_ROPE))).reshape(B_Q_RANK, QB_W)
    w_kvb = p["mla_w_kvb"][l].reshape(B_KV_RANK, B_HEADS, B_NOPE + B_V)
    w_k = jnp.pad(w_kvb[:, :, :B_NOPE], ((0, 0), (0, 0), (0, LANES - B_NOPE))).reshape(B_KV_RANK, QB_W)
    w_v = w_kvb[:, :, B_NOPE:].reshape(B_KV_RANK, B_HEADS * B_V)

    w_out = p["w_out"][l]
    order = [r + g * A_REP for r in range(A_REP) for g in range(A_KV_HEADS)]
    w_out_a = jnp.concatenate([w_out[hd * HEAD_DIM:(hd + 1) * HEAD_DIM] for hd in order], axis=0)
    w_out_p = jnp.concatenate([w_out_a, w_out[A_Q:]], axis=0).astype(BF16)

    lane_vec = lambda v: jnp.pad(v.reshape(-1), (DT_LANE, LANES - DT_LANE - 2 * C_HEADS))[None, :]
    return {
        "g_pre": p["norm_mix_pre"][l][None, :],
        "w_in": w_in_p,
        "g_q": _slot_vec(p["attn_q_norm"][l]),
        "g_k": _slot_vec(p["attn_k_norm"][l]),
        "g_qc": p["mla_q_norm"][l][None, :],
        "w_qb": w_qb_p.astype(BF16),
        "g_kv": p["mla_kv_norm"][l][None, :],
        "w_k": w_k.astype(BF16),
        "w_v": w_v.astype(BF16),
        "conv_w": p["ssm_conv_w"][l].T,
        "conv_b": p["ssm_conv_b"][l][None, :],
        "dt_bias": lane_vec(p["ssm_dt_bias"][l]),
        "a_log": lane_vec(p["ssm_a_log"][l]),
        "d_vec": jnp.repeat(p["ssm_d"][l], C_HEAD_DIM)[None, :],
        "g_ssm": p["ssm_norm"][l][None, :],
        "w_out": w_out_p,
        "g_mix": p["norm_mix_post"][l][None, :],
        "g_ffn": p["norm_ffn_pre"][l][None, :],
        "g_ffo": p["norm_ffn_post"][l][None, :],
        "w1": p["w_ffn1"][l].astype(BF16),
        "w2": p["w_ffn2"][l].astype(BF16),
    }


def _state_to_pairs(h):
    ht = jnp.swapaxes(h, -1, -2)
    zero = jnp.zeros_like(ht[..., 0, :, :])
    pairs = []
    for pr in range(N_PAIRS):
        heads = (2 * pr, 2 * pr + 1)
        row_blocks = [jnp.concatenate([ht[..., hd, :, :] if hd // HEADS_PER_GROUP == g else zero for hd in heads],
                                      axis=-1) for g in range(C_GROUPS)]
        pairs.append(jnp.concatenate(row_blocks, axis=-2))
    return jnp.stack(pairs, axis=-3)


def kernel(x_prompt, x_sample, cache_attn_k, cache_attn_v, cache_mla_ckv, cache_mla_kpe, state_ssm, c, c_ctx, norm_mix_pre, norm_mix_post, norm_ffn_pre, norm_ffn_post, w_mod, b_mod, w_in, attn_q_norm, attn_k_norm, mla_q_norm, mla_w_qb, mla_kv_norm, mla_w_kvb, ssm_conv_w, ssm_conv_b, ssm_dt_bias, ssm_a_log, ssm_d, ssm_norm, w_out, w_ffn1, w_ffn2):
    p = dict(norm_mix_pre=norm_mix_pre, norm_mix_post=norm_mix_post, norm_ffn_pre=norm_ffn_pre,
             norm_ffn_post=norm_ffn_post, w_in=w_in, attn_q_norm=attn_q_norm, attn_k_norm=attn_k_norm,
             mla_q_norm=mla_q_norm, mla_w_qb=mla_w_qb, mla_kv_norm=mla_kv_norm, mla_w_kvb=mla_w_kvb,
             ssm_conv_w=ssm_conv_w, ssm_conv_b=ssm_conv_b, ssm_dt_bias=ssm_dt_bias, ssm_a_log=ssm_a_log,
             ssm_d=ssm_d, ssm_norm=ssm_norm, w_out=w_out, w_ffn1=w_ffn1, w_ffn2=w_ffn2)
    lws = [_layer_weights(l, p) for l in range(DEPTH)]
    tabs = _rope_tables()

    cvec = jnp.concatenate([c_ctx[None, :], c, jnp.zeros((MOD_ROWS - 1 - DEC_BATCH, D_MODEL), F32)], axis=0)
    mod = _modulation(cvec, w_mod, b_mod)

    kpe_p = jnp.pad(cache_mla_kpe, ((0, 0),) * 3 + ((KPE_LANE, LANES - KPE_LANE - B_ROPE),))
    kc, vct, kbc, vbct = _cache_prep(
        cache_attn_k.reshape(DEC_BATCH, DEPTH, PAST_LEN, A_KV), cache_attn_v.reshape(DEC_BATCH, DEPTH, PAST_LEN, A_KV),
        cache_mla_ckv, kpe_p, jnp.stack([lw["w_k"] for lw in lws]), jnp.stack([lw["w_v"] for lw in lws]))
    h0_lat = _state_to_pairs(state_ssm)

    xp = x_prompt.reshape(BATCH * SEQ, D_MODEL)
    xs = x_sample.reshape(DEC_BATCH * DEC_SEQ, D_MODEL)
    new_k, new_v, new_ckv, new_kpe, new_ssm = [], [], [], [], []
    for l in range(DEPTH):
        lw = lws[l]
        mod_l = mod[l][:, None, :]
        qat, ka, vat, qbt, kb, vbt, z, xbc, misc, kf, vf, ckvf = _pre(True, xp, mod_l, lw, tabs)
        oa, ob = _attn_ctx(qat, ka, vat, qbt, kb, vbt)
        y, hfin = _ssd(xbc, misc, lw, None, SEQ)
        xp = _post(True, xp, mod_l, oa, ob, y, z, lw)
        new_k.append(kf.reshape(BATCH, SEQ, A_KV_HEADS, HEAD_DIM))
        new_v.append(vf.reshape(BATCH, SEQ, A_KV_HEADS, HEAD_DIM))
        new_ckv.append(ckvf.reshape(BATCH, SEQ, B_KV_RANK))
        new_kpe.append(misc[:, KPE_LANE:KPE_LANE + B_ROPE].reshape(BATCH, SEQ, B_ROPE))
        new_ssm.append(hfin)
        qat, ka, vat, qbt, kb, vbt, z, xbc, misc = _pre(False, xs, mod_l, lw, tabs)
        oa = _attn_lat(True, qat, ka, vat, kc[l], vct[l])
        ob = _attn_lat(False, qbt, kb, vbt, kbc[l], vbct[l])
        y, _ = _ssd(xbc, misc, lw, h0_lat[:, l], DEC_SEQ)
        xs = _post(False, xs, mod_l, oa, ob, y, z, lw)
    return (xp.reshape(BATCH, SEQ, D_MODEL), xs.reshape(DEC_BATCH, DEC_SEQ, D_MODEL),
            jnp.stack(new_k, axis=1), jnp.stack(new_v, axis=1), jnp.stack(new_ckv, axis=1),
            jnp.stack(new_kpe, axis=1), jnp.stack(new_ssm, axis=1))
```

```python
import functools
import math

import numpy as np
import jax
import jax.numpy as jnp
from jax import lax
from jax.experimental import pallas as pl
from jax.experimental.pallas import tpu as pltpu

F32 = jnp.float32
BF16 = jnp.bfloat16

D_MODEL = 1024
BATCH = 16
SEQ = 256
DEPTH = 4
DEC_BATCH = 2
DEC_SEQ = 4096
PAST_LEN = 256
GRID_W = 64
HEAD_DIM = 64
A_HEADS = 6
A_KV_HEADS = 2
A_REP = A_HEADS // A_KV_HEADS
B_HEADS = 4
B_Q_RANK = 256
B_KV_RANK = 128
B_NOPE = 64
B_ROPE = 32
B_V = 64
C_HEADS = 6
C_HEAD_DIM = 64
C_INNER = C_HEADS * C_HEAD_DIM
C_GROUPS = 2
C_STATE = 64
C_CONV = 5
C_CHUNK = 128
C_CONV_CH = C_INNER + 2 * C_GROUPS * C_STATE
D_FF = 4 * D_MODEL
A_Q = A_HEADS * HEAD_DIM
A_KV = A_KV_HEADS * HEAD_DIM
ROPE_THETA = 10000.0
EPS = 1e-6
LOG2E = math.log2(math.e)

LANES = 128
SUBLANES = 8
VMEM_LIMIT = 56 * 1024 * 1024

OFF_QA = 0
OFF_KA = OFF_QA + A_Q
OFF_VA = OFF_KA + A_KV
OFF_QC = OFF_VA + A_KV
OFF_KVC = OFF_QC + B_Q_RANK
OFF_Z = OFF_KVC + B_KV_RANK
OFF_XBC = OFF_Z + C_INNER
OFF_MISC = OFF_XBC + C_CONV_CH
IN_COLS_P = OFF_MISC + LANES
KPE_LANE = 64
DT_LANE = KPE_LANE + B_ROPE
QB_W = B_HEADS * LANES

TM = 512
TQ = 512
TK = 1024


def _cparams(sem):
    return pltpu.CompilerParams(dimension_semantics=sem, vmem_limit_bytes=VMEM_LIMIT)


def _rms(x, g):
    return x * lax.rsqrt(jnp.mean(x * x, axis=-1, keepdims=True) + EPS) * g


def _bdot(a, b):
    return jnp.dot(a.astype(BF16), b.astype(BF16), preferred_element_type=F32)


def _lane_iota(shape):
    return lax.broadcasted_iota(jnp.int32, shape, len(shape) - 1)


def _swap_lanes(x, dist):
    lane = _lane_iota(x.shape)
    fwd = pltpu.roll(x, LANES - dist, axis=1)
    bwd = pltpu.roll(x, dist, axis=1)
    return jnp.where((lane % (2 * dist)) < dist, fwd, bwd)


def _rope(x, cos, sin, dist):
    return x * cos + _swap_lanes(x, dist) * sin


def _head_pair_rms(col, gain):
    lo = _lane_iota(col.shape) < HEAD_DIM
    c2 = col * col
    s_lo = jnp.sum(jnp.where(lo, c2, 0.0), axis=-1, keepdims=True)
    s_hi = jnp.sum(jnp.where(lo, 0.0, c2), axis=-1, keepdims=True)
    return col * lax.rsqrt(jnp.where(lo, s_lo, s_hi) * (1.0 / HEAD_DIM) + EPS) * gain


MOD_ROWS = SUBLANES
MOD_TN = 1536


def _mod_kernel(c_ref, w_ref, b_ref, o_ref):
    c = c_ref[...]
    o_ref[...] = _bdot(c * jax.nn.sigmoid(c), w_ref[...]) + b_ref[...]


def _modulation(cvec, w_mod, b_mod):
    return pl.pallas_call(
        _mod_kernel,
        out_shape=jax.ShapeDtypeStruct((DEPTH, MOD_ROWS, 6 * D_MODEL), F32),
        grid=(DEPTH, 6 * D_MODEL // MOD_TN),
        in_specs=[
            pl.BlockSpec((MOD_ROWS, D_MODEL), lambda l, j: (0, 0)),
            pl.BlockSpec((None, D_MODEL, MOD_TN), lambda l, j: (l, 0, j)),
            pl.BlockSpec((None, 1, MOD_TN), lambda l, j: (l, 0, j)),
        ],
        out_specs=pl.BlockSpec((None, MOD_ROWS, MOD_TN), lambda l, j: (l, 0, j)),
        compiler_params=_cparams(("parallel", "parallel")),
        name="modulation",
    )(cvec, w_mod, b_mod.reshape(DEPTH, 1, 6 * D_MODEL))


def _cache_kernel(k_ref, v_ref, ckv_ref, kpe_ref, wk_ref, wv_ref, ko_ref, vt_ref, kb_ref, vbt_ref):
    ko_ref[...] = k_ref[...].astype(BF16)
    vt_ref[...] = v_ref[...].T.astype(BF16)
    ckv = ckv_ref[...]
    kn = _bdot(ckv, wk_ref[...])
    kpe = kpe_ref[...]
    for h in range(B_HEADS):
        kb_ref[:, h * LANES:(h + 1) * LANES] = (kn[:, h * LANES:(h + 1) * LANES] + kpe).astype(BF16)
    vbt_ref[...] = _bdot(ckv, wv_ref[...]).T.astype(BF16)


def _cache_prep(cache_k, cache_v, cache_ckv, cache_kpe_p, wk_p, wv_p):
    spec_tok = lambda w: pl.BlockSpec((None, None, PAST_LEN, w), lambda l, b: (b, l, 0, 0))
    spec_t = lambda w: pl.BlockSpec((None, None, w, PAST_LEN), lambda l, b: (l, b, 0, 0))
    spec_o = lambda w: pl.BlockSpec((None, None, PAST_LEN, w), lambda l, b: (l, b, 0, 0))
    return pl.pallas_call(
        _cache_kernel,
        out_shape=(
            jax.ShapeDtypeStruct((DEPTH, DEC_BATCH, PAST_LEN, A_KV), BF16),
            jax.ShapeDtypeStruct((DEPTH, DEC_BATCH, A_KV, PAST_LEN), BF16),
            jax.ShapeDtypeStruct((DEPTH, DEC_BATCH, PAST_LEN, QB_W), BF16),
            jax.ShapeDtypeStruct((DEPTH, DEC_BATCH, B_HEADS * B_V, PAST_LEN), BF16),
        ),
        grid=(DEPTH, DEC_BATCH),
        in_specs=[
            spec_tok(A_KV), spec_tok(A_KV), spec_tok(B_KV_RANK), spec_tok(LANES),
            pl.BlockSpec((None, B_KV_RANK, QB_W), lambda l, b: (l, 0, 0)),
            pl.BlockSpec((None, B_KV_RANK, B_HEADS * B_V), lambda l, b: (l, 0, 0)),
        ],
        out_specs=(spec_o(A_KV), spec_t(A_KV), spec_o(QB_W), spec_t(B_HEADS * B_V)),
        compiler_params=_cparams(("parallel", "parallel")),
        name="cache_prep",
    )(cache_k, cache_v, cache_ckv, cache_kpe_p, wk_p, wv_p)


def _pre_kernel(is_ctx, x_ref, mod_ref, gpre_ref, win_ref, gq_ref, gk_ref, gqc_ref, wqb_ref, gkv_ref,
                wk_ref, wv_ref, ca_ref, sa_ref, cb_ref, sb_ref, *outs):
    if is_ctx:
        (qat_ref, ka_ref, vat_ref, qbt_ref, kb_ref, vbt_ref, z_ref, xbc_ref, misc_ref,
         kf_ref, vf_ref, ckvf_ref) = outs
    else:
        qat_ref, ka_ref, vat_ref, qbt_ref, kb_ref, vbt_ref, z_ref, xbc_ref, misc_ref = outs
    x = x_ref[...]
    shift1 = mod_ref[:, 0:D_MODEL]
    scale1 = mod_ref[:, D_MODEL:2 * D_MODEL]
    h = _rms(x, gpre_ref[...]) * (1.0 + scale1) + shift1
    proj = _bdot(h, win_ref[...])

    ca, sa, cb, sb = ca_ref[...], sa_ref[...], cb_ref[...], sb_ref[...]
    lane = _lane_iota((x.shape[0], LANES))

    gq = gq_ref[...]
    for cidx in range(A_Q // LANES):
        col = proj[:, OFF_QA + cidx * LANES:OFF_QA + (cidx + 1) * LANES]
        qn = _rope(_head_pair_rms(col, gq), ca, sa, HEAD_DIM // 4) * (HEAD_DIM ** -0.5 * LOG2E)
        qat_ref[cidx * LANES:(cidx + 1) * LANES, :] = qn.T.astype(BF16)

    kn = _head_pair_rms(proj[:, OFF_KA:OFF_KA + A_KV], gk_ref[...])
    vcol = proj[:, OFF_VA:OFF_VA + A_KV]
    if is_ctx:
        kf_ref[...] = kn
        vf_ref[...] = vcol
    ka_ref[...] = _rope(kn, ca, sa, HEAD_DIM // 4).astype(BF16)
    vat_ref[...] = vcol.T.astype(BF16)

    qc = _rms(proj[:, OFF_QC:OFF_QC + B_Q_RANK], gqc_ref[...])
    qb = _bdot(qc, wqb_ref[...])
    for hd in range(B_HEADS):
        col = qb[:, hd * LANES:(hd + 1) * LANES]
        col = _rope(col, cb, sb, B_ROPE // 4) * ((B_NOPE + B_ROPE) ** -0.5 * LOG2E)
        qbt_ref[hd * LANES:(hd + 1) * LANES, :] = col.T.astype(BF16)

    ckv = _rms(proj[:, OFF_KVC:OFF_KVC + B_KV_RANK], gkv_ref[...])
    if is_ctx:
        ckvf_ref[...] = ckv
    knope = _bdot(ckv, wk_ref[...])
    misc = proj[:, OFF_MISC:OFF_MISC + LANES]
    misc_ref[...] = misc
    kpe = jnp.where((lane >= KPE_LANE) & (lane < KPE_LANE + B_ROPE), _rope(misc, cb, sb, B_ROPE // 4), 0.0)
    for hd in range(B_HEADS):
        kb_ref[:, hd * LANES:(hd + 1) * LANES] = (knope[:, hd * LANES:(hd + 1) * LANES] + kpe).astype(BF16)
    vbt_ref[...] = _bdot(ckv, wv_ref[...]).T.astype(BF16)

    z_ref[...] = proj[:, OFF_Z:OFF_Z + C_INNER]
    xbc_ref[...] = proj[:, OFF_XBC:OFF_XBC + C_CONV_CH]


def _pre(l, is_ctx, x2d, mod, lw, tabs):
    t = x2d.shape[0]
    nt = t // TM
    per_seq = DEC_SEQ // TM
    if is_ctx:
        mod_map = lambda i: (l, 0, 0, 0)
        tab_map = lambda i: (per_seq, 0)
    else:
        mod_map = lambda i: (l, 1 + i // per_seq, 0, 0)
        tab_map = lambda i: (i % per_seq, 0)
    const = lambda shape: pl.BlockSpec((None,) + shape, lambda i: (l,) + (0,) * len(shape))
    tab_spec = pl.BlockSpec((TM, LANES), tab_map)
    tok = lambda w: pl.BlockSpec((TM, w), lambda i: (i, 0))
    tok_t = lambda w: pl.BlockSpec((w, TM), lambda i: (0, i))
    out_shape = [
        jax.ShapeDtypeStruct((A_Q, t), BF16),
        jax.ShapeDtypeStruct((t, A_KV), BF16),
        jax.ShapeDtypeStruct((A_KV, t), BF16),
        jax.ShapeDtypeStruct((QB_W, t), BF16),
        jax.ShapeDtypeStruct((t, QB_W), BF16),
        jax.ShapeDtypeStruct((B_HEADS * B_V, t), BF16),
        jax.ShapeDtypeStruct((t, C_INNER), F32),
        jax.ShapeDtypeStruct((t, C_CONV_CH), F32),
        jax.ShapeDtypeStruct((t, LANES), F32),
    ]
    out_specs = [tok_t(A_Q), tok(A_KV), tok_t(A_KV), tok_t(QB_W), tok(QB_W), tok_t(B_HEADS * B_V),
                 tok(C_INNER), tok(C_CONV_CH), tok(LANES)]
    if is_ctx:
        out_shape += [jax.ShapeDtypeStruct((t, A_KV), F32)] * 2 + [jax.ShapeDtypeStruct((t, B_KV_RANK), F32)]
        out_specs += [tok(A_KV), tok(A_KV), tok(B_KV_RANK)]
    return pl.pallas_call(
        functools.partial(_pre_kernel, is_ctx),
        out_shape=tuple(out_shape),
        grid=(nt,),
        in_specs=[
            tok(D_MODEL),
            pl.BlockSpec((None, None, 1, 6 * D_MODEL), mod_map),
            const((1, D_MODEL)),
            const((D_MODEL, IN_COLS_P)),
            const((1, LANES)), const((1, LANES)), const((1, B_Q_RANK)),
            const((B_Q_RANK, QB_W)), const((1, B_KV_RANK)),
            const((B_KV_RANK, QB_W)), const((B_KV_RANK, B_HEADS * B_V)),
            tab_spec, tab_spec, tab_spec, tab_spec,
        ],
        out_specs=tuple(out_specs),
        compiler_params=_cparams(("parallel",)),
        name="pre_ctx" if is_ctx else "pre_lat",
    )(x2d, mod, lw["g_pre"], lw["w_in"], lw["g_q"], lw["g_k"], lw["g_qc"], lw["w_qb"], lw["g_kv"],
      lw["w_k"], lw["w_v"], *tabs)


def _merge_halves(lo_part, hi_part):
    return jnp.where(_lane_iota(lo_part.shape) < HEAD_DIM, lo_part, hi_part)


def _gqa_query_slot(qat_ref, hd):
    q = qat_ref[hd * HEAD_DIM:(hd + 1) * HEAD_DIM, :]
    zero = jnp.zeros_like(q)
    return jnp.concatenate([q, zero] if hd // A_REP == 0 else [zero, q], axis=0)


ONES_ROWS = 16
ACC_ROWS = LANES + ONES_ROWS


def _with_ones(v_t):
    return jnp.concatenate([v_t, jnp.ones((ONES_ROWS, v_t.shape[1]), BF16)], axis=0)


def _first_tile(k, q_t, v_t):
    return _first_scores(jnp.dot(k, q_t, preferred_element_type=F32), v_t)


def _first_scores(s, v_t):
    m = jnp.max(s, axis=0, keepdims=True)
    p = jnp.exp2(s - m)
    return m, jnp.dot(_with_ones(v_t), p.astype(BF16), preferred_element_type=F32)


def _next_tile(s, m, acc, v_t):
    m_new = jnp.maximum(m, jnp.max(s, axis=0, keepdims=True))
    p = jnp.exp2(s - m_new)
    acc = jnp.exp2(m - m_new) * acc + jnp.dot(_with_ones(v_t), p.astype(BF16), preferred_element_type=F32)
    return m_new, acc


def _normalised(acc):
    return (acc[0:LANES, :] / acc[LANES:LANES + 1, :]).T


def _write_heads(outs_a, outs_b, oa_ref, ob_ref):
    for r in range(A_REP):
        oa_ref[:, r * LANES:(r + 1) * LANES] = _merge_halves(outs_a[r], outs_a[A_REP + r]).astype(BF16)
    for pr in range(B_HEADS // 2):
        ob_ref[:, pr * LANES:(pr + 1) * LANES] = _merge_halves(outs_b[2 * pr], outs_b[2 * pr + 1]).astype(BF16)


def _attn_ctx_kernel(qat_ref, ka_ref, vat_ref, qbt_ref, kb_ref, vbt_ref, oa_ref, ob_ref):
    def split_heads(acc, n):
        o = acc[0:LANES, :] / acc[LANES:LANES + 1, :]
        return [o[:, i * SEQ:(i + 1) * SEQ].T for i in range(n)]

    q_all = jnp.concatenate([_gqa_query_slot(qat_ref, hd) for hd in range(A_HEADS)], axis=1)
    outs_a = split_heads(_first_tile(ka_ref[...], q_all, vat_ref[...])[1], A_HEADS)
    outs_b = []
    for pr in range(B_HEADS // 2):
        s = jnp.concatenate([jnp.dot(kb_ref[:, hd * LANES:(hd + 1) * LANES], qbt_ref[hd * LANES:(hd + 1) * LANES, :],
                                     preferred_element_type=F32) for hd in (2 * pr, 2 * pr + 1)], axis=1)
        outs_b += split_heads(_first_scores(s, vbt_ref[pr * LANES:(pr + 1) * LANES, :])[1], 2)
    _write_heads(outs_a, outs_b, oa_ref, ob_ref)


def _attn_ctx(qat, ka, vat, qbt, kb, vbt):
    t = ka.shape[0]
    tok = lambda w: pl.BlockSpec((SEQ, w), lambda b: (b, 0))
    tok_t = lambda w: pl.BlockSpec((w, SEQ), lambda b: (0, b))
    return pl.pallas_call(
        _attn_ctx_kernel,
        out_shape=(jax.ShapeDtypeStruct((t, A_Q), BF16), jax.ShapeDtypeStruct((t, B_HEADS * B_V), BF16)),
        grid=(t // SEQ,),
        in_specs=[tok_t(A_Q), tok(A_KV), tok_t(A_KV), tok_t(QB_W), tok(QB_W), tok_t(B_HEADS * B_V)],
        out_specs=(tok(A_Q), tok(B_HEADS * B_V)),
        compiler_params=_cparams(("parallel",)),
        name="attn_ctx",
    )(qat, ka, vat, qbt, kb, vbt)


def _attn_lat_kernel(is_gqa, *refs):
    if is_gqa:
        q0_ref, q1_ref, k_ref, vt_ref, kc_ref, vct_ref, o_ref, s_ref, acc_ref = refs
        zero = jnp.zeros((HEAD_DIM, TQ), BF16)
        q_slots = (jnp.concatenate([q0_ref[...], zero], axis=0), jnp.concatenate([zero, q1_ref[...]], axis=0))
        key_cols = (0, 0)
    else:
        q_ref, k_ref, vt_ref, kc_ref, vct_ref, o_ref, s_ref, acc_ref = refs
        q_slots = (q_ref[0:LANES, :], q_ref[LANES:2 * LANES, :])
        key_cols = (0, LANES)
    n_tiles = DEC_SEQ // TK
    vct = vct_ref[...]

    row_max = []
    for hh in range(2):
        c0 = key_cols[hh]
        m, acc = _first_tile(kc_ref[:, c0:c0 + LANES], q_slots[hh], vct)
        acc_ref[hh] = acc
        s_ref[hh, 0] = jnp.dot(k_ref[0:TK, c0:c0 + LANES], q_slots[hh], preferred_element_type=F32)
        row_max.append(m)
    for j in range(n_tiles):
        for hh in range(2):
            c0 = key_cols[hh]
            if j + 1 < n_tiles:
                s_ref[hh, (j + 1) % 2] = jnp.dot(k_ref[(j + 1) * TK:(j + 2) * TK, c0:c0 + LANES], q_slots[hh],
                                                 preferred_element_type=F32)
            row_max[hh], acc = _next_tile(s_ref[hh, j % 2], row_max[hh], acc_ref[hh], vt_ref[:, j * TK:(j + 1) * TK])
            acc_ref[hh] = acc
    o_ref[...] = _merge_halves(_normalised(acc_ref[0]), _normalised(acc_ref[1])).astype(BF16)


def _attn_lat(l, is_gqa, q_t, k, v_t, kc, vc_t):
    t = k.shape[0]
    nq = DEC_SEQ // TQ
    n_pairs = A_REP if is_gqa else B_HEADS // 2
    kw = A_KV if is_gqa else 2 * LANES
    if is_gqa:
        q_specs = [pl.BlockSpec((HEAD_DIM, TQ), lambda b, r, i: (r, b * nq + i)),
                   pl.BlockSpec((HEAD_DIM, TQ), lambda b, r, i: (A_REP + r, b * nq + i))]
        q_args = [q_t, q_t]
        pair_col = lambda r: 0
    else:
        q_specs = [pl.BlockSpec((2 * LANES, TQ), lambda b, r, i: (r, b * nq + i))]
        q_args = [q_t]
        pair_col = lambda r: r
    return pl.pallas_call(
        functools.partial(_attn_lat_kernel, is_gqa),
        out_shape=jax.ShapeDtypeStruct((t, n_pairs * LANES), BF16),
        grid=(DEC_BATCH, n_pairs, nq),
        in_specs=q_specs + [
            pl.BlockSpec((DEC_SEQ, kw), lambda b, r, i: (b, pair_col(r))),
            pl.BlockSpec((LANES, DEC_SEQ), lambda b, r, i: (pair_col(r), b)),
            pl.BlockSpec((None, None, PAST_LEN, kw), lambda b, r, i: (l, b, 0, pair_col(r))),
            pl.BlockSpec((None, None, LANES, PAST_LEN), lambda b, r, i: (l, b, pair_col(r), 0)),
        ],
        out_specs=pl.BlockSpec((TQ, LANES), lambda b, r, i: (b * nq + i, r)),
        scratch_shapes=[pltpu.VMEM((2, 2, TK, TQ), F32), pltpu.VMEM((2, ACC_ROWS, TQ), F32)],
        compiler_params=_cparams(("parallel", "parallel", "parallel")),
        name="attn_lat_gqa" if is_gqa else "attn_lat_mla",
    )(*q_args, k, v_t, kc, vc_t)


HALO = SUBLANES
N_PAIRS = C_HEADS // 2
HEADS_PER_GROUP = C_HEADS // C_GROUPS


def _ssd_kernel(n_chunks, has_h0, xbc_ref, misc_ref, cw_ref, cb_ref, dtb_ref, alog_ref, dvec_ref, *rest):
    if has_h0:
        h0_ref, y_ref, hout_ref, stf_ref, stb_ref, newb_ref, cm_ref, ecx_ref = rest
        stf_ref[...] = h0_ref[0]
        stb_ref[...] = h0_ref[1]
    else:
        y_ref, hout_ref, stf_ref, stb_ref, newb_ref, cm_ref, ecx_ref = rest
        stf_ref[...] = jnp.zeros_like(stf_ref)
        stb_ref[...] = jnp.zeros_like(stb_ref)
    seq_len = n_chunks * C_CHUNK
    row = lax.broadcasted_iota(jnp.int32, (C_CHUNK, LANES), 0)
    lane = _lane_iota((C_CHUNK, LANES))
    lo = lane < C_HEAD_DIM
    lane_group = (lane >= C_STATE).astype(jnp.int32)
    row_group = (row >= C_STATE).astype(jnp.int32)
    causal = (row >= lane, row <= lane)
    neg_a = -jnp.exp(alog_ref[...])

    def lane_of(direction, hd):
        return DT_LANE + C_HEADS * direction + hd

    def pair_cols(v, direction, heads):
        j0, j1 = lane_of(direction, heads[0]), lane_of(direction, heads[1])
        return jnp.where(lo[0:v.shape[0], :], v[:, j0:j0 + 1], v[:, j1:j1 + 1])

    def pass1(c, carry):
        start = pl.multiple_of(c * C_CHUNK, C_CHUNK)
        prev0 = pl.multiple_of(jnp.maximum(start - HALO, 0), HALO)
        next0 = pl.multiple_of(jnp.minimum(start + C_CHUNK, seq_len - HALO), HALO)
        keep_prev = jnp.where(c > 0, 1.0, 0.0).astype(F32)
        keep_next = jnp.where(c < n_chunks - 1, 1.0, 0.0).astype(F32)
        rows_c = pl.ds(start, C_CHUNK)

        ue = jnp.concatenate([xbc_ref[pl.ds(prev0, HALO), :] * keep_prev, xbc_ref[rows_c, :],
                              xbc_ref[pl.ds(next0, HALO), :] * keep_next], axis=0)
        acc = jnp.broadcast_to(cb_ref[...], (C_CHUNK, C_CONV_CH))
        for k in range(C_CONV):
            sh = (C_CONV // 2 - k) % ue.shape[0]
            r = ue if sh == 0 else pltpu.roll(ue, sh, axis=0)
            acc = acc + r[HALO:HALO + C_CHUNK, :] * cw_ref[k:k + 1, :]
        xc = acc * jax.nn.sigmoid(acc)
        xs = xc[:, 0:C_INNER]
        bm = xc[:, C_INNER:C_INNER + LANES]
        cm = xc[:, C_INNER + LANES:C_INNER + 2 * LANES]
        bmt = bm.T.astype(BF16)
        cm16 = cm.astype(BF16)
        g_mats = [jnp.dot(jnp.where(lane_group == g, cm, 0.0).astype(BF16), bmt, preferred_element_type=F32)
                  for g in range(C_GROUPS)]

        x_dt = misc_ref[rows_c, :] + dtb_ref[...]
        dtv = jnp.maximum(x_dt, 0.0) + jnp.log(1.0 + jnp.exp(-jnp.abs(x_dt)))
        dta = dtv * neg_a
        cum = dta
        step = 1
        while step < C_CHUNK:
            cum = cum + jnp.where(row >= step, pltpu.roll(cum, step, axis=0), 0.0)
            step *= 2
        tot = cum[C_CHUNK - 1:C_CHUNK, :]
        cxs = (cum, tot - cum + dta)
        cxts = (cxs[0].T, cxs[1].T)
        dtt = dtv.T
        wsts = tuple(jnp.exp(tot - cx) * dtv for cx in cxs)
        ecx_f = jnp.exp(cxs[0])
        cdec = jnp.exp(tot)
        cm_ref[c] = cm16
        ecx_ref[c, 0:C_CHUNK, :] = jnp.exp(cxs[1])
        ecx_ref[c, C_CHUNK:C_CHUNK + SUBLANES, :] = jnp.broadcast_to(cdec, (SUBLANES, LANES))

        for pr in range(N_PAIRS):
            heads = (2 * pr, 2 * pr + 1)
            groups = tuple(hd // HEADS_PER_GROUP for hd in heads)
            xs_pair = xs[:, pr * LANES:(pr + 1) * LANES]
            xs16 = xs_pair.astype(BF16)
            own = row_group == jnp.where(lo, groups[0], groups[1])
            y = xs_pair * dvec_ref[:, pr * LANES:(pr + 1) * LANES]
            for direction in range(2):
                yd = []
                for hd in heads:
                    j = lane_of(direction, hd)
                    seg = jnp.where(causal[direction], cxs[direction][:, j:j + 1] - cxts[direction][j:j + 1, :],
                                    -jnp.inf)
                    sc = g_mats[hd // HEADS_PER_GROUP] * jnp.exp(seg) * dtt[j:j + 1, :]
                    yd.append(jnp.dot(sc.astype(BF16), xs16, preferred_element_type=F32))
                y = y + _merge_halves(yd[0], yd[1])
                xw = xs_pair * pair_cols(wsts[direction], direction, heads)
                new = jnp.where(own, jnp.dot(bmt, xw.astype(BF16), preferred_element_type=F32), 0.0)
                if direction == 0:
                    st = stf_ref[pr]
                    y_off = jnp.dot(cm16, st.astype(BF16), preferred_element_type=F32)
                    y = y + y_off * pair_cols(ecx_f, 0, heads)
                    stf_ref[pr] = st * pair_cols(cdec, 0, heads) + new
                else:
                    newb_ref[c, pr] = new
            y_ref[rows_c, pr * LANES:(pr + 1) * LANES] = y
        return carry

    def pass2(i, carry):
        c = n_chunks - 1 - i
        rows_c = pl.ds(pl.multiple_of(c * C_CHUNK, C_CHUNK), C_CHUNK)
        cm16 = cm_ref[c]
        ecx_b = ecx_ref[c, 0:C_CHUNK, :]
        cdec = ecx_ref[c, C_CHUNK:C_CHUNK + 1, :]
        for pr in range(N_PAIRS):
            heads = (2 * pr, 2 * pr + 1)
            st = stb_ref[pr]
            y_off = jnp.dot(cm16, st.astype(BF16), preferred_element_type=F32)
            cols = pl.ds(pr * LANES, LANES)
            y_ref[rows_c, cols] = y_ref[rows_c, cols] + y_off * pair_cols(ecx_b, 1, heads)
            stb_ref[pr] = st * pair_cols(cdec, 1, heads) + newb_ref[c, pr]
        return carry

    unroll = n_chunks <= 2
    lax.fori_loop(0, n_chunks, pass1, 0, unroll=unroll)
    lax.fori_loop(0, n_chunks, pass2, 0, unroll=unroll)

    for direction, st_ref in enumerate((stf_ref, stb_ref)):
        for pr in range(N_PAIRS):
            st_t = st_ref[pr].T
            for hh, hd in enumerate((2 * pr, 2 * pr + 1)):
                g = hd // HEADS_PER_GROUP
                hout_ref[direction, hd] = st_t[hh * C_HEAD_DIM:(hh + 1) * C_HEAD_DIM,
                                               g * C_STATE:(g + 1) * C_STATE]


def _ssd(l, xbc, misc, lw, h0, seq_len):
    t = xbc.shape[0]
    nb = t // seq_len
    nc = seq_len // C_CHUNK
    has_h0 = h0 is not None
    const = lambda shape: pl.BlockSpec((None,) + shape, lambda b: (l,) + (0,) * len(shape))
    in_specs = [
        pl.BlockSpec((seq_len, C_CONV_CH), lambda b: (b, 0)),
        pl.BlockSpec((seq_len, LANES), lambda b: (b, 0)),
        const((C_CONV, C_CONV_CH)), const((1, C_CONV_CH)), const((1, LANES)), const((1, LANES)),
        const((1, C_INNER)),
    ]
    args = [xbc, misc, lw["conv_w"], lw["conv_b"], lw["dt_bias"], lw["a_log"], lw["d_vec"]]
    if has_h0:
        in_specs.append(pl.BlockSpec((None, None, 2, N_PAIRS, LANES, LANES), lambda b: (b, l, 0, 0, 0, 0)))
        args.append(h0)
    return pl.pallas_call(
        functools.partial(_ssd_kernel, nc, has_h0),
        out_shape=(jax.ShapeDtypeStruct((t, C_INNER), F32),
                   jax.ShapeDtypeStruct((nb, 2, C_HEADS, C_HEAD_DIM, C_STATE), F32)),
        grid=(nb,),
        in_specs=in_specs,
        out_specs=(pl.BlockSpec((seq_len, C_INNER), lambda b: (b, 0)),
                   pl.BlockSpec((None, 2, C_HEADS, C_HEAD_DIM, C_STATE), lambda b: (b, 0, 0, 0, 0))),
        scratch_shapes=[pltpu.VMEM((N_PAIRS, LANES, LANES), F32), pltpu.VMEM((N_PAIRS, LANES, LANES), F32),
                        pltpu.VMEM((nc, N_PAIRS, LANES, LANES), F32), pltpu.VMEM((nc, C_CHUNK, LANES), BF16),
                        pltpu.VMEM((nc, C_CHUNK + SUBLANES, LANES), F32)],
        compiler_params=_cparams(("parallel",)),
        name="ssd_lat" if has_h0 else "ssd_ctx",
    )(*args)


def _post_kernel(x_ref, mod_ref, oa_ref, ob_ref, y_ref, z_ref, gssm_ref, wout_ref, gmix_ref, gffn_ref,
                 gffo_ref, w1_ref, w2_ref, o_ref):
    x = x_ref[...]
    gate1 = mod_ref[:, 2 * D_MODEL:3 * D_MODEL]
    shift2 = mod_ref[:, 3 * D_MODEL:4 * D_MODEL]
    scale2 = mod_ref[:, 4 * D_MODEL:5 * D_MODEL]
    gate2 = mod_ref[:, 5 * D_MODEL:6 * D_MODEL]
    z = z_ref[...]
    oc = _rms(y_ref[...] * (z * jax.nn.sigmoid(z)), gssm_ref[...])
    mix = jnp.concatenate([oa_ref[...], ob_ref[...], oc.astype(BF16)], axis=1)
    out = jnp.dot(mix, wout_ref[...], preferred_element_type=F32)
    x = x + gate1 * _rms(out, gmix_ref[...])
    h = _rms(x, gffn_ref[...]) * (1.0 + scale2) + shift2
    u = jnp.maximum(_bdot(h, w1_ref[...]), 0.0)
    f = _bdot(u * u, w2_ref[...])
    o_ref[...] = x + gate2 * _rms(f, gffo_ref[...])


def _post(l, is_ctx, x2d, mod, oa, ob, y, z, lw):
    t = x2d.shape[0]
    per_seq = DEC_SEQ // TM
    mod_map = (lambda i: (l, 0, 0, 0)) if is_ctx else (lambda i: (l, 1 + i // per_seq, 0, 0))
    tok = lambda w: pl.BlockSpec((TM, w), lambda i: (i, 0))
    const = lambda shape: pl.BlockSpec((None,) + shape, lambda i: (l,) + (0,) * len(shape),
                                       pipeline_mode=pl.Buffered(1))
    return pl.pallas_call(
        _post_kernel,
        out_shape=jax.ShapeDtypeStruct((t, D_MODEL), F32),
        grid=(t // TM,),
        in_specs=[
            tok(D_MODEL),
            pl.BlockSpec((None, None, 1, 6 * D_MODEL), mod_map),
            tok(A_Q), tok(B_HEADS * B_V), tok(C_INNER), tok(C_INNER),
            const((1, C_INNER)), const((D_MODEL, D_MODEL)), const((1, D_MODEL)), const((1, D_MODEL)),
            const((1, D_MODEL)), const((D_MODEL, D_FF)), const((D_FF, D_MODEL)),
        ],
        out_specs=tok(D_MODEL),
        compiler_params=_cparams(("parallel",)),
        name="post_ctx" if is_ctx else "post_lat",
    )(x2d, mod, oa, ob, y, z, lw["g_ssm"], lw["w_out"], lw["g_mix"], lw["g_ffn"], lw["g_ffo"],
      lw["w1"], lw["w2"])


def _rope_tables():
    pos = np.arange(DEC_SEQ)
    axis_pos = np.stack([(pos // GRID_W), (pos % GRID_W)], axis=0).astype(np.float32)

    def pattern(rot_dim):
        half = rot_dim // 2
        quarter = half // 2
        inv = (1.0 / (np.float32(ROPE_THETA) ** (np.arange(0, half, 2, dtype=np.float32) / np.float32(half))))
        inv = inv.astype(np.float32)
        dd = np.arange(rot_dim)
        ang = (axis_pos[dd // half].T * inv[dd % quarter][None, :]).astype(np.float32)
        sign = np.where((dd % half) < quarter, -1.0, 1.0).astype(np.float32)
        return np.cos(ang).astype(np.float32), (np.sin(ang) * sign).astype(np.float32)

    ca, sa = pattern(HEAD_DIM)
    ca = np.tile(ca, (1, LANES // HEAD_DIM))
    sa = np.tile(sa, (1, LANES // HEAD_DIM))
    cb32, sb32 = pattern(B_ROPE)
    tail = LANES - KPE_LANE - B_ROPE
    cb = np.concatenate([np.ones((DEC_SEQ, KPE_LANE), np.float32), cb32, np.ones((DEC_SEQ, tail), np.float32)], 1)
    sb = np.concatenate([np.zeros((DEC_SEQ, KPE_LANE), np.float32), sb32, np.zeros((DEC_SEQ, tail), np.float32)], 1)
    ident_c = np.ones((TM, LANES), np.float32)
    ident_s = np.zeros((TM, LANES), np.float32)
    return tuple(jnp.asarray(np.concatenate([tb, idt], axis=0))
                 for tb, idt in ((ca, ident_c), (sa, ident_s), (cb, ident_c), (sb, ident_s)))


def _prep_weights(p):
    w_in = p["w_in"]
    o_kpe = A_Q + 2 * A_KV + B_Q_RANK + B_KV_RANK
    o_z = o_kpe + B_ROPE
    o_xbc = o_z + C_INNER
    o_dt = o_xbc + C_CONV_CH
    zeros = lambda n: jnp.zeros((DEPTH, D_MODEL, n), F32)
    w_in_p = jnp.concatenate(
        [w_in[..., :o_kpe], w_in[..., o_z:o_dt], zeros(KPE_LANE), w_in[..., o_kpe:o_z], w_in[..., o_dt:],
         zeros(LANES - DT_LANE - 2 * C_HEADS)], axis=-1).astype(BF16)

    w_qb = p["mla_w_qb"].reshape(DEPTH, B_Q_RANK, B_HEADS, B_NOPE + B_ROPE)
    w_qb_p = jnp.pad(w_qb, ((0, 0),) * 3 + ((0, LANES - B_NOPE - B_ROPE),)).reshape(DEPTH, B_Q_RANK, QB_W)
    w_kvb = p["mla_w_kvb"].reshape(DEPTH, B_KV_RANK, B_HEADS, B_NOPE + B_V)
    w_k = jnp.pad(w_kvb[..., :B_NOPE], ((0, 0),) * 3 + ((0, LANES - B_NOPE),)).reshape(DEPTH, B_KV_RANK, QB_W)
    w_v = w_kvb[..., B_NOPE:].reshape(DEPTH, B_KV_RANK, B_HEADS * B_V)

    w_out = p["w_out"]
    order = [r + g * A_REP for r in range(A_REP) for g in range(A_KV_HEADS)]
    w_out_p = jnp.concatenate([w_out[:, hd * HEAD_DIM:(hd + 1) * HEAD_DIM] for hd in order] + [w_out[:, A_Q:]],
                              axis=1).astype(BF16)

    row = lambda v: v.reshape(DEPTH, 1, -1)
    pair_gain = lambda g: row(jnp.tile(g, (1, LANES // HEAD_DIM)))
    lane_vec = lambda v: jnp.pad(v.reshape(DEPTH, 1, -1), ((0, 0), (0, 0), (DT_LANE, LANES - DT_LANE - 2 * C_HEADS)))
    return {
        "g_pre": row(p["norm_mix_pre"]),
        "w_in": w_in_p,
        "g_q": pair_gain(p["attn_q_norm"]),
        "g_k": pair_gain(p["attn_k_norm"]),
        "g_qc": row(p["mla_q_norm"]),
        "w_qb": w_qb_p.astype(BF16),
        "g_kv": row(p["mla_kv_norm"]),
        "w_k": w_k.astype(BF16),
        "w_v": w_v.astype(BF16),
        "conv_w": jnp.swapaxes(p["ssm_conv_w"], 1, 2),
        "conv_b": row(p["ssm_conv_b"]),
        "dt_bias": lane_vec(p["ssm_dt_bias"]),
        "a_log": lane_vec(p["ssm_a_log"]),
        "d_vec": row(jnp.repeat(p["ssm_d"], C_HEAD_DIM, axis=1)),
        "g_ssm": row(p["ssm_norm"]),
        "w_out": w_out_p,
        "g_mix": row(p["norm_mix_post"]),
        "g_ffn": row(p["norm_ffn_pre"]),
        "g_ffo": row(p["norm_ffn_post"]),
        "w1": p["w_ffn1"].astype(BF16),
        "w2": p["w_ffn2"].astype(BF16),
    }


def _state_to_pairs(h):
    ht = jnp.swapaxes(h, -1, -2)
    zero = jnp.zeros_like(ht[..., 0, :, :])
    pairs = []
    for pr in range(N_PAIRS):
        heads = (2 * pr, 2 * pr + 1)
        row_blocks = [jnp.concatenate([ht[..., hd, :, :] if hd // HEADS_PER_GROUP == g else zero for hd in heads],
                                      axis=-1) for g in range(C_GROUPS)]
        pairs.append(jnp.concatenate(row_blocks, axis=-2))
    return jnp.stack(pairs, axis=-3)


def kernel(x_prompt, x_sample, cache_attn_k, cache_attn_v, cache_mla_ckv, cache_mla_kpe, state_ssm, c, c_ctx, norm_mix_pre, norm_mix_post, norm_ffn_pre, norm_ffn_post, w_mod, b_mod, w_in, attn_q_norm, attn_k_norm, mla_q_norm, mla_w_qb, mla_kv_norm, mla_w_kvb, ssm_conv_w, ssm_conv_b, ssm_dt_bias, ssm_a_log, ssm_d, ssm_norm, w_out, w_ffn1, w_ffn2):
    p = dict(norm_mix_pre=norm_mix_pre, norm_mix_post=norm_mix_post, norm_ffn_pre=norm_ffn_pre,
             norm_ffn_post=norm_ffn_post, w_in=w_in, attn_q_norm=attn_q_norm, attn_k_norm=attn_k_norm,
             mla_q_norm=mla_q_norm, mla_w_qb=mla_w_qb, mla_kv_norm=mla_kv_norm, mla_w_kvb=mla_w_kvb,
             ssm_conv_w=ssm_conv_w, ssm_conv_b=ssm_conv_b, ssm_dt_bias=ssm_dt_bias, ssm_a_log=ssm_a_log,
             ssm_d=ssm_d, ssm_norm=ssm_norm, w_out=w_out, w_ffn1=w_ffn1, w_ffn2=w_ffn2)
    lw = _prep_weights(p)
    tabs = _rope_tables()

    cvec = jnp.concatenate([c_ctx[None, :], c, jnp.zeros((MOD_ROWS - 1 - DEC_BATCH, D_MODEL), F32)], axis=0)
    mod = _modulation(cvec, w_mod, b_mod).reshape(DEPTH, MOD_ROWS, 1, 6 * D_MODEL)

    kpe_p = jnp.pad(cache_mla_kpe, ((0, 0),) * 3 + ((KPE_LANE, LANES - KPE_LANE - B_ROPE),))
    kc, vct, kbc, vbct = _cache_prep(
        cache_attn_k.reshape(DEC_BATCH, DEPTH, PAST_LEN, A_KV), cache_attn_v.reshape(DEC_BATCH, DEPTH, PAST_LEN, A_KV),
        cache_mla_ckv, kpe_p, lw["w_k"], lw["w_v"])
    h0_lat = _state_to_pairs(state_ssm)

    xp = x_prompt.reshape(BATCH * SEQ, D_MODEL)
    xs = x_sample.reshape(DEC_BATCH * DEC_SEQ, D_MODEL)
    new_k, new_v, new_ckv, new_kpe, new_ssm = [], [], [], [], []
    for l in range(DEPTH):
        qat, ka, vat, qbt, kb, vbt, z, xbc, misc, kf, vf, ckvf = _pre(l, True, xp, mod, lw, tabs)
        oa, ob = _attn_ctx(qat, ka, vat, qbt, kb, vbt)
        y, hfin = _ssd(l, xbc, misc, lw, None, SEQ)
        xp = _post(l, True, xp, mod, oa, ob, y, z, lw)
        new_k.append(kf.reshape(BATCH, SEQ, A_KV_HEADS, HEAD_DIM))
        new_v.append(vf.reshape(BATCH, SEQ, A_KV_HEADS, HEAD_DIM))
        new_ckv.append(ckvf.reshape(BATCH, SEQ, B_KV_RANK))
        new_kpe.append(misc[:, KPE_LANE:KPE_LANE + B_ROPE].reshape(BATCH, SEQ, B_ROPE))
        new_ssm.append(hfin)
        qat, ka, vat, qbt, kb, vbt, z, xbc, misc = _pre(l, False, xs, mod, lw, tabs)
        oa = _attn_lat(l, True, qat, ka, vat, kc, vct)
        ob = _attn_lat(l, False, qbt, kb, vbt, kbc, vbct)
        y, _ = _ssd(l, xbc, misc, lw, h0_lat, DEC_SEQ)
        xs = _post(l, False, xs, mod, oa, ob, y, z, lw)
    return (xp.reshape(BATCH, SEQ, D_MODEL), xs.reshape(DEC_BATCH, DEC_SEQ, D_MODEL),
            jnp.stack(new_k, axis=1), jnp.stack(new_v, axis=1), jnp.stack(new_ckv, axis=1),
            jnp.stack(new_kpe, axis=1), jnp.stack(new_ssm, axis=1))
```

```python
import functools
import math

import numpy as np
import jax
import jax.numpy as jnp
from jax import lax
from jax.experimental import pallas as pl
from jax.experimental.pallas import tpu as pltpu

F32 = jnp.float32
BF16 = jnp.bfloat16

D_MODEL = 1024
BATCH = 16
SEQ = 256
DEPTH = 4
DEC_BATCH = 2
DEC_SEQ = 4096
PAST_LEN = 256
GRID_W = 64
HEAD_DIM = 64
A_HEADS = 6
A_KV_HEADS = 2
A_REP = A_HEADS // A_KV_HEADS
B_HEADS = 4
B_Q_RANK = 256
B_KV_RANK = 128
B_NOPE = 64
B_ROPE = 32
B_V = 64
C_HEADS = 6
C_HEAD_DIM = 64
C_INNER = C_HEADS * C_HEAD_DIM
C_GROUPS = 2
C_STATE = 64
C_CONV = 5
C_CHUNK = 128
C_CONV_CH = C_INNER + 2 * C_GROUPS * C_STATE
D_FF = 4 * D_MODEL
A_Q = A_HEADS * HEAD_DIM
A_KV = A_KV_HEADS * HEAD_DIM
ROPE_THETA = 10000.0
EPS = 1e-6
LOG2E = math.log2(math.e)

LANES = 128
SUBLANES = 8
VMEM_LIMIT = 56 * 1024 * 1024

OFF_QA = 0
OFF_KA = OFF_QA + A_Q
OFF_VA = OFF_KA + A_KV
OFF_QC = OFF_VA + A_KV
OFF_KVC = OFF_QC + B_Q_RANK
OFF_Z = OFF_KVC + B_KV_RANK
OFF_XBC = OFF_Z + C_INNER
OFF_MISC = OFF_XBC + C_CONV_CH
IN_COLS_P = OFF_MISC + LANES
KPE_LANE = 64
DT_LANE = KPE_LANE + B_ROPE
QB_W = B_HEADS * LANES

TM = 512
TQ = 512
TK = 1024


def _cparams(sem):
    return pltpu.CompilerParams(dimension_semantics=sem, vmem_limit_bytes=VMEM_LIMIT)


def _rms(x, g):
    return x * lax.rsqrt(jnp.mean(x * x, axis=-1, keepdims=True) + EPS) * g


def _bdot(a, b):
    return jnp.dot(a.astype(BF16), b.astype(BF16), preferred_element_type=F32)


def _lane_iota(shape):
    return lax.broadcasted_iota(jnp.int32, shape, len(shape) - 1)


def _swap_lanes(x, dist):
    lane = _lane_iota(x.shape)
    fwd = pltpu.roll(x, LANES - dist, axis=1)
    bwd = pltpu.roll(x, dist, axis=1)
    return jnp.where((lane % (2 * dist)) < dist, fwd, bwd)


def _rope(x, cos, sin, dist):
    return x * cos + _swap_lanes(x, dist) * sin


def _head_pair_rms(col, gain):
    lo = _lane_iota(col.shape) < HEAD_DIM
    c2 = col * col
    s_lo = jnp.sum(jnp.where(lo, c2, 0.0), axis=-1, keepdims=True)
    s_hi = jnp.sum(jnp.where(lo, 0.0, c2), axis=-1, keepdims=True)
    return col * lax.rsqrt(jnp.where(lo, s_lo, s_hi) * (1.0 / HEAD_DIM) + EPS) * gain


MOD_ROWS = SUBLANES
MOD_TN = 1536


def _mod_kernel(c_ref, w_ref, b_ref, o_ref):
    c = c_ref[...]
    o_ref[...] = _bdot(c * jax.nn.sigmoid(c), w_ref[...]) + b_ref[...]


def _modulation(cvec, w_mod, b_mod):
    return pl.pallas_call(
        _mod_kernel,
        out_shape=jax.ShapeDtypeStruct((DEPTH, MOD_ROWS, 6 * D_MODEL), F32),
        grid=(DEPTH, 6 * D_MODEL // MOD_TN),
        in_specs=[
            pl.BlockSpec((MOD_ROWS, D_MODEL), lambda l, j: (0, 0)),
            pl.BlockSpec((None, D_MODEL, MOD_TN), lambda l, j: (l, 0, j)),
            pl.BlockSpec((None, 1, MOD_TN), lambda l, j: (l, 0, j)),
        ],
        out_specs=pl.BlockSpec((None, MOD_ROWS, MOD_TN), lambda l, j: (l, 0, j)),
        compiler_params=_cparams(("parallel", "parallel")),
        name="modulation",
    )(cvec, w_mod, b_mod.reshape(DEPTH, 1, 6 * D_MODEL))


def _cache_kernel(k_ref, v_ref, ckv_ref, kpe_ref, wk_ref, wv_ref, ko_ref, vt_ref, kb_ref, vbt_ref):
    ko_ref[...] = k_ref[...].astype(BF16)
    vt_ref[...] = v_ref[...].T.astype(BF16)
    ckv = ckv_ref[...]
    kn = _bdot(ckv, wk_ref[...])
    kpe = kpe_ref[...]
    for h in range(B_HEADS):
        kb_ref[:, h * LANES:(h + 1) * LANES] = (kn[:, h * LANES:(h + 1) * LANES] + kpe).astype(BF16)
    vbt_ref[...] = _bdot(ckv, wv_ref[...]).T.astype(BF16)


def _cache_prep(cache_k, cache_v, cache_ckv, cache_kpe_p, wk_p, wv_p):
    spec_tok = lambda w: pl.BlockSpec((None, None, PAST_LEN, w), lambda l, b: (b, l, 0, 0))
    spec_t = lambda w: pl.BlockSpec((None, None, w, PAST_LEN), lambda l, b: (l, b, 0, 0))
    spec_o = lambda w: pl.BlockSpec((None, None, PAST_LEN, w), lambda l, b: (l, b, 0, 0))
    return pl.pallas_call(
        _cache_kernel,
        out_shape=(
            jax.ShapeDtypeStruct((DEPTH, DEC_BATCH, PAST_LEN, A_KV), BF16),
            jax.ShapeDtypeStruct((DEPTH, DEC_BATCH, A_KV, PAST_LEN), BF16),
            jax.ShapeDtypeStruct((DEPTH, DEC_BATCH, PAST_LEN, QB_W), BF16),
            jax.ShapeDtypeStruct((DEPTH, DEC_BATCH, B_HEADS * B_V, PAST_LEN), BF16),
        ),
        grid=(DEPTH, DEC_BATCH),
        in_specs=[
            spec_tok(A_KV), spec_tok(A_KV), spec_tok(B_KV_RANK), spec_tok(LANES),
            pl.BlockSpec((None, B_KV_RANK, QB_W), lambda l, b: (l, 0, 0)),
            pl.BlockSpec((None, B_KV_RANK, B_HEADS * B_V), lambda l, b: (l, 0, 0)),
        ],
        out_specs=(spec_o(A_KV), spec_t(A_KV), spec_o(QB_W), spec_t(B_HEADS * B_V)),
        compiler_params=_cparams(("parallel", "parallel")),
        name="cache_prep",
    )(cache_k, cache_v, cache_ckv, cache_kpe_p, wk_p, wv_p)


PRE_SUB = 2


def _pre_kernel(is_ctx, *refs):
    for sub in range(PRE_SUB):
        _pre_rows(is_ctx, slice(sub * (TM // PRE_SUB), (sub + 1) * (TM // PRE_SUB)), *refs)


def _pre_rows(is_ctx, rows, x_ref, mod_ref, gpre_ref, win_ref, gq_ref, gk_ref, gqc_ref, wqb_ref, gkv_ref,
              wk_ref, wv_ref, dtb_ref, ca_ref, sa_ref, cb_ref, sb_ref, *outs):
    if is_ctx:
        (qat_ref, ka_ref, vat_ref, qbt_ref, kb_ref, vbt_ref, z_ref, xbc_ref, misc_ref,
         kf_ref, vf_ref, ckvf_ref) = outs
    else:
        qat_ref, ka_ref, vat_ref, qbt_ref, kb_ref, vbt_ref, z_ref, xbc_ref, misc_ref = outs
    x = x_ref[rows, :]
    shift1 = mod_ref[:, 0:D_MODEL]
    scale1 = mod_ref[:, D_MODEL:2 * D_MODEL]
    h = _rms(x, gpre_ref[...]) * (1.0 + scale1) + shift1
    proj = _bdot(h, win_ref[...])

    ca, sa, cb, sb = ca_ref[rows, :], sa_ref[rows, :], cb_ref[rows, :], sb_ref[rows, :]
    lane = _lane_iota((x.shape[0], LANES))

    gq = gq_ref[...]
    for cidx in range(A_Q // LANES):
        col = proj[:, OFF_QA + cidx * LANES:OFF_QA + (cidx + 1) * LANES]
        qn = _rope(_head_pair_rms(col, gq), ca, sa, HEAD_DIM // 4) * (HEAD_DIM ** -0.5 * LOG2E)
        qat_ref[cidx * LANES:(cidx + 1) * LANES, rows] = qn.T.astype(BF16)

    kn = _head_pair_rms(proj[:, OFF_KA:OFF_KA + A_KV], gk_ref[...])
    vcol = proj[:, OFF_VA:OFF_VA + A_KV]
    if is_ctx:
        kf_ref[rows, :] = kn
        vf_ref[rows, :] = vcol
    ka_ref[rows, :] = _rope(kn, ca, sa, HEAD_DIM // 4).astype(BF16)
    vat_ref[:, rows] = vcol.T.astype(BF16)

    qc = _rms(proj[:, OFF_QC:OFF_QC + B_Q_RANK], gqc_ref[...])
    qb = _bdot(qc, wqb_ref[...])
    for hd in range(B_HEADS):
        col = qb[:, hd * LANES:(hd + 1) * LANES]
        col = _rope(col, cb, sb, B_ROPE // 4) * ((B_NOPE + B_ROPE) ** -0.5 * LOG2E)
        qbt_ref[hd * LANES:(hd + 1) * LANES, rows] = col.T.astype(BF16)

    ckv = _rms(proj[:, OFF_KVC:OFF_KVC + B_KV_RANK], gkv_ref[...])
    if is_ctx:
        ckvf_ref[rows, :] = ckv
    knope = _bdot(ckv, wk_ref[...])
    misc = proj[:, OFF_MISC:OFF_MISC + LANES]
    kpe = jnp.where((lane >= KPE_LANE) & (lane < KPE_LANE + B_ROPE), _rope(misc, cb, sb, B_ROPE // 4), 0.0)
    for hd in range(B_HEADS):
        kb_ref[rows, hd * LANES:(hd + 1) * LANES] = (knope[:, hd * LANES:(hd + 1) * LANES] + kpe).astype(BF16)
    vbt_ref[:, rows] = _bdot(ckv, wv_ref[...]).T.astype(BF16)

    x_dt = misc + dtb_ref[...]
    dtv = jnp.maximum(x_dt, 0.0) + jnp.log(1.0 + jnp.exp(-jnp.abs(x_dt)))
    misc_ref[rows, :] = jnp.where(lane >= DT_LANE, dtv, misc)

    z_ref[rows, :] = proj[:, OFF_Z:OFF_Z + C_INNER]
    xbc_ref[rows, :] = proj[:, OFF_XBC:OFF_XBC + C_CONV_CH]


def _pre(l, is_ctx, x2d, mod, lw, tabs):
    t = x2d.shape[0]
    nt = t // TM
    per_seq = DEC_SEQ // TM
    if is_ctx:
        mod_map = lambda i: (l, 0, 0, 0)
        tab_map = lambda i: (per_seq, 0)
    else:
        mod_map = lambda i: (l, 1 + i // per_seq, 0, 0)
        tab_map = lambda i: (i % per_seq, 0)
    const = lambda shape: pl.BlockSpec((None,) + shape, lambda i: (l,) + (0,) * len(shape))
    tab_spec = pl.BlockSpec((TM, LANES), tab_map)
    tok = lambda w: pl.BlockSpec((TM, w), lambda i: (i, 0))
    tok_t = lambda w: pl.BlockSpec((w, TM), lambda i: (0, i))
    out_shape = [
        jax.ShapeDtypeStruct((A_Q, t), BF16),
        jax.ShapeDtypeStruct((t, A_KV), BF16),
        jax.ShapeDtypeStruct((A_KV, t), BF16),
        jax.ShapeDtypeStruct((QB_W, t), BF16),
        jax.ShapeDtypeStruct((t, QB_W), BF16),
        jax.ShapeDtypeStruct((B_HEADS * B_V, t), BF16),
        jax.ShapeDtypeStruct((t, C_INNER), F32),
        jax.ShapeDtypeStruct((t, C_CONV_CH), F32),
        jax.ShapeDtypeStruct((t, LANES), F32),
    ]
    out_specs = [tok_t(A_Q), tok(A_KV), tok_t(A_KV), tok_t(QB_W), tok(QB_W), tok_t(B_HEADS * B_V),
                 tok(C_INNER), tok(C_CONV_CH), tok(LANES)]
    if is_ctx:
        out_shape += [jax.ShapeDtypeStruct((t, A_KV), F32)] * 2 + [jax.ShapeDtypeStruct((t, B_KV_RANK), F32)]
        out_specs += [tok(A_KV), tok(A_KV), tok(B_KV_RANK)]
    return pl.pallas_call(
        functools.partial(_pre_kernel, is_ctx),
        out_shape=tuple(out_shape),
        grid=(nt,),
        in_specs=[
            tok(D_MODEL),
            pl.BlockSpec((None, None, 1, 6 * D_MODEL), mod_map),
            const((1, D_MODEL)),
            const((D_MODEL, IN_COLS_P)),
            const((1, LANES)), const((1, LANES)), const((1, B_Q_RANK)),
            const((B_Q_RANK, QB_W)), const((1, B_KV_RANK)),
            const((B_KV_RANK, QB_W)), const((B_KV_RANK, B_HEADS * B_V)), const((1, LANES)),
            tab_spec, tab_spec, tab_spec, tab_spec,
        ],
        out_specs=tuple(out_specs),
        compiler_params=_cparams(("parallel",)),
        name="pre_ctx" if is_ctx else "pre_lat",
    )(x2d, mod, lw["g_pre"], lw["w_in"], lw["g_q"], lw["g_k"], lw["g_qc"], lw["w_qb"], lw["g_kv"],
      lw["w_k"], lw["w_v"], lw["dt_bias"], *tabs)


def _merge_halves(lo_part, hi_part):
    return jnp.where(_lane_iota(lo_part.shape) < HEAD_DIM, lo_part, hi_part)


def _gqa_query_slot(qat_ref, hd):
    q = qat_ref[hd * HEAD_DIM:(hd + 1) * HEAD_DIM, :]
    zero = jnp.zeros_like(q)
    return jnp.concatenate([q, zero] if hd // A_REP == 0 else [zero, q], axis=0)


ONES_ROWS = 16
ACC_ROWS = LANES + ONES_ROWS


def _with_ones(v_t):
    return jnp.concatenate([v_t, jnp.ones((ONES_ROWS, v_t.shape[1]), BF16)], axis=0)


def _first_tile(k, q_t, v_t):
    return _first_scores(jnp.dot(k, q_t, preferred_element_type=F32), v_t)


def _first_scores(s, v_t):
    m = jnp.max(s, axis=0, keepdims=True)
    p = jnp.exp2(s - m)
    return m, jnp.dot(_with_ones(v_t), p.astype(BF16), preferred_element_type=F32)


def _next_tile(s, m, acc, v_t):
    m_new = jnp.maximum(m, jnp.max(s, axis=0, keepdims=True))
    p = jnp.exp2(s - m_new)
    acc = jnp.exp2(m - m_new) * acc + jnp.dot(_with_ones(v_t), p.astype(BF16), preferred_element_type=F32)
    return m_new, acc


def _normalised(acc):
    return (acc[0:LANES, :] / acc[LANES:LANES + 1, :]).T


def _write_heads(outs_a, outs_b, oa_ref, ob_ref):
    for r in range(A_REP):
        oa_ref[:, r * LANES:(r + 1) * LANES] = _merge_halves(outs_a[r], outs_a[A_REP + r]).astype(BF16)
    for pr in range(B_HEADS // 2):
        ob_ref[:, pr * LANES:(pr + 1) * LANES] = _merge_halves(outs_b[2 * pr], outs_b[2 * pr + 1]).astype(BF16)


def _attn_ctx_kernel(qat_ref, ka_ref, vat_ref, qbt_ref, kb_ref, vbt_ref, oa_ref, ob_ref):
    def split_heads(acc, n):
        o = acc[0:LANES, :] / acc[LANES:LANES + 1, :]
        return [o[:, i * SEQ:(i + 1) * SEQ].T for i in range(n)]

    q_all = jnp.concatenate([_gqa_query_slot(qat_ref, hd) for hd in range(A_HEADS)], axis=1)
    outs_a = split_heads(_first_tile(ka_ref[...], q_all, vat_ref[...])[1], A_HEADS)
    outs_b = []
    for pr in range(B_HEADS // 2):
        s = jnp.concatenate([jnp.dot(kb_ref[:, hd * LANES:(hd + 1) * LANES], qbt_ref[hd * LANES:(hd + 1) * LANES, :],
                                     preferred_element_type=F32) for hd in (2 * pr, 2 * pr + 1)], axis=1)
        outs_b += split_heads(_first_scores(s, vbt_ref[pr * LANES:(pr + 1) * LANES, :])[1], 2)
    _write_heads(outs_a, outs_b, oa_ref, ob_ref)


def _attn_ctx(qat, ka, vat, qbt, kb, vbt):
    t = ka.shape[0]
    tok = lambda w: pl.BlockSpec((SEQ, w), lambda b: (b, 0))
    tok_t = lambda w: pl.BlockSpec((w, SEQ), lambda b: (0, b))
    return pl.pallas_call(
        _attn_ctx_kernel,
        out_shape=(jax.ShapeDtypeStruct((t, A_Q), BF16), jax.ShapeDtypeStruct((t, B_HEADS * B_V), BF16)),
        grid=(t // SEQ,),
        in_specs=[tok_t(A_Q), tok(A_KV), tok_t(A_KV), tok_t(QB_W), tok(QB_W), tok_t(B_HEADS * B_V)],
        out_specs=(tok(A_Q), tok(B_HEADS * B_V)),
        compiler_params=_cparams(("parallel",)),
        name="attn_ctx",
    )(qat, ka, vat, qbt, kb, vbt)


def _attn_lat_kernel(is_gqa, *refs):
    if is_gqa:
        q0_ref, q1_ref, k_ref, vt_ref, kc_ref, vct_ref, o_ref, s_ref, acc_ref = refs
        zero = jnp.zeros((HEAD_DIM, TQ), BF16)
        q_slots = (jnp.concatenate([q0_ref[...], zero], axis=0), jnp.concatenate([zero, q1_ref[...]], axis=0))
        key_cols = (0, 0)
    else:
        q_ref, k_ref, vt_ref, kc_ref, vct_ref, o_ref, s_ref, acc_ref = refs
        q_slots = (q_ref[0:LANES, :], q_ref[LANES:2 * LANES, :])
        key_cols = (0, LANES)
    n_tiles = DEC_SEQ // TK
    head_cut, tail_cut = TK // 4, TK - TK // 4
    plans = ([("cache", 0, PAST_LEN)] + [("lat", j * TK, TK) for j in range(n_tiles)],
             [("lat", 0, head_cut)] + [("lat", head_cut + j * TK, TK) for j in range(n_tiles - 1)]
             + [("lat", DEC_SEQ - tail_cut, tail_cut), ("cache", 0, PAST_LEN)])

    def keys(seg, c0):
        src, k0, n = seg
        return (kc_ref if src == "cache" else k_ref)[k0:k0 + n, c0:c0 + LANES]

    def vals(seg):
        src, k0, n = seg
        return (vct_ref if src == "cache" else vt_ref)[:, k0:k0 + n]

    row_max = [None, None]
    for hh in range(2):
        seg = plans[hh][0]
        s_ref[hh, 0, 0:seg[2], :] = jnp.dot(keys(seg, key_cols[hh]), q_slots[hh], preferred_element_type=F32)
    for j in range(max(len(plan) for plan in plans)):
        for hh in range(2):
            plan = plans[hh]
            if j >= len(plan):
                continue
            if j + 1 < len(plan):
                nxt = plan[j + 1]
                s_ref[hh, (j + 1) % 2, 0:nxt[2], :] = jnp.dot(keys(nxt, key_cols[hh]), q_slots[hh],
                                                              preferred_element_type=F32)
            seg = plan[j]
            s = s_ref[hh, j % 2, 0:seg[2], :]
            if j == 0:
                row_max[hh], acc = _first_scores(s, vals(seg))
            else:
                row_max[hh], acc = _next_tile(s, row_max[hh], acc_ref[hh], vals(seg))
            acc_ref[hh] = acc
    o_ref[...] = _merge_halves(_normalised(acc_ref[0]), _normalised(acc_ref[1])).astype(BF16)


def _attn_lat(l, is_gqa, q_t, k, v_t, kc, vc_t):
    t = k.shape[0]
    nq = DEC_SEQ // TQ
    n_pairs = A_REP if is_gqa else B_HEADS // 2
    kw = A_KV if is_gqa else 2 * LANES
    if is_gqa:
        q_specs = [pl.BlockSpec((HEAD_DIM, TQ), lambda b, r, i: (r, b * nq + i)),
                   pl.BlockSpec((HEAD_DIM, TQ), lambda b, r, i: (A_REP + r, b * nq + i))]
        q_args = [q_t, q_t]
        pair_col = lambda r: 0
    else:
        q_specs = [pl.BlockSpec((2 * LANES, TQ), lambda b, r, i: (r, b * nq + i))]
        q_args = [q_t]
        pair_col = lambda r: r
    return pl.pallas_call(
        functools.partial(_attn_lat_kernel, is_gqa),
        out_shape=jax.ShapeDtypeStruct((t, n_pairs * LANES), BF16),
        grid=(DEC_BATCH, n_pairs, nq),
        in_specs=q_specs + [
            pl.BlockSpec((DEC_SEQ, kw), lambda b, r, i: (b, pair_col(r))),
            pl.BlockSpec((LANES, DEC_SEQ), lambda b, r, i: (pair_col(r), b)),
            pl.BlockSpec((None, None, PAST_LEN, kw), lambda b, r, i: (l, b, 0, pair_col(r))),
            pl.BlockSpec((None, None, LANES, PAST_LEN), lambda b, r, i: (l, b, pair_col(r), 0)),
        ],
        out_specs=pl.BlockSpec((TQ, LANES), lambda b, r, i: (b * nq + i, r)),
        scratch_shapes=[pltpu.VMEM((2, 2, TK, TQ), F32), pltpu.VMEM((2, ACC_ROWS, TQ), F32)],
        compiler_params=_cparams(("parallel", "parallel", "parallel")),
        name="attn_lat_gqa" if is_gqa else "attn_lat_mla",
    )(*q_args, k, v_t, kc, vc_t)


HALO = SUBLANES
N_PAIRS = C_HEADS // 2
HEADS_PER_GROUP = C_HEADS // C_GROUPS


def _ssd_kernel(n_chunks, has_h0, xbc_ref, misc_ref, cw_ref, cb_ref, alog_ref, dvec_ref, *rest):
    if has_h0:
        h0_ref, y_ref, hout_ref, stf_ref, stb_ref, newb_ref, cm_ref, ecx_ref = rest
        stf_ref[...] = h0_ref[0]
        stb_ref[...] = h0_ref[1]
    else:
        y_ref, hout_ref, stf_ref, stb_ref, newb_ref, cm_ref, ecx_ref = rest
        stf_ref[...] = jnp.zeros_like(stf_ref)
        stb_ref[...] = jnp.zeros_like(stb_ref)
    seq_len = n_chunks * C_CHUNK
    row = lax.broadcasted_iota(jnp.int32, (C_CHUNK, LANES), 0)
    lane = _lane_iota((C_CHUNK, LANES))
    lo = lane < C_HEAD_DIM
    lane_group = (lane >= C_STATE).astype(jnp.int32)
    row_group = (row >= C_STATE).astype(jnp.int32)
    causal = (row >= lane, row <= lane)
    neg_a = -jnp.exp(alog_ref[...])
    dt_lanes = (lane >= DT_LANE) & (lane < DT_LANE + 2 * C_HEADS)

    def lane_of(direction, hd):
        return DT_LANE + C_HEADS * direction + hd

    def pair_cols(v, direction, heads):
        j0, j1 = lane_of(direction, heads[0]), lane_of(direction, heads[1])
        return jnp.where(lo[0:v.shape[0], :], v[:, j0:j0 + 1], v[:, j1:j1 + 1])

    def pass1(c, carry):
        start = pl.multiple_of(c * C_CHUNK, C_CHUNK)
        prev0 = pl.multiple_of(jnp.maximum(start - HALO, 0), HALO)
        next0 = pl.multiple_of(jnp.minimum(start + C_CHUNK, seq_len - HALO), HALO)
        keep_prev = jnp.where(c > 0, 1.0, 0.0).astype(F32)
        keep_next = jnp.where(c < n_chunks - 1, 1.0, 0.0).astype(F32)
        rows_c = pl.ds(start, C_CHUNK)

        ue = jnp.concatenate([xbc_ref[pl.ds(prev0, HALO), :] * keep_prev, xbc_ref[rows_c, :],
                              xbc_ref[pl.ds(next0, HALO), :] * keep_next], axis=0)
        acc = jnp.broadcast_to(cb_ref[...], (C_CHUNK, C_CONV_CH))
        for k in range(C_CONV):
            sh = (C_CONV // 2 - k) % ue.shape[0]
            r = ue if sh == 0 else pltpu.roll(ue, sh, axis=0)
            acc = acc + r[HALO:HALO + C_CHUNK, :] * cw_ref[k:k + 1, :]
        xc = acc * jax.nn.sigmoid(acc)
        xs = xc[:, 0:C_INNER]
        bm = xc[:, C_INNER:C_INNER + LANES]
        cm = xc[:, C_INNER + LANES:C_INNER + 2 * LANES]
        bmt = bm.T.astype(BF16)
        cm16 = cm.astype(BF16)
        g_mats = [jnp.dot(jnp.where(lane_group == g, cm, 0.0).astype(BF16), bmt, preferred_element_type=F32)
                  for g in range(C_GROUPS)]

        dtv = jnp.where(dt_lanes, misc_ref[rows_c, :], 0.0)
        dta = dtv * neg_a
        cum = dta
        step = 1
        while step < C_CHUNK:
            cum = cum + jnp.where(row >= step, pltpu.roll(cum, step, axis=0), 0.0)
            step *= 2
        tot = cum[C_CHUNK - 1:C_CHUNK, :]
        cxs = (cum, tot - cum + dta)
        cxts = (cxs[0].T, cxs[1].T)
        dtt = dtv.T
        wsts = tuple(jnp.exp(tot - cx) * dtv for cx in cxs)
        ecx_f = jnp.exp(cxs[0])
        cdec = jnp.exp(tot)
        cm_ref[c] = cm16
        ecx_ref[c, 0:C_CHUNK, :] = jnp.exp(cxs[1])
        ecx_ref[c, C_CHUNK:C_CHUNK + SUBLANES, :] = jnp.broadcast_to(cdec, (SUBLANES, LANES))

        for pr in range(N_PAIRS):
            heads = (2 * pr, 2 * pr + 1)
            groups = tuple(hd // HEADS_PER_GROUP for hd in heads)
            xs_pair = xs[:, pr * LANES:(pr + 1) * LANES]
            xs16 = xs_pair.astype(BF16)
            own = row_group == jnp.where(lo, groups[0], groups[1])
            y = xs_pair * dvec_ref[:, pr * LANES:(pr + 1) * LANES]
            for direction in range(2):
                yd = []
                for hd in heads:
                    j = lane_of(direction, hd)
                    seg = jnp.where(causal[direction], cxs[direction][:, j:j + 1] - cxts[direction][j:j + 1, :],
                                    -jnp.inf)
                    sc = g_mats[hd // HEADS_PER_GROUP] * jnp.exp(seg) * dtt[j:j + 1, :]
                    yd.append(jnp.dot(sc.astype(BF16), xs16, preferred_element_type=F32))
                y = y + _merge_halves(yd[0], yd[1])
                xw = xs_pair * pair_cols(wsts[direction], direction, heads)
                new = jnp.where(own, jnp.dot(bmt, xw.astype(BF16), preferred_element_type=F32), 0.0)
                if direction == 0:
                    st = stf_ref[pr]
                    y_off = jnp.dot(cm16, st.astype(BF16), preferred_element_type=F32)
                    y = y + y_off * pair_cols(ecx_f, 0, heads)
                    stf_ref[pr] = st * pair_cols(cdec, 0, heads) + new
                else:
                    newb_ref[c, pr] = new
            y_ref[rows_c, pr * LANES:(pr + 1) * LANES] = y
        return carry

    def pass2(i, carry):
        c = n_chunks - 1 - i
        rows_c = pl.ds(pl.multiple_of(c * C_CHUNK, C_CHUNK), C_CHUNK)
        cm16 = cm_ref[c]
        ecx_b = ecx_ref[c, 0:C_CHUNK, :]
        cdec = ecx_ref[c, C_CHUNK:C_CHUNK + 1, :]
        for pr in range(N_PAIRS):
            heads = (2 * pr, 2 * pr + 1)
            st = stb_ref[pr]
            y_off = jnp.dot(cm16, st.astype(BF16), preferred_element_type=F32)
            cols = pl.ds(pr * LANES, LANES)
            y_ref[rows_c, cols] = y_ref[rows_c, cols] + y_off * pair_cols(ecx_b, 1, heads)
            stb_ref[pr] = st * pair_cols(cdec, 1, heads) + newb_ref[c, pr]
        return carry

    unroll = n_chunks <= 2
    lax.fori_loop(0, n_chunks, pass1, 0, unroll=unroll)
    lax.fori_loop(0, n_chunks, pass2, 0, unroll=unroll)

    for direction, st_ref in enumerate((stf_ref, stb_ref)):
        for pr in range(N_PAIRS):
            st_t = st_ref[pr].T
            for hh, hd in enumerate((2 * pr, 2 * pr + 1)):
                g = hd // HEADS_PER_GROUP
                hout_ref[direction, hd] = st_t[hh * C_HEAD_DIM:(hh + 1) * C_HEAD_DIM,
                                               g * C_STATE:(g + 1) * C_STATE]


def _ssd(l, xbc, misc, lw, h0, seq_len):
    t = xbc.shape[0]
    nb = t // seq_len
    nc = seq_len // C_CHUNK
    has_h0 = h0 is not None
    const = lambda shape: pl.BlockSpec((None,) + shape, lambda b: (l,) + (0,) * len(shape))
    in_specs = [
        pl.BlockSpec((seq_len, C_CONV_CH), lambda b: (b, 0)),
        pl.BlockSpec((seq_len, LANES), lambda b: (b, 0)),
        const((C_CONV, C_CONV_CH)), const((1, C_CONV_CH)), const((1, LANES)), const((1, C_INNER)),
    ]
    args = [xbc, misc, lw["conv_w"], lw["conv_b"], lw["a_log"], lw["d_vec"]]
    if has_h0:
        in_specs.append(pl.BlockSpec((None, None, 2, N_PAIRS, LANES, LANES), lambda b: (b, l, 0, 0, 0, 0)))
        args.append(h0)
    return pl.pallas_call(
        functools.partial(_ssd_kernel, nc, has_h0),
        out_shape=(jax.ShapeDtypeStruct((t, C_INNER), F32),
                   jax.ShapeDtypeStruct((nb, 2, C_HEADS, C_HEAD_DIM, C_STATE), F32)),
        grid=(nb,),
        in_specs=in_specs,
        out_specs=(pl.BlockSpec((seq_len, C_INNER), lambda b: (b, 0)),
                   pl.BlockSpec((None, 2, C_HEADS, C_HEAD_DIM, C_STATE), lambda b: (b, 0, 0, 0, 0))),
        scratch_shapes=[pltpu.VMEM((N_PAIRS, LANES, LANES), F32), pltpu.VMEM((N_PAIRS, LANES, LANES), F32),
                        pltpu.VMEM((nc, N_PAIRS, LANES, LANES), F32), pltpu.VMEM((nc, C_CHUNK, LANES), BF16),
                        pltpu.VMEM((nc, C_CHUNK + SUBLANES, LANES), F32)],
        compiler_params=_cparams(("parallel",)),
        name="ssd_lat" if has_h0 else "ssd_ctx",
    )(*args)


def _post_kernel(x_ref, mod_ref, oa_ref, ob_ref, y_ref, z_ref, gssm_ref, wout_ref, gmix_ref, gffn_ref,
                 gffo_ref, w1_ref, w2_ref, o_ref):
    x = x_ref[...]
    gate1 = mod_ref[:, 2 * D_MODEL:3 * D_MODEL]
    shift2 = mod_ref[:, 3 * D_MODEL:4 * D_MODEL]
    scale2 = mod_ref[:, 4 * D_MODEL:5 * D_MODEL]
    gate2 = mod_ref[:, 5 * D_MODEL:6 * D_MODEL]
    z = z_ref[...]
    oc = _rms(y_ref[...] * (z * jax.nn.sigmoid(z)), gssm_ref[...])
    mix = jnp.concatenate([oa_ref[...], ob_ref[...], oc.astype(BF16)], axis=1)
    out = jnp.dot(mix, wout_ref[...], preferred_element_type=F32)
    x = x + gate1 * _rms(out, gmix_ref[...])
    h = _rms(x, gffn_ref[...]) * (1.0 + scale2) + shift2
    u = jnp.maximum(_bdot(h, w1_ref[...]), 0.0)
    f = _bdot(u * u, w2_ref[...])
    o_ref[...] = x + gate2 * _rms(f, gffo_ref[...])


def _post(l, is_ctx, x2d, mod, oa, ob, y, z, lw):
    t = x2d.shape[0]
    per_seq = DEC_SEQ // TM
    mod_map = (lambda i: (l, 0, 0, 0)) if is_ctx else (lambda i: (l, 1 + i // per_seq, 0, 0))
    tok = lambda w: pl.BlockSpec((TM, w), lambda i: (i, 0))
    const = lambda shape: pl.BlockSpec((None,) + shape, lambda i: (l,) + (0,) * len(shape),
                                       pipeline_mode=pl.Buffered(1))
    return pl.pallas_call(
        _post_kernel,
        out_shape=jax.ShapeDtypeStruct((t, D_MODEL), F32),
        grid=(t // TM,),
        in_specs=[
            tok(D_MODEL),
            pl.BlockSpec((None, None, 1, 6 * D_MODEL), mod_map),
            tok(A_Q), tok(B_HEADS * B_V), tok(C_INNER), tok(C_INNER),
            const((1, C_INNER)), const((D_MODEL, D_MODEL)), const((1, D_MODEL)), const((1, D_MODEL)),
            const((1, D_MODEL)), const((D_MODEL, D_FF)), const((D_FF, D_MODEL)),
        ],
        out_specs=tok(D_MODEL),
        compiler_params=_cparams(("parallel",)),
        name="post_ctx" if is_ctx else "post_lat",
    )(x2d, mod, oa, ob, y, z, lw["g_ssm"], lw["w_out"], lw["g_mix"], lw["g_ffn"], lw["g_ffo"],
      lw["w1"], lw["w2"])


def _rope_tables():
    pos = np.arange(DEC_SEQ)
    axis_pos = np.stack([(pos // GRID_W), (pos % GRID_W)], axis=0).astype(np.float32)

    def pattern(rot_dim):
        half = rot_dim // 2
        quarter = half // 2
        inv = (1.0 / (np.float32(ROPE_THETA) ** (np.arange(0, half, 2, dtype=np.float32) / np.float32(half))))
        inv = inv.astype(np.float32)
        dd = np.arange(rot_dim)
        ang = (axis_pos[dd // half].T * inv[dd % quarter][None, :]).astype(np.float32)
        sign = np.where((dd % half) < quarter, -1.0, 1.0).astype(np.float32)
        return np.cos(ang).astype(np.float32), (np.sin(ang) * sign).astype(np.float32)

    ca, sa = pattern(HEAD_DIM)
    ca = np.tile(ca, (1, LANES // HEAD_DIM))
    sa = np.tile(sa, (1, LANES // HEAD_DIM))
    cb32, sb32 = pattern(B_ROPE)
    tail = LANES - KPE_LANE - B_ROPE
    cb = np.concatenate([np.ones((DEC_SEQ, KPE_LANE), np.float32), cb32, np.ones((DEC_SEQ, tail), np.float32)], 1)
    sb = np.concatenate([np.zeros((DEC_SEQ, KPE_LANE), np.float32), sb32, np.zeros((DEC_SEQ, tail), np.float32)], 1)
    ident_c = np.ones((TM, LANES), np.float32)
    ident_s = np.zeros((TM, LANES), np.float32)
    return tuple(jnp.asarray(np.concatenate([tb, idt], axis=0))
                 for tb, idt in ((ca, ident_c), (sa, ident_s), (cb, ident_c), (sb, ident_s)))


def _prep_weights(p):
    w_in = p["w_in"]
    o_kpe = A_Q + 2 * A_KV + B_Q_RANK + B_KV_RANK
    o_z = o_kpe + B_ROPE
    o_xbc = o_z + C_INNER
    o_dt = o_xbc + C_CONV_CH
    zeros = lambda n: jnp.zeros((DEPTH, D_MODEL, n), F32)
    w_in_p = jnp.concatenate(
        [w_in[..., :o_kpe], w_in[..., o_z:o_dt], zeros(KPE_LANE), w_in[..., o_kpe:o_z], w_in[..., o_dt:],
         zeros(LANES - DT_LANE - 2 * C_HEADS)], axis=-1).astype(BF16)

    w_qb = p["mla_w_qb"].reshape(DEPTH, B_Q_RANK, B_HEADS, B_NOPE + B_ROPE)
    w_qb_p = jnp.pad(w_qb, ((0, 0),) * 3 + ((0, LANES - B_NOPE - B_ROPE),)).reshape(DEPTH, B_Q_RANK, QB_W)
    w_kvb = p["mla_w_kvb"].reshape(DEPTH, B_KV_RANK, B_HEADS, B_NOPE + B_V)
    w_k = jnp.pad(w_kvb[..., :B_NOPE], ((0, 0),) * 3 + ((0, LANES - B_NOPE),)).reshape(DEPTH, B_KV_RANK, QB_W)
    w_v = w_kvb[..., B_NOPE:].reshape(DEPTH, B_KV_RANK, B_HEADS * B_V)

    w_out = p["w_out"]
    order = [r + g * A_REP for r in range(A_REP) for g in range(A_KV_HEADS)]
    w_out_p = jnp.concatenate([w_out[:, hd * HEAD_DIM:(hd + 1) * HEAD_DIM] for hd in order] + [w_out[:, A_Q:]],
                              axis=1).astype(BF16)

    row = lambda v: v.reshape(DEPTH, 1, -1)
    pair_gain = lambda g: row(jnp.tile(g, (1, LANES // HEAD_DIM)))
    lane_vec = lambda v: jnp.pad(v.reshape(DEPTH, 1, -1), ((0, 0), (0, 0), (DT_LANE, LANES - DT_LANE - 2 * C_HEADS)))
    return {
        "g_pre": row(p["norm_mix_pre"]),
        "w_in": w_in_p,
        "g_q": pair_gain(p["attn_q_norm"]),
        "g_k": pair_gain(p["attn_k_norm"]),
        "g_qc": row(p["mla_q_norm"]),
        "w_qb": w_qb_p.astype(BF16),
        "g_kv": row(p["mla_kv_norm"]),
        "w_k": w_k.astype(BF16),
        "w_v": w_v.astype(BF16),
        "conv_w": jnp.swapaxes(p["ssm_conv_w"], 1, 2),
        "conv_b": row(p["ssm_conv_b"]),
        "dt_bias": lane_vec(p["ssm_dt_bias"]),
        "a_log": lane_vec(p["ssm_a_log"]),
        "d_vec": row(jnp.repeat(p["ssm_d"], C_HEAD_DIM, axis=1)),
        "g_ssm": row(p["ssm_norm"]),
        "w_out": w_out_p,
        "g_mix": row(p["norm_mix_post"]),
        "g_ffn": row(p["norm_ffn_pre"]),
        "g_ffo": row(p["norm_ffn_post"]),
        "w1": p["w_ffn1"].astype(BF16),
        "w2": p["w_ffn2"].astype(BF16),
    }


def _state_to_pairs(h):
    ht = jnp.swapaxes(h, -1, -2)
    zero = jnp.zeros_like(ht[..., 0, :, :])
    pairs = []
    for pr in range(N_PAIRS):
        heads = (2 * pr, 2 * pr + 1)
        row_blocks = [jnp.concatenate([ht[..., hd, :, :] if hd // HEADS_PER_GROUP == g else zero for hd in heads],
                                      axis=-1) for g in range(C_GROUPS)]
        pairs.append(jnp.concatenate(row_blocks, axis=-2))
    return jnp.stack(pairs, axis=-3)


def kernel(x_prompt, x_sample, cache_attn_k, cache_attn_v, cache_mla_ckv, cache_mla_kpe, state_ssm, c, c_ctx, norm_mix_pre, norm_mix_post, norm_ffn_pre, norm_ffn_post, w_mod, b_mod, w_in, attn_q_norm, attn_k_norm, mla_q_norm, mla_w_qb, mla_kv_norm, mla_w_kvb, ssm_conv_w, ssm_conv_b, ssm_dt_bias, ssm_a_log, ssm_d, ssm_norm, w_out, w_ffn1, w_ffn2):
    p = dict(norm_mix_pre=norm_mix_pre, norm_mix_post=norm_mix_post, norm_ffn_pre=norm_ffn_pre,
             norm_ffn_post=norm_ffn_post, w_in=w_in, attn_q_norm=attn_q_norm, attn_k_norm=attn_k_norm,
             mla_q_norm=mla_q_norm, mla_w_qb=mla_w_qb, mla_kv_norm=mla_kv_norm, mla_w_kvb=mla_w_kvb,
             ssm_conv_w=ssm_conv_w, ssm_conv_b=ssm_conv_b, ssm_dt_bias=ssm_dt_bias, ssm_a_log=ssm_a_log,
             ssm_d=ssm_d, ssm_norm=ssm_norm, w_out=w_out, w_ffn1=w_ffn1, w_ffn2=w_ffn2)
    lw = _prep_weights(p)
    tabs = _rope_tables()

    cvec = jnp.concatenate([c_ctx[None, :], c, jnp.zeros((MOD_ROWS - 1 - DEC_BATCH, D_MODEL), F32)], axis=0)
    mod = _modulation(cvec, w_mod, b_mod).reshape(DEPTH, MOD_ROWS, 1, 6 * D_MODEL)

    kpe_p = jnp.pad(cache_mla_kpe, ((0, 0),) * 3 + ((KPE_LANE, LANES - KPE_LANE - B_ROPE),))
    kc, vct, kbc, vbct = _cache_prep(
        cache_attn_k.reshape(DEC_BATCH, DEPTH, PAST_LEN, A_KV), cache_attn_v.reshape(DEC_BATCH, DEPTH, PAST_LEN, A_KV),
        cache_mla_ckv, kpe_p, lw["w_k"], lw["w_v"])
    h0_lat = _state_to_pairs(state_ssm)

    xp = x_prompt.reshape(BATCH * SEQ, D_MODEL)
    xs = x_sample.reshape(DEC_BATCH * DEC_SEQ, D_MODEL)
    new_k, new_v, new_ckv, new_kpe, new_ssm = [], [], [], [], []
    for l in range(DEPTH):
        qat, ka, vat, qbt, kb, vbt, z, xbc, misc, kf, vf, ckvf = _pre(l, True, xp, mod, lw, tabs)
        oa, ob = _attn_ctx(qat, ka, vat, qbt, kb, vbt)
        y, hfin = _ssd(l, xbc, misc, lw, None, SEQ)
        xp = _post(l, True, xp, mod, oa, ob, y, z, lw)
        new_k.append(kf.reshape(BATCH, SEQ, A_KV_HEADS, HEAD_DIM))
        new_v.append(vf.reshape(BATCH, SEQ, A_KV_HEADS, HEAD_DIM))
        new_ckv.append(ckvf.reshape(BATCH, SEQ, B_KV_RANK))
        new_kpe.append(misc[:, KPE_LANE:KPE_LANE + B_ROPE].reshape(BATCH, SEQ, B_ROPE))
        new_ssm.append(hfin)
        qat, ka, vat, qbt, kb, vbt, z, xbc, misc = _pre(l, False, xs, mod, lw, tabs)
        oa = _attn_lat(l, True, qat, ka, vat, kc, vct)
        ob = _attn_lat(l, False, qbt, kb, vbt, kbc, vbct)
        y, _ = _ssd(l, xbc, misc, lw, h0_lat, DEC_SEQ)
        xs = _post(l, False, xs, mod, oa, ob, y, z, lw)
    return (xp.reshape(BATCH, SEQ, D_MODEL), xs.reshape(DEC_BATCH, DEC_SEQ, D_MODEL),
            jnp.stack(new_k, axis=1), jnp.stack(new_v, axis=1), jnp.stack(new_ckv, axis=1),
            jnp.stack(new_kpe, axis=1), jnp.stack(new_ssm, axis=1))
```

```python
import functools
import math

import numpy as np
import jax
import jax.numpy as jnp
from jax import lax
from jax.experimental import pallas as pl
from jax.experimental.pallas import tpu as pltpu

F32 = jnp.float32
BF16 = jnp.bfloat16

D_MODEL = 1024
BATCH = 16
SEQ = 256
DEPTH = 4
DEC_BATCH = 2
DEC_SEQ = 4096
PAST_LEN = 256
GRID_W = 64
HEAD_DIM = 64
A_HEADS = 6
A_KV_HEADS = 2
A_REP = A_HEADS // A_KV_HEADS
B_HEADS = 4
B_Q_RANK = 256
B_KV_RANK = 128
B_NOPE = 64
B_ROPE = 32
B_V = 64
C_HEADS = 6
C_HEAD_DIM = 64
C_INNER = C_HEADS * C_HEAD_DIM
C_GROUPS = 2
C_STATE = 64
C_CONV = 5
C_CHUNK = 128
C_CONV_CH = C_INNER + 2 * C_GROUPS * C_STATE
D_FF = 4 * D_MODEL
A_Q = A_HEADS * HEAD_DIM
A_KV = A_KV_HEADS * HEAD_DIM
ROPE_THETA = 10000.0
EPS = 1e-6
LOG2E = math.log2(math.e)

LANES = 128
SUBLANES = 8
VMEM_LIMIT = 56 * 1024 * 1024

OFF_QA = 0
OFF_KA = OFF_QA + A_Q
OFF_VA = OFF_KA + A_KV
OFF_QC = OFF_VA + A_KV
OFF_KVC = OFF_QC + B_Q_RANK
OFF_Z = OFF_KVC + B_KV_RANK
OFF_XBC = OFF_Z + C_INNER
OFF_MISC = OFF_XBC + C_CONV_CH
IN_COLS_P = OFF_MISC + LANES
KPE_LANE = 64
DT_LANE = KPE_LANE + B_ROPE
QB_W = B_HEADS * LANES

TM = 512
TQ = 512
TK = 1024


def _cparams(sem):
    return pltpu.CompilerParams(dimension_semantics=sem, vmem_limit_bytes=VMEM_LIMIT)


def _rms(x, g):
    return x * lax.rsqrt(jnp.mean(x * x, axis=-1, keepdims=True) + EPS) * g


def _bdot(a, b):
    return jnp.dot(a.astype(BF16), b.astype(BF16), preferred_element_type=F32)


def _lane_iota(shape):
    return lax.broadcasted_iota(jnp.int32, shape, len(shape) - 1)


def _swap_lanes(x, dist):
    lane = _lane_iota(x.shape)
    fwd = pltpu.roll(x, LANES - dist, axis=1)
    bwd = pltpu.roll(x, dist, axis=1)
    return jnp.where((lane % (2 * dist)) < dist, fwd, bwd)


def _rope(x, cos, sin, dist):
    return x * cos + _swap_lanes(x, dist) * sin


def _head_pair_rms(col, gain):
    lo = _lane_iota(col.shape) < HEAD_DIM
    c2 = col * col
    s_lo = jnp.sum(jnp.where(lo, c2, 0.0), axis=-1, keepdims=True)
    s_hi = jnp.sum(jnp.where(lo, 0.0, c2), axis=-1, keepdims=True)
    return col * lax.rsqrt(jnp.where(lo, s_lo, s_hi) * (1.0 / HEAD_DIM) + EPS) * gain


MOD_ROWS = SUBLANES
MOD_TN = 1536


def _mod_kernel(c_ref, w_ref, b_ref, o_ref):
    c = c_ref[...]
    o_ref[...] = _bdot(c * jax.nn.sigmoid(c), w_ref[...]) + b_ref[...]


def _modulation(cvec, w_mod, b_mod):
    return pl.pallas_call(
        _mod_kernel,
        out_shape=jax.ShapeDtypeStruct((DEPTH, MOD_ROWS, 6 * D_MODEL), F32),
        grid=(DEPTH, 6 * D_MODEL // MOD_TN),
        in_specs=[
            pl.BlockSpec((MOD_ROWS, D_MODEL), lambda l, j: (0, 0)),
            pl.BlockSpec((None, D_MODEL, MOD_TN), lambda l, j: (l, 0, j)),
            pl.BlockSpec((None, 1, MOD_TN), lambda l, j: (l, 0, j)),
        ],
        out_specs=pl.BlockSpec((None, MOD_ROWS, MOD_TN), lambda l, j: (l, 0, j)),
        compiler_params=_cparams(("parallel", "parallel")),
        name="modulation",
    )(cvec, w_mod, b_mod.reshape(DEPTH, 1, 6 * D_MODEL))


def _cache_kernel(k_ref, v_ref, ckv_ref, kpe_ref, wk_ref, wv_ref, ko_ref, vt_ref, kb_ref, vbt_ref):
    ko_ref[...] = k_ref[...].astype(BF16)
    vt_ref[...] = v_ref[...].T.astype(BF16)
    ckv = ckv_ref[...]
    kn = _bdot(ckv, wk_ref[...])
    kpe = kpe_ref[...]
    for h in range(B_HEADS):
        kb_ref[:, h * LANES:(h + 1) * LANES] = (kn[:, h * LANES:(h + 1) * LANES] + kpe).astype(BF16)
    vbt_ref[...] = _bdot(ckv, wv_ref[...]).T.astype(BF16)


def _cache_prep(cache_k, cache_v, cache_ckv, cache_kpe_p, wk_p, wv_p):
    spec_tok = lambda w: pl.BlockSpec((None, None, PAST_LEN, w), lambda l, b: (b, l, 0, 0))
    spec_t = lambda w: pl.BlockSpec((None, None, w, PAST_LEN), lambda l, b: (l, b, 0, 0))
    spec_o = lambda w: pl.BlockSpec((None, None, PAST_LEN, w), lambda l, b: (l, b, 0, 0))
    return pl.pallas_call(
        _cache_kernel,
        out_shape=(
            jax.ShapeDtypeStruct((DEPTH, DEC_BATCH, PAST_LEN, A_KV), BF16),
            jax.ShapeDtypeStruct((DEPTH, DEC_BATCH, A_KV, PAST_LEN), BF16),
            jax.ShapeDtypeStruct((DEPTH, DEC_BATCH, PAST_LEN, QB_W), BF16),
            jax.ShapeDtypeStruct((DEPTH, DEC_BATCH, B_HEADS * B_V, PAST_LEN), BF16),
        ),
        grid=(DEPTH, DEC_BATCH),
        in_specs=[
            spec_tok(A_KV), spec_tok(A_KV), spec_tok(B_KV_RANK), spec_tok(LANES),
            pl.BlockSpec((None, B_KV_RANK, QB_W), lambda l, b: (l, 0, 0)),
            pl.BlockSpec((None, B_KV_RANK, B_HEADS * B_V), lambda l, b: (l, 0, 0)),
        ],
        out_specs=(spec_o(A_KV), spec_t(A_KV), spec_o(QB_W), spec_t(B_HEADS * B_V)),
        compiler_params=_cparams(("parallel", "parallel")),
        name="cache_prep",
    )(cache_k, cache_v, cache_ckv, cache_kpe_p, wk_p, wv_p)


PRE_SUB = 2


def _pre_kernel(is_ctx, *refs):
    for sub in range(PRE_SUB):
        _pre_rows(is_ctx, slice(sub * (TM // PRE_SUB), (sub + 1) * (TM // PRE_SUB)), *refs)


def _pre_rows(is_ctx, rows, x_ref, mod_ref, gpre_ref, win_ref, gq_ref, gk_ref, gqc_ref, wqb_ref, gkv_ref,
              wk_ref, wv_ref, dtb_ref, ca_ref, sa_ref, cb_ref, sb_ref, *outs):
    if is_ctx:
        (qat_ref, ka_ref, vat_ref, qbt_ref, kb_ref, vbt_ref, z_ref, xbc_ref, misc_ref,
         kf_ref, vf_ref, ckvf_ref) = outs
    else:
        qat_ref, ka_ref, vat_ref, qbt_ref, kb_ref, vbt_ref, z_ref, xbc_ref, misc_ref = outs
    x = x_ref[rows, :]
    shift1 = mod_ref[:, 0:D_MODEL]
    scale1 = mod_ref[:, D_MODEL:2 * D_MODEL]
    h = _rms(x, gpre_ref[...]) * (1.0 + scale1) + shift1
    proj = _bdot(h, win_ref[...])

    ca, sa, cb, sb = ca_ref[rows, :], sa_ref[rows, :], cb_ref[rows, :], sb_ref[rows, :]
    lane = _lane_iota((x.shape[0], LANES))

    gq = gq_ref[...]
    for cidx in range(A_Q // LANES):
        col = proj[:, OFF_QA + cidx * LANES:OFF_QA + (cidx + 1) * LANES]
        qn = _rope(_head_pair_rms(col, gq), ca, sa, HEAD_DIM // 4) * (HEAD_DIM ** -0.5 * LOG2E)
        qat_ref[cidx * LANES:(cidx + 1) * LANES, rows] = qn.T.astype(BF16)

    kn = _head_pair_rms(proj[:, OFF_KA:OFF_KA + A_KV], gk_ref[...])
    vcol = proj[:, OFF_VA:OFF_VA + A_KV]
    if is_ctx:
        kf_ref[rows, :] = kn
        vf_ref[rows, :] = vcol
    ka_ref[rows, :] = _rope(kn, ca, sa, HEAD_DIM // 4).astype(BF16)
    vat_ref[:, rows] = vcol.T.astype(BF16)

    qc = _rms(proj[:, OFF_QC:OFF_QC + B_Q_RANK], gqc_ref[...])
    qb = _bdot(qc, wqb_ref[...])
    for hd in range(B_HEADS):
        col = qb[:, hd * LANES:(hd + 1) * LANES]
        col = _rope(col, cb, sb, B_ROPE // 4) * ((B_NOPE + B_ROPE) ** -0.5 * LOG2E)
        qbt_ref[hd * LANES:(hd + 1) * LANES, rows] = col.T.astype(BF16)

    ckv = _rms(proj[:, OFF_KVC:OFF_KVC + B_KV_RANK], gkv_ref[...])
    if is_ctx:
        ckvf_ref[rows, :] = ckv
    knope = _bdot(ckv, wk_ref[...])
    misc = proj[:, OFF_MISC:OFF_MISC + LANES]
    kpe = jnp.where((lane >= KPE_LANE) & (lane < KPE_LANE + B_ROPE), _rope(misc, cb, sb, B_ROPE // 4), 0.0)
    for hd in range(B_HEADS):
        kb_ref[rows, hd * LANES:(hd + 1) * LANES] = (knope[:, hd * LANES:(hd + 1) * LANES] + kpe).astype(BF16)
    vbt_ref[:, rows] = _bdot(ckv, wv_ref[...]).T.astype(BF16)

    x_dt = misc + dtb_ref[...]
    dtv = jnp.maximum(x_dt, 0.0) + jnp.log(1.0 + jnp.exp(-jnp.abs(x_dt)))
    misc_ref[rows, :] = jnp.where(lane >= DT_LANE, dtv, misc)

    z_ref[rows, :] = proj[:, OFF_Z:OFF_Z + C_INNER]
    xbc_ref[rows, :] = proj[:, OFF_XBC:OFF_XBC + C_CONV_CH]


def _pre(l, is_ctx, x2d, mod, lw, tabs):
    t = x2d.shape[0]
    nt = t // TM
    per_seq = DEC_SEQ // TM
    if is_ctx:
        mod_map = lambda i: (l, 0, 0, 0)
        tab_map = lambda i: (per_seq, 0)
    else:
        mod_map = lambda i: (l, 1 + i // per_seq, 0, 0)
        tab_map = lambda i: (i % per_seq, 0)
    const = lambda shape: pl.BlockSpec((None,) + shape, lambda i: (l,) + (0,) * len(shape))
    tab_spec = pl.BlockSpec((TM, LANES), tab_map)
    tok = lambda w: pl.BlockSpec((TM, w), lambda i: (i, 0))
    tok_t = lambda w: pl.BlockSpec((w, TM), lambda i: (0, i))
    out_shape = [
        jax.ShapeDtypeStruct((A_Q, t), BF16),
        jax.ShapeDtypeStruct((t, A_KV), BF16),
        jax.ShapeDtypeStruct((A_KV, t), BF16),
        jax.ShapeDtypeStruct((QB_W, t), BF16),
        jax.ShapeDtypeStruct((t, QB_W), BF16),
        jax.ShapeDtypeStruct((B_HEADS * B_V, t), BF16),
        jax.ShapeDtypeStruct((t, C_INNER), F32),
        jax.ShapeDtypeStruct((t, C_CONV_CH), F32),
        jax.ShapeDtypeStruct((t, LANES), F32),
    ]
    out_specs = [tok_t(A_Q), tok(A_KV), tok_t(A_KV), tok_t(QB_W), tok(QB_W), tok_t(B_HEADS * B_V),
                 tok(C_INNER), tok(C_CONV_CH), tok(LANES)]
    if is_ctx:
        out_shape += [jax.ShapeDtypeStruct((t, A_KV), F32)] * 2 + [jax.ShapeDtypeStruct((t, B_KV_RANK), F32)]
        out_specs += [tok(A_KV), tok(A_KV), tok(B_KV_RANK)]
    return pl.pallas_call(
        functools.partial(_pre_kernel, is_ctx),
        out_shape=tuple(out_shape),
        grid=(nt,),
        in_specs=[
            tok(D_MODEL),
            pl.BlockSpec((None, None, 1, 6 * D_MODEL), mod_map),
            const((1, D_MODEL)),
            const((D_MODEL, IN_COLS_P)),
            const((1, LANES)), const((1, LANES)), const((1, B_Q_RANK)),
            const((B_Q_RANK, QB_W)), const((1, B_KV_RANK)),
            const((B_KV_RANK, QB_W)), const((B_KV_RANK, B_HEADS * B_V)), const((1, LANES)),
            tab_spec, tab_spec, tab_spec, tab_spec,
        ],
        out_specs=tuple(out_specs),
        compiler_params=_cparams(("parallel",)),
        name="pre_ctx" if is_ctx else "pre_lat",
    )(x2d, mod, lw["g_pre"], lw["w_in"], lw["g_q"], lw["g_k"], lw["g_qc"], lw["w_qb"], lw["g_kv"],
      lw["w_k"], lw["w_v"], lw["dt_bias"], *tabs)


def _merge_halves(lo_part, hi_part):
    return jnp.where(_lane_iota(lo_part.shape) < HEAD_DIM, lo_part, hi_part)


def _gqa_query_slot(qat_ref, hd):
    q = qat_ref[hd * HEAD_DIM:(hd + 1) * HEAD_DIM, :]
    zero = jnp.zeros_like(q)
    return jnp.concatenate([q, zero] if hd // A_REP == 0 else [zero, q], axis=0)


ONES_ROWS = 16
ACC_ROWS = LANES + ONES_ROWS


def _with_ones(v_t):
    return jnp.concatenate([v_t, jnp.ones((ONES_ROWS, v_t.shape[1]), BF16)], axis=0)


def _first_tile(k, q_t, v_t):
    return _first_scores(jnp.dot(k, q_t, preferred_element_type=F32), v_t)


def _first_scores(s, v_t):
    m = jnp.max(s, axis=0, keepdims=True)
    p = jnp.exp2(s - m)
    return m, jnp.dot(_with_ones(v_t), p.astype(BF16), preferred_element_type=F32)


def _next_tile(s, m, acc, v_t):
    m_new = jnp.maximum(m, jnp.max(s, axis=0, keepdims=True))
    p = jnp.exp2(s - m_new)
    acc = jnp.exp2(m - m_new) * acc + jnp.dot(_with_ones(v_t), p.astype(BF16), preferred_element_type=F32)
    return m_new, acc


def _normalised(acc):
    return (acc[0:LANES, :] / acc[LANES:LANES + 1, :]).T


def _write_heads(outs_a, outs_b, oa_ref, ob_ref):
    for r in range(A_REP):
        oa_ref[:, r * LANES:(r + 1) * LANES] = _merge_halves(outs_a[r], outs_a[A_REP + r]).astype(BF16)
    for pr in range(B_HEADS // 2):
        ob_ref[:, pr * LANES:(pr + 1) * LANES] = _merge_halves(outs_b[2 * pr], outs_b[2 * pr + 1]).astype(BF16)


def _attn_ctx_kernel(qat_ref, ka_ref, vat_ref, qbt_ref, kb_ref, vbt_ref, oa_ref, ob_ref):
    def split_heads(acc, n):
        o = acc[0:LANES, :] / acc[LANES:LANES + 1, :]
        return [o[:, i * SEQ:(i + 1) * SEQ].T for i in range(n)]

    q_all = jnp.concatenate([_gqa_query_slot(qat_ref, hd) for hd in range(A_HEADS)], axis=1)
    outs_a = split_heads(_first_tile(ka_ref[...], q_all, vat_ref[...])[1], A_HEADS)
    outs_b = []
    for pr in range(B_HEADS // 2):
        s = jnp.concatenate([jnp.dot(kb_ref[:, hd * LANES:(hd + 1) * LANES], qbt_ref[hd * LANES:(hd + 1) * LANES, :],
                                     preferred_element_type=F32) for hd in (2 * pr, 2 * pr + 1)], axis=1)
        outs_b += split_heads(_first_scores(s, vbt_ref[pr * LANES:(pr + 1) * LANES, :])[1], 2)
    _write_heads(outs_a, outs_b, oa_ref, ob_ref)


def _attn_ctx(qat, ka, vat, qbt, kb, vbt):
    t = ka.shape[0]
    tok = lambda w: pl.BlockSpec((SEQ, w), lambda b: (b, 0))
    tok_t = lambda w: pl.BlockSpec((w, SEQ), lambda b: (0, b))
    return pl.pallas_call(
        _attn_ctx_kernel,
        out_shape=(jax.ShapeDtypeStruct((t, A_Q), BF16), jax.ShapeDtypeStruct((t, B_HEADS * B_V), BF16)),
        grid=(t // SEQ,),
        in_specs=[tok_t(A_Q), tok(A_KV), tok_t(A_KV), tok_t(QB_W), tok(QB_W), tok_t(B_HEADS * B_V)],
        out_specs=(tok(A_Q), tok(B_HEADS * B_V)),
        compiler_params=_cparams(("parallel",)),
        name="attn_ctx",
    )(qat, ka, vat, qbt, kb, vbt)


def _attn_lat_kernel(is_gqa, *refs):
    if is_gqa:
        q0_ref, q1_ref, k_ref, vt_ref, kc_ref, vct_ref, o_ref, s_ref, acc_ref = refs
        zero = jnp.zeros((HEAD_DIM, TQ), BF16)
        q_slots = (jnp.concatenate([q0_ref[...], zero], axis=0), jnp.concatenate([zero, q1_ref[...]], axis=0))
        key_cols = (0, 0)
    else:
        q_ref, k_ref, vt_ref, kc_ref, vct_ref, o_ref, s_ref, acc_ref = refs
        q_slots = (q_ref[0:LANES, :], q_ref[LANES:2 * LANES, :])
        key_cols = (0, LANES)
    n_tiles = DEC_SEQ // TK
    head_cut, tail_cut = TK // 4, TK - TK // 4
    plans = ([("cache", 0, PAST_LEN), ("lat", 0, TK // 2)] + [("lat", TK // 2 + j * TK, TK) for j in range(n_tiles - 1)]
             + [("lat", DEC_SEQ - TK // 2, TK // 2)],
             [("lat", 0, head_cut)] + [("lat", head_cut + j * TK, TK) for j in range(n_tiles - 1)]
             + [("lat", DEC_SEQ - tail_cut, tail_cut), ("cache", 0, PAST_LEN)])

    def keys(seg, c0):
        src, k0, n = seg
        return (kc_ref if src == "cache" else k_ref)[k0:k0 + n, c0:c0 + LANES]

    def vals(seg):
        src, k0, n = seg
        return (vct_ref if src == "cache" else vt_ref)[:, k0:k0 + n]

    row_max = [None, None]
    for hh in range(2):
        seg = plans[hh][0]
        s_ref[hh, 0, 0:seg[2], :] = jnp.dot(keys(seg, key_cols[hh]), q_slots[hh], preferred_element_type=F32)
    for j in range(max(len(plan) for plan in plans)):
        for hh in range(2):
            plan = plans[hh]
            if j >= len(plan):
                continue
            if j + 1 < len(plan):
                nxt = plan[j + 1]
                s_ref[hh, (j + 1) % 2, 0:nxt[2], :] = jnp.dot(keys(nxt, key_cols[hh]), q_slots[hh],
                                                              preferred_element_type=F32)
            seg = plan[j]
            s = s_ref[hh, j % 2, 0:seg[2], :]
            if j == 0:
                row_max[hh], acc = _first_scores(s, vals(seg))
            else:
                row_max[hh], acc = _next_tile(s, row_max[hh], acc_ref[hh], vals(seg))
            acc_ref[hh] = acc
    o_ref[...] = _merge_halves(_normalised(acc_ref[0]), _normalised(acc_ref[1])).astype(BF16)


def _attn_lat(l, is_gqa, q_t, k, v_t, kc, vc_t):
    t = k.shape[0]
    nq = DEC_SEQ // TQ
    n_pairs = A_REP if is_gqa else B_HEADS // 2
    kw = A_KV if is_gqa else 2 * LANES
    if is_gqa:
        q_specs = [pl.BlockSpec((HEAD_DIM, TQ), lambda b, r, i: (r, b * nq + i)),
                   pl.BlockSpec((HEAD_DIM, TQ), lambda b, r, i: (A_REP + r, b * nq + i))]
        q_args = [q_t, q_t]
        pair_col = lambda r: 0
    else:
        q_specs = [pl.BlockSpec((2 * LANES, TQ), lambda b, r, i: (r, b * nq + i))]
        q_args = [q_t]
        pair_col = lambda r: r
    return pl.pallas_call(
        functools.partial(_attn_lat_kernel, is_gqa),
        out_shape=jax.ShapeDtypeStruct((t, n_pairs * LANES), BF16),
        grid=(DEC_BATCH, n_pairs, nq),
        in_specs=q_specs + [
            pl.BlockSpec((DEC_SEQ, kw), lambda b, r, i: (b, pair_col(r))),
            pl.BlockSpec((LANES, DEC_SEQ), lambda b, r, i: (pair_col(r), b)),
            pl.BlockSpec((None, None, PAST_LEN, kw), lambda b, r, i: (l, b, 0, pair_col(r))),
            pl.BlockSpec((None, None, LANES, PAST_LEN), lambda b, r, i: (l, b, pair_col(r), 0)),
        ],
        out_specs=pl.BlockSpec((TQ, LANES), lambda b, r, i: (b * nq + i, r)),
        scratch_shapes=[pltpu.VMEM((2, 2, TK, TQ), F32), pltpu.VMEM((2, ACC_ROWS, TQ), F32)],
        compiler_params=_cparams(("parallel", "parallel", "parallel")),
        name="attn_lat_gqa" if is_gqa else "attn_lat_mla",
    )(*q_args, k, v_t, kc, vc_t)


HALO = SUBLANES
N_PAIRS = C_HEADS // 2
HEADS_PER_GROUP = C_HEADS // C_GROUPS


def _ssd_kernel(n_chunks, has_h0, xbc_ref, misc_ref, cw_ref, cb_ref, alog_ref, dvec_ref, *rest):
    if has_h0:
        h0_ref, y_ref, hout_ref, stf_ref, stb_ref, newb_ref, cm_ref, ecx_ref = rest
        stf_ref[...] = h0_ref[0]
        stb_ref[...] = h0_ref[1]
    else:
        y_ref, hout_ref, stf_ref, stb_ref, newb_ref, cm_ref, ecx_ref = rest
        stf_ref[...] = jnp.zeros_like(stf_ref)
        stb_ref[...] = jnp.zeros_like(stb_ref)
    seq_len = n_chunks * C_CHUNK
    row = lax.broadcasted_iota(jnp.int32, (C_CHUNK, LANES), 0)
    lane = _lane_iota((C_CHUNK, LANES))
    lo = lane < C_HEAD_DIM
    lane_group = (lane >= C_STATE).astype(jnp.int32)
    row_group = (row >= C_STATE).astype(jnp.int32)
    causal = (row >= lane, row <= lane)
    neg_a = -jnp.exp(alog_ref[...])
    dt_lanes = (lane >= DT_LANE) & (lane < DT_LANE + 2 * C_HEADS)

    def lane_of(direction, hd):
        return DT_LANE + C_HEADS * direction + hd

    def pair_cols(v, direction, heads):
        j0, j1 = lane_of(direction, heads[0]), lane_of(direction, heads[1])
        return jnp.where(lo[0:v.shape[0], :], v[:, j0:j0 + 1], v[:, j1:j1 + 1])

    def pass1(c, carry):
        start = pl.multiple_of(c * C_CHUNK, C_CHUNK)
        prev0 = pl.multiple_of(jnp.maximum(start - HALO, 0), HALO)
        next0 = pl.multiple_of(jnp.minimum(start + C_CHUNK, seq_len - HALO), HALO)
        keep_prev = jnp.where(c > 0, 1.0, 0.0).astype(F32)
        keep_next = jnp.where(c < n_chunks - 1, 1.0, 0.0).astype(F32)
        rows_c = pl.ds(start, C_CHUNK)

        ue = jnp.concatenate([xbc_ref[pl.ds(prev0, HALO), :] * keep_prev, xbc_ref[rows_c, :],
                              xbc_ref[pl.ds(next0, HALO), :] * keep_next], axis=0)
        acc = jnp.broadcast_to(cb_ref[...], (C_CHUNK, C_CONV_CH))
        for k in range(C_CONV):
            sh = (C_CONV // 2 - k) % ue.shape[0]
            r = ue if sh == 0 else pltpu.roll(ue, sh, axis=0)
            acc = acc + r[HALO:HALO + C_CHUNK, :] * cw_ref[k:k + 1, :]
        xc = acc * jax.nn.sigmoid(acc)
        xs = xc[:, 0:C_INNER]
        bm = xc[:, C_INNER:C_INNER + LANES]
        cm = xc[:, C_INNER + LANES:C_INNER + 2 * LANES]
        bmt = bm.T.astype(BF16)
        cm16 = cm.astype(BF16)
        g_mats = [jnp.dot(jnp.where(lane_group == g, cm, 0.0).astype(BF16), bmt, preferred_element_type=F32)
                  for g in range(C_GROUPS)]

        dtv = jnp.where(dt_lanes, misc_ref[rows_c, :], 0.0)
        dta = dtv * neg_a
        cum = dta
        step = 1
        while step < C_CHUNK:
            cum = cum + jnp.where(row >= step, pltpu.roll(cum, step, axis=0), 0.0)
            step *= 2
        tot = cum[C_CHUNK - 1:C_CHUNK, :]
        cxs = (cum, tot - cum + dta)
        cxts = (cxs[0].T, cxs[1].T)
        dtt = dtv.T
        wsts = tuple(jnp.exp(tot - cx) * dtv for cx in cxs)
        ecx_f = jnp.exp(cxs[0])
        cdec = jnp.exp(tot)
        cm_ref[c] = cm16
        ecx_ref[c, 0:C_CHUNK, :] = jnp.exp(cxs[1])
        ecx_ref[c, C_CHUNK:C_CHUNK + SUBLANES, :] = jnp.broadcast_to(cdec, (SUBLANES, LANES))

        for pr in range(N_PAIRS):
            heads = (2 * pr, 2 * pr + 1)
            groups = tuple(hd // HEADS_PER_GROUP for hd in heads)
            xs_pair = xs[:, pr * LANES:(pr + 1) * LANES]
            xs16 = xs_pair.astype(BF16)
            own = row_group == jnp.where(lo, groups[0], groups[1])
            y = xs_pair * dvec_ref[:, pr * LANES:(pr + 1) * LANES]
            for direction in range(2):
                yd = []
                for hd in heads:
                    j = lane_of(direction, hd)
                    seg = jnp.where(causal[direction], cxs[direction][:, j:j + 1] - cxts[direction][j:j + 1, :],
                                    -jnp.inf)
                    sc = g_mats[hd // HEADS_PER_GROUP] * jnp.exp(seg) * dtt[j:j + 1, :]
                    yd.append(jnp.dot(sc.astype(BF16), xs16, preferred_element_type=F32))
                y = y + _merge_halves(yd[0], yd[1])
                xw = xs_pair * pair_cols(wsts[direction], direction, heads)
                new = jnp.where(own, jnp.dot(bmt, xw.astype(BF16), preferred_element_type=F32), 0.0)
                if direction == 0:
                    st = stf_ref[pr]
                    y_off = jnp.dot(cm16, st.astype(BF16), preferred_element_type=F32)
                    y = y + y_off * pair_cols(ecx_f, 0, heads)
                    stf_ref[pr] = st * pair_cols(cdec, 0, heads) + new
                else:
                    newb_ref[c, pr] = new
            y_ref[rows_c, pr * LANES:(pr + 1) * LANES] = y
        return carry

    def pass2(i, carry):
        c = n_chunks - 1 - i
        rows_c = pl.ds(pl.multiple_of(c * C_CHUNK, C_CHUNK), C_CHUNK)
        cm16 = cm_ref[c]
        ecx_b = ecx_ref[c, 0:C_CHUNK, :]
        cdec = ecx_ref[c, C_CHUNK:C_CHUNK + 1, :]
        for pr in range(N_PAIRS):
            heads = (2 * pr, 2 * pr + 1)
            st = stb_ref[pr]
            y_off = jnp.dot(cm16, st.astype(BF16), preferred_element_type=F32)
            cols = pl.ds(pr * LANES, LANES)
            y_ref[rows_c, cols] = y_ref[rows_c, cols] + y_off * pair_cols(ecx_b, 1, heads)
            stb_ref[pr] = st * pair_cols(cdec, 1, heads) + newb_ref[c, pr]
        return carry

    unroll = n_chunks <= 2
    lax.fori_loop(0, n_chunks, pass1, 0, unroll=unroll)
    lax.fori_loop(0, n_chunks, pass2, 0, unroll=unroll)

    for direction, st_ref in enumerate((stf_ref, stb_ref)):
        for pr in range(N_PAIRS):
            st_t = st_ref[pr].T
            for hh, hd in enumerate((2 * pr, 2 * pr + 1)):
                g = hd // HEADS_PER_GROUP
                hout_ref[direction, hd] = st_t[hh * C_HEAD_DIM:(hh + 1) * C_HEAD_DIM,
                                               g * C_STATE:(g + 1) * C_STATE]


def _ssd(l, xbc, misc, lw, h0, seq_len):
    t = xbc.shape[0]
    nb = t // seq_len
    nc = seq_len // C_CHUNK
    has_h0 = h0 is not None
    const = lambda shape: pl.BlockSpec((None,) + shape, lambda b: (l,) + (0,) * len(shape))
    in_specs = [
        pl.BlockSpec((seq_len, C_CONV_CH), lambda b: (b, 0)),
        pl.BlockSpec((seq_len, LANES), lambda b: (b, 0)),
        const((C_CONV, C_CONV_CH)), const((1, C_CONV_CH)), const((1, LANES)), const((1, C_INNER)),
    ]
    args = [xbc, misc, lw["conv_w"], lw["conv_b"], lw["a_log"], lw["d_vec"]]
    if has_h0:
        in_specs.append(pl.BlockSpec((None, None, 2, N_PAIRS, LANES, LANES), lambda b: (b, l, 0, 0, 0, 0)))
        args.append(h0)
    return pl.pallas_call(
        functools.partial(_ssd_kernel, nc, has_h0),
        out_shape=(jax.ShapeDtypeStruct((t, C_INNER), F32),
                   jax.ShapeDtypeStruct((nb, 2, C_HEADS, C_HEAD_DIM, C_STATE), F32)),
        grid=(nb,),
        in_specs=in_specs,
        out_specs=(pl.BlockSpec((seq_len, C_INNER), lambda b: (b, 0)),
                   pl.BlockSpec((None, 2, C_HEADS, C_HEAD_DIM, C_STATE), lambda b: (b, 0, 0, 0, 0))),
        scratch_shapes=[pltpu.VMEM((N_PAIRS, LANES, LANES), F32), pltpu.VMEM((N_PAIRS, LANES, LANES), F32),
                        pltpu.VMEM((nc, N_PAIRS, LANES, LANES), F32), pltpu.VMEM((nc, C_CHUNK, LANES), BF16),
                        pltpu.VMEM((nc, C_CHUNK + SUBLANES, LANES), F32)],
        compiler_params=_cparams(("parallel",)),
        name="ssd_lat" if has_h0 else "ssd_ctx",
    )(*args)


def _post_kernel(x_ref, mod_ref, oa_ref, ob_ref, y_ref, z_ref, gssm_ref, wout_ref, gmix_ref, gffn_ref,
                 gffo_ref, w1_ref, w2_ref, o_ref):
    x = x_ref[...]
    gate1 = mod_ref[:, 2 * D_MODEL:3 * D_MODEL]
    shift2 = mod_ref[:, 3 * D_MODEL:4 * D_MODEL]
    scale2 = mod_ref[:, 4 * D_MODEL:5 * D_MODEL]
    gate2 = mod_ref[:, 5 * D_MODEL:6 * D_MODEL]
    z = z_ref[...]
    oc = _rms(y_ref[...] * (z * jax.nn.sigmoid(z)), gssm_ref[...])
    mix = jnp.concatenate([oa_ref[...], ob_ref[...], oc.astype(BF16)], axis=1)
    out = jnp.dot(mix, wout_ref[...], preferred_element_type=F32)
    x = x + gate1 * _rms(out, gmix_ref[...])
    h = _rms(x, gffn_ref[...]) * (1.0 + scale2) + shift2
    u = jnp.maximum(_bdot(h, w1_ref[...]), 0.0)
    f = _bdot(u * u, w2_ref[...])
    o_ref[...] = x + gate2 * _rms(f, gffo_ref[...])


def _post(l, is_ctx, x2d, mod, oa, ob, y, z, lw):
    t = x2d.shape[0]
    per_seq = DEC_SEQ // TM
    mod_map = (lambda i: (l, 0, 0, 0)) if is_ctx else (lambda i: (l, 1 + i // per_seq, 0, 0))
    tok = lambda w: pl.BlockSpec((TM, w), lambda i: (i, 0))
    const = lambda shape: pl.BlockSpec((None,) + shape, lambda i: (l,) + (0,) * len(shape),
                                       pipeline_mode=pl.Buffered(1))
    return pl.pallas_call(
        _post_kernel,
        out_shape=jax.ShapeDtypeStruct((t, D_MODEL), F32),
        grid=(t // TM,),
        in_specs=[
            tok(D_MODEL),
            pl.BlockSpec((None, None, 1, 6 * D_MODEL), mod_map),
            tok(A_Q), tok(B_HEADS * B_V), tok(C_INNER), tok(C_INNER),
            const((1, C_INNER)), const((D_MODEL, D_MODEL)), const((1, D_MODEL)), const((1, D_MODEL)),
            const((1, D_MODEL)), const((D_MODEL, D_FF)), const((D_FF, D_MODEL)),
        ],
        out_specs=tok(D_MODEL),
        compiler_params=_cparams(("parallel",)),
        name="post_ctx" if is_ctx else "post_lat",
    )(x2d, mod, oa, ob, y, z, lw["g_ssm"], lw["w_out"], lw["g_mix"], lw["g_ffn"], lw["g_ffo"],
      lw["w1"], lw["w2"])


def _rope_tables():
    pos = np.arange(DEC_SEQ)
    axis_pos = np.stack([(pos // GRID_W), (pos % GRID_W)], axis=0).astype(np.float32)

    def pattern(rot_dim):
        half = rot_dim // 2
        quarter = half // 2
        inv = (1.0 / (np.float32(ROPE_THETA) ** (np.arange(0, half, 2, dtype=np.float32) / np.float32(half))))
        inv = inv.astype(np.float32)
        dd = np.arange(rot_dim)
        ang = (axis_pos[dd // half].T * inv[dd % quarter][None, :]).astype(np.float32)
        sign = np.where((dd % half) < quarter, -1.0, 1.0).astype(np.float32)
        return np.cos(ang).astype(np.float32), (np.sin(ang) * sign).astype(np.float32)

    ca, sa = pattern(HEAD_DIM)
    ca = np.tile(ca, (1, LANES // HEAD_DIM))
    sa = np.tile(sa, (1, LANES // HEAD_DIM))
    cb32, sb32 = pattern(B_ROPE)
    tail = LANES - KPE_LANE - B_ROPE
    cb = np.concatenate([np.ones((DEC_SEQ, KPE_LANE), np.float32), cb32, np.ones((DEC_SEQ, tail), np.float32)], 1)
    sb = np.concatenate([np.zeros((DEC_SEQ, KPE_LANE), np.float32), sb32, np.zeros((DEC_SEQ, tail), np.float32)], 1)
    ident_c = np.ones((TM, LANES), np.float32)
    ident_s = np.zeros((TM, LANES), np.float32)
    return tuple(jnp.asarray(np.concatenate([tb, idt], axis=0))
                 for tb, idt in ((ca, ident_c), (sa, ident_s), (cb, ident_c), (sb, ident_s)))


REPACK_ROWS = 256
W_IN_COLS = A_Q + 2 * A_KV + B_Q_RANK + B_KV_RANK + B_ROPE + C_INNER + C_CONV_CH + 2 * C_HEADS


def _repack_kernel(w_ref, o_ref):
    o_kpe = OFF_KVC + B_KV_RANK
    o_z = o_kpe + B_ROPE
    o_dt = o_z + C_INNER + C_CONV_CH
    o_ref[:, 0:o_kpe] = w_ref[:, 0:o_kpe].astype(BF16)
    o_ref[:, OFF_Z:OFF_MISC] = w_ref[:, o_z:o_dt].astype(BF16)
    lane = _lane_iota((REPACK_ROWS, LANES))
    kpe = pltpu.roll(w_ref[:, o_kpe:o_kpe + LANES], KPE_LANE, axis=1)
    dt = pltpu.roll(w_ref[:, W_IN_COLS - LANES:W_IN_COLS], DT_LANE + 2 * C_HEADS, axis=1)
    misc = jnp.where((lane >= KPE_LANE) & (lane < DT_LANE), kpe,
                     jnp.where((lane >= DT_LANE) & (lane < DT_LANE + 2 * C_HEADS), dt, 0.0))
    o_ref[:, OFF_MISC:IN_COLS_P] = misc.astype(BF16)


def _repack_w_in(w_in):
    return pl.pallas_call(
        _repack_kernel,
        out_shape=jax.ShapeDtypeStruct((DEPTH, D_MODEL, IN_COLS_P), BF16),
        grid=(DEPTH, D_MODEL // REPACK_ROWS),
        in_specs=[pl.BlockSpec((None, REPACK_ROWS, W_IN_COLS), lambda l, i: (l, i, 0))],
        out_specs=pl.BlockSpec((None, REPACK_ROWS, IN_COLS_P), lambda l, i: (l, i, 0)),
        compiler_params=_cparams(("parallel", "parallel")),
        name="repack_w_in",
    )(w_in)


def _prep_weights(p):
    w_in_p = _repack_w_in(p["w_in"])

    w_qb = p["mla_w_qb"].reshape(DEPTH, B_Q_RANK, B_HEADS, B_NOPE + B_ROPE)
    w_qb_p = jnp.pad(w_qb, ((0, 0),) * 3 + ((0, LANES - B_NOPE - B_ROPE),)).reshape(DEPTH, B_Q_RANK, QB_W)
    w_kvb = p["mla_w_kvb"].reshape(DEPTH, B_KV_RANK, B_HEADS, B_NOPE + B_V)
    w_k = jnp.pad(w_kvb[..., :B_NOPE], ((0, 0),) * 3 + ((0, LANES - B_NOPE),)).reshape(DEPTH, B_KV_RANK, QB_W)
    w_v = w_kvb[..., B_NOPE:].reshape(DEPTH, B_KV_RANK, B_HEADS * B_V)

    w_out = p["w_out"]
    order = [r + g * A_REP for r in range(A_REP) for g in range(A_KV_HEADS)]
    w_out_p = jnp.concatenate([w_out[:, hd * HEAD_DIM:(hd + 1) * HEAD_DIM] for hd in order] + [w_out[:, A_Q:]],
                              axis=1).astype(BF16)

    row = lambda v: v.reshape(DEPTH, 1, -1)
    pair_gain = lambda g: row(jnp.tile(g, (1, LANES // HEAD_DIM)))
    lane_vec = lambda v: jnp.pad(v.reshape(DEPTH, 1, -1), ((0, 0), (0, 0), (DT_LANE, LANES - DT_LANE - 2 * C_HEADS)))
    return {
        "g_pre": row(p["norm_mix_pre"]),
        "w_in": w_in_p,
        "g_q": pair_gain(p["attn_q_norm"]),
        "g_k": pair_gain(p["attn_k_norm"]),
        "g_qc": row(p["mla_q_norm"]),
        "w_qb": w_qb_p.astype(BF16),
        "g_kv": row(p["mla_kv_norm"]),
        "w_k": w_k.astype(BF16),
        "w_v": w_v.astype(BF16),
        "conv_w": jnp.swapaxes(p["ssm_conv_w"], 1, 2),
        "conv_b": row(p["ssm_conv_b"]),
        "dt_bias": lane_vec(p["ssm_dt_bias"]),
        "a_log": lane_vec(p["ssm_a_log"]),
        "d_vec": row(jnp.repeat(p["ssm_d"], C_HEAD_DIM, axis=1)),
        "g_ssm": row(p["ssm_norm"]),
        "w_out": w_out_p,
        "g_mix": row(p["norm_mix_post"]),
        "g_ffn": row(p["norm_ffn_pre"]),
        "g_ffo": row(p["norm_ffn_post"]),
        "w1": p["w_ffn1"].astype(BF16),
        "w2": p["w_ffn2"].astype(BF16),
    }


def _state_to_pairs(h):
    ht = jnp.swapaxes(h, -1, -2)
    zero = jnp.zeros_like(ht[..., 0, :, :])
    pairs = []
    for pr in range(N_PAIRS):
        heads = (2 * pr, 2 * pr + 1)
        row_blocks = [jnp.concatenate([ht[..., hd, :, :] if hd // HEADS_PER_GROUP == g else zero for hd in heads],
                                      axis=-1) for g in range(C_GROUPS)]
        pairs.append(jnp.concatenate(row_blocks, axis=-2))
    return jnp.stack(pairs, axis=-3)


def kernel(x_prompt, x_sample, cache_attn_k, cache_attn_v, cache_mla_ckv, cache_mla_kpe, state_ssm, c, c_ctx, norm_mix_pre, norm_mix_post, norm_ffn_pre, norm_ffn_post, w_mod, b_mod, w_in, attn_q_norm, attn_k_norm, mla_q_norm, mla_w_qb, mla_kv_norm, mla_w_kvb, ssm_conv_w, ssm_conv_b, ssm_dt_bias, ssm_a_log, ssm_d, ssm_norm, w_out, w_ffn1, w_ffn2):
    p = dict(norm_mix_pre=norm_mix_pre, norm_mix_post=norm_mix_post, norm_ffn_pre=norm_ffn_pre,
             norm_ffn_post=norm_ffn_post, w_in=w_in, attn_q_norm=attn_q_norm, attn_k_norm=attn_k_norm,
             mla_q_norm=mla_q_norm, mla_w_qb=mla_w_qb, mla_kv_norm=mla_kv_norm, mla_w_kvb=mla_w_kvb,
             ssm_conv_w=ssm_conv_w, ssm_conv_b=ssm_conv_b, ssm_dt_bias=ssm_dt_bias, ssm_a_log=ssm_a_log,
             ssm_d=ssm_d, ssm_norm=ssm_norm, w_out=w_out, w_ffn1=w_ffn1, w_ffn2=w_ffn2)
    lw = _prep_weights(p)
    tabs = _rope_tables()

    cvec = jnp.concatenate([c_ctx[None, :], c, jnp.zeros((MOD_ROWS - 1 - DEC_BATCH, D_MODEL), F32)], axis=0)
    mod = _modulation(cvec, w_mod, b_mod).reshape(DEPTH, MOD_ROWS, 1, 6 * D_MODEL)

    kpe_p = jnp.pad(cache_mla_kpe, ((0, 0),) * 3 + ((KPE_LANE, LANES - KPE_LANE - B_ROPE),))
    kc, vct, kbc, vbct = _cache_prep(
        cache_attn_k.reshape(DEC_BATCH, DEPTH, PAST_LEN, A_KV), cache_attn_v.reshape(DEC_BATCH, DEPTH, PAST_LEN, A_KV),
        cache_mla_ckv, kpe_p, lw["w_k"], lw["w_v"])
    h0_lat = _state_to_pairs(state_ssm)

    xp = x_prompt.reshape(BATCH * SEQ, D_MODEL)
    xs = x_sample.reshape(DEC_BATCH * DEC_SEQ, D_MODEL)
    new_k, new_v, new_ckv, new_kpe, new_ssm = [], [], [], [], []
    for l in range(DEPTH):
        qat, ka, vat, qbt, kb, vbt, z, xbc, misc, kf, vf, ckvf = _pre(l, True, xp, mod, lw, tabs)
        oa, ob = _attn_ctx(qat, ka, vat, qbt, kb, vbt)
        y, hfin = _ssd(l, xbc, misc, lw, None, SEQ)
        xp = _post(l, True, xp, mod, oa, ob, y, z, lw)
        new_k.append(kf.reshape(BATCH, SEQ, A_KV_HEADS, HEAD_DIM))
        new_v.append(vf.reshape(BATCH, SEQ, A_KV_HEADS, HEAD_DIM))
        new_ckv.append(ckvf.reshape(BATCH, SEQ, B_KV_RANK))
        new_kpe.append(misc[:, KPE_LANE:KPE_LANE + B_ROPE].reshape(BATCH, SEQ, B_ROPE))
        new_ssm.append(hfin)
        qat, ka, vat, qbt, kb, vbt, z, xbc, misc = _pre(l, False, xs, mod, lw, tabs)
        oa = _attn_lat(l, True, qat, ka, vat, kc, vct)
        ob = _attn_lat(l, False, qbt, kb, vbt, kbc, vbct)
        y, _ = _ssd(l, xbc, misc, lw, h0_lat, DEC_SEQ)
        xs = _post(l, False, xs, mod, oa, ob, y, z, lw)
    return (xp.reshape(BATCH, SEQ, D_MODEL), xs.reshape(DEC_BATCH, DEC_SEQ, D_MODEL),
            jnp.stack(new_k, axis=1), jnp.stack(new_v, axis=1), jnp.stack(new_ckv, axis=1),
            jnp.stack(new_kpe, axis=1), jnp.stack(new_ssm, axis=1))
```

```python
import functools
import math

import numpy as np
import jax
import jax.numpy as jnp
from jax import lax
from jax.experimental import pallas as pl
from jax.experimental.pallas import tpu as pltpu

F32 = jnp.float32
BF16 = jnp.bfloat16

D_MODEL = 1024
BATCH = 16
SEQ = 256
DEPTH = 4
DEC_BATCH = 2
DEC_SEQ = 4096
PAST_LEN = 256
GRID_W = 64
HEAD_DIM = 64
A_HEADS = 6
A_KV_HEADS = 2
A_REP = A_HEADS // A_KV_HEADS
B_HEADS = 4
B_Q_RANK = 256
B_KV_RANK = 128
B_NOPE = 64
B_ROPE = 32
B_V = 64
C_HEADS = 6
C_HEAD_DIM = 64
C_INNER = C_HEADS * C_HEAD_DIM
C_GROUPS = 2
C_STATE = 64
C_CONV = 5
C_CHUNK = 128
C_CONV_CH = C_INNER + 2 * C_GROUPS * C_STATE
D_FF = 4 * D_MODEL
A_Q = A_HEADS * HEAD_DIM
A_KV = A_KV_HEADS * HEAD_DIM
ROPE_THETA = 10000.0
EPS = 1e-6
LOG2E = math.log2(math.e)

LANES = 128
SUBLANES = 8
VMEM_LIMIT = 56 * 1024 * 1024

OFF_QA = 0
OFF_KA = OFF_QA + A_Q
OFF_VA = OFF_KA + A_KV
OFF_QC = OFF_VA + A_KV
OFF_KVC = OFF_QC + B_Q_RANK
OFF_Z = OFF_KVC + B_KV_RANK
OFF_XBC = OFF_Z + C_INNER
OFF_MISC = OFF_XBC + C_CONV_CH
IN_COLS_P = OFF_MISC + LANES
KPE_LANE = 64
DT_LANE = KPE_LANE + B_ROPE
QB_W = B_HEADS * LANES

TM = 512
TQ = 512
TK = 1024


def _cparams(sem):
    return pltpu.CompilerParams(dimension_semantics=sem, vmem_limit_bytes=VMEM_LIMIT)


def _rms(x, g):
    return x * lax.rsqrt(jnp.mean(x * x, axis=-1, keepdims=True) + EPS) * g


def _bdot(a, b):
    return jnp.dot(a.astype(BF16), b.astype(BF16), preferred_element_type=F32)


def _lane_iota(shape):
    return lax.broadcasted_iota(jnp.int32, shape, len(shape) - 1)


def _swap_lanes(x, dist):
    lane = _lane_iota(x.shape)
    fwd = pltpu.roll(x, LANES - dist, axis=1)
    bwd = pltpu.roll(x, dist, axis=1)
    return jnp.where((lane % (2 * dist)) < dist, fwd, bwd)


def _rope(x, cos, sin, dist):
    return x * cos + _swap_lanes(x, dist) * sin


def _head_pair_rms(col, gain):
    lo = _lane_iota(col.shape) < HEAD_DIM
    c2 = col * col
    s_lo = jnp.sum(jnp.where(lo, c2, 0.0), axis=-1, keepdims=True)
    s_hi = jnp.sum(jnp.where(lo, 0.0, c2), axis=-1, keepdims=True)
    return col * lax.rsqrt(jnp.where(lo, s_lo, s_hi) * (1.0 / HEAD_DIM) + EPS) * gain


MOD_ROWS = SUBLANES
MOD_TN = 1536


def _mod_kernel(c_ref, w_ref, b_ref, o_ref):
    c = c_ref[...]
    o_ref[...] = _bdot(c * jax.nn.sigmoid(c), w_ref[...]) + b_ref[...]


def _modulation(cvec, w_mod, b_mod):
    return pl.pallas_call(
        _mod_kernel,
        out_shape=jax.ShapeDtypeStruct((DEPTH, MOD_ROWS, 6 * D_MODEL), F32),
        grid=(DEPTH, 6 * D_MODEL // MOD_TN),
        in_specs=[
            pl.BlockSpec((MOD_ROWS, D_MODEL), lambda l, j: (0, 0)),
            pl.BlockSpec((None, D_MODEL, MOD_TN), lambda l, j: (l, 0, j)),
            pl.BlockSpec((None, 1, MOD_TN), lambda l, j: (l, 0, j)),
        ],
        out_specs=pl.BlockSpec((None, MOD_ROWS, MOD_TN), lambda l, j: (l, 0, j)),
        compiler_params=_cparams(("parallel", "parallel")),
        name="modulation",
    )(cvec, w_mod, b_mod.reshape(DEPTH, 1, 6 * D_MODEL))


def _cache_kernel(k_ref, v_ref, ckv_ref, kpe_ref, wk_ref, wv_ref, ko_ref, vt_ref, kb_ref, vbt_ref):
    ko_ref[...] = k_ref[...].astype(BF16)
    vt_ref[...] = v_ref[...].T.astype(BF16)
    ckv = ckv_ref[...]
    kn = _bdot(ckv, wk_ref[...])
    kpe = kpe_ref[...]
    for h in range(B_HEADS):
        kb_ref[:, h * LANES:(h + 1) * LANES] = (kn[:, h * LANES:(h + 1) * LANES] + kpe).astype(BF16)
    vbt_ref[...] = _bdot(ckv, wv_ref[...]).T.astype(BF16)


def _cache_prep(cache_k, cache_v, cache_ckv, cache_kpe_p, wk_p, wv_p):
    spec_tok = lambda w: pl.BlockSpec((None, None, PAST_LEN, w), lambda l, b: (b, l, 0, 0))
    spec_t = lambda w: pl.BlockSpec((None, None, w, PAST_LEN), lambda l, b: (l, b, 0, 0))
    spec_o = lambda w: pl.BlockSpec((None, None, PAST_LEN, w), lambda l, b: (l, b, 0, 0))
    return pl.pallas_call(
        _cache_kernel,
        out_shape=(
            jax.ShapeDtypeStruct((DEPTH, DEC_BATCH, PAST_LEN, A_KV), BF16),
            jax.ShapeDtypeStruct((DEPTH, DEC_BATCH, A_KV, PAST_LEN), BF16),
            jax.ShapeDtypeStruct((DEPTH, DEC_BATCH, PAST_LEN, QB_W), BF16),
            jax.ShapeDtypeStruct((DEPTH, DEC_BATCH, B_HEADS * B_V, PAST_LEN), BF16),
        ),
        grid=(DEPTH, DEC_BATCH),
        in_specs=[
            spec_tok(A_KV), spec_tok(A_KV), spec_tok(B_KV_RANK), spec_tok(LANES),
            pl.BlockSpec((None, B_KV_RANK, QB_W), lambda l, b: (l, 0, 0)),
            pl.BlockSpec((None, B_KV_RANK, B_HEADS * B_V), lambda l, b: (l, 0, 0)),
        ],
        out_specs=(spec_o(A_KV), spec_t(A_KV), spec_o(QB_W), spec_t(B_HEADS * B_V)),
        compiler_params=_cparams(("parallel", "parallel")),
        name="cache_prep",
    )(cache_k, cache_v, cache_ckv, cache_kpe_p, wk_p, wv_p)


PRE_SUB = 2


def _pre_kernel(is_ctx, *refs):
    for sub in range(PRE_SUB):
        _pre_rows(is_ctx, slice(sub * (TM // PRE_SUB), (sub + 1) * (TM // PRE_SUB)), *refs)


def _pre_rows(is_ctx, rows, x_ref, mod_ref, gpre_ref, win_ref, gq_ref, gk_ref, gqc_ref, wqb_ref, gkv_ref,
              wk_ref, wv_ref, dtb_ref, ca_ref, sa_ref, cb_ref, sb_ref, *outs):
    if is_ctx:
        (qat_ref, ka_ref, vat_ref, qbt_ref, kb_ref, vbt_ref, z_ref, xbc_ref, misc_ref,
         kf_ref, vf_ref, ckvf_ref) = outs
    else:
        qat_ref, ka_ref, vat_ref, qbt_ref, kb_ref, vbt_ref, z_ref, xbc_ref, misc_ref = outs
    x = x_ref[rows, :]
    shift1 = mod_ref[:, 0:D_MODEL]
    scale1 = mod_ref[:, D_MODEL:2 * D_MODEL]
    h = _rms(x, gpre_ref[...]) * (1.0 + scale1) + shift1
    proj = _bdot(h, win_ref[...])

    ca, sa, cb, sb = ca_ref[rows, :], sa_ref[rows, :], cb_ref[rows, :], sb_ref[rows, :]
    lane = _lane_iota((x.shape[0], LANES))

    gq = gq_ref[...]
    for cidx in range(A_Q // LANES):
        col = proj[:, OFF_QA + cidx * LANES:OFF_QA + (cidx + 1) * LANES]
        qn = _rope(_head_pair_rms(col, gq), ca, sa, HEAD_DIM // 4) * (HEAD_DIM ** -0.5 * LOG2E)
        qat_ref[cidx * LANES:(cidx + 1) * LANES, rows] = qn.T.astype(BF16)

    kn = _head_pair_rms(proj[:, OFF_KA:OFF_KA + A_KV], gk_ref[...])
    vcol = proj[:, OFF_VA:OFF_VA + A_KV]
    if is_ctx:
        kf_ref[rows, :] = kn
        vf_ref[rows, :] = vcol
    ka_ref[rows, :] = _rope(kn, ca, sa, HEAD_DIM // 4).astype(BF16)
    vat_ref[:, rows] = vcol.T.astype(BF16)

    qc = _rms(proj[:, OFF_QC:OFF_QC + B_Q_RANK], gqc_ref[...])
    qb = _bdot(qc, wqb_ref[...])
    for hd in range(B_HEADS):
        col = qb[:, hd * LANES:(hd + 1) * LANES]
        col = _rope(col, cb, sb, B_ROPE // 4) * ((B_NOPE + B_ROPE) ** -0.5 * LOG2E)
        qbt_ref[hd * LANES:(hd + 1) * LANES, rows] = col.T.astype(BF16)

    ckv = _rms(proj[:, OFF_KVC:OFF_KVC + B_KV_RANK], gkv_ref[...])
    if is_ctx:
        ckvf_ref[rows, :] = ckv
    knope = _bdot(ckv, wk_ref[...])
    misc = proj[:, OFF_MISC:OFF_MISC + LANES]
    kpe = jnp.where((lane >= KPE_LANE) & (lane < KPE_LANE + B_ROPE), _rope(misc, cb, sb, B_ROPE // 4), 0.0)
    for hd in range(B_HEADS):
        kb_ref[rows, hd * LANES:(hd + 1) * LANES] = (knope[:, hd * LANES:(hd + 1) * LANES] + kpe).astype(BF16)
    vbt_ref[:, rows] = _bdot(ckv, wv_ref[...]).T.astype(BF16)

    x_dt = misc + dtb_ref[...]
    dtv = jnp.maximum(x_dt, 0.0) + jnp.log(1.0 + jnp.exp(-jnp.abs(x_dt)))
    misc_ref[rows, :] = jnp.where(lane >= DT_LANE, dtv, misc)

    z_ref[rows, :] = proj[:, OFF_Z:OFF_Z + C_INNER]
    xbc_ref[rows, :] = proj[:, OFF_XBC:OFF_XBC + C_CONV_CH]


def _pre(l, is_ctx, x2d, mod, lw, tabs):
    t = x2d.shape[0]
    nt = t // TM
    per_seq = DEC_SEQ // TM
    if is_ctx:
        mod_map = lambda i: (l, 0, 0, 0)
        tab_map = lambda i: (per_seq, 0)
    else:
        mod_map = lambda i: (l, 1 + i // per_seq, 0, 0)
        tab_map = lambda i: (i % per_seq, 0)
    const = lambda shape: pl.BlockSpec((None,) + shape, lambda i: (l,) + (0,) * len(shape))
    tab_spec = pl.BlockSpec((TM, LANES), tab_map)
    tok = lambda w: pl.BlockSpec((TM, w), lambda i: (i, 0))
    tok_t = lambda w: pl.BlockSpec((w, TM), lambda i: (0, i))
    out_shape = [
        jax.ShapeDtypeStruct((A_Q, t), BF16),
        jax.ShapeDtypeStruct((t, A_KV), BF16),
        jax.ShapeDtypeStruct((A_KV, t), BF16),
        jax.ShapeDtypeStruct((QB_W, t), BF16),
        jax.ShapeDtypeStruct((t, QB_W), BF16),
        jax.ShapeDtypeStruct((B_HEADS * B_V, t), BF16),
        jax.ShapeDtypeStruct((t, C_INNER), F32),
        jax.ShapeDtypeStruct((t, C_CONV_CH), F32),
        jax.ShapeDtypeStruct((t, LANES), F32),
    ]
    out_specs = [tok_t(A_Q), tok(A_KV), tok_t(A_KV), tok_t(QB_W), tok(QB_W), tok_t(B_HEADS * B_V),
                 tok(C_INNER), tok(C_CONV_CH), tok(LANES)]
    if is_ctx:
        out_shape += [jax.ShapeDtypeStruct((t, A_KV), F32)] * 2 + [jax.ShapeDtypeStruct((t, B_KV_RANK), F32)]
        out_specs += [tok(A_KV), tok(A_KV), tok(B_KV_RANK)]
    return pl.pallas_call(
        functools.partial(_pre_kernel, is_ctx),
        out_shape=tuple(out_shape),
        grid=(nt,),
        in_specs=[
            tok(D_MODEL),
            pl.BlockSpec((None, None, 1, 6 * D_MODEL), mod_map),
            const((1, D_MODEL)),
            const((D_MODEL, IN_COLS_P)),
            const((1, LANES)), const((1, LANES)), const((1, B_Q_RANK)),
            const((B_Q_RANK, QB_W)), const((1, B_KV_RANK)),
            const((B_KV_RANK, QB_W)), const((B_KV_RANK, B_HEADS * B_V)), const((1, LANES)),
            tab_spec, tab_spec, tab_spec, tab_spec,
        ],
        out_specs=tuple(out_specs),
        compiler_params=_cparams(("parallel",)),
        name="pre_ctx" if is_ctx else "pre_lat",
    )(x2d, mod, lw["g_pre"], lw["w_in"], lw["g_q"], lw["g_k"], lw["g_qc"], lw["w_qb"], lw["g_kv"],
      lw["w_k"], lw["w_v"], lw["dt_bias"], *tabs)


def _merge_halves(lo_part, hi_part):
    return jnp.where(_lane_iota(lo_part.shape) < HEAD_DIM, lo_part, hi_part)


def _gqa_query_slot(qat_ref, hd):
    q = qat_ref[hd * HEAD_DIM:(hd + 1) * HEAD_DIM, :]
    zero = jnp.zeros_like(q)
    return jnp.concatenate([q, zero] if hd // A_REP == 0 else [zero, q], axis=0)


ONES_ROWS = 16
ACC_ROWS = LANES + ONES_ROWS


def _with_ones(v_t):
    return jnp.concatenate([v_t, jnp.ones((ONES_ROWS, v_t.shape[1]), BF16)], axis=0)


def _first_tile(k, q_t, v_t):
    return _first_scores(jnp.dot(k, q_t, preferred_element_type=F32), v_t)


def _first_scores(s, v_t):
    m = jnp.max(s, axis=0, keepdims=True)
    p = jnp.exp2(s - m)
    return m, jnp.dot(_with_ones(v_t), p.astype(BF16), preferred_element_type=F32)


def _next_tile(s, m, acc, v_t):
    m_new = jnp.maximum(m, jnp.max(s, axis=0, keepdims=True))
    p = jnp.exp2(s - m_new)
    acc = jnp.exp2(m - m_new) * acc + jnp.dot(_with_ones(v_t), p.astype(BF16), preferred_element_type=F32)
    return m_new, acc


def _normalised(acc):
    return (acc[0:LANES, :] / acc[LANES:LANES + 1, :]).T


def _write_heads(outs_a, outs_b, oa_ref, ob_ref):
    for r in range(A_REP):
        oa_ref[:, r * LANES:(r + 1) * LANES] = _merge_halves(outs_a[r], outs_a[A_REP + r]).astype(BF16)
    for pr in range(B_HEADS // 2):
        ob_ref[:, pr * LANES:(pr + 1) * LANES] = _merge_halves(outs_b[2 * pr], outs_b[2 * pr + 1]).astype(BF16)


def _attn_ctx_kernel(qat_ref, ka_ref, vat_ref, qbt_ref, kb_ref, vbt_ref, oa_ref, ob_ref):
    def split_heads(acc, n):
        o = acc[0:LANES, :] / acc[LANES:LANES + 1, :]
        return [o[:, i * SEQ:(i + 1) * SEQ].T for i in range(n)]

    q_all = jnp.concatenate([_gqa_query_slot(qat_ref, hd) for hd in range(A_HEADS)], axis=1)
    outs_a = split_heads(_first_tile(ka_ref[...], q_all, vat_ref[...])[1], A_HEADS)
    outs_b = []
    for pr in range(B_HEADS // 2):
        s = jnp.concatenate([jnp.dot(kb_ref[:, hd * LANES:(hd + 1) * LANES], qbt_ref[hd * LANES:(hd + 1) * LANES, :],
                                     preferred_element_type=F32) for hd in (2 * pr, 2 * pr + 1)], axis=1)
        outs_b += split_heads(_first_scores(s, vbt_ref[pr * LANES:(pr + 1) * LANES, :])[1], 2)
    _write_heads(outs_a, outs_b, oa_ref, ob_ref)


def _attn_lat_kernel(is_gqa, *refs):
    if is_gqa:
        q0_ref, q1_ref, k_ref, vt_ref, kc_ref, vct_ref, o_ref, s_ref, acc_ref = refs
        zero = jnp.zeros((HEAD_DIM, TQ), BF16)
        q_slots = (jnp.concatenate([q0_ref[...], zero], axis=0), jnp.concatenate([zero, q1_ref[...]], axis=0))
        key_cols = (0, 0)
    else:
        q_ref, k_ref, vt_ref, kc_ref, vct_ref, o_ref, s_ref, acc_ref = refs
        q_slots = (q_ref[0:LANES, :], q_ref[LANES:2 * LANES, :])
        key_cols = (0, LANES)
    n_tiles = DEC_SEQ // TK
    head_cut, tail_cut = TK // 4, TK - TK // 4
    plans = ([("cache", 0, PAST_LEN), ("lat", 0, TK // 2)] + [("lat", TK // 2 + j * TK, TK) for j in range(n_tiles - 1)]
             + [("lat", DEC_SEQ - TK // 2, TK // 2)],
             [("lat", 0, head_cut)] + [("lat", head_cut + j * TK, TK) for j in range(n_tiles - 1)]
             + [("lat", DEC_SEQ - tail_cut, tail_cut), ("cache", 0, PAST_LEN)])

    def keys(seg, c0):
        src, k0, n = seg
        return (kc_ref if src == "cache" else k_ref)[k0:k0 + n, c0:c0 + LANES]

    def vals(seg):
        src, k0, n = seg
        return (vct_ref if src == "cache" else vt_ref)[:, k0:k0 + n]

    row_max = [None, None]
    for hh in range(2):
        seg = plans[hh][0]
        s_ref[hh, 0, 0:seg[2], :] = jnp.dot(keys(seg, key_cols[hh]), q_slots[hh], preferred_element_type=F32)
    for j in range(max(len(plan) for plan in plans)):
        for hh in range(2):
            plan = plans[hh]
            if j >= len(plan):
                continue
            if j + 1 < len(plan):
                nxt = plan[j + 1]
                s_ref[hh, (j + 1) % 2, 0:nxt[2], :] = jnp.dot(keys(nxt, key_cols[hh]), q_slots[hh],
                                                              preferred_element_type=F32)
            seg = plan[j]
            s = s_ref[hh, j % 2, 0:seg[2], :]
            if j == 0:
                row_max[hh], acc = _first_scores(s, vals(seg))
            else:
                row_max[hh], acc = _next_tile(s, row_max[hh], acc_ref[hh], vals(seg))
            acc_ref[hh] = acc
    o_ref[...] = _merge_halves(_normalised(acc_ref[0]), _normalised(acc_ref[1])).astype(BF16)


def _attn_lat(l, is_gqa, q_t, k, v_t, kc, vc_t):
    t = k.shape[0]
    nq = DEC_SEQ // TQ
    n_pairs = A_REP if is_gqa else B_HEADS // 2
    kw = A_KV if is_gqa else 2 * LANES
    if is_gqa:
        q_specs = [pl.BlockSpec((HEAD_DIM, TQ), lambda b, r, i: (r, b * nq + i)),
                   pl.BlockSpec((HEAD_DIM, TQ), lambda b, r, i: (A_REP + r, b * nq + i))]
        q_args = [q_t, q_t]
        pair_col = lambda r: 0
    else:
        q_specs = [pl.BlockSpec((2 * LANES, TQ), lambda b, r, i: (r, b * nq + i))]
        q_args = [q_t]
        pair_col = lambda r: r
    return pl.pallas_call(
        functools.partial(_attn_lat_kernel, is_gqa),
        out_shape=jax.ShapeDtypeStruct((t, n_pairs * LANES), BF16),
        grid=(DEC_BATCH, n_pairs, nq),
        in_specs=q_specs + [
            pl.BlockSpec((DEC_SEQ, kw), lambda b, r, i: (b, pair_col(r))),
            pl.BlockSpec((LANES, DEC_SEQ), lambda b, r, i: (pair_col(r), b)),
            pl.BlockSpec((None, None, PAST_LEN, kw), lambda b, r, i: (l, b, 0, pair_col(r))),
            pl.BlockSpec((None, None, LANES, PAST_LEN), lambda b, r, i: (l, b, pair_col(r), 0)),
        ],
        out_specs=pl.BlockSpec((TQ, LANES), lambda b, r, i: (b * nq + i, r)),
        scratch_shapes=[pltpu.VMEM((2, 2, TK, TQ), F32), pltpu.VMEM((2, ACC_ROWS, TQ), F32)],
        compiler_params=_cparams(("parallel", "parallel", "parallel")),
        name="attn_lat_gqa" if is_gqa else "attn_lat_mla",
    )(*q_args, k, v_t, kc, vc_t)


HALO = SUBLANES
N_PAIRS = C_HEADS // 2
HEADS_PER_GROUP = C_HEADS // C_GROUPS


def _ssd_kernel(n_chunks, has_h0, xbc_ref, misc_ref, cw_ref, cb_ref, alog_ref, dvec_ref, *rest):
    if has_h0:
        h0_ref, y_ref, hout_ref, stf_ref, stb_ref, newb_ref, cm_ref, ecx_ref = rest
        stf_ref[...] = h0_ref[0]
        stb_ref[...] = h0_ref[1]
    else:
        y_ref, hout_ref, stf_ref, stb_ref, newb_ref, cm_ref, ecx_ref = rest
        stf_ref[...] = jnp.zeros_like(stf_ref)
        stb_ref[...] = jnp.zeros_like(stb_ref)
    seq_len = n_chunks * C_CHUNK
    row = lax.broadcasted_iota(jnp.int32, (C_CHUNK, LANES), 0)
    lane = _lane_iota((C_CHUNK, LANES))
    lo = lane < C_HEAD_DIM
    lane_group = (lane >= C_STATE).astype(jnp.int32)
    row_group = (row >= C_STATE).astype(jnp.int32)
    causal = (row >= lane, row <= lane)
    neg_a = -jnp.exp(alog_ref[...])
    dt_lanes = (lane >= DT_LANE) & (lane < DT_LANE + 2 * C_HEADS)

    def lane_of(direction, hd):
        return DT_LANE + C_HEADS * direction + hd

    def pair_cols(v, direction, heads):
        j0, j1 = lane_of(direction, heads[0]), lane_of(direction, heads[1])
        return jnp.where(lo[0:v.shape[0], :], v[:, j0:j0 + 1], v[:, j1:j1 + 1])

    def pass1(c, carry):
        start = pl.multiple_of(c * C_CHUNK, C_CHUNK)
        prev0 = pl.multiple_of(jnp.maximum(start - HALO, 0), HALO)
        next0 = pl.multiple_of(jnp.minimum(start + C_CHUNK, seq_len - HALO), HALO)
        keep_prev = jnp.where(c > 0, 1.0, 0.0).astype(F32)
        keep_next = jnp.where(c < n_chunks - 1, 1.0, 0.0).astype(F32)
        rows_c = pl.ds(start, C_CHUNK)

        ue = jnp.concatenate([xbc_ref[pl.ds(prev0, HALO), :] * keep_prev, xbc_ref[rows_c, :],
                              xbc_ref[pl.ds(next0, HALO), :] * keep_next], axis=0)
        acc = jnp.broadcast_to(cb_ref[...], (C_CHUNK, C_CONV_CH))
        for k in range(C_CONV):
            sh = (C_CONV // 2 - k) % ue.shape[0]
            r = ue if sh == 0 else pltpu.roll(ue, sh, axis=0)
            acc = acc + r[HALO:HALO + C_CHUNK, :] * cw_ref[k:k + 1, :]
        xc = acc * jax.nn.sigmoid(acc)
        xs = xc[:, 0:C_INNER]
        bm = xc[:, C_INNER:C_INNER + LANES]
        cm = xc[:, C_INNER + LANES:C_INNER + 2 * LANES]
        bmt = bm.T.astype(BF16)
        cm16 = cm.astype(BF16)
        g_mats = [jnp.dot(jnp.where(lane_group == g, cm, 0.0).astype(BF16), bmt, preferred_element_type=F32)
                  for g in range(C_GROUPS)]

        dtv = jnp.where(dt_lanes, misc_ref[rows_c, :], 0.0)
        dta = dtv * neg_a
        cum = dta
        step = 1
        while step < C_CHUNK:
            cum = cum + jnp.where(row >= step, pltpu.roll(cum, step, axis=0), 0.0)
            step *= 2
        tot = cum[C_CHUNK - 1:C_CHUNK, :]
        cxs = (cum, tot - cum + dta)
        cxts = (cxs[0].T, cxs[1].T)
        dtt = dtv.T
        wsts = tuple(jnp.exp(tot - cx) * dtv for cx in cxs)
        ecx_f = jnp.exp(cxs[0])
        cdec = jnp.exp(tot)
        cm_ref[c] = cm16
        ecx_ref[c, 0:C_CHUNK, :] = jnp.exp(cxs[1])
        ecx_ref[c, C_CHUNK:C_CHUNK + SUBLANES, :] = jnp.broadcast_to(cdec, (SUBLANES, LANES))

        for pr in range(N_PAIRS):
            heads = (2 * pr, 2 * pr + 1)
            groups = tuple(hd // HEADS_PER_GROUP for hd in heads)
            xs_pair = xs[:, pr * LANES:(pr + 1) * LANES]
            xs16 = xs_pair.astype(BF16)
            own = row_group == jnp.where(lo, groups[0], groups[1])
            y = xs_pair * dvec_ref[:, pr * LANES:(pr + 1) * LANES]
            for direction in range(2):
                yd = []
                for hd in heads:
                    j = lane_of(direction, hd)
                    seg = jnp.where(causal[direction], cxs[direction][:, j:j + 1] - cxts[direction][j:j + 1, :],
                                    -jnp.inf)
                    sc = g_mats[hd // HEADS_PER_GROUP] * jnp.exp(seg) * dtt[j:j + 1, :]
                    yd.append(jnp.dot(sc.astype(BF16), xs16, preferred_element_type=F32))
                y = y + _merge_halves(yd[0], yd[1])
                xw = xs_pair * pair_cols(wsts[direction], direction, heads)
                new = jnp.where(own, jnp.dot(bmt, xw.astype(BF16), preferred_element_type=F32), 0.0)
                if direction == 0:
                    st = stf_ref[pr]
                    y_off = jnp.dot(cm16, st.astype(BF16), preferred_element_type=F32)
                    y = y + y_off * pair_cols(ecx_f, 0, heads)
                    stf_ref[pr] = st * pair_cols(cdec, 0, heads) + new
                else:
                    newb_ref[c, pr] = new
            y_ref[rows_c, pr * LANES:(pr + 1) * LANES] = y
        return carry

    def pass2(i, carry):
        c = n_chunks - 1 - i
        rows_c = pl.ds(pl.multiple_of(c * C_CHUNK, C_CHUNK), C_CHUNK)
        cm16 = cm_ref[c]
        ecx_b = ecx_ref[c, 0:C_CHUNK, :]
        cdec = ecx_ref[c, C_CHUNK:C_CHUNK + 1, :]
        for pr in range(N_PAIRS):
            heads = (2 * pr, 2 * pr + 1)
            st = stb_ref[pr]
            y_off = jnp.dot(cm16, st.astype(BF16), preferred_element_type=F32)
            cols = pl.ds(pr * LANES, LANES)
            y_ref[rows_c, cols] = y_ref[rows_c, cols] + y_off * pair_cols(ecx_b, 1, heads)
            stb_ref[pr] = st * pair_cols(cdec, 1, heads) + newb_ref[c, pr]
        return carry

    unroll = n_chunks <= 2
    lax.fori_loop(0, n_chunks, pass1, 0, unroll=unroll)
    lax.fori_loop(0, n_chunks, pass2, 0, unroll=unroll)

    for direction, st_ref in enumerate((stf_ref, stb_ref)):
        for pr in range(N_PAIRS):
            st_t = st_ref[pr].T
            for hh, hd in enumerate((2 * pr, 2 * pr + 1)):
                g = hd // HEADS_PER_GROUP
                hout_ref[direction, hd] = st_t[hh * C_HEAD_DIM:(hh + 1) * C_HEAD_DIM,
                                               g * C_STATE:(g + 1) * C_STATE]


def _ssd_scratch(nc):
    return [pltpu.VMEM((N_PAIRS, LANES, LANES), F32), pltpu.VMEM((N_PAIRS, LANES, LANES), F32),
            pltpu.VMEM((nc, N_PAIRS, LANES, LANES), F32), pltpu.VMEM((nc, C_CHUNK, LANES), BF16),
            pltpu.VMEM((nc, C_CHUNK + SUBLANES, LANES), F32)]


def _ssd_lat(l, xbc, misc, lw, h0):
    t = xbc.shape[0]
    nb = t // DEC_SEQ
    nc = DEC_SEQ // C_CHUNK
    const = lambda shape: pl.BlockSpec((None,) + shape, lambda b: (l,) + (0,) * len(shape))
    return pl.pallas_call(
        functools.partial(_ssd_kernel, nc, True),
        out_shape=(jax.ShapeDtypeStruct((t, C_INNER), F32),
                   jax.ShapeDtypeStruct((nb, 2, C_HEADS, C_HEAD_DIM, C_STATE), F32)),
        grid=(nb,),
        in_specs=[
            pl.BlockSpec((DEC_SEQ, C_CONV_CH), lambda b: (b, 0)),
            pl.BlockSpec((DEC_SEQ, LANES), lambda b: (b, 0)),
            const((C_CONV, C_CONV_CH)), const((1, C_CONV_CH)), const((1, LANES)), const((1, C_INNER)),
            pl.BlockSpec((None, None, 2, N_PAIRS, LANES, LANES), lambda b: (b, l, 0, 0, 0, 0)),
        ],
        out_specs=(pl.BlockSpec((DEC_SEQ, C_INNER), lambda b: (b, 0)),
                   pl.BlockSpec((None, 2, C_HEADS, C_HEAD_DIM, C_STATE), lambda b: (b, 0, 0, 0, 0))),
        scratch_shapes=_ssd_scratch(nc),
        compiler_params=_cparams(("parallel",)),
        name="ssd_lat",
    )(xbc, misc, lw["conv_w"], lw["conv_b"], lw["a_log"], lw["d_vec"], h0)


def _ctx_mix_kernel(n_chunks, qat_ref, ka_ref, vat_ref, qbt_ref, kb_ref, vbt_ref, xbc_ref, misc_ref, cw_ref, cb_ref,
                    alog_ref, dvec_ref, oa_ref, ob_ref, y_ref, hout_ref, *scratch):
    _attn_ctx_kernel(qat_ref, ka_ref, vat_ref, qbt_ref, kb_ref, vbt_ref, oa_ref, ob_ref)
    _ssd_kernel(n_chunks, False, xbc_ref, misc_ref, cw_ref, cb_ref, alog_ref, dvec_ref, y_ref, hout_ref, *scratch)


def _ctx_mix(l, qat, ka, vat, qbt, kb, vbt, xbc, misc, lw):
    t = ka.shape[0]
    nb = t // SEQ
    nc = SEQ // C_CHUNK
    tok = lambda w: pl.BlockSpec((SEQ, w), lambda b: (b, 0))
    tok_t = lambda w: pl.BlockSpec((w, SEQ), lambda b: (0, b))
    const = lambda shape: pl.BlockSpec((None,) + shape, lambda b: (l,) + (0,) * len(shape))
    return pl.pallas_call(
        functools.partial(_ctx_mix_kernel, nc),
        out_shape=(jax.ShapeDtypeStruct((t, A_Q), BF16), jax.ShapeDtypeStruct((t, B_HEADS * B_V), BF16),
                   jax.ShapeDtypeStruct((t, C_INNER), F32),
                   jax.ShapeDtypeStruct((nb, 2, C_HEADS, C_HEAD_DIM, C_STATE), F32)),
        grid=(nb,),
        in_specs=[tok_t(A_Q), tok(A_KV), tok_t(A_KV), tok_t(QB_W), tok(QB_W), tok_t(B_HEADS * B_V),
                  tok(C_CONV_CH), tok(LANES),
                  const((C_CONV, C_CONV_CH)), const((1, C_CONV_CH)), const((1, LANES)), const((1, C_INNER))],
        out_specs=(tok(A_Q), tok(B_HEADS * B_V), tok(C_INNER),
                   pl.BlockSpec((None, 2, C_HEADS, C_HEAD_DIM, C_STATE), lambda b: (b, 0, 0, 0, 0))),
        scratch_shapes=_ssd_scratch(nc),
        compiler_params=_cparams(("parallel",)),
        name="mix_ctx",
    )(qat, ka, vat, qbt, kb, vbt, xbc, misc, lw["conv_w"], lw["conv_b"], lw["a_log"], lw["d_vec"])


def _post_kernel(x_ref, mod_ref, oa_ref, ob_ref, y_ref, z_ref, gssm_ref, wout_ref, gmix_ref, gffn_ref,
                 gffo_ref, w1_ref, w2_ref, o_ref):
    x = x_ref[...]
    gate1 = mod_ref[:, 2 * D_MODEL:3 * D_MODEL]
    shift2 = mod_ref[:, 3 * D_MODEL:4 * D_MODEL]
    scale2 = mod_ref[:, 4 * D_MODEL:5 * D_MODEL]
    gate2 = mod_ref[:, 5 * D_MODEL:6 * D_MODEL]
    z = z_ref[...]
    oc = _rms(y_ref[...] * (z * jax.nn.sigmoid(z)), gssm_ref[...])
    mix = jnp.concatenate([oa_ref[...], ob_ref[...], oc.astype(BF16)], axis=1)
    out = jnp.dot(mix, wout_ref[...], preferred_element_type=F32)
    x = x + gate1 * _rms(out, gmix_ref[...])
    h = _rms(x, gffn_ref[...]) * (1.0 + scale2) + shift2
    u = jnp.maximum(_bdot(h, w1_ref[...]), 0.0)
    f = _bdot(u * u, w2_ref[...])
    o_ref[...] = x + gate2 * _rms(f, gffo_ref[...])


def _post(l, is_ctx, x2d, mod, oa, ob, y, z, lw):
    t = x2d.shape[0]
    per_seq = DEC_SEQ // TM
    mod_map = (lambda i: (l, 0, 0, 0)) if is_ctx else (lambda i: (l, 1 + i // per_seq, 0, 0))
    tok = lambda w: pl.BlockSpec((TM, w), lambda i: (i, 0))
    const = lambda shape: pl.BlockSpec((None,) + shape, lambda i: (l,) + (0,) * len(shape),
                                       pipeline_mode=pl.Buffered(1))
    return pl.pallas_call(
        _post_kernel,
        out_shape=jax.ShapeDtypeStruct((t, D_MODEL), F32),
        grid=(t // TM,),
        in_specs=[
            tok(D_MODEL),
            pl.BlockSpec((None, None, 1, 6 * D_MODEL), mod_map),
            tok(A_Q), tok(B_HEADS * B_V), tok(C_INNER), tok(C_INNER),
            const((1, C_INNER)), const((D_MODEL, D_MODEL)), const((1, D_MODEL)), const((1, D_MODEL)),
            const((1, D_MODEL)), const((D_MODEL, D_FF)), const((D_FF, D_MODEL)),
        ],
        out_specs=tok(D_MODEL),
        compiler_params=_cparams(("parallel",)),
        name="post_ctx" if is_ctx else "post_lat",
    )(x2d, mod, oa, ob, y, z, lw["g_ssm"], lw["w_out"], lw["g_mix"], lw["g_ffn"], lw["g_ffo"],
      lw["w1"], lw["w2"])


def _rope_tables():
    pos = np.arange(DEC_SEQ)
    axis_pos = np.stack([(pos // GRID_W), (pos % GRID_W)], axis=0).astype(np.float32)

    def pattern(rot_dim):
        half = rot_dim // 2
        quarter = half // 2
        inv = (1.0 / (np.float32(ROPE_THETA) ** (np.arange(0, half, 2, dtype=np.float32) / np.float32(half))))
        inv = inv.astype(np.float32)
        dd = np.arange(rot_dim)
        ang = (axis_pos[dd // half].T * inv[dd % quarter][None, :]).astype(np.float32)
        sign = np.where((dd % half) < quarter, -1.0, 1.0).astype(np.float32)
        return np.cos(ang).astype(np.float32), (np.sin(ang) * sign).astype(np.float32)

    ca, sa = pattern(HEAD_DIM)
    ca = np.tile(ca, (1, LANES // HEAD_DIM))
    sa = np.tile(sa, (1, LANES // HEAD_DIM))
    cb32, sb32 = pattern(B_ROPE)
    tail = LANES - KPE_LANE - B_ROPE
    cb = np.concatenate([np.ones((DEC_SEQ, KPE_LANE), np.float32), cb32, np.ones((DEC_SEQ, tail), np.float32)], 1)
    sb = np.concatenate([np.zeros((DEC_SEQ, KPE_LANE), np.float32), sb32, np.zeros((DEC_SEQ, tail), np.float32)], 1)
    ident_c = np.ones((TM, LANES), np.float32)
    ident_s = np.zeros((TM, LANES), np.float32)
    return tuple(jnp.asarray(np.concatenate([tb, idt], axis=0))
                 for tb, idt in ((ca, ident_c), (sa, ident_s), (cb, ident_c), (sb, ident_s)))


REPACK_ROWS = 256
W_IN_COLS = A_Q + 2 * A_KV + B_Q_RANK + B_KV_RANK + B_ROPE + C_INNER + C_CONV_CH + 2 * C_HEADS


def _repack_kernel(w_ref, o_ref):
    o_kpe = OFF_KVC + B_KV_RANK
    o_z = o_kpe + B_ROPE
    o_dt = o_z + C_INNER + C_CONV_CH
    o_ref[:, 0:o_kpe] = w_ref[:, 0:o_kpe].astype(BF16)
    o_ref[:, OFF_Z:OFF_MISC] = w_ref[:, o_z:o_dt].astype(BF16)
    lane = _lane_iota((REPACK_ROWS, LANES))
    kpe = pltpu.roll(w_ref[:, o_kpe:o_kpe + LANES], KPE_LANE, axis=1)
    dt = pltpu.roll(w_ref[:, W_IN_COLS - LANES:W_IN_COLS], DT_LANE + 2 * C_HEADS, axis=1)
    misc = jnp.where((lane >= KPE_LANE) & (lane < DT_LANE), kpe,
                     jnp.where((lane >= DT_LANE) & (lane < DT_LANE + 2 * C_HEADS), dt, 0.0))
    o_ref[:, OFF_MISC:IN_COLS_P] = misc.astype(BF16)


def _repack_w_in(w_in):
    return pl.pallas_call(
        _repack_kernel,
        out_shape=jax.ShapeDtypeStruct((DEPTH, D_MODEL, IN_COLS_P), BF16),
        grid=(DEPTH, D_MODEL // REPACK_ROWS),
        in_specs=[pl.BlockSpec((None, REPACK_ROWS, W_IN_COLS), lambda l, i: (l, i, 0))],
        out_specs=pl.BlockSpec((None, REPACK_ROWS, IN_COLS_P), lambda l, i: (l, i, 0)),
        compiler_params=_cparams(("parallel", "parallel")),
        name="repack_w_in",
    )(w_in)


def _prep_weights(p):
    w_in_p = _repack_w_in(p["w_in"])

    w_qb = p["mla_w_qb"].reshape(DEPTH, B_Q_RANK, B_HEADS, B_NOPE + B_ROPE)
    w_qb_p = jnp.pad(w_qb, ((0, 0),) * 3 + ((0, LANES - B_NOPE - B_ROPE),)).reshape(DEPTH, B_Q_RANK, QB_W)
    w_kvb = p["mla_w_kvb"].reshape(DEPTH, B_KV_RANK, B_HEADS, B_NOPE + B_V)
    w_k = jnp.pad(w_kvb[..., :B_NOPE], ((0, 0),) * 3 + ((0, LANES - B_NOPE),)).reshape(DEPTH, B_KV_RANK, QB_W)
    w_v = w_kvb[..., B_NOPE:].reshape(DEPTH, B_KV_RANK, B_HEADS * B_V)

    w_out = p["w_out"]
    order = [r + g * A_REP for r in range(A_REP) for g in range(A_KV_HEADS)]
    w_out_p = jnp.concatenate([w_out[:, hd * HEAD_DIM:(hd + 1) * HEAD_DIM] for hd in order] + [w_out[:, A_Q:]],
                              axis=1).astype(BF16)

    row = lambda v: v.reshape(DEPTH, 1, -1)
    pair_gain = lambda g: row(jnp.tile(g, (1, LANES // HEAD_DIM)))
    lane_vec = lambda v: jnp.pad(v.reshape(DEPTH, 1, -1), ((0, 0), (0, 0), (DT_LANE, LANES - DT_LANE - 2 * C_HEADS)))
    return {
        "g_pre": row(p["norm_mix_pre"]),
        "w_in": w_in_p,
        "g_q": pair_gain(p["attn_q_norm"]),
        "g_k": pair_gain(p["attn_k_norm"]),
        "g_qc": row(p["mla_q_norm"]),
        "w_qb": w_qb_p.astype(BF16),
        "g_kv": row(p["mla_kv_norm"]),
        "w_k": w_k.astype(BF16),
        "w_v": w_v.astype(BF16),
        "conv_w": jnp.swapaxes(p["ssm_conv_w"], 1, 2),
        "conv_b": row(p["ssm_conv_b"]),
        "dt_bias": lane_vec(p["ssm_dt_bias"]),
        "a_log": lane_vec(p["ssm_a_log"]),
        "d_vec": row(jnp.repeat(p["ssm_d"], C_HEAD_DIM, axis=1)),
        "g_ssm": row(p["ssm_norm"]),
        "w_out": w_out_p,
        "g_mix": row(p["norm_mix_post"]),
        "g_ffn": row(p["norm_ffn_pre"]),
        "g_ffo": row(p["norm_ffn_post"]),
        "w1": p["w_ffn1"].astype(BF16),
        "w2": p["w_ffn2"].astype(BF16),
    }


def _state_to_pairs(h):
    ht = jnp.swapaxes(h, -1, -2)
    zero = jnp.zeros_like(ht[..., 0, :, :])
    pairs = []
    for pr in range(N_PAIRS):
        heads = (2 * pr, 2 * pr + 1)
        row_blocks = [jnp.concatenate([ht[..., hd, :, :] if hd // HEADS_PER_GROUP == g else zero for hd in heads],
                                      axis=-1) for g in range(C_GROUPS)]
        pairs.append(jnp.concatenate(row_blocks, axis=-2))
    return jnp.stack(pairs, axis=-3)


def kernel(x_prompt, x_sample, cache_attn_k, cache_attn_v, cache_mla_ckv, cache_mla_kpe, state_ssm, c, c_ctx, norm_mix_pre, norm_mix_post, norm_ffn_pre, norm_ffn_post, w_mod, b_mod, w_in, attn_q_norm, attn_k_norm, mla_q_norm, mla_w_qb, mla_kv_norm, mla_w_kvb, ssm_conv_w, ssm_conv_b, ssm_dt_bias, ssm_a_log, ssm_d, ssm_norm, w_out, w_ffn1, w_ffn2):
    p = dict(norm_mix_pre=norm_mix_pre, norm_mix_post=norm_mix_post, norm_ffn_pre=norm_ffn_pre,
             norm_ffn_post=norm_ffn_post, w_in=w_in, attn_q_norm=attn_q_norm, attn_k_norm=attn_k_norm,
             mla_q_norm=mla_q_norm, mla_w_qb=mla_w_qb, mla_kv_norm=mla_kv_norm, mla_w_kvb=mla_w_kvb,
             ssm_conv_w=ssm_conv_w, ssm_conv_b=ssm_conv_b, ssm_dt_bias=ssm_dt_bias, ssm_a_log=ssm_a_log,
             ssm_d=ssm_d, ssm_norm=ssm_norm, w_out=w_out, w_ffn1=w_ffn1, w_ffn2=w_ffn2)
    lw = _prep_weights(p)
    tabs = _rope_tables()

    cvec = jnp.concatenate([c_ctx[None, :], c, jnp.zeros((MOD_ROWS - 1 - DEC_BATCH, D_MODEL), F32)], axis=0)
    mod = _modulation(cvec, w_mod, b_mod).reshape(DEPTH, MOD_ROWS, 1, 6 * D_MODEL)

    kpe_p = jnp.pad(cache_mla_kpe, ((0, 0),) * 3 + ((KPE_LANE, LANES - KPE_LANE - B_ROPE),))
    kc, vct, kbc, vbct = _cache_prep(
        cache_attn_k.reshape(DEC_BATCH, DEPTH, PAST_LEN, A_KV), cache_attn_v.reshape(DEC_BATCH, DEPTH, PAST_LEN, A_KV),
        cache_mla_ckv, kpe_p, lw["w_k"], lw["w_v"])
    h0_lat = _state_to_pairs(state_ssm)

    xp = x_prompt.reshape(BATCH * SEQ, D_MODEL)
    xs = x_sample.reshape(DEC_BATCH * DEC_SEQ, D_MODEL)
    new_k, new_v, new_ckv, new_kpe, new_ssm = [], [], [], [], []
    for l in range(DEPTH):
        qat, ka, vat, qbt, kb, vbt, z, xbc, misc, kf, vf, ckvf = _pre(l, True, xp, mod, lw, tabs)
        oa, ob, y, hfin = _ctx_mix(l, qat, ka, vat, qbt, kb, vbt, xbc, misc, lw)
        xp = _post(l, True, xp, mod, oa, ob, y, z, lw)
        new_k.append(kf.reshape(BATCH, SEQ, A_KV_HEADS, HEAD_DIM))
        new_v.append(vf.reshape(BATCH, SEQ, A_KV_HEADS, HEAD_DIM))
        new_ckv.append(ckvf.reshape(BATCH, SEQ, B_KV_RANK))
        new_kpe.append(misc[:, KPE_LANE:KPE_LANE + B_ROPE].reshape(BATCH, SEQ, B_ROPE))
        new_ssm.append(hfin)
        qat, ka, vat, qbt, kb, vbt, z, xbc, misc = _pre(l, False, xs, mod, lw, tabs)
        oa = _attn_lat(l, True, qat, ka, vat, kc, vct)
        ob = _attn_lat(l, False, qbt, kb, vbt, kbc, vbct)
        y, _ = _ssd_lat(l, xbc, misc, lw, h0_lat)
        xs = _post(l, False, xs, mod, oa, ob, y, z, lw)
    return (xp.reshape(BATCH, SEQ, D_MODEL), xs.reshape(DEC_BATCH, DEC_SEQ, D_MODEL),
            jnp.stack(new_k, axis=1), jnp.stack(new_v, axis=1), jnp.stack(new_ckv, axis=1),
            jnp.stack(new_kpe, axis=1), jnp.stack(new_ssm, axis=1))
```

```python
import functools
import math

import numpy as np
import jax
import jax.numpy as jnp
from jax import lax
from jax.experimental import pallas as pl
from jax.experimental.pallas import tpu as pltpu

F32 = jnp.float32
BF16 = jnp.bfloat16

D_MODEL = 1024
BATCH = 16
SEQ = 256
DEPTH = 4
DEC_BATCH = 2
DEC_SEQ = 4096
PAST_LEN = 256
GRID_W = 64
HEAD_DIM = 64
A_HEADS = 6
A_KV_HEADS = 2
A_REP = A_HEADS // A_KV_HEADS
B_HEADS = 4
B_Q_RANK = 256
B_KV_RANK = 128
B_NOPE = 64
B_ROPE = 32
B_V = 64
C_HEADS = 6
C_HEAD_DIM = 64
C_INNER = C_HEADS * C_HEAD_DIM
C_GROUPS = 2
C_STATE = 64
C_CONV = 5
C_CHUNK = 128
C_CONV_CH = C_INNER + 2 * C_GROUPS * C_STATE
D_FF = 4 * D_MODEL
A_Q = A_HEADS * HEAD_DIM
A_KV = A_KV_HEADS * HEAD_DIM
ROPE_THETA = 10000.0
EPS = 1e-6
LOG2E = math.log2(math.e)

LANES = 128
SUBLANES = 8
VMEM_LIMIT = 56 * 1024 * 1024

OFF_QA = 0
OFF_KA = OFF_QA + A_Q
OFF_VA = OFF_KA + A_KV
OFF_QC = OFF_VA + A_KV
OFF_KVC = OFF_QC + B_Q_RANK
OFF_Z = OFF_KVC + B_KV_RANK
OFF_XBC = OFF_Z + C_INNER
OFF_MISC = OFF_XBC + C_CONV_CH
IN_COLS_P = OFF_MISC + LANES
KPE_LANE = 64
DT_LANE = KPE_LANE + B_ROPE
QB_W = B_HEADS * LANES

TM = 512
TQ = 512
TK = 1024


def _cparams(sem):
    return pltpu.CompilerParams(dimension_semantics=sem, vmem_limit_bytes=VMEM_LIMIT)


def _rms(x, g):
    return x * lax.rsqrt(jnp.mean(x * x, axis=-1, keepdims=True) + EPS) * g


def _bdot(a, b):
    return jnp.dot(a.astype(BF16), b.astype(BF16), preferred_element_type=F32)


def _lane_iota(shape):
    return lax.broadcasted_iota(jnp.int32, shape, len(shape) - 1)


def _swap_lanes(x, dist):
    lane = _lane_iota(x.shape)
    fwd = pltpu.roll(x, LANES - dist, axis=1)
    bwd = pltpu.roll(x, dist, axis=1)
    return jnp.where((lane % (2 * dist)) < dist, fwd, bwd)


def _rope(x, cos, sin, dist):
    return x * cos + _swap_lanes(x, dist) * sin


def _head_pair_rms(col, gain):
    lo = _lane_iota(col.shape) < HEAD_DIM
    c2 = col * col
    s_lo = jnp.sum(jnp.where(lo, c2, 0.0), axis=-1, keepdims=True)
    s_hi = jnp.sum(jnp.where(lo, 0.0, c2), axis=-1, keepdims=True)
    return col * lax.rsqrt(jnp.where(lo, s_lo, s_hi) * (1.0 / HEAD_DIM) + EPS) * gain


MOD_ROWS = SUBLANES
MOD_TN = 1536


def _mod_kernel(c_ref, w_ref, b_ref, o_ref):
    c = c_ref[...]
    o_ref[...] = _bdot(c * jax.nn.sigmoid(c), w_ref[...]) + b_ref[...]


def _modulation(cvec, w_mod, b_mod):
    return pl.pallas_call(
        _mod_kernel,
        out_shape=jax.ShapeDtypeStruct((DEPTH, MOD_ROWS, 6 * D_MODEL), F32),
        grid=(DEPTH, 6 * D_MODEL // MOD_TN),
        in_specs=[
            pl.BlockSpec((MOD_ROWS, D_MODEL), lambda l, j: (0, 0)),
            pl.BlockSpec((None, D_MODEL, MOD_TN), lambda l, j: (l, 0, j)),
            pl.BlockSpec((None, 1, MOD_TN), lambda l, j: (l, 0, j)),
        ],
        out_specs=pl.BlockSpec((None, MOD_ROWS, MOD_TN), lambda l, j: (l, 0, j)),
        compiler_params=_cparams(("parallel", "parallel")),
        name="modulation",
    )(cvec, w_mod, b_mod.reshape(DEPTH, 1, 6 * D_MODEL))


def _cache_kernel(k_ref, v_ref, ckv_ref, kpe_ref, wk_ref, wv_ref, ko_ref, vt_ref, kb_ref, vbt_ref):
    ko_ref[...] = k_ref[...].astype(BF16)
    vt_ref[...] = v_ref[...].T.astype(BF16)
    ckv = ckv_ref[...]
    kn = _bdot(ckv, wk_ref[...])
    kpe = kpe_ref[...]
    for h in range(B_HEADS):
        kb_ref[:, h * LANES:(h + 1) * LANES] = (kn[:, h * LANES:(h + 1) * LANES] + kpe).astype(BF16)
    vbt_ref[...] = _bdot(ckv, wv_ref[...]).T.astype(BF16)


def _cache_prep(cache_k, cache_v, cache_ckv, cache_kpe_p, wk_p, wv_p):
    spec_tok = lambda w: pl.BlockSpec((None, None, PAST_LEN, w), lambda l, b: (b, l, 0, 0))
    spec_t = lambda w: pl.BlockSpec((None, None, w, PAST_LEN), lambda l, b: (l, b, 0, 0))
    spec_o = lambda w: pl.BlockSpec((None, None, PAST_LEN, w), lambda l, b: (l, b, 0, 0))
    return pl.pallas_call(
        _cache_kernel,
        out_shape=(
            jax.ShapeDtypeStruct((DEPTH, DEC_BATCH, PAST_LEN, A_KV), BF16),
            jax.ShapeDtypeStruct((DEPTH, DEC_BATCH, A_KV, PAST_LEN), BF16),
            jax.ShapeDtypeStruct((DEPTH, DEC_BATCH, PAST_LEN, QB_W), BF16),
            jax.ShapeDtypeStruct((DEPTH, DEC_BATCH, B_HEADS * B_V, PAST_LEN), BF16),
        ),
        grid=(DEPTH, DEC_BATCH),
        in_specs=[
            spec_tok(A_KV), spec_tok(A_KV), spec_tok(B_KV_RANK), spec_tok(LANES),
            pl.BlockSpec((None, B_KV_RANK, QB_W), lambda l, b: (l, 0, 0)),
            pl.BlockSpec((None, B_KV_RANK, B_HEADS * B_V), lambda l, b: (l, 0, 0)),
        ],
        out_specs=(spec_o(A_KV), spec_t(A_KV), spec_o(QB_W), spec_t(B_HEADS * B_V)),
        compiler_params=_cparams(("parallel", "parallel")),
        name="cache_prep",
    )(cache_k, cache_v, cache_ckv, cache_kpe_p, wk_p, wv_p)


PRE_SUB = 2


def _pre_kernel(is_ctx, *refs):
    for sub in range(PRE_SUB):
        _pre_rows(is_ctx, slice(sub * (TM // PRE_SUB), (sub + 1) * (TM // PRE_SUB)), *refs)


def _pre_rows(is_ctx, rows, x_ref, mod_ref, gpre_ref, win_ref, gq_ref, gk_ref, gqc_ref, wqb_ref, gkv_ref,
              wk_ref, wv_ref, dtb_ref, ca_ref, sa_ref, cb_ref, sb_ref, *outs):
    if is_ctx:
        (qat_ref, ka_ref, vat_ref, qbt_ref, kb_ref, vbt_ref, z_ref, xbc_ref, misc_ref,
         kf_ref, vf_ref, ckvf_ref) = outs
    else:
        qat_ref, ka_ref, vat_ref, qbt_ref, kb_ref, vbt_ref, z_ref, xbc_ref, misc_ref = outs
    x = x_ref[rows, :]
    shift1 = mod_ref[:, 0:D_MODEL]
    scale1 = mod_ref[:, D_MODEL:2 * D_MODEL]
    h = _rms(x, gpre_ref[...]) * (1.0 + scale1) + shift1
    proj = _bdot(h, win_ref[...])

    ca, sa, cb, sb = ca_ref[rows, :], sa_ref[rows, :], cb_ref[rows, :], sb_ref[rows, :]
    lane = _lane_iota((x.shape[0], LANES))

    gq = gq_ref[...]
    for cidx in range(A_Q // LANES):
        col = proj[:, OFF_QA + cidx * LANES:OFF_QA + (cidx + 1) * LANES]
        qn = _rope(_head_pair_rms(col, gq), ca, sa, HEAD_DIM // 4) * (HEAD_DIM ** -0.5 * LOG2E)
        qat_ref[cidx * LANES:(cidx + 1) * LANES, rows] = qn.T.astype(BF16)

    kn = _head_pair_rms(proj[:, OFF_KA:OFF_KA + A_KV], gk_ref[...])
    vcol = proj[:, OFF_VA:OFF_VA + A_KV]
    if is_ctx:
        kf_ref[rows, :] = kn
        vf_ref[rows, :] = vcol
    ka_ref[rows, :] = _rope(kn, ca, sa, HEAD_DIM // 4).astype(BF16)
    vat_ref[:, rows] = vcol.T.astype(BF16)

    qc = _rms(proj[:, OFF_QC:OFF_QC + B_Q_RANK], gqc_ref[...])
    qb = _bdot(qc, wqb_ref[...])
    for hd in range(B_HEADS):
        col = qb[:, hd * LANES:(hd + 1) * LANES]
        col = _rope(col, cb, sb, B_ROPE // 4) * ((B_NOPE + B_ROPE) ** -0.5 * LOG2E)
        qbt_ref[hd * LANES:(hd + 1) * LANES, rows] = col.T.astype(BF16)

    ckv = _rms(proj[:, OFF_KVC:OFF_KVC + B_KV_RANK], gkv_ref[...])
    if is_ctx:
        ckvf_ref[rows, :] = ckv
    knope = _bdot(ckv, wk_ref[...])
    misc = proj[:, OFF_MISC:OFF_MISC + LANES]
    kpe = jnp.where((lane >= KPE_LANE) & (lane < KPE_LANE + B_ROPE), _rope(misc, cb, sb, B_ROPE // 4), 0.0)
    for hd in range(B_HEADS):
        kb_ref[rows, hd * LANES:(hd + 1) * LANES] = (knope[:, hd * LANES:(hd + 1) * LANES] + kpe).astype(BF16)
    vbt_ref[:, rows] = _bdot(ckv, wv_ref[...]).T.astype(BF16)

    x_dt = misc + dtb_ref[...]
    dtv = jnp.maximum(x_dt, 0.0) + jnp.log(1.0 + jnp.exp(-jnp.abs(x_dt)))
    misc_ref[rows, :] = jnp.where(lane >= DT_LANE, dtv, misc)

    z_ref[rows, :] = proj[:, OFF_Z:OFF_Z + C_INNER]
    xbc_ref[rows, :] = proj[:, OFF_XBC:OFF_XBC + C_CONV_CH]


def _pre_parts(l, is_ctx, x2d, mod, lw, tabs):
    t = x2d.shape[0]
    per_seq = DEC_SEQ // TM
    if is_ctx:
        mod_map = lambda i: (l, 0, 0, 0)
        tab_map = lambda i: (per_seq, 0)
    else:
        mod_map = lambda i: (l, 1 + i // per_seq, 0, 0)
        tab_map = lambda i: (i % per_seq, 0)
    const = lambda shape: pl.BlockSpec((None,) + shape, lambda i: (l,) + (0,) * len(shape))
    tab_spec = pl.BlockSpec((TM, LANES), tab_map)
    tok = lambda w: pl.BlockSpec((TM, w), lambda i: (i, 0))
    tok_t = lambda w: pl.BlockSpec((w, TM), lambda i: (0, i))
    out_shape = [
        jax.ShapeDtypeStruct((A_Q, t), BF16),
        jax.ShapeDtypeStruct((t, A_KV), BF16),
        jax.ShapeDtypeStruct((A_KV, t), BF16),
        jax.ShapeDtypeStruct((QB_W, t), BF16),
        jax.ShapeDtypeStruct((t, QB_W), BF16),
        jax.ShapeDtypeStruct((B_HEADS * B_V, t), BF16),
        jax.ShapeDtypeStruct((t, C_INNER), F32),
        jax.ShapeDtypeStruct((t, C_CONV_CH), F32),
        jax.ShapeDtypeStruct((t, LANES), F32),
    ]
    out_specs = [tok_t(A_Q), tok(A_KV), tok_t(A_KV), tok_t(QB_W), tok(QB_W), tok_t(B_HEADS * B_V),
                 tok(C_INNER), tok(C_CONV_CH), tok(LANES)]
    if is_ctx:
        out_shape += [jax.ShapeDtypeStruct((t, A_KV), F32)] * 2 + [jax.ShapeDtypeStruct((t, B_KV_RANK), F32)]
        out_specs += [tok(A_KV), tok(A_KV), tok(B_KV_RANK)]
    in_specs = [
        tok(D_MODEL),
        pl.BlockSpec((None, None, 1, 6 * D_MODEL), mod_map),
        const((1, D_MODEL)),
        const((D_MODEL, IN_COLS_P)),
        const((1, LANES)), const((1, LANES)), const((1, B_Q_RANK)),
        const((B_Q_RANK, QB_W)), const((1, B_KV_RANK)),
        const((B_KV_RANK, QB_W)), const((B_KV_RANK, B_HEADS * B_V)), const((1, LANES)),
        tab_spec, tab_spec, tab_spec, tab_spec,
    ]
    args = [x2d, mod, lw["g_pre"], lw["w_in"], lw["g_q"], lw["g_k"], lw["g_qc"], lw["w_qb"], lw["g_kv"],
            lw["w_k"], lw["w_v"], lw["dt_bias"], *tabs]
    return in_specs, args, out_shape, out_specs


def _pre_ctx(l, x2d, mod, lw, tabs):
    in_specs, args, out_shape, out_specs = _pre_parts(l, True, x2d, mod, lw, tabs)
    return pl.pallas_call(
        functools.partial(_pre_kernel, True),
        out_shape=tuple(out_shape),
        grid=(x2d.shape[0] // TM,),
        in_specs=in_specs,
        out_specs=tuple(out_specs),
        compiler_params=_cparams(("parallel",)),
        name="pre_ctx",
    )(*args)


def _merge_halves(lo_part, hi_part):
    return jnp.where(_lane_iota(lo_part.shape) < HEAD_DIM, lo_part, hi_part)


def _gqa_query_slot(qat_ref, hd):
    q = qat_ref[hd * HEAD_DIM:(hd + 1) * HEAD_DIM, :]
    zero = jnp.zeros_like(q)
    return jnp.concatenate([q, zero] if hd // A_REP == 0 else [zero, q], axis=0)


ONES_ROWS = 16
ACC_ROWS = LANES + ONES_ROWS


def _with_ones(v_t):
    return jnp.concatenate([v_t, jnp.ones((ONES_ROWS, v_t.shape[1]), BF16)], axis=0)


def _first_tile(k, q_t, v_t):
    return _first_scores(jnp.dot(k, q_t, preferred_element_type=F32), v_t)


def _first_scores(s, v_t):
    m = jnp.max(s, axis=0, keepdims=True)
    p = jnp.exp2(s - m)
    return m, jnp.dot(_with_ones(v_t), p.astype(BF16), preferred_element_type=F32)


def _next_tile(s, m, acc, v_t):
    m_new = jnp.maximum(m, jnp.max(s, axis=0, keepdims=True))
    p = jnp.exp2(s - m_new)
    acc = jnp.exp2(m - m_new) * acc + jnp.dot(_with_ones(v_t), p.astype(BF16), preferred_element_type=F32)
    return m_new, acc


def _normalised(acc):
    return (acc[0:LANES, :] / acc[LANES:LANES + 1, :]).T


def _write_heads(outs_a, outs_b, oa_ref, ob_ref):
    for r in range(A_REP):
        oa_ref[:, r * LANES:(r + 1) * LANES] = _merge_halves(outs_a[r], outs_a[A_REP + r]).astype(BF16)
    for pr in range(B_HEADS // 2):
        ob_ref[:, pr * LANES:(pr + 1) * LANES] = _merge_halves(outs_b[2 * pr], outs_b[2 * pr + 1]).astype(BF16)


def _attn_ctx_kernel(qat_ref, ka_ref, vat_ref, qbt_ref, kb_ref, vbt_ref, oa_ref, ob_ref):
    def split_heads(acc, n):
        o = acc[0:LANES, :] / acc[LANES:LANES + 1, :]
        return [o[:, i * SEQ:(i + 1) * SEQ].T for i in range(n)]

    q_all = jnp.concatenate([_gqa_query_slot(qat_ref, hd) for hd in range(A_HEADS)], axis=1)
    outs_a = split_heads(_first_tile(ka_ref[...], q_all, vat_ref[...])[1], A_HEADS)
    outs_b = []
    for pr in range(B_HEADS // 2):
        s = jnp.concatenate([jnp.dot(kb_ref[:, hd * LANES:(hd + 1) * LANES], qbt_ref[hd * LANES:(hd + 1) * LANES, :],
                                     preferred_element_type=F32) for hd in (2 * pr, 2 * pr + 1)], axis=1)
        outs_b += split_heads(_first_scores(s, vbt_ref[pr * LANES:(pr + 1) * LANES, :])[1], 2)
    _write_heads(outs_a, outs_b, oa_ref, ob_ref)


def _attn_lat_kernel(is_gqa, *refs):
    if is_gqa:
        q0_ref, q1_ref, k_ref, vt_ref, kc_ref, vct_ref, o_ref, s_ref, acc_ref = refs
        zero = jnp.zeros((HEAD_DIM, TQ), BF16)
        q_slots = (jnp.concatenate([q0_ref[...], zero], axis=0), jnp.concatenate([zero, q1_ref[...]], axis=0))
        key_cols = (0, 0)
    else:
        q_ref, k_ref, vt_ref, kc_ref, vct_ref, o_ref, s_ref, acc_ref = refs
        q_slots = (q_ref[0:LANES, :], q_ref[LANES:2 * LANES, :])
        key_cols = (0, LANES)
    n_tiles = DEC_SEQ // TK
    head_cut, tail_cut = TK // 4, TK - TK // 4
    plans = ([("cache", 0, PAST_LEN), ("lat", 0, TK // 2)] + [("lat", TK // 2 + j * TK, TK) for j in range(n_tiles - 1)]
             + [("lat", DEC_SEQ - TK // 2, TK // 2)],
             [("lat", 0, head_cut)] + [("lat", head_cut + j * TK, TK) for j in range(n_tiles - 1)]
             + [("lat", DEC_SEQ - tail_cut, tail_cut), ("cache", 0, PAST_LEN)])

    def keys(seg, c0):
        src, k0, n = seg
        return (kc_ref if src == "cache" else k_ref)[k0:k0 + n, c0:c0 + LANES]

    def vals(seg):
        src, k0, n = seg
        return (vct_ref if src == "cache" else vt_ref)[:, k0:k0 + n]

    row_max = [None, None]
    for hh in range(2):
        seg = plans[hh][0]
        s_ref[hh, 0, 0:seg[2], :] = jnp.dot(keys(seg, key_cols[hh]), q_slots[hh], preferred_element_type=F32)
    for j in range(max(len(plan) for plan in plans)):
        for hh in range(2):
            plan = plans[hh]
            if j >= len(plan):
                continue
            if j + 1 < len(plan):
                nxt = plan[j + 1]
                s_ref[hh, (j + 1) % 2, 0:nxt[2], :] = jnp.dot(keys(nxt, key_cols[hh]), q_slots[hh],
                                                              preferred_element_type=F32)
            seg = plan[j]
            s = s_ref[hh, j % 2, 0:seg[2], :]
            if j == 0:
                row_max[hh], acc = _first_scores(s, vals(seg))
            else:
                row_max[hh], acc = _next_tile(s, row_max[hh], acc_ref[hh], vals(seg))
            acc_ref[hh] = acc
    o_ref[...] = _merge_halves(_normalised(acc_ref[0]), _normalised(acc_ref[1])).astype(BF16)


def _attn_lat(l, is_gqa, q_t, k, v_t, kc, vc_t):
    t = k.shape[0]
    nq = DEC_SEQ // TQ
    n_pairs = A_REP if is_gqa else B_HEADS // 2
    kw = A_KV if is_gqa else 2 * LANES
    if is_gqa:
        q_specs = [pl.BlockSpec((HEAD_DIM, TQ), lambda b, r, i: (r, b * nq + i)),
                   pl.BlockSpec((HEAD_DIM, TQ), lambda b, r, i: (A_REP + r, b * nq + i))]
        q_args = [q_t, q_t]
        pair_col = lambda r: 0
    else:
        q_specs = [pl.BlockSpec((2 * LANES, TQ), lambda b, r, i: (r, b * nq + i))]
        q_args = [q_t]
        pair_col = lambda r: r
    return pl.pallas_call(
        functools.partial(_attn_lat_kernel, is_gqa),
        out_shape=jax.ShapeDtypeStruct((t, n_pairs * LANES), BF16),
        grid=(DEC_BATCH, n_pairs, nq),
        in_specs=q_specs + [
            pl.BlockSpec((DEC_SEQ, kw), lambda b, r, i: (b, pair_col(r))),
            pl.BlockSpec((LANES, DEC_SEQ), lambda b, r, i: (pair_col(r), b)),
            pl.BlockSpec((None, None, PAST_LEN, kw), lambda b, r, i: (l, b, 0, pair_col(r))),
            pl.BlockSpec((None, None, LANES, PAST_LEN), lambda b, r, i: (l, b, pair_col(r), 0)),
        ],
        out_specs=pl.BlockSpec((TQ, LANES), lambda b, r, i: (b * nq + i, r)),
        scratch_shapes=[pltpu.VMEM((2, 2, TK, TQ), F32), pltpu.VMEM((2, ACC_ROWS, TQ), F32)],
        compiler_params=_cparams(("parallel", "parallel", "parallel")),
        name="attn_lat_gqa" if is_gqa else "attn_lat_mla",
    )(*q_args, k, v_t, kc, vc_t)


HALO = SUBLANES
N_PAIRS = C_HEADS // 2
HEADS_PER_GROUP = C_HEADS // C_GROUPS


def _ssd_kernel(n_chunks, has_h0, xbc_ref, misc_ref, cw_ref, cb_ref, alog_ref, dvec_ref, *rest):
    if has_h0:
        h0_ref, y_ref, hout_ref, stf_ref, stb_ref, newb_ref, cm_ref, ecx_ref = rest
        stf_ref[...] = h0_ref[0]
        stb_ref[...] = h0_ref[1]
    else:
        y_ref, hout_ref, stf_ref, stb_ref, newb_ref, cm_ref, ecx_ref = rest
        stf_ref[...] = jnp.zeros_like(stf_ref)
        stb_ref[...] = jnp.zeros_like(stb_ref)
    seq_len = n_chunks * C_CHUNK
    row = lax.broadcasted_iota(jnp.int32, (C_CHUNK, LANES), 0)
    lane = _lane_iota((C_CHUNK, LANES))
    lo = lane < C_HEAD_DIM
    lane_group = (lane >= C_STATE).astype(jnp.int32)
    row_group = (row >= C_STATE).astype(jnp.int32)
    causal = (row >= lane, row <= lane)
    neg_a = -jnp.exp(alog_ref[...])
    dt_lanes = (lane >= DT_LANE) & (lane < DT_LANE + 2 * C_HEADS)

    def lane_of(direction, hd):
        return DT_LANE + C_HEADS * direction + hd

    def pair_cols(v, direction, heads):
        j0, j1 = lane_of(direction, heads[0]), lane_of(direction, heads[1])
        return jnp.where(lo[0:v.shape[0], :], v[:, j0:j0 + 1], v[:, j1:j1 + 1])

    def pass1(c, carry):
        start = pl.multiple_of(c * C_CHUNK, C_CHUNK)
        prev0 = pl.multiple_of(jnp.maximum(start - HALO, 0), HALO)
        next0 = pl.multiple_of(jnp.minimum(start + C_CHUNK, seq_len - HALO), HALO)
        keep_prev = jnp.where(c > 0, 1.0, 0.0).astype(F32)
        keep_next = jnp.where(c < n_chunks - 1, 1.0, 0.0).astype(F32)
        rows_c = pl.ds(start, C_CHUNK)

        ue = jnp.concatenate([xbc_ref[pl.ds(prev0, HALO), :] * keep_prev, xbc_ref[rows_c, :],
                              xbc_ref[pl.ds(next0, HALO), :] * keep_next], axis=0)
        acc = jnp.broadcast_to(cb_ref[...], (C_CHUNK, C_CONV_CH))
        for k in range(C_CONV):
            sh = (C_CONV // 2 - k) % ue.shape[0]
            r = ue if sh == 0 else pltpu.roll(ue, sh, axis=0)
            acc = acc + r[HALO:HALO + C_CHUNK, :] * cw_ref[k:k + 1, :]
        xc = acc * jax.nn.sigmoid(acc)
        xs = xc[:, 0:C_INNER]
        bm = xc[:, C_INNER:C_INNER + LANES]
        cm = xc[:, C_INNER + LANES:C_INNER + 2 * LANES]
        bmt = bm.T.astype(BF16)
        cm16 = cm.astype(BF16)
        g_mats = [jnp.dot(jnp.where(lane_group == g, cm, 0.0).astype(BF16), bmt, preferred_element_type=F32)
                  for g in range(C_GROUPS)]

        dtv = jnp.where(dt_lanes, misc_ref[rows_c, :], 0.0)
        dta = dtv * neg_a
        cum = dta
        step = 1
        while step < C_CHUNK:
            cum = cum + jnp.where(row >= step, pltpu.roll(cum, step, axis=0), 0.0)
            step *= 2
        tot = cum[C_CHUNK - 1:C_CHUNK, :]
        cxs = (cum, tot - cum + dta)
        cxts = (cxs[0].T, cxs[1].T)
        dtt = dtv.T
        wsts = tuple(jnp.exp(tot - cx) * dtv for cx in cxs)
        ecx_f = jnp.exp(cxs[0])
        cdec = jnp.exp(tot)
        cm_ref[c] = cm16
        ecx_ref[c, 0:C_CHUNK, :] = jnp.exp(cxs[1])
        ecx_ref[c, C_CHUNK:C_CHUNK + SUBLANES, :] = jnp.broadcast_to(cdec, (SUBLANES, LANES))

        for pr in range(N_PAIRS):
            heads = (2 * pr, 2 * pr + 1)
            groups = tuple(hd // HEADS_PER_GROUP for hd in heads)
            xs_pair = xs[:, pr * LANES:(pr + 1) * LANES]
            xs16 = xs_pair.astype(BF16)
            own = row_group == jnp.where(lo, groups[0], groups[1])
            y = xs_pair * dvec_ref[:, pr * LANES:(pr + 1) * LANES]
            for direction in range(2):
                yd = []
                for hd in heads:
                    j = lane_of(direction, hd)
                    seg = jnp.where(causal[direction], cxs[direction][:, j:j + 1] - cxts[direction][j:j + 1, :],
                                    -jnp.inf)
                    sc = g_mats[hd // HEADS_PER_GROUP] * jnp.exp(seg) * dtt[j:j + 1, :]
                    yd.append(jnp.dot(sc.astype(BF16), xs16, preferred_element_type=F32))
                y = y + _merge_halves(yd[0], yd[1])
                xw = xs_pair * pair_cols(wsts[direction], direction, heads)
                new = jnp.where(own, jnp.dot(bmt, xw.astype(BF16), preferred_element_type=F32), 0.0)
                if direction == 0:
                    st = stf_ref[pr]
                    y_off = jnp.dot(cm16, st.astype(BF16), preferred_element_type=F32)
                    y = y + y_off * pair_cols(ecx_f, 0, heads)
                    stf_ref[pr] = st * pair_cols(cdec, 0, heads) + new
                else:
                    newb_ref[c, pr] = new
            y_ref[rows_c, pr * LANES:(pr + 1) * LANES] = y
        return carry

    def pass2(i, carry):
        c = n_chunks - 1 - i
        rows_c = pl.ds(pl.multiple_of(c * C_CHUNK, C_CHUNK), C_CHUNK)
        cm16 = cm_ref[c]
        ecx_b = ecx_ref[c, 0:C_CHUNK, :]
        cdec = ecx_ref[c, C_CHUNK:C_CHUNK + 1, :]
        for pr in range(N_PAIRS):
            heads = (2 * pr, 2 * pr + 1)
            st = stb_ref[pr]
            y_off = jnp.dot(cm16, st.astype(BF16), preferred_element_type=F32)
            cols = pl.ds(pr * LANES, LANES)
            y_ref[rows_c, cols] = y_ref[rows_c, cols] + y_off * pair_cols(ecx_b, 1, heads)
            stb_ref[pr] = st * pair_cols(cdec, 1, heads) + newb_ref[c, pr]
        return carry

    unroll = n_chunks <= 2
    lax.fori_loop(0, n_chunks, pass1, 0, unroll=unroll)
    lax.fori_loop(0, n_chunks, pass2, 0, unroll=unroll)

    for direction, st_ref in enumerate((stf_ref, stb_ref)):
        for pr in range(N_PAIRS):
            st_t = st_ref[pr].T
            for hh, hd in enumerate((2 * pr, 2 * pr + 1)):
                g = hd // HEADS_PER_GROUP
                hout_ref[direction, hd] = st_t[hh * C_HEAD_DIM:(hh + 1) * C_HEAD_DIM,
                                               g * C_STATE:(g + 1) * C_STATE]


def _ssd_scratch(nc):
    return [pltpu.VMEM((N_PAIRS, LANES, LANES), F32), pltpu.VMEM((N_PAIRS, LANES, LANES), F32),
            pltpu.VMEM((nc, N_PAIRS, LANES, LANES), F32), pltpu.VMEM((nc, C_CHUNK, LANES), BF16),
            pltpu.VMEM((nc, C_CHUNK + SUBLANES, LANES), F32)]


def _ssd_lat(l, xbc, misc, lw, h0):
    t = xbc.shape[0]
    nb = t // DEC_SEQ
    nc = DEC_SEQ // C_CHUNK
    const = lambda shape: pl.BlockSpec((None,) + shape, lambda b: (l,) + (0,) * len(shape))
    return pl.pallas_call(
        functools.partial(_ssd_kernel, nc, True),
        out_shape=(jax.ShapeDtypeStruct((t, C_INNER), F32),
                   jax.ShapeDtypeStruct((nb, 2, C_HEADS, C_HEAD_DIM, C_STATE), F32)),
        grid=(nb,),
        in_specs=[
            pl.BlockSpec((DEC_SEQ, C_CONV_CH), lambda b: (b, 0)),
            pl.BlockSpec((DEC_SEQ, LANES), lambda b: (b, 0)),
            const((C_CONV, C_CONV_CH)), const((1, C_CONV_CH)), const((1, LANES)), const((1, C_INNER)),
            pl.BlockSpec((None, None, 2, N_PAIRS, LANES, LANES), lambda b: (b, l, 0, 0, 0, 0)),
        ],
        out_specs=(pl.BlockSpec((DEC_SEQ, C_INNER), lambda b: (b, 0)),
                   pl.BlockSpec((None, 2, C_HEADS, C_HEAD_DIM, C_STATE), lambda b: (b, 0, 0, 0, 0))),
        scratch_shapes=_ssd_scratch(nc),
        compiler_params=_cparams(("parallel",)),
        name="ssd_lat",
    )(xbc, misc, lw["conv_w"], lw["conv_b"], lw["a_log"], lw["d_vec"], h0)


def _ctx_mix_kernel(n_chunks, qat_ref, ka_ref, vat_ref, qbt_ref, kb_ref, vbt_ref, xbc_ref, misc_ref, cw_ref, cb_ref,
                    alog_ref, dvec_ref, oa_ref, ob_ref, y_ref, hout_ref, *scratch):
    _attn_ctx_kernel(qat_ref, ka_ref, vat_ref, qbt_ref, kb_ref, vbt_ref, oa_ref, ob_ref)
    _ssd_kernel(n_chunks, False, xbc_ref, misc_ref, cw_ref, cb_ref, alog_ref, dvec_ref, y_ref, hout_ref, *scratch)


def _ctx_mix_parts(l, qat, ka, vat, qbt, kb, vbt, xbc, misc, lw):
    t = ka.shape[0]
    tok = lambda w: pl.BlockSpec((SEQ, w), lambda b: (b, 0))
    tok_t = lambda w: pl.BlockSpec((w, SEQ), lambda b: (0, b))
    const = lambda shape: pl.BlockSpec((None,) + shape, lambda b: (l,) + (0,) * len(shape))
    in_specs = [tok_t(A_Q), tok(A_KV), tok_t(A_KV), tok_t(QB_W), tok(QB_W), tok_t(B_HEADS * B_V),
                tok(C_CONV_CH), tok(LANES),
                const((C_CONV, C_CONV_CH)), const((1, C_CONV_CH)), const((1, LANES)), const((1, C_INNER))]
    args = [qat, ka, vat, qbt, kb, vbt, xbc, misc, lw["conv_w"], lw["conv_b"], lw["a_log"], lw["d_vec"]]
    out_shape = [jax.ShapeDtypeStruct((t, A_Q), BF16), jax.ShapeDtypeStruct((t, B_HEADS * B_V), BF16),
                 jax.ShapeDtypeStruct((t, C_INNER), F32),
                 jax.ShapeDtypeStruct((t // SEQ, 2, C_HEADS, C_HEAD_DIM, C_STATE), F32)]
    out_specs = [tok(A_Q), tok(B_HEADS * B_V), tok(C_INNER),
                 pl.BlockSpec((None, 2, C_HEADS, C_HEAD_DIM, C_STATE), lambda b: (b, 0, 0, 0, 0))]
    return in_specs, args, out_shape, out_specs


def _pre_lat_mix_ctx_kernel(n_chunks, n_pre_in, n_mix_in, n_pre_out, n_mix_out, *refs):
    pre_in, refs = refs[:n_pre_in], refs[n_pre_in:]
    mix_in, refs = refs[:n_mix_in], refs[n_mix_in:]
    pre_out, refs = refs[:n_pre_out], refs[n_pre_out:]
    mix_out, scratch = refs[:n_mix_out], refs[n_mix_out:]
    _pre_kernel(False, *pre_in, *pre_out)
    _ctx_mix_kernel(n_chunks, *mix_in, *mix_out, *scratch)


def _pre_lat_mix_ctx(l, xs2d, mod, lw, tabs, ctx_mixer_inputs):
    p_in, p_args, p_shape, p_specs = _pre_parts(l, False, xs2d, mod, lw, tabs)
    m_in, m_args, m_shape, m_specs = _ctx_mix_parts(l, *ctx_mixer_inputs, lw)
    steps = xs2d.shape[0] // TM
    assert steps == ctx_mixer_inputs[1].shape[0] // SEQ
    nc = SEQ // C_CHUNK
    outs = pl.pallas_call(
        functools.partial(_pre_lat_mix_ctx_kernel, nc, len(p_in), len(m_in), len(p_shape), len(m_shape)),
        out_shape=tuple(p_shape + m_shape),
        grid=(steps,),
        in_specs=p_in + m_in,
        out_specs=tuple(p_specs + m_specs),
        scratch_shapes=_ssd_scratch(nc),
        compiler_params=_cparams(("parallel",)),
        name="pre_lat_mix_ctx",
    )(*p_args, *m_args)
    return outs[:len(p_shape)], outs[len(p_shape):]


def _post_kernel(x_ref, mod_ref, oa_ref, ob_ref, y_ref, z_ref, gssm_ref, wout_ref, gmix_ref, gffn_ref,
                 gffo_ref, w1_ref, w2_ref, o_ref):
    x = x_ref[...]
    gate1 = mod_ref[:, 2 * D_MODEL:3 * D_MODEL]
    shift2 = mod_ref[:, 3 * D_MODEL:4 * D_MODEL]
    scale2 = mod_ref[:, 4 * D_MODEL:5 * D_MODEL]
    gate2 = mod_ref[:, 5 * D_MODEL:6 * D_MODEL]
    z = z_ref[...]
    oc = _rms(y_ref[...] * (z * jax.nn.sigmoid(z)), gssm_ref[...])
    mix = jnp.concatenate([oa_ref[...], ob_ref[...], oc.astype(BF16)], axis=1)
    out = jnp.dot(mix, wout_ref[...], preferred_element_type=F32)
    x = x + gate1 * _rms(out, gmix_ref[...])
    h = _rms(x, gffn_ref[...]) * (1.0 + scale2) + shift2
    u = jnp.maximum(_bdot(h, w1_ref[...]), 0.0)
    f = _bdot(u * u, w2_ref[...])
    o_ref[...] = x + gate2 * _rms(f, gffo_ref[...])


def _post(l, is_ctx, x2d, mod, oa, ob, y, z, lw):
    t = x2d.shape[0]
    per_seq = DEC_SEQ // TM
    mod_map = (lambda i: (l, 0, 0, 0)) if is_ctx else (lambda i: (l, 1 + i // per_seq, 0, 0))
    tok = lambda w: pl.BlockSpec((TM, w), lambda i: (i, 0))
    const = lambda shape: pl.BlockSpec((None,) + shape, lambda i: (l,) + (0,) * len(shape),
                                       pipeline_mode=pl.Buffered(1))
    return pl.pallas_call(
        _post_kernel,
        out_shape=jax.ShapeDtypeStruct((t, D_MODEL), F32),
        grid=(t // TM,),
        in_specs=[
            tok(D_MODEL),
            pl.BlockSpec((None, None, 1, 6 * D_MODEL), mod_map),
            tok(A_Q), tok(B_HEADS * B_V), tok(C_INNER), tok(C_INNER),
            const((1, C_INNER)), const((D_MODEL, D_MODEL)), const((1, D_MODEL)), const((1, D_MODEL)),
            const((1, D_MODEL)), const((D_MODEL, D_FF)), const((D_FF, D_MODEL)),
        ],
        out_specs=tok(D_MODEL),
        compiler_params=_cparams(("parallel",)),
        name="post_ctx" if is_ctx else "post_lat",
    )(x2d, mod, oa, ob, y, z, lw["g_ssm"], lw["w_out"], lw["g_mix"], lw["g_ffn"], lw["g_ffo"],
      lw["w1"], lw["w2"])


def _rope_tables():
    pos = np.arange(DEC_SEQ)
    axis_pos = np.stack([(pos // GRID_W), (pos % GRID_W)], axis=0).astype(np.float32)

    def pattern(rot_dim):
        half = rot_dim // 2
        quarter = half // 2
        inv = (1.0 / (np.float32(ROPE_THETA) ** (np.arange(0, half, 2, dtype=np.float32) / np.float32(half))))
        inv = inv.astype(np.float32)
        dd = np.arange(rot_dim)
        ang = (axis_pos[dd // half].T * inv[dd % quarter][None, :]).astype(np.float32)
        sign = np.where((dd % half) < quarter, -1.0, 1.0).astype(np.float32)
        return np.cos(ang).astype(np.float32), (np.sin(ang) * sign).astype(np.float32)

    ca, sa = pattern(HEAD_DIM)
    ca = np.tile(ca, (1, LANES // HEAD_DIM))
    sa = np.tile(sa, (1, LANES // HEAD_DIM))
    cb32, sb32 = pattern(B_ROPE)
    tail = LANES - KPE_LANE - B_ROPE
    cb = np.concatenate([np.ones((DEC_SEQ, KPE_LANE), np.float32), cb32, np.ones((DEC_SEQ, tail), np.float32)], 1)
    sb = np.concatenate([np.zeros((DEC_SEQ, KPE_LANE), np.float32), sb32, np.zeros((DEC_SEQ, tail), np.float32)], 1)
    ident_c = np.ones((TM, LANES), np.float32)
    ident_s = np.zeros((TM, LANES), np.float32)
    return tuple(jnp.asarray(np.concatenate([tb, idt], axis=0))
                 for tb, idt in ((ca, ident_c), (sa, ident_s), (cb, ident_c), (sb, ident_s)))


REPACK_ROWS = 256
W_IN_COLS = A_Q + 2 * A_KV + B_Q_RANK + B_KV_RANK + B_ROPE + C_INNER + C_CONV_CH + 2 * C_HEADS


def _repack_kernel(w_ref, o_ref):
    o_kpe = OFF_KVC + B_KV_RANK
    o_z = o_kpe + B_ROPE
    o_dt = o_z + C_INNER + C_CONV_CH
    o_ref[:, 0:o_kpe] = w_ref[:, 0:o_kpe].astype(BF16)
    o_ref[:, OFF_Z:OFF_MISC] = w_ref[:, o_z:o_dt].astype(BF16)
    lane = _lane_iota((REPACK_ROWS, LANES))
    kpe = pltpu.roll(w_ref[:, o_kpe:o_kpe + LANES], KPE_LANE, axis=1)
    dt = pltpu.roll(w_ref[:, W_IN_COLS - LANES:W_IN_COLS], DT_LANE + 2 * C_HEADS, axis=1)
    misc = jnp.where((lane >= KPE_LANE) & (lane < DT_LANE), kpe,
                     jnp.where((lane >= DT_LANE) & (lane < DT_LANE + 2 * C_HEADS), dt, 0.0))
    o_ref[:, OFF_MISC:IN_COLS_P] = misc.astype(BF16)


def _repack_w_in(w_in):
    return pl.pallas_call(
        _repack_kernel,
        out_shape=jax.ShapeDtypeStruct((DEPTH, D_MODEL, IN_COLS_P), BF16),
        grid=(DEPTH, D_MODEL // REPACK_ROWS),
        in_specs=[pl.BlockSpec((None, REPACK_ROWS, W_IN_COLS), lambda l, i: (l, i, 0))],
        out_specs=pl.BlockSpec((None, REPACK_ROWS, IN_COLS_P), lambda l, i: (l, i, 0)),
        compiler_params=_cparams(("parallel", "parallel")),
        name="repack_w_in",
    )(w_in)


def _prep_weights(p):
    w_in_p = _repack_w_in(p["w_in"])

    w_qb = p["mla_w_qb"].reshape(DEPTH, B_Q_RANK, B_HEADS, B_NOPE + B_ROPE)
    w_qb_p = jnp.pad(w_qb, ((0, 0),) * 3 + ((0, LANES - B_NOPE - B_ROPE),)).reshape(DEPTH, B_Q_RANK, QB_W)
    w_kvb = p["mla_w_kvb"].reshape(DEPTH, B_KV_RANK, B_HEADS, B_NOPE + B_V)
    w_k = jnp.pad(w_kvb[..., :B_NOPE], ((0, 0),) * 3 + ((0, LANES - B_NOPE),)).reshape(DEPTH, B_KV_RANK, QB_W)
    w_v = w_kvb[..., B_NOPE:].reshape(DEPTH, B_KV_RANK, B_HEADS * B_V)

    w_out = p["w_out"]
    order = [r + g * A_REP for r in range(A_REP) for g in range(A_KV_HEADS)]
    w_out_p = jnp.concatenate([w_out[:, hd * HEAD_DIM:(hd + 1) * HEAD_DIM] for hd in order] + [w_out[:, A_Q:]],
                              axis=1).astype(BF16)

    row = lambda v: v.reshape(DEPTH, 1, -1)
    pair_gain = lambda g: row(jnp.tile(g, (1, LANES // HEAD_DIM)))
    lane_vec = lambda v: jnp.pad(v.reshape(DEPTH, 1, -1), ((0, 0), (0, 0), (DT_LANE, LANES - DT_LANE - 2 * C_HEADS)))
    return {
        "g_pre": row(p["norm_mix_pre"]),
        "w_in": w_in_p,
        "g_q": pair_gain(p["attn_q_norm"]),
        "g_k": pair_gain(p["attn_k_norm"]),
        "g_qc": row(p["mla_q_norm"]),
        "w_qb": w_qb_p.astype(BF16),
        "g_kv": row(p["mla_kv_norm"]),
        "w_k": w_k.astype(BF16),
        "w_v": w_v.astype(BF16),
        "conv_w": jnp.swapaxes(p["ssm_conv_w"], 1, 2),
        "conv_b": row(p["ssm_conv_b"]),
        "dt_bias": lane_vec(p["ssm_dt_bias"]),
        "a_log": lane_vec(p["ssm_a_log"]),
        "d_vec": row(jnp.repeat(p["ssm_d"], C_HEAD_DIM, axis=1)),
        "g_ssm": row(p["ssm_norm"]),
        "w_out": w_out_p,
        "g_mix": row(p["norm_mix_post"]),
        "g_ffn": row(p["norm_ffn_pre"]),
        "g_ffo": row(p["norm_ffn_post"]),
        "w1": p["w_ffn1"].astype(BF16),
        "w2": p["w_ffn2"].astype(BF16),
    }


def _state_to_pairs(h):
    ht = jnp.swapaxes(h, -1, -2)
    zero = jnp.zeros_like(ht[..., 0, :, :])
    pairs = []
    for pr in range(N_PAIRS):
        heads = (2 * pr, 2 * pr + 1)
        row_blocks = [jnp.concatenate([ht[..., hd, :, :] if hd // HEADS_PER_GROUP == g else zero for hd in heads],
                                      axis=-1) for g in range(C_GROUPS)]
        pairs.append(jnp.concatenate(row_blocks, axis=-2))
    return jnp.stack(pairs, axis=-3)


def kernel(x_prompt, x_sample, cache_attn_k, cache_attn_v, cache_mla_ckv, cache_mla_kpe, state_ssm, c, c_ctx, norm_mix_pre, norm_mix_post, norm_ffn_pre, norm_ffn_post, w_mod, b_mod, w_in, attn_q_norm, attn_k_norm, mla_q_norm, mla_w_qb, mla_kv_norm, mla_w_kvb, ssm_conv_w, ssm_conv_b, ssm_dt_bias, ssm_a_log, ssm_d, ssm_norm, w_out, w_ffn1, w_ffn2):
    p = dict(norm_mix_pre=norm_mix_pre, norm_mix_post=norm_mix_post, norm_ffn_pre=norm_ffn_pre,
             norm_ffn_post=norm_ffn_post, w_in=w_in, attn_q_norm=attn_q_norm, attn_k_norm=attn_k_norm,
             mla_q_norm=mla_q_norm, mla_w_qb=mla_w_qb, mla_kv_norm=mla_kv_norm, mla_w_kvb=mla_w_kvb,
             ssm_conv_w=ssm_conv_w, ssm_conv_b=ssm_conv_b, ssm_dt_bias=ssm_dt_bias, ssm_a_log=ssm_a_log,
             ssm_d=ssm_d, ssm_norm=ssm_norm, w_out=w_out, w_ffn1=w_ffn1, w_ffn2=w_ffn2)
    lw = _prep_weights(p)
    tabs = _rope_tables()

    cvec = jnp.concatenate([c_ctx[None, :], c, jnp.zeros((MOD_ROWS - 1 - DEC_BATCH, D_MODEL), F32)], axis=0)
    mod = _modulation(cvec, w_mod, b_mod).reshape(DEPTH, MOD_ROWS, 1, 6 * D_MODEL)

    kpe_p = jnp.pad(cache_mla_kpe, ((0, 0),) * 3 + ((KPE_LANE, LANES - KPE_LANE - B_ROPE),))
    kc, vct, kbc, vbct = _cache_prep(
        cache_attn_k.reshape(DEC_BATCH, DEPTH, PAST_LEN, A_KV), cache_attn_v.reshape(DEC_BATCH, DEPTH, PAST_LEN, A_KV),
        cache_mla_ckv, kpe_p, lw["w_k"], lw["w_v"])
    h0_lat = _state_to_pairs(state_ssm)

    xp = x_prompt.reshape(BATCH * SEQ, D_MODEL)
    xs = x_sample.reshape(DEC_BATCH * DEC_SEQ, D_MODEL)
    new_k, new_v, new_ckv, new_kpe, new_ssm = [], [], [], [], []
    for l in range(DEPTH):
        qat, ka, vat, qbt, kb, vbt, z, xbc, misc, kf, vf, ckvf = _pre_ctx(l, xp, mod, lw, tabs)
        lat_pre, (oa, ob, y, hfin) = _pre_lat_mix_ctx(l, xs, mod, lw, tabs, (qat, ka, vat, qbt, kb, vbt, xbc, misc))
        xp = _post(l, True, xp, mod, oa, ob, y, z, lw)
        new_k.append(kf.reshape(BATCH, SEQ, A_KV_HEADS, HEAD_DIM))
        new_v.append(vf.reshape(BATCH, SEQ, A_KV_HEADS, HEAD_DIM))
        new_ckv.append(ckvf.reshape(BATCH, SEQ, B_KV_RANK))
        new_kpe.append(misc[:, KPE_LANE:KPE_LANE + B_ROPE].reshape(BATCH, SEQ, B_ROPE))
        new_ssm.append(hfin)
        qat, ka, vat, qbt, kb, vbt, z, xbc, misc = lat_pre
        oa = _attn_lat(l, True, qat, ka, vat, kc, vct)
        ob = _attn_lat(l, False, qbt, kb, vbt, kbc, vbct)
        y, _ = _ssd_lat(l, xbc, misc, lw, h0_lat)
        xs = _post(l, False, xs, mod, oa, ob, y, z, lw)
    return (xp.reshape(BATCH, SEQ, D_MODEL), xs.reshape(DEC_BATCH, DEC_SEQ, D_MODEL),
            jnp.stack(new_k, axis=1), jnp.stack(new_v, axis=1), jnp.stack(new_ckv, axis=1),
            jnp.stack(new_kpe, axis=1), jnp.stack(new_ssm, axis=1))
```

```python
import functools
import math

import numpy as np
import jax
import jax.numpy as jnp
from jax import lax
from jax.experimental import pallas as pl
from jax.experimental.pallas import tpu as pltpu

F32 = jnp.float32
BF16 = jnp.bfloat16

D_MODEL = 1024
BATCH = 16
SEQ = 256
DEPTH = 4
DEC_BATCH = 2
DEC_SEQ = 4096
PAST_LEN = 256
GRID_W = 64
HEAD_DIM = 64
A_HEADS = 6
A_KV_HEADS = 2
A_REP = A_HEADS // A_KV_HEADS
B_HEADS = 4
B_Q_RANK = 256
B_KV_RANK = 128
B_NOPE = 64
B_ROPE = 32
B_V = 64
C_HEADS = 6
C_HEAD_DIM = 64
C_INNER = C_HEADS * C_HEAD_DIM
C_GROUPS = 2
C_STATE = 64
C_CONV = 5
C_CHUNK = 128
C_CONV_CH = C_INNER + 2 * C_GROUPS * C_STATE
D_FF = 4 * D_MODEL
A_Q = A_HEADS * HEAD_DIM
A_KV = A_KV_HEADS * HEAD_DIM
ROPE_THETA = 10000.0
EPS = 1e-6
LOG2E = math.log2(math.e)

LANES = 128
SUBLANES = 8
VMEM_LIMIT = 56 * 1024 * 1024

OFF_QA = 0
OFF_KA = OFF_QA + A_Q
OFF_VA = OFF_KA + A_KV
OFF_QC = OFF_VA + A_KV
OFF_KVC = OFF_QC + B_Q_RANK
OFF_Z = OFF_KVC + B_KV_RANK
OFF_XBC = OFF_Z + C_INNER
OFF_MISC = OFF_XBC + C_CONV_CH
IN_COLS_P = OFF_MISC + LANES
KPE_LANE = 64
DT_LANE = KPE_LANE + B_ROPE
QB_W = B_HEADS * LANES

TM = 512
TQ = 512
TK = 1024


def _cparams(sem):
    return pltpu.CompilerParams(dimension_semantics=sem, vmem_limit_bytes=VMEM_LIMIT)


def _rms(x, g):
    return x * lax.rsqrt(jnp.mean(x * x, axis=-1, keepdims=True) + EPS) * g


def _bdot(a, b):
    return jnp.dot(a.astype(BF16), b.astype(BF16), preferred_element_type=F32)


def _lane_iota(shape):
    return lax.broadcasted_iota(jnp.int32, shape, len(shape) - 1)


def _swap_lanes(x, dist):
    lane = _lane_iota(x.shape)
    fwd = pltpu.roll(x, LANES - dist, axis=1)
    bwd = pltpu.roll(x, dist, axis=1)
    return jnp.where((lane % (2 * dist)) < dist, fwd, bwd)


def _rope(x, cos, sin, dist):
    return x * cos + _swap_lanes(x, dist) * sin


def _head_pair_rms(col, gain):
    lo = _lane_iota(col.shape) < HEAD_DIM
    c2 = col * col
    s_lo = jnp.sum(jnp.where(lo, c2, 0.0), axis=-1, keepdims=True)
    s_hi = jnp.sum(jnp.where(lo, 0.0, c2), axis=-1, keepdims=True)
    return col * lax.rsqrt(jnp.where(lo, s_lo, s_hi) * (1.0 / HEAD_DIM) + EPS) * gain


MOD_ROWS = SUBLANES
MOD_TN = 1536


def _mod_kernel(c_ref, w_ref, b_ref, o_ref):
    c = c_ref[...]
    o_ref[...] = _bdot(c * jax.nn.sigmoid(c), w_ref[...]) + b_ref[...]


def _modulation(cvec, w_mod, b_mod):
    return pl.pallas_call(
        _mod_kernel,
        out_shape=jax.ShapeDtypeStruct((DEPTH, MOD_ROWS, 6 * D_MODEL), F32),
        grid=(DEPTH, 6 * D_MODEL // MOD_TN),
        in_specs=[
            pl.BlockSpec((MOD_ROWS, D_MODEL), lambda l, j: (0, 0)),
            pl.BlockSpec((None, D_MODEL, MOD_TN), lambda l, j: (l, 0, j)),
            pl.BlockSpec((None, 1, MOD_TN), lambda l, j: (l, 0, j)),
        ],
        out_specs=pl.BlockSpec((None, MOD_ROWS, MOD_TN), lambda l, j: (l, 0, j)),
        compiler_params=_cparams(("parallel", "parallel")),
        name="modulation",
    )(cvec, w_mod, b_mod.reshape(DEPTH, 1, 6 * D_MODEL))


def _cache_kernel(k_ref, v_ref, ckv_ref, kpe_ref, wk_ref, wv_ref, ko_ref, vt_ref, kb_ref, vbt_ref):
    ko_ref[...] = k_ref[...].astype(BF16)
    vt_ref[...] = v_ref[...].T.astype(BF16)
    ckv = ckv_ref[...]
    kn = _bdot(ckv, wk_ref[...])
    kpe = kpe_ref[...]
    for h in range(B_HEADS):
        kb_ref[:, h * LANES:(h + 1) * LANES] = (kn[:, h * LANES:(h + 1) * LANES] + kpe).astype(BF16)
    vbt_ref[...] = _bdot(ckv, wv_ref[...]).T.astype(BF16)


def _cache_prep(cache_k, cache_v, cache_ckv, cache_kpe_p, wk_p, wv_p):
    spec_tok = lambda w: pl.BlockSpec((None, None, PAST_LEN, w), lambda l, b: (b, l, 0, 0))
    spec_t = lambda w: pl.BlockSpec((None, None, w, PAST_LEN), lambda l, b: (l, b, 0, 0))
    spec_o = lambda w: pl.BlockSpec((None, None, PAST_LEN, w), lambda l, b: (l, b, 0, 0))
    return pl.pallas_call(
        _cache_kernel,
        out_shape=(
            jax.ShapeDtypeStruct((DEPTH, DEC_BATCH, PAST_LEN, A_KV), BF16),
            jax.ShapeDtypeStruct((DEPTH, DEC_BATCH, A_KV, PAST_LEN), BF16),
            jax.ShapeDtypeStruct((DEPTH, DEC_BATCH, PAST_LEN, QB_W), BF16),
            jax.ShapeDtypeStruct((DEPTH, DEC_BATCH, B_HEADS * B_V, PAST_LEN), BF16),
        ),
        grid=(DEPTH, DEC_BATCH),
        in_specs=[
            spec_tok(A_KV), spec_tok(A_KV), spec_tok(B_KV_RANK), spec_tok(LANES),
            pl.BlockSpec((None, B_KV_RANK, QB_W), lambda l, b: (l, 0, 0)),
            pl.BlockSpec((None, B_KV_RANK, B_HEADS * B_V), lambda l, b: (l, 0, 0)),
        ],
        out_specs=(spec_o(A_KV), spec_t(A_KV), spec_o(QB_W), spec_t(B_HEADS * B_V)),
        compiler_params=_cparams(("parallel", "parallel")),
        name="cache_prep",
    )(cache_k, cache_v, cache_ckv, cache_kpe_p, wk_p, wv_p)


PRE_SUB = 2


def _pre_kernel(is_ctx, *refs):
    for sub in range(PRE_SUB):
        _pre_rows(is_ctx, slice(sub * (TM // PRE_SUB), (sub + 1) * (TM // PRE_SUB)), *refs)


def _pre_rows(is_ctx, rows, x_ref, mod_ref, gpre_ref, win_ref, gq_ref, gk_ref, gqc_ref, wqb_ref, gkv_ref,
              wk_ref, wv_ref, dtb_ref, ca_ref, sa_ref, cb_ref, sb_ref, *outs):
    if is_ctx:
        (qat_ref, ka_ref, vat_ref, qbt_ref, kb_ref, vbt_ref, z_ref, xbc_ref, misc_ref,
         kf_ref, vf_ref, ckvf_ref) = outs
    else:
        qat_ref, ka_ref, vat_ref, qbt_ref, kb_ref, vbt_ref, z_ref, xbc_ref, misc_ref = outs
    x = x_ref[rows, :]
    shift1 = mod_ref[:, 0:D_MODEL]
    scale1 = mod_ref[:, D_MODEL:2 * D_MODEL]
    h = _rms(x, gpre_ref[...]) * (1.0 + scale1) + shift1
    proj = _bdot(h, win_ref[...])

    ca, sa, cb, sb = ca_ref[rows, :], sa_ref[rows, :], cb_ref[rows, :], sb_ref[rows, :]
    lane = _lane_iota((x.shape[0], LANES))

    gq = gq_ref[...]
    for cidx in range(A_Q // LANES):
        col = proj[:, OFF_QA + cidx * LANES:OFF_QA + (cidx + 1) * LANES]
        qn = _rope(_head_pair_rms(col, gq), ca, sa, HEAD_DIM // 4) * (HEAD_DIM ** -0.5 * LOG2E)
        qat_ref[cidx * LANES:(cidx + 1) * LANES, rows] = qn.T.astype(BF16)

    kn = _head_pair_rms(proj[:, OFF_KA:OFF_KA + A_KV], gk_ref[...])
    vcol = proj[:, OFF_VA:OFF_VA + A_KV]
    if is_ctx:
        kf_ref[rows, :] = kn
        vf_ref[rows, :] = vcol
    ka_ref[rows, :] = _rope(kn, ca, sa, HEAD_DIM // 4).astype(BF16)
    vat_ref[:, rows] = vcol.T.astype(BF16)

    qc = _rms(proj[:, OFF_QC:OFF_QC + B_Q_RANK], gqc_ref[...])
    qb = _bdot(qc, wqb_ref[...])
    for hd in range(B_HEADS):
        col = qb[:, hd * LANES:(hd + 1) * LANES]
        col = _rope(col, cb, sb, B_ROPE // 4) * ((B_NOPE + B_ROPE) ** -0.5 * LOG2E)
        qbt_ref[hd * LANES:(hd + 1) * LANES, rows] = col.T.astype(BF16)

    ckv = _rms(proj[:, OFF_KVC:OFF_KVC + B_KV_RANK], gkv_ref[...])
    if is_ctx:
        ckvf_ref[rows, :] = ckv
    knope = _bdot(ckv, wk_ref[...])
    misc = proj[:, OFF_MISC:OFF_MISC + LANES]
    kpe = jnp.where((lane >= KPE_LANE) & (lane < KPE_LANE + B_ROPE), _rope(misc, cb, sb, B_ROPE // 4), 0.0)
    for hd in range(B_HEADS):
        kb_ref[rows, hd * LANES:(hd + 1) * LANES] = (knope[:, hd * LANES:(hd + 1) * LANES] + kpe).astype(BF16)
    vbt_ref[:, rows] = _bdot(ckv, wv_ref[...]).T.astype(BF16)

    x_dt = misc + dtb_ref[...]
    dtv = jnp.maximum(x_dt, 0.0) + jnp.log(1.0 + jnp.exp(-jnp.abs(x_dt)))
    misc_ref[rows, :] = jnp.where(lane >= DT_LANE, dtv, misc)

    z_ref[rows, :] = proj[:, OFF_Z:OFF_Z + C_INNER]
    xbc_ref[rows, :] = proj[:, OFF_XBC:OFF_XBC + C_CONV_CH]


def _pre_parts(l, is_ctx, x2d, mod, lw, tabs):
    t = x2d.shape[0]
    per_seq = DEC_SEQ // TM
    if is_ctx:
        mod_map = lambda i: (l, 0, 0, 0)
        tab_map = lambda i: (per_seq, 0)
    else:
        mod_map = lambda i: (l, 1 + i // per_seq, 0, 0)
        tab_map = lambda i: (i % per_seq, 0)
    const = lambda shape: pl.BlockSpec((None,) + shape, lambda i: (l,) + (0,) * len(shape))
    tab_spec = pl.BlockSpec((TM, LANES), tab_map)
    tok = lambda w: pl.BlockSpec((TM, w), lambda i: (i, 0))
    tok_t = lambda w: pl.BlockSpec((w, TM), lambda i: (0, i))
    out_shape = [
        jax.ShapeDtypeStruct((A_Q, t), BF16),
        jax.ShapeDtypeStruct((t, A_KV), BF16),
        jax.ShapeDtypeStruct((A_KV, t), BF16),
        jax.ShapeDtypeStruct((QB_W, t), BF16),
        jax.ShapeDtypeStruct((t, QB_W), BF16),
        jax.ShapeDtypeStruct((B_HEADS * B_V, t), BF16),
        jax.ShapeDtypeStruct((t, C_INNER), F32),
        jax.ShapeDtypeStruct((t, C_CONV_CH), F32),
        jax.ShapeDtypeStruct((t, LANES), F32),
    ]
    out_specs = [tok_t(A_Q), tok(A_KV), tok_t(A_KV), tok_t(QB_W), tok(QB_W), tok_t(B_HEADS * B_V),
                 tok(C_INNER), tok(C_CONV_CH), tok(LANES)]
    if is_ctx:
        out_shape += [jax.ShapeDtypeStruct((t, A_KV), F32)] * 2 + [jax.ShapeDtypeStruct((t, B_KV_RANK), F32)]
        out_specs += [tok(A_KV), tok(A_KV), tok(B_KV_RANK)]
    in_specs = [
        tok(D_MODEL),
        pl.BlockSpec((None, None, 1, 6 * D_MODEL), mod_map),
        const((1, D_MODEL)),
        const((D_MODEL, IN_COLS_P)),
        const((1, LANES)), const((1, LANES)), const((1, B_Q_RANK)),
        const((B_Q_RANK, QB_W)), const((1, B_KV_RANK)),
        const((B_KV_RANK, QB_W)), const((B_KV_RANK, B_HEADS * B_V)), const((1, LANES)),
        tab_spec, tab_spec, tab_spec, tab_spec,
    ]
    args = [x2d, mod, lw["g_pre"], lw["w_in"], lw["g_q"], lw["g_k"], lw["g_qc"], lw["w_qb"], lw["g_kv"],
            lw["w_k"], lw["w_v"], lw["dt_bias"], *tabs]
    return in_specs, args, out_shape, out_specs


def _pre_ctx(l, x2d, mod, lw, tabs):
    in_specs, args, out_shape, out_specs = _pre_parts(l, True, x2d, mod, lw, tabs)
    return pl.pallas_call(
        functools.partial(_pre_kernel, True),
        out_shape=tuple(out_shape),
        grid=(x2d.shape[0] // TM,),
        in_specs=in_specs,
        out_specs=tuple(out_specs),
        compiler_params=_cparams(("parallel",)),
        name="pre_ctx",
    )(*args)


def _merge_halves(lo_part, hi_part):
    return jnp.where(_lane_iota(lo_part.shape) < HEAD_DIM, lo_part, hi_part)


def _gqa_query_slot(qat_ref, hd):
    q = qat_ref[hd * HEAD_DIM:(hd + 1) * HEAD_DIM, :]
    zero = jnp.zeros_like(q)
    return jnp.concatenate([q, zero] if hd // A_REP == 0 else [zero, q], axis=0)


ONES_ROWS = 16
ACC_ROWS = LANES + ONES_ROWS


def _with_ones(v_t):
    return jnp.concatenate([v_t, jnp.ones((ONES_ROWS, v_t.shape[1]), BF16)], axis=0)


def _first_tile(k, q_t, v_t):
    return _first_scores(jnp.dot(k, q_t, preferred_element_type=F32), v_t)


def _first_scores(s, v_t):
    m = jnp.max(s, axis=0, keepdims=True)
    p = jnp.exp2(s - m)
    return m, jnp.dot(_with_ones(v_t), p.astype(BF16), preferred_element_type=F32)


def _next_tile(s, m, acc, v_t):
    m_new = jnp.maximum(m, jnp.max(s, axis=0, keepdims=True))
    p = jnp.exp2(s - m_new)
    acc = jnp.exp2(m - m_new) * acc + jnp.dot(_with_ones(v_t), p.astype(BF16), preferred_element_type=F32)
    return m_new, acc


def _normalised(acc):
    return (acc[0:LANES, :] / acc[LANES:LANES + 1, :]).T


def _write_heads(outs_a, outs_b, oa_ref, ob_ref):
    for r in range(A_REP):
        oa_ref[:, r * LANES:(r + 1) * LANES] = _merge_halves(outs_a[r], outs_a[A_REP + r]).astype(BF16)
    for pr in range(B_HEADS // 2):
        ob_ref[:, pr * LANES:(pr + 1) * LANES] = _merge_halves(outs_b[2 * pr], outs_b[2 * pr + 1]).astype(BF16)


def _attn_ctx_kernel(qat_ref, ka_ref, vat_ref, qbt_ref, kb_ref, vbt_ref, oa_ref, ob_ref):
    def split_heads(acc, n):
        o = acc[0:LANES, :] / acc[LANES:LANES + 1, :]
        return [o[:, i * SEQ:(i + 1) * SEQ].T for i in range(n)]

    q_all = jnp.concatenate([_gqa_query_slot(qat_ref, hd) for hd in range(A_HEADS)], axis=1)
    outs_a = split_heads(_first_tile(ka_ref[...], q_all, vat_ref[...])[1], A_HEADS)
    outs_b = []
    for pr in range(B_HEADS // 2):
        s = jnp.concatenate([jnp.dot(kb_ref[:, hd * LANES:(hd + 1) * LANES], qbt_ref[hd * LANES:(hd + 1) * LANES, :],
                                     preferred_element_type=F32) for hd in (2 * pr, 2 * pr + 1)], axis=1)
        outs_b += split_heads(_first_scores(s, vbt_ref[pr * LANES:(pr + 1) * LANES, :])[1], 2)
    _write_heads(outs_a, outs_b, oa_ref, ob_ref)


def _attn_lat_kernel(is_gqa, *refs):
    if is_gqa:
        q0_ref, q1_ref, k_ref, vt_ref, kc_ref, vct_ref, o_ref, s_ref, acc_ref = refs
        zero = jnp.zeros((HEAD_DIM, TQ), BF16)
        q_slots = (jnp.concatenate([q0_ref[...], zero], axis=0), jnp.concatenate([zero, q1_ref[...]], axis=0))
        key_cols = (0, 0)
    else:
        q_ref, k_ref, vt_ref, kc_ref, vct_ref, o_ref, s_ref, acc_ref = refs
        q_slots = (q_ref[0:LANES, :], q_ref[LANES:2 * LANES, :])
        key_cols = (0, LANES)
    n_tiles = DEC_SEQ // TK
    head_cut, tail_cut = TK // 4, TK - TK // 4
    plans = ([("cache", 0, PAST_LEN), ("lat", 0, TK // 2)] + [("lat", TK // 2 + j * TK, TK) for j in range(n_tiles - 1)]
             + [("lat", DEC_SEQ - TK // 2, TK // 2)],
             [("lat", 0, head_cut)] + [("lat", head_cut + j * TK, TK) for j in range(n_tiles - 1)]
             + [("lat", DEC_SEQ - tail_cut, tail_cut), ("cache", 0, PAST_LEN)])

    def keys(seg, c0):
        src, k0, n = seg
        return (kc_ref if src == "cache" else k_ref)[k0:k0 + n, c0:c0 + LANES]

    def vals(seg):
        src, k0, n = seg
        return (vct_ref if src == "cache" else vt_ref)[:, k0:k0 + n]

    row_max = [None, None]
    for hh in range(2):
        seg = plans[hh][0]
        s_ref[hh, 0, 0:seg[2], 0:TQ] = jnp.dot(keys(seg, key_cols[hh]), q_slots[hh], preferred_element_type=F32)
    for j in range(max(len(plan) for plan in plans)):
        for hh in range(2):
            plan = plans[hh]
            if j >= len(plan):
                continue
            if j + 1 < len(plan):
                nxt = plan[j + 1]
                s_ref[hh, (j + 1) % 2, 0:nxt[2], 0:TQ] = jnp.dot(keys(nxt, key_cols[hh]), q_slots[hh],
                                                              preferred_element_type=F32)
            seg = plan[j]
            s = s_ref[hh, j % 2, 0:seg[2], 0:TQ]
            if j == 0:
                row_max[hh], acc = _first_scores(s, vals(seg))
            else:
                row_max[hh], acc = _next_tile(s, row_max[hh], acc_ref[hh], vals(seg))
            acc_ref[hh] = acc
    o_ref[...] = _merge_halves(_normalised(acc_ref[0]), _normalised(acc_ref[1])).astype(BF16)


def _attn_lat(l, is_gqa, q_t, k, v_t, kc, vc_t):
    t = k.shape[0]
    nq = DEC_SEQ // TQ
    n_pairs = A_REP if is_gqa else B_HEADS // 2
    kw = A_KV if is_gqa else 2 * LANES
    if is_gqa:
        q_specs = [pl.BlockSpec((HEAD_DIM, TQ), lambda b, r, i: (r, b * nq + i)),
                   pl.BlockSpec((HEAD_DIM, TQ), lambda b, r, i: (A_REP + r, b * nq + i))]
        q_args = [q_t, q_t]
        pair_col = lambda r: 0
    else:
        q_specs = [pl.BlockSpec((2 * LANES, TQ), lambda b, r, i: (r, b * nq + i))]
        q_args = [q_t]
        pair_col = lambda r: r
    return pl.pallas_call(
        functools.partial(_attn_lat_kernel, is_gqa),
        out_shape=jax.ShapeDtypeStruct((t, n_pairs * LANES), BF16),
        grid=(DEC_BATCH, n_pairs, nq),
        in_specs=q_specs + [
            pl.BlockSpec((DEC_SEQ, kw), lambda b, r, i: (b, pair_col(r))),
            pl.BlockSpec((LANES, DEC_SEQ), lambda b, r, i: (pair_col(r), b)),
            pl.BlockSpec((None, None, PAST_LEN, kw), lambda b, r, i: (l, b, 0, pair_col(r))),
            pl.BlockSpec((None, None, LANES, PAST_LEN), lambda b, r, i: (l, b, pair_col(r), 0)),
        ],
        out_specs=pl.BlockSpec((TQ, LANES), lambda b, r, i: (b * nq + i, r)),
        scratch_shapes=[pltpu.VMEM((2, 2, TK, TQ + (0 if is_gqa else LANES)), F32),
                        pltpu.VMEM((2, ACC_ROWS, TQ), F32)],
        compiler_params=_cparams(("parallel", "parallel", "parallel")),
        name="attn_lat_gqa" if is_gqa else "attn_lat_mla",
    )(*q_args, k, v_t, kc, vc_t)


HALO = SUBLANES
N_PAIRS = C_HEADS // 2
HEADS_PER_GROUP = C_HEADS // C_GROUPS


def _ssd_kernel(n_chunks, has_h0, xbc_ref, misc_ref, cw_ref, cb_ref, alog_ref, dvec_ref, *rest):
    if has_h0:
        h0_ref, y_ref, hout_ref, stf_ref, stb_ref, newb_ref, cm_ref, ecx_ref = rest
        stf_ref[...] = h0_ref[0]
        stb_ref[...] = h0_ref[1]
    else:
        y_ref, hout_ref, stf_ref, stb_ref, newb_ref, cm_ref, ecx_ref = rest
        stf_ref[...] = jnp.zeros_like(stf_ref)
        stb_ref[...] = jnp.zeros_like(stb_ref)
    seq_len = n_chunks * C_CHUNK
    row = lax.broadcasted_iota(jnp.int32, (C_CHUNK, LANES), 0)
    lane = _lane_iota((C_CHUNK, LANES))
    lo = lane < C_HEAD_DIM
    lane_group = (lane >= C_STATE).astype(jnp.int32)
    row_group = (row >= C_STATE).astype(jnp.int32)
    causal = (row >= lane, row <= lane)
    neg_a = -jnp.exp(alog_ref[...])
    dt_lanes = (lane >= DT_LANE) & (lane < DT_LANE + 2 * C_HEADS)

    def lane_of(direction, hd):
        return DT_LANE + C_HEADS * direction + hd

    def pair_cols(v, direction, heads):
        j0, j1 = lane_of(direction, heads[0]), lane_of(direction, heads[1])
        return jnp.where(lo[0:v.shape[0], :], v[:, j0:j0 + 1], v[:, j1:j1 + 1])

    def pass1(c, carry):
        start = pl.multiple_of(c * C_CHUNK, C_CHUNK)
        prev0 = pl.multiple_of(jnp.maximum(start - HALO, 0), HALO)
        next0 = pl.multiple_of(jnp.minimum(start + C_CHUNK, seq_len - HALO), HALO)
        keep_prev = jnp.where(c > 0, 1.0, 0.0).astype(F32)
        keep_next = jnp.where(c < n_chunks - 1, 1.0, 0.0).astype(F32)
        rows_c = pl.ds(start, C_CHUNK)

        ue = jnp.concatenate([xbc_ref[pl.ds(prev0, HALO), :] * keep_prev, xbc_ref[rows_c, :],
                              xbc_ref[pl.ds(next0, HALO), :] * keep_next], axis=0)
        acc = jnp.broadcast_to(cb_ref[...], (C_CHUNK, C_CONV_CH))
        for k in range(C_CONV):
            sh = (C_CONV // 2 - k) % ue.shape[0]
            r = ue if sh == 0 else pltpu.roll(ue, sh, axis=0)
            acc = acc + r[HALO:HALO + C_CHUNK, :] * cw_ref[k:k + 1, :]
        xc = acc * jax.nn.sigmoid(acc)
        xs = xc[:, 0:C_INNER]
        bm = xc[:, C_INNER:C_INNER + LANES]
        cm = xc[:, C_INNER + LANES:C_INNER + 2 * LANES]
        bmt = bm.T.astype(BF16)
        cm16 = cm.astype(BF16)
        g_mats = [jnp.dot(jnp.where(lane_group == g, cm, 0.0).astype(BF16), bmt, preferred_element_type=F32)
                  for g in range(C_GROUPS)]

        dtv = jnp.where(dt_lanes, misc_ref[rows_c, :], 0.0)
        dta = dtv * neg_a
        cum = dta
        step = 1
        while step < C_CHUNK:
            cum = cum + jnp.where(row >= step, pltpu.roll(cum, step, axis=0), 0.0)
            step *= 2
        tot = cum[C_CHUNK - 1:C_CHUNK, :]
        cxs = (cum, tot - cum + dta)
        cxts = (cxs[0].T, cxs[1].T)
        dtt = dtv.T
        wsts = tuple(jnp.exp(tot - cx) * dtv for cx in cxs)
        ecx_f = jnp.exp(cxs[0])
        cdec = jnp.exp(tot)
        cm_ref[c] = cm16
        ecx_ref[c, 0:C_CHUNK, :] = jnp.exp(cxs[1])
        ecx_ref[c, C_CHUNK:C_CHUNK + SUBLANES, :] = jnp.broadcast_to(cdec, (SUBLANES, LANES))

        for pr in range(N_PAIRS):
            heads = (2 * pr, 2 * pr + 1)
            groups = tuple(hd // HEADS_PER_GROUP for hd in heads)
            xs_pair = xs[:, pr * LANES:(pr + 1) * LANES]
            xs16 = xs_pair.astype(BF16)
            own = row_group == jnp.where(lo, groups[0], groups[1])
            y = xs_pair * dvec_ref[:, pr * LANES:(pr + 1) * LANES]
            for direction in range(2):
                yd = []
                for hd in heads:
                    j = lane_of(direction, hd)
                    seg = jnp.where(causal[direction], cxs[direction][:, j:j + 1] - cxts[direction][j:j + 1, :],
                                    -jnp.inf)
                    sc = g_mats[hd // HEADS_PER_GROUP] * jnp.exp(seg) * dtt[j:j + 1, :]
                    yd.append(jnp.dot(sc.astype(BF16), xs16, preferred_element_type=F32))
                y = y + _merge_halves(yd[0], yd[1])
                xw = xs_pair * pair_cols(wsts[direction], direction, heads)
                new = jnp.where(own, jnp.dot(bmt, xw.astype(BF16), preferred_element_type=F32), 0.0)
                if direction == 0:
                    st = stf_ref[pr]
                    y_off = jnp.dot(cm16, st.astype(BF16), preferred_element_type=F32)
                    y = y + y_off * pair_cols(ecx_f, 0, heads)
                    stf_ref[pr] = st * pair_cols(cdec, 0, heads) + new
                else:
                    newb_ref[c, pr] = new
            y_ref[rows_c, pr * LANES:(pr + 1) * LANES] = y
        return carry

    def pass2(i, carry):
        c = n_chunks - 1 - i
        rows_c = pl.ds(pl.multiple_of(c * C_CHUNK, C_CHUNK), C_CHUNK)
        cm16 = cm_ref[c]
        ecx_b = ecx_ref[c, 0:C_CHUNK, :]
        cdec = ecx_ref[c, C_CHUNK:C_CHUNK + 1, :]
        for pr in range(N_PAIRS):
            heads = (2 * pr, 2 * pr + 1)
            st = stb_ref[pr]
            y_off = jnp.dot(cm16, st.astype(BF16), preferred_element_type=F32)
            cols = pl.ds(pr * LANES, LANES)
            y_ref[rows_c, cols] = y_ref[rows_c, cols] + y_off * pair_cols(ecx_b, 1, heads)
            stb_ref[pr] = st * pair_cols(cdec, 1, heads) + newb_ref[c, pr]
        return carry

    unroll = n_chunks <= 2
    lax.fori_loop(0, n_chunks, pass1, 0, unroll=unroll)
    lax.fori_loop(0, n_chunks, pass2, 0, unroll=unroll)

    for direction, st_ref in enumerate((stf_ref, stb_ref)):
        for pr in range(N_PAIRS):
            st_t = st_ref[pr].T
            for hh, hd in enumerate((2 * pr, 2 * pr + 1)):
                g = hd // HEADS_PER_GROUP
                hout_ref[direction, hd] = st_t[hh * C_HEAD_DIM:(hh + 1) * C_HEAD_DIM,
                                               g * C_STATE:(g + 1) * C_STATE]


def _ssd_scratch(nc):
    return [pltpu.VMEM((N_PAIRS, LANES, LANES), F32), pltpu.VMEM((N_PAIRS, LANES, LANES), F32),
            pltpu.VMEM((nc, N_PAIRS, LANES, LANES), F32), pltpu.VMEM((nc, C_CHUNK, LANES), BF16),
            pltpu.VMEM((nc, C_CHUNK + SUBLANES, LANES), F32)]


def _ssd_lat(l, xbc, misc, lw, h0):
    t = xbc.shape[0]
    nb = t // DEC_SEQ
    nc = DEC_SEQ // C_CHUNK
    const = lambda shape: pl.BlockSpec((None,) + shape, lambda b: (l,) + (0,) * len(shape))
    return pl.pallas_call(
        functools.partial(_ssd_kernel, nc, True),
        out_shape=(jax.ShapeDtypeStruct((t, C_INNER), F32),
                   jax.ShapeDtypeStruct((nb, 2, C_HEADS, C_HEAD_DIM, C_STATE), F32)),
        grid=(nb,),
        in_specs=[
            pl.BlockSpec((DEC_SEQ, C_CONV_CH), lambda b: (b, 0)),
            pl.BlockSpec((DEC_SEQ, LANES), lambda b: (b, 0)),
            const((C_CONV, C_CONV_CH)), const((1, C_CONV_CH)), const((1, LANES)), const((1, C_INNER)),
            pl.BlockSpec((None, None, 2, N_PAIRS, LANES, LANES), lambda b: (b, l, 0, 0, 0, 0)),
        ],
        out_specs=(pl.BlockSpec((DEC_SEQ, C_INNER), lambda b: (b, 0)),
                   pl.BlockSpec((None, 2, C_HEADS, C_HEAD_DIM, C_STATE), lambda b: (b, 0, 0, 0, 0))),
        scratch_shapes=_ssd_scratch(nc),
        compiler_params=_cparams(("parallel",)),
        name="ssd_lat",
    )(xbc, misc, lw["conv_w"], lw["conv_b"], lw["a_log"], lw["d_vec"], h0)


def _ctx_mix_kernel(n_chunks, qat_ref, ka_ref, vat_ref, qbt_ref, kb_ref, vbt_ref, xbc_ref, misc_ref, cw_ref, cb_ref,
                    alog_ref, dvec_ref, oa_ref, ob_ref, y_ref, hout_ref, *scratch):
    _attn_ctx_kernel(qat_ref, ka_ref, vat_ref, qbt_ref, kb_ref, vbt_ref, oa_ref, ob_ref)
    _ssd_kernel(n_chunks, False, xbc_ref, misc_ref, cw_ref, cb_ref, alog_ref, dvec_ref, y_ref, hout_ref, *scratch)


def _ctx_mix_parts(l, qat, ka, vat, qbt, kb, vbt, xbc, misc, lw):
    t = ka.shape[0]
    tok = lambda w: pl.BlockSpec((SEQ, w), lambda b: (b, 0))
    tok_t = lambda w: pl.BlockSpec((w, SEQ), lambda b: (0, b))
    const = lambda shape: pl.BlockSpec((None,) + shape, lambda b: (l,) + (0,) * len(shape))
    in_specs = [tok_t(A_Q), tok(A_KV), tok_t(A_KV), tok_t(QB_W), tok(QB_W), tok_t(B_HEADS * B_V),
                tok(C_CONV_CH), tok(LANES),
                const((C_CONV, C_CONV_CH)), const((1, C_CONV_CH)), const((1, LANES)), const((1, C_INNER))]
    args = [qat, ka, vat, qbt, kb, vbt, xbc, misc, lw["conv_w"], lw["conv_b"], lw["a_log"], lw["d_vec"]]
    out_shape = [jax.ShapeDtypeStruct((t, A_Q), BF16), jax.ShapeDtypeStruct((t, B_HEADS * B_V), BF16),
                 jax.ShapeDtypeStruct((t, C_INNER), F32),
                 jax.ShapeDtypeStruct((t // SEQ, 2, C_HEADS, C_HEAD_DIM, C_STATE), F32)]
    out_specs = [tok(A_Q), tok(B_HEADS * B_V), tok(C_INNER),
                 pl.BlockSpec((None, 2, C_HEADS, C_HEAD_DIM, C_STATE), lambda b: (b, 0, 0, 0, 0))]
    return in_specs, args, out_shape, out_specs


def _pre_lat_mix_ctx_kernel(n_chunks, n_pre_in, n_mix_in, n_pre_out, n_mix_out, *refs):
    pre_in, refs = refs[:n_pre_in], refs[n_pre_in:]
    mix_in, refs = refs[:n_mix_in], refs[n_mix_in:]
    pre_out, refs = refs[:n_pre_out], refs[n_pre_out:]
    mix_out, scratch = refs[:n_mix_out], refs[n_mix_out:]
    _pre_kernel(False, *pre_in, *pre_out)
    _ctx_mix_kernel(n_chunks, *mix_in, *mix_out, *scratch)


def _pre_lat_mix_ctx(l, xs2d, mod, lw, tabs, ctx_mixer_inputs):
    p_in, p_args, p_shape, p_specs = _pre_parts(l, False, xs2d, mod, lw, tabs)
    m_in, m_args, m_shape, m_specs = _ctx_mix_parts(l, *ctx_mixer_inputs, lw)
    steps = xs2d.shape[0] // TM
    assert steps == ctx_mixer_inputs[1].shape[0] // SEQ
    nc = SEQ // C_CHUNK
    outs = pl.pallas_call(
        functools.partial(_pre_lat_mix_ctx_kernel, nc, len(p_in), len(m_in), len(p_shape), len(m_shape)),
        out_shape=tuple(p_shape + m_shape),
        grid=(steps,),
        in_specs=p_in + m_in,
        out_specs=tuple(p_specs + m_specs),
        scratch_shapes=_ssd_scratch(nc),
        compiler_params=_cparams(("parallel",)),
        name="pre_lat_mix_ctx",
    )(*p_args, *m_args)
    return outs[:len(p_shape)], outs[len(p_shape):]


def _post_kernel(x_ref, mod_ref, oa_ref, ob_ref, y_ref, z_ref, gssm_ref, wout_ref, gmix_ref, gffn_ref,
                 gffo_ref, w1_ref, w2_ref, o_ref):
    x = x_ref[...]
    gate1 = mod_ref[:, 2 * D_MODEL:3 * D_MODEL]
    shift2 = mod_ref[:, 3 * D_MODEL:4 * D_MODEL]
    scale2 = mod_ref[:, 4 * D_MODEL:5 * D_MODEL]
    gate2 = mod_ref[:, 5 * D_MODEL:6 * D_MODEL]
    z = z_ref[...]
    oc = _rms(y_ref[...] * (z * jax.nn.sigmoid(z)), gssm_ref[...])
    mix = jnp.concatenate([oa_ref[...], ob_ref[...], oc.astype(BF16)], axis=1)
    out = jnp.dot(mix, wout_ref[...], preferred_element_type=F32)
    x = x + gate1 * _rms(out, gmix_ref[...])
    h = _rms(x, gffn_ref[...]) * (1.0 + scale2) + shift2
    u = jnp.maximum(_bdot(h, w1_ref[...]), 0.0)
    f = _bdot(u * u, w2_ref[...])
    o_ref[...] = x + gate2 * _rms(f, gffo_ref[...])


def _post(l, is_ctx, x2d, mod, oa, ob, y, z, lw):
    t = x2d.shape[0]
    per_seq = DEC_SEQ // TM
    mod_map = (lambda i: (l, 0, 0, 0)) if is_ctx else (lambda i: (l, 1 + i // per_seq, 0, 0))
    tok = lambda w: pl.BlockSpec((TM, w), lambda i: (i, 0))
    const = lambda shape: pl.BlockSpec((None,) + shape, lambda i: (l,) + (0,) * len(shape),
                                       pipeline_mode=pl.Buffered(1))
    return pl.pallas_call(
        _post_kernel,
        out_shape=jax.ShapeDtypeStruct((t, D_MODEL), F32),
        grid=(t // TM,),
        in_specs=[
            tok(D_MODEL),
            pl.BlockSpec((None, None, 1, 6 * D_MODEL), mod_map),
            tok(A_Q), tok(B_HEADS * B_V), tok(C_INNER), tok(C_INNER),
            const((1, C_INNER)), const((D_MODEL, D_MODEL)), const((1, D_MODEL)), const((1, D_MODEL)),
            const((1, D_MODEL)), const((D_MODEL, D_FF)), const((D_FF, D_MODEL)),
        ],
        out_specs=tok(D_MODEL),
        compiler_params=_cparams(("parallel",)),
        name="post_ctx" if is_ctx else "post_lat",
    )(x2d, mod, oa, ob, y, z, lw["g_ssm"], lw["w_out"], lw["g_mix"], lw["g_ffn"], lw["g_ffo"],
      lw["w1"], lw["w2"])


def _rope_tables():
    pos = np.arange(DEC_SEQ)
    axis_pos = np.stack([(pos // GRID_W), (pos % GRID_W)], axis=0).astype(np.float32)

    def pattern(rot_dim):
        half = rot_dim // 2
        quarter = half // 2
        inv = (1.0 / (np.float32(ROPE_THETA) ** (np.arange(0, half, 2, dtype=np.float32) / np.float32(half))))
        inv = inv.astype(np.float32)
        dd = np.arange(rot_dim)
        ang = (axis_pos[dd // half].T * inv[dd % quarter][None, :]).astype(np.float32)
        sign = np.where((dd % half) < quarter, -1.0, 1.0).astype(np.float32)
        return np.cos(ang).astype(np.float32), (np.sin(ang) * sign).astype(np.float32)

    ca, sa = pattern(HEAD_DIM)
    ca = np.tile(ca, (1, LANES // HEAD_DIM))
    sa = np.tile(sa, (1, LANES // HEAD_DIM))
    cb32, sb32 = pattern(B_ROPE)
    tail = LANES - KPE_LANE - B_ROPE
    cb = np.concatenate([np.ones((DEC_SEQ, KPE_LANE), np.float32), cb32, np.ones((DEC_SEQ, tail), np.float32)], 1)
    sb = np.concatenate([np.zeros((DEC_SEQ, KPE_LANE), np.float32), sb32, np.zeros((DEC_SEQ, tail), np.float32)], 1)
    ident_c = np.ones((TM, LANES), np.float32)
    ident_s = np.zeros((TM, LANES), np.float32)
    return tuple(jnp.asarray(np.concatenate([tb, idt], axis=0))
                 for tb, idt in ((ca, ident_c), (sa, ident_s), (cb, ident_c), (sb, ident_s)))


REPACK_ROWS = 256
W_IN_COLS = A_Q + 2 * A_KV + B_Q_RANK + B_KV_RANK + B_ROPE + C_INNER + C_CONV_CH + 2 * C_HEADS


def _repack_kernel(w_ref, o_ref):
    o_kpe = OFF_KVC + B_KV_RANK
    o_z = o_kpe + B_ROPE
    o_dt = o_z + C_INNER + C_CONV_CH
    o_ref[:, 0:o_kpe] = w_ref[:, 0:o_kpe].astype(BF16)
    o_ref[:, OFF_Z:OFF_MISC] = w_ref[:, o_z:o_dt].astype(BF16)
    lane = _lane_iota((REPACK_ROWS, LANES))
    kpe = pltpu.roll(w_ref[:, o_kpe:o_kpe + LANES], KPE_LANE, axis=1)
    dt = pltpu.roll(w_ref[:, W_IN_COLS - LANES:W_IN_COLS], DT_LANE + 2 * C_HEADS, axis=1)
    misc = jnp.where((lane >= KPE_LANE) & (lane < DT_LANE), kpe,
                     jnp.where((lane >= DT_LANE) & (lane < DT_LANE + 2 * C_HEADS), dt, 0.0))
    o_ref[:, OFF_MISC:IN_COLS_P] = misc.astype(BF16)


def _repack_w_in(w_in):
    return pl.pallas_call(
        _repack_kernel,
        out_shape=jax.ShapeDtypeStruct((DEPTH, D_MODEL, IN_COLS_P), BF16),
        grid=(DEPTH, D_MODEL // REPACK_ROWS),
        in_specs=[pl.BlockSpec((None, REPACK_ROWS, W_IN_COLS), lambda l, i: (l, i, 0))],
        out_specs=pl.BlockSpec((None, REPACK_ROWS, IN_COLS_P), lambda l, i: (l, i, 0)),
        compiler_params=_cparams(("parallel", "parallel")),
        name="repack_w_in",
    )(w_in)


def _prep_weights(p):
    w_in_p = _repack_w_in(p["w_in"])

    w_qb = p["mla_w_qb"].reshape(DEPTH, B_Q_RANK, B_HEADS, B_NOPE + B_ROPE)
    w_qb_p = jnp.pad(w_qb, ((0, 0),) * 3 + ((0, LANES - B_NOPE - B_ROPE),)).reshape(DEPTH, B_Q_RANK, QB_W)
    w_kvb = p["mla_w_kvb"].reshape(DEPTH, B_KV_RANK, B_HEADS, B_NOPE + B_V)
    w_k = jnp.pad(w_kvb[..., :B_NOPE], ((0, 0),) * 3 + ((0, LANES - B_NOPE),)).reshape(DEPTH, B_KV_RANK, QB_W)
    w_v = w_kvb[..., B_NOPE:].reshape(DEPTH, B_KV_RANK, B_HEADS * B_V)

    w_out = p["w_out"]
    order = [r + g * A_REP for r in range(A_REP) for g in range(A_KV_HEADS)]
    w_out_p = jnp.concatenate([w_out[:, hd * HEAD_DIM:(hd + 1) * HEAD_DIM] for hd in order] + [w_out[:, A_Q:]],
                              axis=1).astype(BF16)

    row = lambda v: v.reshape(DEPTH, 1, -1)
    pair_gain = lambda g: row(jnp.tile(g, (1, LANES // HEAD_DIM)))
    lane_vec = lambda v: jnp.pad(v.reshape(DEPTH, 1, -1), ((0, 0), (0, 0), (DT_LANE, LANES - DT_LANE - 2 * C_HEADS)))
    return {
        "g_pre": row(p["norm_mix_pre"]),
        "w_in": w_in_p,
        "g_q": pair_gain(p["attn_q_norm"]),
        "g_k": pair_gain(p["attn_k_norm"]),
        "g_qc": row(p["mla_q_norm"]),
        "w_qb": w_qb_p.astype(BF16),
        "g_kv": row(p["mla_kv_norm"]),
        "w_k": w_k.astype(BF16),
        "w_v": w_v.astype(BF16),
        "conv_w": jnp.swapaxes(p["ssm_conv_w"], 1, 2),
        "conv_b": row(p["ssm_conv_b"]),
        "dt_bias": lane_vec(p["ssm_dt_bias"]),
        "a_log": lane_vec(p["ssm_a_log"]),
        "d_vec": row(jnp.repeat(p["ssm_d"], C_HEAD_DIM, axis=1)),
        "g_ssm": row(p["ssm_norm"]),
        "w_out": w_out_p,
        "g_mix": row(p["norm_mix_post"]),
        "g_ffn": row(p["norm_ffn_pre"]),
        "g_ffo": row(p["norm_ffn_post"]),
        "w1": p["w_ffn1"].astype(BF16),
        "w2": p["w_ffn2"].astype(BF16),
    }


def _state_to_pairs(h):
    ht = jnp.swapaxes(h, -1, -2)
    zero = jnp.zeros_like(ht[..., 0, :, :])
    pairs = []
    for pr in range(N_PAIRS):
        heads = (2 * pr, 2 * pr + 1)
        row_blocks = [jnp.concatenate([ht[..., hd, :, :] if hd // HEADS_PER_GROUP == g else zero for hd in heads],
                                      axis=-1) for g in range(C_GROUPS)]
        pairs.append(jnp.concatenate(row_blocks, axis=-2))
    return jnp.stack(pairs, axis=-3)


def kernel(x_prompt, x_sample, cache_attn_k, cache_attn_v, cache_mla_ckv, cache_mla_kpe, state_ssm, c, c_ctx, norm_mix_pre, norm_mix_post, norm_ffn_pre, norm_ffn_post, w_mod, b_mod, w_in, attn_q_norm, attn_k_norm, mla_q_norm, mla_w_qb, mla_kv_norm, mla_w_kvb, ssm_conv_w, ssm_conv_b, ssm_dt_bias, ssm_a_log, ssm_d, ssm_norm, w_out, w_ffn1, w_ffn2):
    p = dict(norm_mix_pre=norm_mix_pre, norm_mix_post=norm_mix_post, norm_ffn_pre=norm_ffn_pre,
             norm_ffn_post=norm_ffn_post, w_in=w_in, attn_q_norm=attn_q_norm, attn_k_norm=attn_k_norm,
             mla_q_norm=mla_q_norm, mla_w_qb=mla_w_qb, mla_kv_norm=mla_kv_norm, mla_w_kvb=mla_w_kvb,
             ssm_conv_w=ssm_conv_w, ssm_conv_b=ssm_conv_b, ssm_dt_bias=ssm_dt_bias, ssm_a_log=ssm_a_log,
             ssm_d=ssm_d, ssm_norm=ssm_norm, w_out=w_out, w_ffn1=w_ffn1, w_ffn2=w_ffn2)
    lw = _prep_weights(p)
    tabs = _rope_tables()

    cvec = jnp.concatenate([c_ctx[None, :], c, jnp.zeros((MOD_ROWS - 1 - DEC_BATCH, D_MODEL), F32)], axis=0)
    mod = _modulation(cvec, w_mod, b_mod).reshape(DEPTH, MOD_ROWS, 1, 6 * D_MODEL)

    kpe_p = jnp.pad(cache_mla_kpe, ((0, 0),) * 3 + ((KPE_LANE, LANES - KPE_LANE - B_ROPE),))
    kc, vct, kbc, vbct = _cache_prep(
        cache_attn_k.reshape(DEC_BATCH, DEPTH, PAST_LEN, A_KV), cache_attn_v.reshape(DEC_BATCH, DEPTH, PAST_LEN, A_KV),
        cache_mla_ckv, kpe_p, lw["w_k"], lw["w_v"])
    h0_lat = _state_to_pairs(state_ssm)

    xp = x_prompt.reshape(BATCH * SEQ, D_MODEL)
    xs = x_sample.reshape(DEC_BATCH * DEC_SEQ, D_MODEL)
    new_k, new_v, new_ckv, new_kpe, new_ssm = [], [], [], [], []
    for l in range(DEPTH):
        qat, ka, vat, qbt, kb, vbt, z, xbc, misc, kf, vf, ckvf = _pre_ctx(l, xp, mod, lw, tabs)
        lat_pre, (oa, ob, y, hfin) = _pre_lat_mix_ctx(l, xs, mod, lw, tabs, (qat, ka, vat, qbt, kb, vbt, xbc, misc))
        xp = _post(l, True, xp, mod, oa, ob, y, z, lw)
        new_k.append(kf.reshape(BATCH, SEQ, A_KV_HEADS, HEAD_DIM))
        new_v.append(vf.reshape(BATCH, SEQ, A_KV_HEADS, HEAD_DIM))
        new_ckv.append(ckvf.reshape(BATCH, SEQ, B_KV_RANK))
        new_kpe.append(misc[:, KPE_LANE:KPE_LANE + B_ROPE].reshape(BATCH, SEQ, B_ROPE))
        new_ssm.append(hfin)
        qat, ka, vat, qbt, kb, vbt, z, xbc, misc = lat_pre
        oa = _attn_lat(l, True, qat, ka, vat, kc, vct)
        ob = _attn_lat(l, False, qbt, kb, vbt, kbc, vbct)
        y, _ = _ssd_lat(l, xbc, misc, lw, h0_lat)
        xs = _post(l, False, xs, mod, oa, ob, y, z, lw)
    return (xp.reshape(BATCH, SEQ, D_MODEL), xs.reshape(DEC_BATCH, DEC_SEQ, D_MODEL),
            jnp.stack(new_k, axis=1), jnp.stack(new_v, axis=1), jnp.stack(new_ckv, axis=1),
            jnp.stack(new_kpe, axis=1), jnp.stack(new_ssm, axis=1))
```

```python
import functools
import math

import numpy as np
import jax
import jax.numpy as jnp
from jax import lax
from jax.experimental import pallas as pl
from jax.experimental.pallas import tpu as pltpu

F32 = jnp.float32
BF16 = jnp.bfloat16

D_MODEL = 1024
BATCH = 16
SEQ = 256
DEPTH = 4
DEC_BATCH = 2
DEC_SEQ = 4096
PAST_LEN = 256
GRID_W = 64
HEAD_DIM = 64
A_HEADS = 6
A_KV_HEADS = 2
A_REP = A_HEADS // A_KV_HEADS
B_HEADS = 4
B_Q_RANK = 256
B_KV_RANK = 128
B_NOPE = 64
B_ROPE = 32
B_V = 64
C_HEADS = 6
C_HEAD_DIM = 64
C_INNER = C_HEADS * C_HEAD_DIM
C_GROUPS = 2
C_STATE = 64
C_CONV = 5
C_CHUNK = 128
C_CONV_CH = C_INNER + 2 * C_GROUPS * C_STATE
D_FF = 4 * D_MODEL
A_Q = A_HEADS * HEAD_DIM
A_KV = A_KV_HEADS * HEAD_DIM
ROPE_THETA = 10000.0
EPS = 1e-6
LOG2E = math.log2(math.e)

LANES = 128
SUBLANES = 8
VMEM_LIMIT = 56 * 1024 * 1024

OFF_QA = 0
OFF_KA = OFF_QA + A_Q
OFF_VA = OFF_KA + A_KV
OFF_QC = OFF_VA + A_KV
OFF_KVC = OFF_QC + B_Q_RANK
OFF_Z = OFF_KVC + B_KV_RANK
OFF_XBC = OFF_Z + C_INNER
OFF_MISC = OFF_XBC + C_CONV_CH
IN_COLS_P = OFF_MISC + LANES
KPE_LANE = 64
DT_LANE = KPE_LANE + B_ROPE
QB_W = B_HEADS * LANES

TM = 512
TQ = 512
TK = 1024


def _cparams(sem):
    return pltpu.CompilerParams(dimension_semantics=sem, vmem_limit_bytes=VMEM_LIMIT)


def _rms(x, g):
    return x * lax.rsqrt(jnp.mean(x * x, axis=-1, keepdims=True) + EPS) * g


def _bdot(a, b):
    return jnp.dot(a.astype(BF16), b.astype(BF16), preferred_element_type=F32)


def _lane_iota(shape):
    return lax.broadcasted_iota(jnp.int32, shape, len(shape) - 1)


def _swap_lanes(x, dist):
    lane = _lane_iota(x.shape)
    fwd = pltpu.roll(x, LANES - dist, axis=1)
    bwd = pltpu.roll(x, dist, axis=1)
    return jnp.where((lane % (2 * dist)) < dist, fwd, bwd)


def _rope(x, cos, sin, dist):
    return x * cos + _swap_lanes(x, dist) * sin


def _head_pair_rms(col, gain):
    lo = _lane_iota(col.shape) < HEAD_DIM
    c2 = col * col
    s_lo = jnp.sum(jnp.where(lo, c2, 0.0), axis=-1, keepdims=True)
    s_hi = jnp.sum(jnp.where(lo, 0.0, c2), axis=-1, keepdims=True)
    return col * lax.rsqrt(jnp.where(lo, s_lo, s_hi) * (1.0 / HEAD_DIM) + EPS) * gain


MOD_ROWS = SUBLANES
MOD_TN = 1536


def _mod_kernel(c_ref, w_ref, b_ref, o_ref):
    c = c_ref[...]
    o_ref[...] = _bdot(c * jax.nn.sigmoid(c), w_ref[...]) + b_ref[...]


def _modulation(cvec, w_mod, b_mod):
    return pl.pallas_call(
        _mod_kernel,
        out_shape=jax.ShapeDtypeStruct((DEPTH, MOD_ROWS, 6 * D_MODEL), F32),
        grid=(DEPTH, 6 * D_MODEL // MOD_TN),
        in_specs=[
            pl.BlockSpec((MOD_ROWS, D_MODEL), lambda l, j: (0, 0)),
            pl.BlockSpec((None, D_MODEL, MOD_TN), lambda l, j: (l, 0, j)),
            pl.BlockSpec((None, 1, MOD_TN), lambda l, j: (l, 0, j)),
        ],
        out_specs=pl.BlockSpec((None, MOD_ROWS, MOD_TN), lambda l, j: (l, 0, j)),
        compiler_params=_cparams(("parallel", "parallel")),
        name="modulation",
    )(cvec, w_mod, b_mod.reshape(DEPTH, 1, 6 * D_MODEL))


def _cache_kernel(k_ref, v_ref, ckv_ref, kpe_ref, wk_ref, wv_ref, ko_ref, vt_ref, kb_ref, vbt_ref):
    ko_ref[...] = k_ref[...].astype(BF16)
    vt_ref[...] = v_ref[...].T.astype(BF16)
    ckv = ckv_ref[...]
    kn = _bdot(ckv, wk_ref[...])
    kpe = kpe_ref[...]
    for h in range(B_HEADS):
        kb_ref[:, h * LANES:(h + 1) * LANES] = (kn[:, h * LANES:(h + 1) * LANES] + kpe).astype(BF16)
    vbt_ref[...] = _bdot(ckv, wv_ref[...]).T.astype(BF16)


def _cache_prep(cache_k, cache_v, cache_ckv, cache_kpe_p, wk_p, wv_p):
    spec_tok = lambda w: pl.BlockSpec((None, None, PAST_LEN, w), lambda l, b: (b, l, 0, 0))
    spec_t = lambda w: pl.BlockSpec((None, None, w, PAST_LEN), lambda l, b: (l, b, 0, 0))
    spec_o = lambda w: pl.BlockSpec((None, None, PAST_LEN, w), lambda l, b: (l, b, 0, 0))
    return pl.pallas_call(
        _cache_kernel,
        out_shape=(
            jax.ShapeDtypeStruct((DEPTH, DEC_BATCH, PAST_LEN, A_KV), BF16),
            jax.ShapeDtypeStruct((DEPTH, DEC_BATCH, A_KV, PAST_LEN), BF16),
            jax.ShapeDtypeStruct((DEPTH, DEC_BATCH, PAST_LEN, QB_W), BF16),
            jax.ShapeDtypeStruct((DEPTH, DEC_BATCH, B_HEADS * B_V, PAST_LEN), BF16),
        ),
        grid=(DEPTH, DEC_BATCH),
        in_specs=[
            spec_tok(A_KV), spec_tok(A_KV), spec_tok(B_KV_RANK), spec_tok(LANES),
            pl.BlockSpec((None, B_KV_RANK, QB_W), lambda l, b: (l, 0, 0)),
            pl.BlockSpec((None, B_KV_RANK, B_HEADS * B_V), lambda l, b: (l, 0, 0)),
        ],
        out_specs=(spec_o(A_KV), spec_t(A_KV), spec_o(QB_W), spec_t(B_HEADS * B_V)),
        compiler_params=_cparams(("parallel", "parallel")),
        name="cache_prep",
    )(cache_k, cache_v, cache_ckv, cache_kpe_p, wk_p, wv_p)


PRE_SUB = 2


def _pre_kernel(is_ctx, *refs):
    for sub in range(PRE_SUB):
        _pre_rows(is_ctx, slice(sub * (TM // PRE_SUB), (sub + 1) * (TM // PRE_SUB)), *refs)


def _pre_rows(is_ctx, rows, x_ref, mod_ref, gpre_ref, win_ref, gq_ref, gk_ref, gqc_ref, wqb_ref, gkv_ref,
              wk_ref, wv_ref, dtb_ref, ca_ref, sa_ref, cb_ref, sb_ref, *outs):
    if is_ctx:
        (qat_ref, ka_ref, vat_ref, qbt_ref, kb_ref, vbt_ref, z_ref, xbc_ref, misc_ref,
         kf_ref, vf_ref, ckvf_ref) = outs
    else:
        qat_ref, ka_ref, vat_ref, qbt_ref, kb_ref, vbt_ref, z_ref, xbc_ref, misc_ref = outs
    x = x_ref[rows, :]
    shift1 = mod_ref[:, 0:D_MODEL]
    scale1 = mod_ref[:, D_MODEL:2 * D_MODEL]
    h = _rms(x, gpre_ref[...]) * (1.0 + scale1) + shift1
    proj = _bdot(h, win_ref[...])

    ca, sa, cb, sb = ca_ref[rows, :], sa_ref[rows, :], cb_ref[rows, :], sb_ref[rows, :]
    lane = _lane_iota((x.shape[0], LANES))

    gq = gq_ref[...]
    for cidx in range(A_Q // LANES):
        col = proj[:, OFF_QA + cidx * LANES:OFF_QA + (cidx + 1) * LANES]
        qn = _rope(_head_pair_rms(col, gq), ca, sa, HEAD_DIM // 4) * (HEAD_DIM ** -0.5 * LOG2E)
        qat_ref[cidx * LANES:(cidx + 1) * LANES, rows] = qn.T.astype(BF16)

    kn = _head_pair_rms(proj[:, OFF_KA:OFF_KA + A_KV], gk_ref[...])
    vcol = proj[:, OFF_VA:OFF_VA + A_KV]
    if is_ctx:
        kf_ref[rows, :] = kn
        vf_ref[rows, :] = vcol
    ka_ref[rows, :] = _rope(kn, ca, sa, HEAD_DIM // 4).astype(BF16)
    vat_ref[:, rows] = vcol.T.astype(BF16)

    qc = _rms(proj[:, OFF_QC:OFF_QC + B_Q_RANK], gqc_ref[...])
    qb = _bdot(qc, wqb_ref[...])
    for hd in range(B_HEADS):
        col = qb[:, hd * LANES:(hd + 1) * LANES]
        col = _rope(col, cb, sb, B_ROPE // 4) * ((B_NOPE + B_ROPE) ** -0.5 * LOG2E)
        qbt_ref[hd * LANES:(hd + 1) * LANES, rows] = col.T.astype(BF16)

    ckv = _rms(proj[:, OFF_KVC:OFF_KVC + B_KV_RANK], gkv_ref[...])
    if is_ctx:
        ckvf_ref[rows, :] = ckv
    knope = _bdot(ckv, wk_ref[...])
    misc = proj[:, OFF_MISC:OFF_MISC + LANES]
    kpe = jnp.where((lane >= KPE_LANE) & (lane < KPE_LANE + B_ROPE), _rope(misc, cb, sb, B_ROPE // 4), 0.0)
    for hd in range(B_HEADS):
        kb_ref[rows, hd * LANES:(hd + 1) * LANES] = (knope[:, hd * LANES:(hd + 1) * LANES] + kpe).astype(BF16)
    vbt_ref[:, rows] = _bdot(ckv, wv_ref[...]).T.astype(BF16)

    x_dt = misc + dtb_ref[...]
    dtv = jnp.maximum(x_dt, 0.0) + jnp.log(1.0 + jnp.exp(-jnp.abs(x_dt)))
    misc_ref[rows, :] = jnp.where(lane >= DT_LANE, dtv, misc)

    z_ref[rows, :] = proj[:, OFF_Z:OFF_Z + C_INNER]
    xbc_ref[rows, :] = proj[:, OFF_XBC:OFF_XBC + C_CONV_CH]


def _pre_parts(l, is_ctx, x2d, mod, lw, tabs):
    t = x2d.shape[0]
    per_seq = DEC_SEQ // TM
    if is_ctx:
        mod_map = lambda i: (l, 0, 0, 0)
        tab_map = lambda i: (per_seq, 0)
    else:
        mod_map = lambda i: (l, 1 + i // per_seq, 0, 0)
        tab_map = lambda i: (i % per_seq, 0)
    const = lambda shape: pl.BlockSpec((None,) + shape, lambda i: (l,) + (0,) * len(shape))
    tab_spec = pl.BlockSpec((TM, LANES), tab_map)
    tok = lambda w: pl.BlockSpec((TM, w), lambda i: (i, 0))
    tok_t = lambda w: pl.BlockSpec((w, TM), lambda i: (0, i))
    out_shape = [
        jax.ShapeDtypeStruct((A_Q, t), BF16),
        jax.ShapeDtypeStruct((t, A_KV), BF16),
        jax.ShapeDtypeStruct((A_KV, t), BF16),
        jax.ShapeDtypeStruct((QB_W, t), BF16),
        jax.ShapeDtypeStruct((t, QB_W), BF16),
        jax.ShapeDtypeStruct((B_HEADS * B_V, t), BF16),
        jax.ShapeDtypeStruct((t, C_INNER), F32),
        jax.ShapeDtypeStruct((t, C_CONV_CH), F32),
        jax.ShapeDtypeStruct((t, LANES), F32),
    ]
    out_specs = [tok_t(A_Q), tok(A_KV), tok_t(A_KV), tok_t(QB_W), tok(QB_W), tok_t(B_HEADS * B_V),
                 tok(C_INNER), tok(C_CONV_CH), tok(LANES)]
    if is_ctx:
        out_shape += [jax.ShapeDtypeStruct((t, A_KV), F32)] * 2 + [jax.ShapeDtypeStruct((t, B_KV_RANK), F32)]
        out_specs += [tok(A_KV), tok(A_KV), tok(B_KV_RANK)]
    in_specs = [
        tok(D_MODEL),
        pl.BlockSpec((None, None, 1, 6 * D_MODEL), mod_map),
        const((1, D_MODEL)),
        const((D_MODEL, IN_COLS_P)),
        const((1, LANES)), const((1, LANES)), const((1, B_Q_RANK)),
        const((B_Q_RANK, QB_W)), const((1, B_KV_RANK)),
        const((B_KV_RANK, QB_W)), const((B_KV_RANK, B_HEADS * B_V)), const((1, LANES)),
        tab_spec, tab_spec, tab_spec, tab_spec,
    ]
    args = [x2d, mod, lw["g_pre"], lw["w_in"], lw["g_q"], lw["g_k"], lw["g_qc"], lw["w_qb"], lw["g_kv"],
            lw["w_k"], lw["w_v"], lw["dt_bias"], *tabs]
    return in_specs, args, out_shape, out_specs


def _pre_ctx(l, x2d, mod, lw, tabs):
    in_specs, args, out_shape, out_specs = _pre_parts(l, True, x2d, mod, lw, tabs)
    return pl.pallas_call(
        functools.partial(_pre_kernel, True),
        out_shape=tuple(out_shape),
        grid=(x2d.shape[0] // TM,),
        in_specs=in_specs,
        out_specs=tuple(out_specs),
        compiler_params=_cparams(("parallel",)),
        name="pre_ctx",
    )(*args)


def _merge_halves(lo_part, hi_part):
    return jnp.where(_lane_iota(lo_part.shape) < HEAD_DIM, lo_part, hi_part)


def _gqa_query_slot(qat_ref, hd):
    q = qat_ref[hd * HEAD_DIM:(hd + 1) * HEAD_DIM, :]
    zero = jnp.zeros_like(q)
    return jnp.concatenate([q, zero] if hd // A_REP == 0 else [zero, q], axis=0)


ONES_ROWS = 16
ACC_ROWS = LANES + ONES_ROWS


def _with_ones(v_t):
    return jnp.concatenate([v_t, jnp.ones((ONES_ROWS, v_t.shape[1]), BF16)], axis=0)


def _first_tile(k, q_t, v_t):
    return _first_scores(jnp.dot(k, q_t, preferred_element_type=F32), v_t)


def _first_scores(s, v_t):
    m = jnp.max(s, axis=0, keepdims=True)
    p = jnp.exp2(s - m)
    return m, jnp.dot(_with_ones(v_t), p.astype(BF16), preferred_element_type=F32)


def _next_tile(s, m, acc, v_t):
    m_new = jnp.maximum(m, jnp.max(s, axis=0, keepdims=True))
    p = jnp.exp2(s - m_new)
    acc = jnp.exp2(m - m_new) * acc + jnp.dot(_with_ones(v_t), p.astype(BF16), preferred_element_type=F32)
    return m_new, acc


def _normalised(acc):
    return (acc[0:LANES, :] / acc[LANES:LANES + 1, :]).T


def _write_heads(outs_a, outs_b, oa_ref, ob_ref):
    for r in range(A_REP):
        oa_ref[:, r * LANES:(r + 1) * LANES] = _merge_halves(outs_a[r], outs_a[A_REP + r]).astype(BF16)
    for pr in range(B_HEADS // 2):
        ob_ref[:, pr * LANES:(pr + 1) * LANES] = _merge_halves(outs_b[2 * pr], outs_b[2 * pr + 1]).astype(BF16)


def _attn_ctx_kernel(qat_ref, ka_ref, vat_ref, qbt_ref, kb_ref, vbt_ref, oa_ref, ob_ref):
    def split_heads(acc, n):
        o = acc[0:LANES, :] / acc[LANES:LANES + 1, :]
        return [o[:, i * SEQ:(i + 1) * SEQ].T for i in range(n)]

    q_all = jnp.concatenate([_gqa_query_slot(qat_ref, hd) for hd in range(A_HEADS)], axis=1)
    outs_a = split_heads(_first_tile(ka_ref[...], q_all, vat_ref[...])[1], A_HEADS)
    outs_b = []
    for pr in range(B_HEADS // 2):
        s = jnp.concatenate([jnp.dot(kb_ref[:, hd * LANES:(hd + 1) * LANES], qbt_ref[hd * LANES:(hd + 1) * LANES, :],
                                     preferred_element_type=F32) for hd in (2 * pr, 2 * pr + 1)], axis=1)
        outs_b += split_heads(_first_scores(s, vbt_ref[pr * LANES:(pr + 1) * LANES, :])[1], 2)
    _write_heads(outs_a, outs_b, oa_ref, ob_ref)


def _attn_lat_kernel(is_gqa, *refs):
    if is_gqa:
        q0_ref, q1_ref, k_ref, vt_ref, kc_ref, vct_ref, o_ref, s_ref, acc_ref = refs
        zero = jnp.zeros((HEAD_DIM, TQ), BF16)
        q_slots = (jnp.concatenate([q0_ref[...], zero], axis=0), jnp.concatenate([zero, q1_ref[...]], axis=0))
        key_cols = (0, 0)
    else:
        q_ref, k_ref, vt_ref, kc_ref, vct_ref, o_ref, s_ref, acc_ref = refs
        q_slots = (q_ref[0:LANES, :], q_ref[LANES:2 * LANES, :])
        key_cols = (0, LANES)
    n_tiles = DEC_SEQ // TK
    head_cut, tail_cut = TK // 4, TK - TK // 4
    plans = ([("cache", 0, PAST_LEN), ("lat", 0, TK // 2)] + [("lat", TK // 2 + j * TK, TK) for j in range(n_tiles - 1)]
             + [("lat", DEC_SEQ - TK // 2, TK // 2)],
             [("lat", 0, head_cut)] + [("lat", head_cut + j * TK, TK) for j in range(n_tiles - 1)]
             + [("lat", DEC_SEQ - tail_cut, tail_cut), ("cache", 0, PAST_LEN)])

    def keys(seg, c0):
        src, k0, n = seg
        return (kc_ref if src == "cache" else k_ref)[k0:k0 + n, c0:c0 + LANES]

    def vals(seg):
        src, k0, n = seg
        return (vct_ref if src == "cache" else vt_ref)[:, k0:k0 + n]

    row_max = [None, None]
    for hh in range(2):
        seg = plans[hh][0]
        s_ref[hh, 0, 0:seg[2], :] = jnp.dot(keys(seg, key_cols[hh]), q_slots[hh], preferred_element_type=F32)
    for j in range(max(len(plan) for plan in plans)):
        for hh in range(2):
            plan = plans[hh]
            if j >= len(plan):
                continue
            if j + 1 < len(plan):
                nxt = plan[j + 1]
                s_ref[hh, (j + 1) % 2, 0:nxt[2], :] = jnp.dot(keys(nxt, key_cols[hh]), q_slots[hh],
                                                              preferred_element_type=F32)
            seg = plan[j]
            s = s_ref[hh, j % 2, 0:seg[2], :]
            if j == 0:
                row_max[hh], acc = _first_scores(s, vals(seg))
            else:
                row_max[hh], acc = _next_tile(s, row_max[hh], acc_ref[hh], vals(seg))
            acc_ref[hh] = acc
    o_ref[...] = _merge_halves(_normalised(acc_ref[0]), _normalised(acc_ref[1])).astype(BF16)


def _attn_lat(l, is_gqa, q_t, k, v_t, kc, vc_t):
    t = k.shape[0]
    nq = DEC_SEQ // TQ
    n_pairs = A_REP if is_gqa else B_HEADS // 2
    kw = A_KV if is_gqa else 2 * LANES
    if is_gqa:
        q_specs = [pl.BlockSpec((HEAD_DIM, TQ), lambda b, r, i: (r, b * nq + i)),
                   pl.BlockSpec((HEAD_DIM, TQ), lambda b, r, i: (A_REP + r, b * nq + i))]
        q_args = [q_t, q_t]
        pair_col = lambda r: 0
    else:
        q_specs = [pl.BlockSpec((2 * LANES, TQ), lambda b, r, i: (r, b * nq + i))]
        q_args = [q_t]
        pair_col = lambda r: r
    return pl.pallas_call(
        functools.partial(_attn_lat_kernel, is_gqa),
        out_shape=jax.ShapeDtypeStruct((t, n_pairs * LANES), BF16),
        grid=(DEC_BATCH, n_pairs, nq),
        in_specs=q_specs + [
            pl.BlockSpec((DEC_SEQ, kw), lambda b, r, i: (b, pair_col(r))),
            pl.BlockSpec((LANES, DEC_SEQ), lambda b, r, i: (pair_col(r), b)),
            pl.BlockSpec((None, None, PAST_LEN, kw), lambda b, r, i: (l, b, 0, pair_col(r))),
            pl.BlockSpec((None, None, LANES, PAST_LEN), lambda b, r, i: (l, b, pair_col(r), 0)),
        ],
        out_specs=pl.BlockSpec((TQ, LANES), lambda b, r, i: (b * nq + i, r)),
        scratch_shapes=[pltpu.VMEM((2, 2, TK, TQ), F32), pltpu.VMEM((2, ACC_ROWS, TQ), F32)],
        compiler_params=_cparams(("parallel", "parallel", "parallel")),
        name="attn_lat_gqa" if is_gqa else "attn_lat_mla",
    )(*q_args, k, v_t, kc, vc_t)


HALO = SUBLANES
N_PAIRS = C_HEADS // 2
HEADS_PER_GROUP = C_HEADS // C_GROUPS


def _ssd_kernel(n_chunks, has_h0, xbc_ref, misc_ref, cw_ref, cb_ref, alog_ref, dvec_ref, *rest):
    if has_h0:
        h0_ref, y_ref, hout_ref, stf_ref, stb_ref, newb_ref, cm_ref, ecx_ref = rest
        stf_ref[...] = h0_ref[0]
        stb_ref[...] = h0_ref[1]
    else:
        y_ref, hout_ref, stf_ref, stb_ref, newb_ref, cm_ref, ecx_ref = rest
        stf_ref[...] = jnp.zeros_like(stf_ref)
        stb_ref[...] = jnp.zeros_like(stb_ref)
    seq_len = n_chunks * C_CHUNK
    row = lax.broadcasted_iota(jnp.int32, (C_CHUNK, LANES), 0)
    lane = _lane_iota((C_CHUNK, LANES))
    lo = lane < C_HEAD_DIM
    lane_group = (lane >= C_STATE).astype(jnp.int32)
    row_group = (row >= C_STATE).astype(jnp.int32)
    causal = (row >= lane, row <= lane)
    neg_a = -jnp.exp(alog_ref[...])
    dt_lanes = (lane >= DT_LANE) & (lane < DT_LANE + 2 * C_HEADS)

    def lane_of(direction, hd):
        return DT_LANE + C_HEADS * direction + hd

    def pair_cols(v, direction, heads):
        j0, j1 = lane_of(direction, heads[0]), lane_of(direction, heads[1])
        return jnp.where(lo[0:v.shape[0], :], v[:, j0:j0 + 1], v[:, j1:j1 + 1])

    def pass1(c, carry):
        start = pl.multiple_of(c * C_CHUNK, C_CHUNK)
        prev0 = pl.multiple_of(jnp.maximum(start - HALO, 0), HALO)
        next0 = pl.multiple_of(jnp.minimum(start + C_CHUNK, seq_len - HALO), HALO)
        keep_prev = jnp.where(c > 0, 1.0, 0.0).astype(F32)
        keep_next = jnp.where(c < n_chunks - 1, 1.0, 0.0).astype(F32)
        rows_c = pl.ds(start, C_CHUNK)

        ue = jnp.concatenate([xbc_ref[pl.ds(prev0, HALO), :] * keep_prev, xbc_ref[rows_c, :],
                              xbc_ref[pl.ds(next0, HALO), :] * keep_next], axis=0)
        acc = jnp.broadcast_to(cb_ref[...], (C_CHUNK, C_CONV_CH))
        for k in range(C_CONV):
            sh = (C_CONV // 2 - k) % ue.shape[0]
            r = ue if sh == 0 else pltpu.roll(ue, sh, axis=0)
            acc = acc + r[HALO:HALO + C_CHUNK, :] * cw_ref[k:k + 1, :]
        xc = acc * jax.nn.sigmoid(acc)
        xs = xc[:, 0:C_INNER]
        bm = xc[:, C_INNER:C_INNER + LANES]
        cm = xc[:, C_INNER + LANES:C_INNER + 2 * LANES]
        bmt = bm.T.astype(BF16)
        cm16 = cm.astype(BF16)
        g_mats = [jnp.dot(jnp.where(lane_group == g, cm, 0.0).astype(BF16), bmt, preferred_element_type=F32)
                  for g in range(C_GROUPS)]

        dtv = jnp.where(dt_lanes, misc_ref[rows_c, :], 0.0)
        dta = dtv * neg_a
        cum = dta
        step = 1
        while step < C_CHUNK:
            cum = cum + jnp.where(row >= step, pltpu.roll(cum, step, axis=0), 0.0)
            step *= 2
        tot = cum[C_CHUNK - 1:C_CHUNK, :]
        cxs = (cum, tot - cum + dta)
        cxts = (cxs[0].T, cxs[1].T)
        dtt = dtv.T
        wsts = tuple(jnp.exp(tot - cx) * dtv for cx in cxs)
        ecx_f = jnp.exp(cxs[0])
        cdec = jnp.exp(tot)
        cm_ref[c] = cm16
        ecx_ref[c, 0:C_CHUNK, :] = jnp.exp(cxs[1])
        ecx_ref[c, C_CHUNK:C_CHUNK + SUBLANES, :] = jnp.broadcast_to(cdec, (SUBLANES, LANES))

        for pr in range(N_PAIRS):
            heads = (2 * pr, 2 * pr + 1)
            groups = tuple(hd // HEADS_PER_GROUP for hd in heads)
            xs_pair = xs[:, pr * LANES:(pr + 1) * LANES]
            xs16 = xs_pair.astype(BF16)
            own = row_group == jnp.where(lo, groups[0], groups[1])
            y = xs_pair * dvec_ref[:, pr * LANES:(pr + 1) * LANES]
            for direction in range(2):
                yd = []
                for hd in heads:
                    j = lane_of(direction, hd)
                    seg = jnp.where(causal[direction], cxs[direction][:, j:j + 1] - cxts[direction][j:j + 1, :],
                                    -jnp.inf)
                    sc = g_mats[hd // HEADS_PER_GROUP] * jnp.exp(seg) * dtt[j:j + 1, :]
                    yd.append(jnp.dot(sc.astype(BF16), xs16, preferred_element_type=F32))
                y = y + _merge_halves(yd[0], yd[1])
                xw = xs_pair * pair_cols(wsts[direction], direction, heads)
                new = jnp.where(own, jnp.dot(bmt, xw.astype(BF16), preferred_element_type=F32), 0.0)
                if direction == 0:
                    st = stf_ref[pr]
                    y_off = jnp.dot(cm16, st.astype(BF16), preferred_element_type=F32)
                    y = y + y_off * pair_cols(ecx_f, 0, heads)
                    stf_ref[pr] = st * pair_cols(cdec, 0, heads) + new
                else:
                    newb_ref[c, pr] = new
            y_ref[rows_c, pr * LANES:(pr + 1) * LANES] = y
        return carry

    def pass2(i, carry):
        c = n_chunks - 1 - i
        rows_c = pl.ds(pl.multiple_of(c * C_CHUNK, C_CHUNK), C_CHUNK)
        cm16 = cm_ref[c]
        ecx_b = ecx_ref[c, 0:C_CHUNK, :]
        cdec = ecx_ref[c, C_CHUNK:C_CHUNK + 1, :]
        for pr in range(N_PAIRS):
            heads = (2 * pr, 2 * pr + 1)
            st = stb_ref[pr]
            y_off = jnp.dot(cm16, st.astype(BF16), preferred_element_type=F32)
            cols = pl.ds(pr * LANES, LANES)
            y_ref[rows_c, cols] = y_ref[rows_c, cols] + y_off * pair_cols(ecx_b, 1, heads)
            stb_ref[pr] = st * pair_cols(cdec, 1, heads) + newb_ref[c, pr]
        return carry

    unroll = True if n_chunks <= 2 else 2
    lax.fori_loop(0, n_chunks, pass1, 0, unroll=unroll)
    lax.fori_loop(0, n_chunks, pass2, 0, unroll=unroll)

    for direction, st_ref in enumerate((stf_ref, stb_ref)):
        for pr in range(N_PAIRS):
            st_t = st_ref[pr].T
            for hh, hd in enumerate((2 * pr, 2 * pr + 1)):
                g = hd // HEADS_PER_GROUP
                hout_ref[direction, hd] = st_t[hh * C_HEAD_DIM:(hh + 1) * C_HEAD_DIM,
                                               g * C_STATE:(g + 1) * C_STATE]


def _ssd_scratch(nc):
    return [pltpu.VMEM((N_PAIRS, LANES, LANES), F32), pltpu.VMEM((N_PAIRS, LANES, LANES), F32),
            pltpu.VMEM((nc, N_PAIRS, LANES, LANES), F32), pltpu.VMEM((nc, C_CHUNK, LANES), BF16),
            pltpu.VMEM((nc, C_CHUNK + SUBLANES, LANES), F32)]


def _ssd_lat(l, xbc, misc, lw, h0):
    t = xbc.shape[0]
    nb = t // DEC_SEQ
    nc = DEC_SEQ // C_CHUNK
    const = lambda shape: pl.BlockSpec((None,) + shape, lambda b: (l,) + (0,) * len(shape))
    return pl.pallas_call(
        functools.partial(_ssd_kernel, nc, True),
        out_shape=(jax.ShapeDtypeStruct((t, C_INNER), F32),
                   jax.ShapeDtypeStruct((nb, 2, C_HEADS, C_HEAD_DIM, C_STATE), F32)),
        grid=(nb,),
        in_specs=[
            pl.BlockSpec((DEC_SEQ, C_CONV_CH), lambda b: (b, 0)),
            pl.BlockSpec((DEC_SEQ, LANES), lambda b: (b, 0)),
            const((C_CONV, C_CONV_CH)), const((1, C_CONV_CH)), const((1, LANES)), const((1, C_INNER)),
            pl.BlockSpec((None, None, 2, N_PAIRS, LANES, LANES), lambda b: (b, l, 0, 0, 0, 0)),
        ],
        out_specs=(pl.BlockSpec((DEC_SEQ, C_INNER), lambda b: (b, 0)),
                   pl.BlockSpec((None, 2, C_HEADS, C_HEAD_DIM, C_STATE), lambda b: (b, 0, 0, 0, 0))),
        scratch_shapes=_ssd_scratch(nc),
        compiler_params=_cparams(("parallel",)),
        name="ssd_lat",
    )(xbc, misc, lw["conv_w"], lw["conv_b"], lw["a_log"], lw["d_vec"], h0)


def _ctx_mix_kernel(n_chunks, qat_ref, ka_ref, vat_ref, qbt_ref, kb_ref, vbt_ref, xbc_ref, misc_ref, cw_ref, cb_ref,
                    alog_ref, dvec_ref, oa_ref, ob_ref, y_ref, hout_ref, *scratch):
    _attn_ctx_kernel(qat_ref, ka_ref, vat_ref, qbt_ref, kb_ref, vbt_ref, oa_ref, ob_ref)
    _ssd_kernel(n_chunks, False, xbc_ref, misc_ref, cw_ref, cb_ref, alog_ref, dvec_ref, y_ref, hout_ref, *scratch)


def _ctx_mix_parts(l, qat, ka, vat, qbt, kb, vbt, xbc, misc, lw):
    t = ka.shape[0]
    tok = lambda w: pl.BlockSpec((SEQ, w), lambda b: (b, 0))
    tok_t = lambda w: pl.BlockSpec((w, SEQ), lambda b: (0, b))
    const = lambda shape: pl.BlockSpec((None,) + shape, lambda b: (l,) + (0,) * len(shape))
    in_specs = [tok_t(A_Q), tok(A_KV), tok_t(A_KV), tok_t(QB_W), tok(QB_W), tok_t(B_HEADS * B_V),
                tok(C_CONV_CH), tok(LANES),
                const((C_CONV, C_CONV_CH)), const((1, C_CONV_CH)), const((1, LANES)), const((1, C_INNER))]
    args = [qat, ka, vat, qbt, kb, vbt, xbc, misc, lw["conv_w"], lw["conv_b"], lw["a_log"], lw["d_vec"]]
    out_shape = [jax.ShapeDtypeStruct((t, A_Q), BF16), jax.ShapeDtypeStruct((t, B_HEADS * B_V), BF16),
                 jax.ShapeDtypeStruct((t, C_INNER), F32),
                 jax.ShapeDtypeStruct((t // SEQ, 2, C_HEADS, C_HEAD_DIM, C_STATE), F32)]
    out_specs = [tok(A_Q), tok(B_HEADS * B_V), tok(C_INNER),
                 pl.BlockSpec((None, 2, C_HEADS, C_HEAD_DIM, C_STATE), lambda b: (b, 0, 0, 0, 0))]
    return in_specs, args, out_shape, out_specs


def _pre_lat_mix_ctx_kernel(n_chunks, n_pre_in, n_mix_in, n_pre_out, n_mix_out, *refs):
    pre_in, refs = refs[:n_pre_in], refs[n_pre_in:]
    mix_in, refs = refs[:n_mix_in], refs[n_mix_in:]
    pre_out, refs = refs[:n_pre_out], refs[n_pre_out:]
    mix_out, scratch = refs[:n_mix_out], refs[n_mix_out:]
    _pre_kernel(False, *pre_in, *pre_out)
    _ctx_mix_kernel(n_chunks, *mix_in, *mix_out, *scratch)


def _pre_lat_mix_ctx(l, xs2d, mod, lw, tabs, ctx_mixer_inputs):
    p_in, p_args, p_shape, p_specs = _pre_parts(l, False, xs2d, mod, lw, tabs)
    m_in, m_args, m_shape, m_specs = _ctx_mix_parts(l, *ctx_mixer_inputs, lw)
    steps = xs2d.shape[0] // TM
    assert steps == ctx_mixer_inputs[1].shape[0] // SEQ
    nc = SEQ // C_CHUNK
    outs = pl.pallas_call(
        functools.partial(_pre_lat_mix_ctx_kernel, nc, len(p_in), len(m_in), len(p_shape), len(m_shape)),
        out_shape=tuple(p_shape + m_shape),
        grid=(steps,),
        in_specs=p_in + m_in,
        out_specs=tuple(p_specs + m_specs),
        scratch_shapes=_ssd_scratch(nc),
        compiler_params=_cparams(("parallel",)),
        name="pre_lat_mix_ctx",
    )(*p_args, *m_args)
    return outs[:len(p_shape)], outs[len(p_shape):]


def _post_kernel(x_ref, mod_ref, oa_ref, ob_ref, y_ref, z_ref, gssm_ref, wout_ref, gmix_ref, gffn_ref,
                 gffo_ref, w1_ref, w2_ref, o_ref):
    x = x_ref[...]
    gate1 = mod_ref[:, 2 * D_MODEL:3 * D_MODEL]
    shift2 = mod_ref[:, 3 * D_MODEL:4 * D_MODEL]
    scale2 = mod_ref[:, 4 * D_MODEL:5 * D_MODEL]
    gate2 = mod_ref[:, 5 * D_MODEL:6 * D_MODEL]
    z = z_ref[...]
    oc = _rms(y_ref[...] * (z * jax.nn.sigmoid(z)), gssm_ref[...])
    mix = jnp.concatenate([oa_ref[...], ob_ref[...], oc.astype(BF16)], axis=1)
    out = jnp.dot(mix, wout_ref[...], preferred_element_type=F32)
    x = x + gate1 * _rms(out, gmix_ref[...])
    h = _rms(x, gffn_ref[...]) * (1.0 + scale2) + shift2
    u = jnp.maximum(_bdot(h, w1_ref[...]), 0.0)
    f = _bdot(u * u, w2_ref[...])
    o_ref[...] = x + gate2 * _rms(f, gffo_ref[...])


def _post(l, is_ctx, x2d, mod, oa, ob, y, z, lw):
    t = x2d.shape[0]
    per_seq = DEC_SEQ // TM
    mod_map = (lambda i: (l, 0, 0, 0)) if is_ctx else (lambda i: (l, 1 + i // per_seq, 0, 0))
    tok = lambda w: pl.BlockSpec((TM, w), lambda i: (i, 0))
    const = lambda shape: pl.BlockSpec((None,) + shape, lambda i: (l,) + (0,) * len(shape),
                                       pipeline_mode=pl.Buffered(1))
    return pl.pallas_call(
        _post_kernel,
        out_shape=jax.ShapeDtypeStruct((t, D_MODEL), F32),
        grid=(t // TM,),
        in_specs=[
            tok(D_MODEL),
            pl.BlockSpec((None, None, 1, 6 * D_MODEL), mod_map),
            tok(A_Q), tok(B_HEADS * B_V), tok(C_INNER), tok(C_INNER),
            const((1, C_INNER)), const((D_MODEL, D_MODEL)), const((1, D_MODEL)), const((1, D_MODEL)),
            const((1, D_MODEL)), const((D_MODEL, D_FF)), const((D_FF, D_MODEL)),
        ],
        out_specs=tok(D_MODEL),
        compiler_params=_cparams(("parallel",)),
        name="post_ctx" if is_ctx else "post_lat",
    )(x2d, mod, oa, ob, y, z, lw["g_ssm"], lw["w_out"], lw["g_mix"], lw["g_ffn"], lw["g_ffo"],
      lw["w1"], lw["w2"])


def _rope_tables():
    pos = np.arange(DEC_SEQ)
    axis_pos = np.stack([(pos // GRID_W), (pos % GRID_W)], axis=0).astype(np.float32)

    def pattern(rot_dim):
        half = rot_dim // 2
        quarter = half // 2
        inv = (1.0 / (np.float32(ROPE_THETA) ** (np.arange(0, half, 2, dtype=np.float32) / np.float32(half))))
        inv = inv.astype(np.float32)
        dd = np.arange(rot_dim)
        ang = (axis_pos[dd // half].T * inv[dd % quarter][None, :]).astype(np.float32)
        sign = np.where((dd % half) < quarter, -1.0, 1.0).astype(np.float32)
        return np.cos(ang).astype(np.float32), (np.sin(ang) * sign).astype(np.float32)

    ca, sa = pattern(HEAD_DIM)
    ca = np.tile(ca, (1, LANES // HEAD_DIM))
    sa = np.tile(sa, (1, LANES // HEAD_DIM))
    cb32, sb32 = pattern(B_ROPE)
    tail = LANES - KPE_LANE - B_ROPE
    cb = np.concatenate([np.ones((DEC_SEQ, KPE_LANE), np.float32), cb32, np.ones((DEC_SEQ, tail), np.float32)], 1)
    sb = np.concatenate([np.zeros((DEC_SEQ, KPE_LANE), np.float32), sb32, np.zeros((DEC_SEQ, tail), np.float32)], 1)
    ident_c = np.ones((TM, LANES), np.float32)
    ident_s = np.zeros((TM, LANES), np.float32)
    return tuple(jnp.asarray(np.concatenate([tb, idt], axis=0))
                 for tb, idt in ((ca, ident_c), (sa, ident_s), (cb, ident_c), (sb, ident_s)))


REPACK_ROWS = 256
W_IN_COLS = A_Q + 2 * A_KV + B_Q_RANK + B_KV_RANK + B_ROPE + C_INNER + C_CONV_CH + 2 * C_HEADS


def _repack_kernel(w_ref, o_ref):
    o_kpe = OFF_KVC + B_KV_RANK
    o_z = o_kpe + B_ROPE
    o_dt = o_z + C_INNER + C_CONV_CH
    o_ref[:, 0:o_kpe] = w_ref[:, 0:o_kpe].astype(BF16)
    o_ref[:, OFF_Z:OFF_MISC] = w_ref[:, o_z:o_dt].astype(BF16)
    lane = _lane_iota((REPACK_ROWS, LANES))
    kpe = pltpu.roll(w_ref[:, o_kpe:o_kpe + LANES], KPE_LANE, axis=1)
    dt = pltpu.roll(w_ref[:, W_IN_COLS - LANES:W_IN_COLS], DT_LANE + 2 * C_HEADS, axis=1)
    misc = jnp.where((lane >= KPE_LANE) & (lane < DT_LANE), kpe,
                     jnp.where((lane >= DT_LANE) & (lane < DT_LANE + 2 * C_HEADS), dt, 0.0))
    o_ref[:, OFF_MISC:IN_COLS_P] = misc.astype(BF16)


def _repack_w_in(w_in):
    return pl.pallas_call(
        _repack_kernel,
        out_shape=jax.ShapeDtypeStruct((DEPTH, D_MODEL, IN_COLS_P), BF16),
        grid=(DEPTH, D_MODEL // REPACK_ROWS),
        in_specs=[pl.BlockSpec((None, REPACK_ROWS, W_IN_COLS), lambda l, i: (l, i, 0))],
        out_specs=pl.BlockSpec((None, REPACK_ROWS, IN_COLS_P), lambda l, i: (l, i, 0)),
        compiler_params=_cparams(("parallel", "parallel")),
        name="repack_w_in",
    )(w_in)


def _prep_weights(p):
    w_in_p = _repack_w_in(p["w_in"])

    w_qb = p["mla_w_qb"].reshape(DEPTH, B_Q_RANK, B_HEADS, B_NOPE + B_ROPE)
    w_qb_p = jnp.pad(w_qb, ((0, 0),) * 3 + ((0, LANES - B_NOPE - B_ROPE),)).reshape(DEPTH, B_Q_RANK, QB_W)
    w_kvb = p["mla_w_kvb"].reshape(DEPTH, B_KV_RANK, B_HEADS, B_NOPE + B_V)
    w_k = jnp.pad(w_kvb[..., :B_NOPE], ((0, 0),) * 3 + ((0, LANES - B_NOPE),)).reshape(DEPTH, B_KV_RANK, QB_W)
    w_v = w_kvb[..., B_NOPE:].reshape(DEPTH, B_KV_RANK, B_HEADS * B_V)

    w_out = p["w_out"]
    order = [r + g * A_REP for r in range(A_REP) for g in range(A_KV_HEADS)]
    w_out_p = jnp.concatenate([w_out[:, hd * HEAD_DIM:(hd + 1) * HEAD_DIM] for hd in order] + [w_out[:, A_Q:]],
                              axis=1).astype(BF16)

    row = lambda v: v.reshape(DEPTH, 1, -1)
    pair_gain = lambda g: row(jnp.tile(g, (1, LANES // HEAD_DIM)))
    lane_vec = lambda v: jnp.pad(v.reshape(DEPTH, 1, -1), ((0, 0), (0, 0), (DT_LANE, LANES - DT_LANE - 2 * C_HEADS)))
    return {
        "g_pre": row(p["norm_mix_pre"]),
        "w_in": w_in_p,
        "g_q": pair_gain(p["attn_q_norm"]),
        "g_k": pair_gain(p["attn_k_norm"]),
        "g_qc": row(p["mla_q_norm"]),
        "w_qb": w_qb_p.astype(BF16),
        "g_kv": row(p["mla_kv_norm"]),
        "w_k": w_k.astype(BF16),
        "w_v": w_v.astype(BF16),
        "conv_w": jnp.swapaxes(p["ssm_conv_w"], 1, 2),
        "conv_b": row(p["ssm_conv_b"]),
        "dt_bias": lane_vec(p["ssm_dt_bias"]),
        "a_log": lane_vec(p["ssm_a_log"]),
        "d_vec": row(jnp.repeat(p["ssm_d"], C_HEAD_DIM, axis=1)),
        "g_ssm": row(p["ssm_norm"]),
        "w_out": w_out_p,
        "g_mix": row(p["norm_mix_post"]),
        "g_ffn": row(p["norm_ffn_pre"]),
        "g_ffo": row(p["norm_ffn_post"]),
        "w1": p["w_ffn1"].astype(BF16),
        "w2": p["w_ffn2"].astype(BF16),
    }


def _state_to_pairs(h):
    ht = jnp.swapaxes(h, -1, -2)
    zero = jnp.zeros_like(ht[..., 0, :, :])
    pairs = []
    for pr in range(N_PAIRS):
        heads = (2 * pr, 2 * pr + 1)
        row_blocks = [jnp.concatenate([ht[..., hd, :, :] if hd // HEADS_PER_GROUP == g else zero for hd in heads],
                                      axis=-1) for g in range(C_GROUPS)]
        pairs.append(jnp.concatenate(row_blocks, axis=-2))
    return jnp.stack(pairs, axis=-3)


def kernel(x_prompt, x_sample, cache_attn_k, cache_attn_v, cache_mla_ckv, cache_mla_kpe, state_ssm, c, c_ctx, norm_mix_pre, norm_mix_post, norm_ffn_pre, norm_ffn_post, w_mod, b_mod, w_in, attn_q_norm, attn_k_norm, mla_q_norm, mla_w_qb, mla_kv_norm, mla_w_kvb, ssm_conv_w, ssm_conv_b, ssm_dt_bias, ssm_a_log, ssm_d, ssm_norm, w_out, w_ffn1, w_ffn2):
    p = dict(norm_mix_pre=norm_mix_pre, norm_mix_post=norm_mix_post, norm_ffn_pre=norm_ffn_pre,
             norm_ffn_post=norm_ffn_post, w_in=w_in, attn_q_norm=attn_q_norm, attn_k_norm=attn_k_norm,
             mla_q_norm=mla_q_norm, mla_w_qb=mla_w_qb, mla_kv_norm=mla_kv_norm, mla_w_kvb=mla_w_kvb,
             ssm_conv_w=ssm_conv_w, ssm_conv_b=ssm_conv_b, ssm_dt_bias=ssm_dt_bias, ssm_a_log=ssm_a_log,
             ssm_d=ssm_d, ssm_norm=ssm_norm, w_out=w_out, w_ffn1=w_ffn1, w_ffn2=w_ffn2)
    lw = _prep_weights(p)
    tabs = _rope_tables()

    cvec = jnp.concatenate([c_ctx[None, :], c, jnp.zeros((MOD_ROWS - 1 - DEC_BATCH, D_MODEL), F32)], axis=0)
    mod = _modulation(cvec, w_mod, b_mod).reshape(DEPTH, MOD_ROWS, 1, 6 * D_MODEL)

    kpe_p = jnp.pad(cache_mla_kpe, ((0, 0),) * 3 + ((KPE_LANE, LANES - KPE_LANE - B_ROPE),))
    kc, vct, kbc, vbct = _cache_prep(
        cache_attn_k.reshape(DEC_BATCH, DEPTH, PAST_LEN, A_KV), cache_attn_v.reshape(DEC_BATCH, DEPTH, PAST_LEN, A_KV),
        cache_mla_ckv, kpe_p, lw["w_k"], lw["w_v"])
    h0_lat = _state_to_pairs(state_ssm)

    xp = x_prompt.reshape(BATCH * SEQ, D_MODEL)
    xs = x_sample.reshape(DEC_BATCH * DEC_SEQ, D_MODEL)
    new_k, new_v, new_ckv, new_kpe, new_ssm = [], [], [], [], []
    for l in range(DEPTH):
        qat, ka, vat, qbt, kb, vbt, z, xbc, misc, kf, vf, ckvf = _pre_ctx(l, xp, mod, lw, tabs)
        lat_pre, (oa, ob, y, hfin) = _pre_lat_mix_ctx(l, xs, mod, lw, tabs, (qat, ka, vat, qbt, kb, vbt, xbc, misc))
        xp = _post(l, True, xp, mod, oa, ob, y, z, lw)
        new_k.append(kf.reshape(BATCH, SEQ, A_KV_HEADS, HEAD_DIM))
        new_v.append(vf.reshape(BATCH, SEQ, A_KV_HEADS, HEAD_DIM))
        new_ckv.append(ckvf.reshape(BATCH, SEQ, B_KV_RANK))
        new_kpe.append(misc[:, KPE_LANE:KPE_LANE + B_ROPE].reshape(BATCH, SEQ, B_ROPE))
        new_ssm.append(hfin)
        qat, ka, vat, qbt, kb, vbt, z, xbc, misc = lat_pre
        oa = _attn_lat(l, True, qat, ka, vat, kc, vct)
        ob = _attn_lat(l, False, qbt, kb, vbt, kbc, vbct)
        y, _ = _ssd_lat(l, xbc, misc, lw, h0_lat)
        xs = _post(l, False, xs, mod, oa, ob, y, z, lw)
    return (xp.reshape(BATCH, SEQ, D_MODEL), xs.reshape(DEC_BATCH, DEC_SEQ, D_MODEL),
            jnp.stack(new_k, axis=1), jnp.stack(new_v, axis=1), jnp.stack(new_ckv, axis=1),
            jnp.stack(new_kpe, axis=1), jnp.stack(new_ssm, axis=1))
```

```python
import functools
import math

import numpy as np
import jax
import jax.numpy as jnp
from jax import lax
from jax.experimental import pallas as pl
from jax.experimental.pallas import tpu as pltpu

F32 = jnp.float32
BF16 = jnp.bfloat16

D_MODEL = 1024
BATCH = 16
SEQ = 256
DEPTH = 4
DEC_BATCH = 2
DEC_SEQ = 4096
PAST_LEN = 256
GRID_W = 64
HEAD_DIM = 64
A_HEADS = 6
A_KV_HEADS = 2
A_REP = A_HEADS // A_KV_HEADS
B_HEADS = 4
B_Q_RANK = 256
B_KV_RANK = 128
B_NOPE = 64
B_ROPE = 32
B_V = 64
C_HEADS = 6
C_HEAD_DIM = 64
C_INNER = C_HEADS * C_HEAD_DIM
C_GROUPS = 2
C_STATE = 64
C_CONV = 5
C_CHUNK = 128
C_CONV_CH = C_INNER + 2 * C_GROUPS * C_STATE
D_FF = 4 * D_MODEL
A_Q = A_HEADS * HEAD_DIM
A_KV = A_KV_HEADS * HEAD_DIM
ROPE_THETA = 10000.0
EPS = 1e-6
LOG2E = math.log2(math.e)

LANES = 128
SUBLANES = 8
VMEM_LIMIT = 56 * 1024 * 1024

OFF_QA = 0
OFF_KA = OFF_QA + A_Q
OFF_VA = OFF_KA + A_KV
OFF_QC = OFF_VA + A_KV
OFF_KVC = OFF_QC + B_Q_RANK
OFF_Z = OFF_KVC + B_KV_RANK
OFF_XBC = OFF_Z + C_INNER
OFF_MISC = OFF_XBC + C_CONV_CH
IN_COLS_P = OFF_MISC + LANES
KPE_LANE = 64
DT_LANE = KPE_LANE + B_ROPE
QB_W = B_HEADS * LANES

TM = 512
TQ = 512
TK = 1024


def _cparams(sem):
    return pltpu.CompilerParams(dimension_semantics=sem, vmem_limit_bytes=VMEM_LIMIT)


def _rms(x, g):
    return x * lax.rsqrt(jnp.mean(x * x, axis=-1, keepdims=True) + EPS) * g


def _bdot(a, b):
    return jnp.dot(a.astype(BF16), b.astype(BF16), preferred_element_type=F32)


def _lane_iota(shape):
    return lax.broadcasted_iota(jnp.int32, shape, len(shape) - 1)


def _swap_lanes(x, dist):
    lane = _lane_iota(x.shape)
    fwd = pltpu.roll(x, LANES - dist, axis=1)
    bwd = pltpu.roll(x, dist, axis=1)
    return jnp.where((lane % (2 * dist)) < dist, fwd, bwd)


def _rope(x, cos, sin, dist):
    return x * cos + _swap_lanes(x, dist) * sin


def _head_pair_rms(col, gain):
    lo = _lane_iota(col.shape) < HEAD_DIM
    c2 = col * col
    s_lo = jnp.sum(jnp.where(lo, c2, 0.0), axis=-1, keepdims=True)
    s_hi = jnp.sum(jnp.where(lo, 0.0, c2), axis=-1, keepdims=True)
    return col * lax.rsqrt(jnp.where(lo, s_lo, s_hi) * (1.0 / HEAD_DIM) + EPS) * gain


MOD_ROWS = SUBLANES
MOD_TN = 1536


def _mod_kernel(c_ref, w_ref, b_ref, o_ref):
    c = c_ref[...]
    o_ref[...] = _bdot(c * jax.nn.sigmoid(c), w_ref[...]) + b_ref[...]


def _modulation(cvec, w_mod, b_mod):
    return pl.pallas_call(
        _mod_kernel,
        out_shape=jax.ShapeDtypeStruct((DEPTH, MOD_ROWS, 6 * D_MODEL), F32),
        grid=(DEPTH, 6 * D_MODEL // MOD_TN),
        in_specs=[
            pl.BlockSpec((MOD_ROWS, D_MODEL), lambda l, j: (0, 0)),
            pl.BlockSpec((None, D_MODEL, MOD_TN), lambda l, j: (l, 0, j)),
            pl.BlockSpec((None, 1, MOD_TN), lambda l, j: (l, 0, j)),
        ],
        out_specs=pl.BlockSpec((None, MOD_ROWS, MOD_TN), lambda l, j: (l, 0, j)),
        compiler_params=_cparams(("parallel", "parallel")),
        name="modulation",
    )(cvec, w_mod, b_mod.reshape(DEPTH, 1, 6 * D_MODEL))


def _cache_kernel(k_ref, v_ref, ckv_ref, kpe_ref, wk_ref, wv_ref, ko_ref, vt_ref, kb_ref, vbt_ref):
    ko_ref[...] = k_ref[...].astype(BF16)
    vt_ref[...] = v_ref[...].T.astype(BF16)
    ckv = ckv_ref[...]
    kn = _bdot(ckv, wk_ref[...])
    kpe = kpe_ref[...]
    for h in range(B_HEADS):
        kb_ref[:, h * LANES:(h + 1) * LANES] = (kn[:, h * LANES:(h + 1) * LANES] + kpe).astype(BF16)
    vbt_ref[...] = _bdot(ckv, wv_ref[...]).T.astype(BF16)


def _cache_prep(cache_k, cache_v, cache_ckv, cache_kpe_p, wk_p, wv_p):
    spec_tok = lambda w: pl.BlockSpec((None, None, PAST_LEN, w), lambda l, b: (b, l, 0, 0))
    spec_t = lambda w: pl.BlockSpec((None, None, w, PAST_LEN), lambda l, b: (l, b, 0, 0))
    spec_o = lambda w: pl.BlockSpec((None, None, PAST_LEN, w), lambda l, b: (l, b, 0, 0))
    return pl.pallas_call(
        _cache_kernel,
        out_shape=(
            jax.ShapeDtypeStruct((DEPTH, DEC_BATCH, PAST_LEN, A_KV), BF16),
            jax.ShapeDtypeStruct((DEPTH, DEC_BATCH, A_KV, PAST_LEN), BF16),
            jax.ShapeDtypeStruct((DEPTH, DEC_BATCH, PAST_LEN, QB_W), BF16),
            jax.ShapeDtypeStruct((DEPTH, DEC_BATCH, B_HEADS * B_V, PAST_LEN), BF16),
        ),
        grid=(DEPTH, DEC_BATCH),
        in_specs=[
            spec_tok(A_KV), spec_tok(A_KV), spec_tok(B_KV_RANK), spec_tok(LANES),
            pl.BlockSpec((None, B_KV_RANK, QB_W), lambda l, b: (l, 0, 0)),
            pl.BlockSpec((None, B_KV_RANK, B_HEADS * B_V), lambda l, b: (l, 0, 0)),
        ],
        out_specs=(spec_o(A_KV), spec_t(A_KV), spec_o(QB_W), spec_t(B_HEADS * B_V)),
        compiler_params=_cparams(("parallel", "parallel")),
        name="cache_prep",
    )(cache_k, cache_v, cache_ckv, cache_kpe_p, wk_p, wv_p)


PRE_SUB = 2


def _pre_kernel(is_ctx, *refs):
    for sub in range(PRE_SUB):
        _pre_rows(is_ctx, slice(sub * (TM // PRE_SUB), (sub + 1) * (TM // PRE_SUB)), *refs)


def _pre_rows(is_ctx, rows, x_ref, mod_ref, gpre_ref, win_ref, gq_ref, gk_ref, gqc_ref, wqb_ref, gkv_ref,
              wk_ref, wv_ref, dtb_ref, ca_ref, sa_ref, cb_ref, sb_ref, *outs):
    if is_ctx:
        (qat_ref, ka_ref, vat_ref, qbt_ref, kb_ref, vbt_ref, z_ref, xbc_ref, misc_ref,
         kf_ref, vf_ref, ckvf_ref, kpef_ref) = outs
    else:
        qat_ref, ka_ref, vat_ref, qbt_ref, kb_ref, vbt_ref, z_ref, xbc_ref, misc_ref = outs
    x = x_ref[rows, :]
    shift1 = mod_ref[:, 0:D_MODEL]
    scale1 = mod_ref[:, D_MODEL:2 * D_MODEL]
    h = _rms(x, gpre_ref[...]) * (1.0 + scale1) + shift1
    proj = _bdot(h, win_ref[...])

    ca, sa, cb, sb = ca_ref[rows, :], sa_ref[rows, :], cb_ref[rows, :], sb_ref[rows, :]
    lane = _lane_iota((x.shape[0], LANES))

    gq = gq_ref[...]
    for cidx in range(A_Q // LANES):
        col = proj[:, OFF_QA + cidx * LANES:OFF_QA + (cidx + 1) * LANES]
        qn = _rope(_head_pair_rms(col, gq), ca, sa, HEAD_DIM // 4) * (HEAD_DIM ** -0.5 * LOG2E)
        qat_ref[cidx * LANES:(cidx + 1) * LANES, rows] = qn.T.astype(BF16)

    kn = _head_pair_rms(proj[:, OFF_KA:OFF_KA + A_KV], gk_ref[...])
    vcol = proj[:, OFF_VA:OFF_VA + A_KV]
    if is_ctx:
        kf_ref[rows, :] = kn
        vf_ref[rows, :] = vcol
    ka_ref[rows, :] = _rope(kn, ca, sa, HEAD_DIM // 4).astype(BF16)
    vat_ref[:, rows] = vcol.T.astype(BF16)

    qc = _rms(proj[:, OFF_QC:OFF_QC + B_Q_RANK], gqc_ref[...])
    qb = _bdot(qc, wqb_ref[...])
    for hd in range(B_HEADS):
        col = qb[:, hd * LANES:(hd + 1) * LANES]
        col = _rope(col, cb, sb, B_ROPE // 4) * ((B_NOPE + B_ROPE) ** -0.5 * LOG2E)
        qbt_ref[hd * LANES:(hd + 1) * LANES, rows] = col.T.astype(BF16)

    ckv = _rms(proj[:, OFF_KVC:OFF_KVC + B_KV_RANK], gkv_ref[...])
    if is_ctx:
        ckvf_ref[rows, :] = ckv
        kpef_ref[rows, :] = proj[:, OFF_MISC + KPE_LANE:OFF_MISC + KPE_LANE + B_ROPE]
    knope = _bdot(ckv, wk_ref[...])
    misc = proj[:, OFF_MISC:OFF_MISC + LANES]
    kpe = jnp.where((lane >= KPE_LANE) & (lane < KPE_LANE + B_ROPE), _rope(misc, cb, sb, B_ROPE // 4), 0.0)
    for hd in range(B_HEADS):
        kb_ref[rows, hd * LANES:(hd + 1) * LANES] = (knope[:, hd * LANES:(hd + 1) * LANES] + kpe).astype(BF16)
    vbt_ref[:, rows] = _bdot(ckv, wv_ref[...]).T.astype(BF16)

    x_dt = misc + dtb_ref[...]
    dtv = jnp.maximum(x_dt, 0.0) + jnp.log(1.0 + jnp.exp(-jnp.abs(x_dt)))
    misc_ref[rows, :] = jnp.where(lane >= DT_LANE, dtv, misc)

    z_ref[rows, :] = proj[:, OFF_Z:OFF_Z + C_INNER]
    xbc_ref[rows, :] = proj[:, OFF_XBC:OFF_XBC + C_CONV_CH]


def _pre_parts(l, is_ctx, x2d, mod, lw, tabs):
    t = x2d.shape[0]
    per_seq = DEC_SEQ // TM
    if is_ctx:
        mod_map = lambda i: (l, 0, 0, 0)
        tab_map = lambda i: (per_seq, 0)
    else:
        mod_map = lambda i: (l, 1 + i // per_seq, 0, 0)
        tab_map = lambda i: (i % per_seq, 0)
    const = lambda shape: pl.BlockSpec((None,) + shape, lambda i: (l,) + (0,) * len(shape))
    tab_spec = pl.BlockSpec((TM, LANES), tab_map)
    tok = lambda w: pl.BlockSpec((TM, w), lambda i: (i, 0))
    tok_t = lambda w: pl.BlockSpec((w, TM), lambda i: (0, i))
    out_shape = [
        jax.ShapeDtypeStruct((A_Q, t), BF16),
        jax.ShapeDtypeStruct((t, A_KV), BF16),
        jax.ShapeDtypeStruct((A_KV, t), BF16),
        jax.ShapeDtypeStruct((QB_W, t), BF16),
        jax.ShapeDtypeStruct((t, QB_W), BF16),
        jax.ShapeDtypeStruct((B_HEADS * B_V, t), BF16),
        jax.ShapeDtypeStruct((t, C_INNER), F32),
        jax.ShapeDtypeStruct((t, C_CONV_CH), F32),
        jax.ShapeDtypeStruct((t, LANES), F32),
    ]
    out_specs = [tok_t(A_Q), tok(A_KV), tok_t(A_KV), tok_t(QB_W), tok(QB_W), tok_t(B_HEADS * B_V),
                 tok(C_INNER), tok(C_CONV_CH), tok(LANES)]
    if is_ctx:
        out_shape += ([jax.ShapeDtypeStruct((t, A_KV), F32)] * 2 + [jax.ShapeDtypeStruct((t, B_KV_RANK), F32)]
                      + [jax.ShapeDtypeStruct((t, B_ROPE), F32)])
        out_specs += [tok(A_KV), tok(A_KV), tok(B_KV_RANK), tok(B_ROPE)]
    in_specs = [
        tok(D_MODEL),
        pl.BlockSpec((None, None, 1, 6 * D_MODEL), mod_map),
        const((1, D_MODEL)),
        const((D_MODEL, IN_COLS_P)),
        const((1, LANES)), const((1, LANES)), const((1, B_Q_RANK)),
        const((B_Q_RANK, QB_W)), const((1, B_KV_RANK)),
        const((B_KV_RANK, QB_W)), const((B_KV_RANK, B_HEADS * B_V)), const((1, LANES)),
        tab_spec, tab_spec, tab_spec, tab_spec,
    ]
    args = [x2d, mod, lw["g_pre"], lw["w_in"], lw["g_q"], lw["g_k"], lw["g_qc"], lw["w_qb"], lw["g_kv"],
            lw["w_k"], lw["w_v"], lw["dt_bias"], *tabs]
    return in_specs, args, out_shape, out_specs


def _pre_ctx(l, x2d, mod, lw, tabs):
    in_specs, args, out_shape, out_specs = _pre_parts(l, True, x2d, mod, lw, tabs)
    return pl.pallas_call(
        functools.partial(_pre_kernel, True),
        out_shape=tuple(out_shape),
        grid=(x2d.shape[0] // TM,),
        in_specs=in_specs,
        out_specs=tuple(out_specs),
        compiler_params=_cparams(("parallel",)),
        name="pre_ctx",
    )(*args)


def _merge_halves(lo_part, hi_part):
    return jnp.where(_lane_iota(lo_part.shape) < HEAD_DIM, lo_part, hi_part)


def _gqa_query_slot(qat_ref, hd):
    q = qat_ref[hd * HEAD_DIM:(hd + 1) * HEAD_DIM, :]
    zero = jnp.zeros_like(q)
    return jnp.concatenate([q, zero] if hd // A_REP == 0 else [zero, q], axis=0)


ONES_ROWS = 16
ACC_ROWS = LANES + ONES_ROWS


def _with_ones(v_t):
    return jnp.concatenate([v_t, jnp.ones((ONES_ROWS, v_t.shape[1]), BF16)], axis=0)


def _first_tile(k, q_t, v_t):
    return _first_scores(jnp.dot(k, q_t, preferred_element_type=F32), v_t)


def _first_scores(s, v_t):
    m = jnp.max(s, axis=0, keepdims=True)
    p = jnp.exp2(s - m)
    return m, jnp.dot(_with_ones(v_t), p.astype(BF16), preferred_element_type=F32)


def _next_tile(s, m, acc, v_t):
    m_new = jnp.maximum(m, jnp.max(s, axis=0, keepdims=True))
    p = jnp.exp2(s - m_new)
    acc = jnp.exp2(m - m_new) * acc + jnp.dot(_with_ones(v_t), p.astype(BF16), preferred_element_type=F32)
    return m_new, acc


def _normalised(acc):
    return (acc[0:LANES, :] / acc[LANES:LANES + 1, :]).T


def _write_heads(outs_a, outs_b, oa_ref, ob_ref):
    for r in range(A_REP):
        oa_ref[:, r * LANES:(r + 1) * LANES] = _merge_halves(outs_a[r], outs_a[A_REP + r]).astype(BF16)
    for pr in range(B_HEADS // 2):
        ob_ref[:, pr * LANES:(pr + 1) * LANES] = _merge_halves(outs_b[2 * pr], outs_b[2 * pr + 1]).astype(BF16)


def _attn_ctx_kernel(qat_ref, ka_ref, vat_ref, qbt_ref, kb_ref, vbt_ref, oa_ref, ob_ref):
    def split_heads(acc, n):
        o = acc[0:LANES, :] / acc[LANES:LANES + 1, :]
        return [o[:, i * SEQ:(i + 1) * SEQ].T for i in range(n)]

    q_all = jnp.concatenate([_gqa_query_slot(qat_ref, hd) for hd in range(A_HEADS)], axis=1)
    outs_a = split_heads(_first_tile(ka_ref[...], q_all, vat_ref[...])[1], A_HEADS)
    outs_b = []
    for pr in range(B_HEADS // 2):
        s = jnp.concatenate([jnp.dot(kb_ref[:, hd * LANES:(hd + 1) * LANES], qbt_ref[hd * LANES:(hd + 1) * LANES, :],
                                     preferred_element_type=F32) for hd in (2 * pr, 2 * pr + 1)], axis=1)
        outs_b += split_heads(_first_scores(s, vbt_ref[pr * LANES:(pr + 1) * LANES, :])[1], 2)
    _write_heads(outs_a, outs_b, oa_ref, ob_ref)


def _attn_lat_kernel(is_gqa, *refs):
    if is_gqa:
        q0_ref, q1_ref, k_ref, vt_ref, kc_ref, vct_ref, o_ref, s_ref, acc_ref = refs
        zero = jnp.zeros((HEAD_DIM, TQ), BF16)
        q_slots = (jnp.concatenate([q0_ref[...], zero], axis=0), jnp.concatenate([zero, q1_ref[...]], axis=0))
        key_cols = (0, 0)
    else:
        q_ref, k_ref, vt_ref, kc_ref, vct_ref, o_ref, s_ref, acc_ref = refs
        q_slots = (q_ref[0:LANES, :], q_ref[LANES:2 * LANES, :])
        key_cols = (0, LANES)
    n_tiles = DEC_SEQ // TK
    head_cut, tail_cut = TK // 4, TK - TK // 4
    plans = ([("cache", 0, PAST_LEN), ("lat", 0, TK // 2)] + [("lat", TK // 2 + j * TK, TK) for j in range(n_tiles - 1)]
             + [("lat", DEC_SEQ - TK // 2, TK // 2)],
             [("lat", 0, head_cut)] + [("lat", head_cut + j * TK, TK) for j in range(n_tiles - 1)]
             + [("lat", DEC_SEQ - tail_cut, tail_cut), ("cache", 0, PAST_LEN)])

    def keys(seg, c0):
        src, k0, n = seg
        return (kc_ref if src == "cache" else k_ref)[k0:k0 + n, c0:c0 + LANES]

    def vals(seg):
        src, k0, n = seg
        return (vct_ref if src == "cache" else vt_ref)[:, k0:k0 + n]

    row_max = [None, None]
    for hh in range(2):
        seg = plans[hh][0]
        s_ref[hh, 0, 0:seg[2], :] = jnp.dot(keys(seg, key_cols[hh]), q_slots[hh], preferred_element_type=F32)
    for j in range(max(len(plan) for plan in plans)):
        for hh in range(2):
            plan = plans[hh]
            if j >= len(plan):
                continue
            if j + 1 < len(plan):
                nxt = plan[j + 1]
                s_ref[hh, (j + 1) % 2, 0:nxt[2], :] = jnp.dot(keys(nxt, key_cols[hh]), q_slots[hh],
                                                              preferred_element_type=F32)
            seg = plan[j]
            s = s_ref[hh, j % 2, 0:seg[2], :]
            if j == 0:
                row_max[hh], acc = _first_scores(s, vals(seg))
            else:
                row_max[hh], acc = _next_tile(s, row_max[hh], acc_ref[hh], vals(seg))
            acc_ref[hh] = acc
    o_ref[...] = _merge_halves(_normalised(acc_ref[0]), _normalised(acc_ref[1])).astype(BF16)


def _attn_lat(l, is_gqa, q_t, k, v_t, kc, vc_t):
    t = k.shape[0]
    nq = DEC_SEQ // TQ
    n_pairs = A_REP if is_gqa else B_HEADS // 2
    kw = A_KV if is_gqa else 2 * LANES
    if is_gqa:
        q_specs = [pl.BlockSpec((HEAD_DIM, TQ), lambda b, r, i: (r, b * nq + i)),
                   pl.BlockSpec((HEAD_DIM, TQ), lambda b, r, i: (A_REP + r, b * nq + i))]
        q_args = [q_t, q_t]
        pair_col = lambda r: 0
    else:
        q_specs = [pl.BlockSpec((2 * LANES, TQ), lambda b, r, i: (r, b * nq + i))]
        q_args = [q_t]
        pair_col = lambda r: r
    return pl.pallas_call(
        functools.partial(_attn_lat_kernel, is_gqa),
        out_shape=jax.ShapeDtypeStruct((t, n_pairs * LANES), BF16),
        grid=(DEC_BATCH, n_pairs, nq),
        in_specs=q_specs + [
            pl.BlockSpec((DEC_SEQ, kw), lambda b, r, i: (b, pair_col(r))),
            pl.BlockSpec((LANES, DEC_SEQ), lambda b, r, i: (pair_col(r), b)),
            pl.BlockSpec((None, None, PAST_LEN, kw), lambda b, r, i: (l, b, 0, pair_col(r))),
            pl.BlockSpec((None, None, LANES, PAST_LEN), lambda b, r, i: (l, b, pair_col(r), 0)),
        ],
        out_specs=pl.BlockSpec((TQ, LANES), lambda b, r, i: (b * nq + i, r)),
        scratch_shapes=[pltpu.VMEM((2, 2, TK, TQ), F32), pltpu.VMEM((2, ACC_ROWS, TQ), F32)],
        compiler_params=_cparams(("parallel", "parallel", "parallel")),
        name="attn_lat_gqa" if is_gqa else "attn_lat_mla",
    )(*q_args, k, v_t, kc, vc_t)


HALO = SUBLANES
N_PAIRS = C_HEADS // 2
HEADS_PER_GROUP = C_HEADS // C_GROUPS


def _ssd_kernel(n_chunks, has_h0, xbc_ref, misc_ref, cw_ref, cb_ref, alog_ref, dvec_ref, *rest):
    if has_h0:
        h0_ref, y_ref, hout_ref, stf_ref, stb_ref, newb_ref, cm_ref, ecx_ref = rest
        stf_ref[...] = h0_ref[0]
        stb_ref[...] = h0_ref[1]
    else:
        y_ref, hout_ref, stf_ref, stb_ref, newb_ref, cm_ref, ecx_ref = rest
        stf_ref[...] = jnp.zeros_like(stf_ref)
        stb_ref[...] = jnp.zeros_like(stb_ref)
    seq_len = n_chunks * C_CHUNK
    row = lax.broadcasted_iota(jnp.int32, (C_CHUNK, LANES), 0)
    lane = _lane_iota((C_CHUNK, LANES))
    lo = lane < C_HEAD_DIM
    lane_group = (lane >= C_STATE).astype(jnp.int32)
    row_group = (row >= C_STATE).astype(jnp.int32)
    causal = (row >= lane, row <= lane)
    neg_a = -jnp.exp(alog_ref[...]) * LOG2E
    dt_lanes = (lane >= DT_LANE) & (lane < DT_LANE + 2 * C_HEADS)

    def lane_of(direction, hd):
        return DT_LANE + C_HEADS * direction + hd

    def pair_cols(v, direction, heads):
        j0, j1 = lane_of(direction, heads[0]), lane_of(direction, heads[1])
        return jnp.where(lo[0:v.shape[0], :], v[:, j0:j0 + 1], v[:, j1:j1 + 1])

    def pass1(c, carry):
        start = pl.multiple_of(c * C_CHUNK, C_CHUNK)
        prev0 = pl.multiple_of(jnp.maximum(start - HALO, 0), HALO)
        next0 = pl.multiple_of(jnp.minimum(start + C_CHUNK, seq_len - HALO), HALO)
        keep_prev = jnp.where(c > 0, 1.0, 0.0).astype(F32)
        keep_next = jnp.where(c < n_chunks - 1, 1.0, 0.0).astype(F32)
        rows_c = pl.ds(start, C_CHUNK)

        ue = jnp.concatenate([xbc_ref[pl.ds(prev0, HALO), :] * keep_prev, xbc_ref[rows_c, :],
                              xbc_ref[pl.ds(next0, HALO), :] * keep_next], axis=0)
        acc = jnp.broadcast_to(cb_ref[...], (C_CHUNK, C_CONV_CH))
        for k in range(C_CONV):
            sh = (C_CONV // 2 - k) % ue.shape[0]
            r = ue if sh == 0 else pltpu.roll(ue, sh, axis=0)
            acc = acc + r[HALO:HALO + C_CHUNK, :] * cw_ref[k:k + 1, :]
        xc = acc * jax.nn.sigmoid(acc)
        xs = xc[:, 0:C_INNER]
        bm = xc[:, C_INNER:C_INNER + LANES]
        cm = xc[:, C_INNER + LANES:C_INNER + 2 * LANES]
        bmt = bm.T.astype(BF16)
        cm16 = cm.astype(BF16)
        g_mats = [jnp.dot(jnp.where(lane_group == g, cm, 0.0).astype(BF16), bmt, preferred_element_type=F32)
                  for g in range(C_GROUPS)]

        dtv = jnp.where(dt_lanes, misc_ref[rows_c, :], 0.0)
        dta = dtv * neg_a
        cum = dta
        step = 1
        while step < C_CHUNK:
            cum = cum + jnp.where(row >= step, pltpu.roll(cum, step, axis=0), 0.0)
            step *= 2
        tot = cum[C_CHUNK - 1:C_CHUNK, :]
        cxs = (cum, tot - cum + dta)
        cxts = (cxs[0].T, cxs[1].T)
        dtt = dtv.T
        wsts = tuple(jnp.exp2(tot - cx) * dtv for cx in cxs)
        ecx_f = jnp.exp2(cxs[0])
        cdec = jnp.exp2(tot)
        cm_ref[c] = cm16
        ecx_ref[c, 0:C_CHUNK, :] = jnp.exp2(cxs[1])
        ecx_ref[c, C_CHUNK:C_CHUNK + SUBLANES, :] = jnp.broadcast_to(cdec, (SUBLANES, LANES))

        for pr in range(N_PAIRS):
            heads = (2 * pr, 2 * pr + 1)
            groups = tuple(hd // HEADS_PER_GROUP for hd in heads)
            xs_pair = xs[:, pr * LANES:(pr + 1) * LANES]
            xs16 = xs_pair.astype(BF16)
            own = row_group == jnp.where(lo, groups[0], groups[1])
            y = xs_pair * dvec_ref[:, pr * LANES:(pr + 1) * LANES]
            for direction in range(2):
                yd = []
                for hd in heads:
                    j = lane_of(direction, hd)
                    seg = jnp.where(causal[direction], cxs[direction][:, j:j + 1] - cxts[direction][j:j + 1, :],
                                    -jnp.inf)
                    sc = g_mats[hd // HEADS_PER_GROUP] * jnp.exp2(seg) * dtt[j:j + 1, :]
                    yd.append(jnp.dot(sc.astype(BF16), xs16, preferred_element_type=F32))
                y = y + _merge_halves(yd[0], yd[1])
                xw = xs_pair * pair_cols(wsts[direction], direction, heads)
                new = jnp.where(own, jnp.dot(bmt, xw.astype(BF16), preferred_element_type=F32), 0.0)
                if direction == 0:
                    st = stf_ref[pr]
                    y_off = jnp.dot(cm16, st.astype(BF16), preferred_element_type=F32)
                    y = y + y_off * pair_cols(ecx_f, 0, heads)
                    stf_ref[pr] = st * pair_cols(cdec, 0, heads) + new
                else:
                    newb_ref[c, pr] = new
            y_ref[rows_c, pr * LANES:(pr + 1) * LANES] = y
        return carry

    def pass2(i, carry):
        c = n_chunks - 1 - i
        rows_c = pl.ds(pl.multiple_of(c * C_CHUNK, C_CHUNK), C_CHUNK)
        cm16 = cm_ref[c]
        ecx_b = ecx_ref[c, 0:C_CHUNK, :]
        cdec = ecx_ref[c, C_CHUNK:C_CHUNK + 1, :]
        for pr in range(N_PAIRS):
            heads = (2 * pr, 2 * pr + 1)
            st = stb_ref[pr]
            y_off = jnp.dot(cm16, st.astype(BF16), preferred_element_type=F32)
            cols = pl.ds(pr * LANES, LANES)
            y_ref[rows_c, cols] = y_ref[rows_c, cols] + y_off * pair_cols(ecx_b, 1, heads)
            stb_ref[pr] = st * pair_cols(cdec, 1, heads) + newb_ref[c, pr]
        return carry

    unroll = True if n_chunks <= 2 else 4
    lax.fori_loop(0, n_chunks, pass1, 0, unroll=unroll)
    lax.fori_loop(0, n_chunks, pass2, 0, unroll=unroll)

    for direction, st_ref in enumerate((stf_ref, stb_ref)):
        for pr in range(N_PAIRS):
            st_t = st_ref[pr].T
            for hh, hd in enumerate((2 * pr, 2 * pr + 1)):
                g = hd // HEADS_PER_GROUP
                hout_ref[direction, hd] = st_t[hh * C_HEAD_DIM:(hh + 1) * C_HEAD_DIM,
                                               g * C_STATE:(g + 1) * C_STATE]


def _ssd_scratch(nc):
    return [pltpu.VMEM((N_PAIRS, LANES, LANES), F32), pltpu.VMEM((N_PAIRS, LANES, LANES), F32),
            pltpu.VMEM((nc, N_PAIRS, LANES, LANES), F32), pltpu.VMEM((nc, C_CHUNK, LANES), BF16),
            pltpu.VMEM((nc, C_CHUNK + SUBLANES, LANES), F32)]


def _ssd_lat(l, xbc, misc, lw, h0):
    t = xbc.shape[0]
    nb = t // DEC_SEQ
    nc = DEC_SEQ // C_CHUNK
    const = lambda shape: pl.BlockSpec((None,) + shape, lambda b: (l,) + (0,) * len(shape))
    return pl.pallas_call(
        functools.partial(_ssd_kernel, nc, True),
        out_shape=(jax.ShapeDtypeStruct((t, C_INNER), F32),
                   jax.ShapeDtypeStruct((nb, 2, C_HEADS, C_HEAD_DIM, C_STATE), F32)),
        grid=(nb,),
        in_specs=[
            pl.BlockSpec((DEC_SEQ, C_CONV_CH), lambda b: (b, 0)),
            pl.BlockSpec((DEC_SEQ, LANES), lambda b: (b, 0)),
            const((C_CONV, C_CONV_CH)), const((1, C_CONV_CH)), const((1, LANES)), const((1, C_INNER)),
            pl.BlockSpec((None, None, 2, N_PAIRS, LANES, LANES), lambda b: (b, l, 0, 0, 0, 0)),
        ],
        out_specs=(pl.BlockSpec((DEC_SEQ, C_INNER), lambda b: (b, 0)),
                   pl.BlockSpec((None, 2, C_HEADS, C_HEAD_DIM, C_STATE), lambda b: (b, 0, 0, 0, 0))),
        scratch_shapes=_ssd_scratch(nc),
        compiler_params=_cparams(("parallel",)),
        name="ssd_lat",
    )(xbc, misc, lw["conv_w"], lw["conv_b"], lw["a_log"], lw["d_vec"], h0)


def _ctx_mix_kernel(n_chunks, qat_ref, ka_ref, vat_ref, qbt_ref, kb_ref, vbt_ref, xbc_ref, misc_ref, cw_ref, cb_ref,
                    alog_ref, dvec_ref, oa_ref, ob_ref, y_ref, hout_ref, *scratch):
    _attn_ctx_kernel(qat_ref, ka_ref, vat_ref, qbt_ref, kb_ref, vbt_ref, oa_ref, ob_ref)
    _ssd_kernel(n_chunks, False, xbc_ref, misc_ref, cw_ref, cb_ref, alog_ref, dvec_ref, y_ref, hout_ref, *scratch)


def _ctx_mix_parts(l, qat, ka, vat, qbt, kb, vbt, xbc, misc, lw):
    t = ka.shape[0]
    tok = lambda w: pl.BlockSpec((SEQ, w), lambda b: (b, 0))
    tok_t = lambda w: pl.BlockSpec((w, SEQ), lambda b: (0, b))
    const = lambda shape: pl.BlockSpec((None,) + shape, lambda b: (l,) + (0,) * len(shape))
    in_specs = [tok_t(A_Q), tok(A_KV), tok_t(A_KV), tok_t(QB_W), tok(QB_W), tok_t(B_HEADS * B_V),
                tok(C_CONV_CH), tok(LANES),
                const((C_CONV, C_CONV_CH)), const((1, C_CONV_CH)), const((1, LANES)), const((1, C_INNER))]
    args = [qat, ka, vat, qbt, kb, vbt, xbc, misc, lw["conv_w"], lw["conv_b"], lw["a_log"], lw["d_vec"]]
    out_shape = [jax.ShapeDtypeStruct((t, A_Q), BF16), jax.ShapeDtypeStruct((t, B_HEADS * B_V), BF16),
                 jax.ShapeDtypeStruct((t, C_INNER), F32),
                 jax.ShapeDtypeStruct((t // SEQ, 2, C_HEADS, C_HEAD_DIM, C_STATE), F32)]
    out_specs = [tok(A_Q), tok(B_HEADS * B_V), tok(C_INNER),
                 pl.BlockSpec((None, 2, C_HEADS, C_HEAD_DIM, C_STATE), lambda b: (b, 0, 0, 0, 0))]
    return in_specs, args, out_shape, out_specs


def _pre_lat_mix_ctx_kernel(n_chunks, n_pre_in, n_mix_in, n_pre_out, n_mix_out, *refs):
    pre_in, refs = refs[:n_pre_in], refs[n_pre_in:]
    mix_in, refs = refs[:n_mix_in], refs[n_mix_in:]
    pre_out, refs = refs[:n_pre_out], refs[n_pre_out:]
    mix_out, scratch = refs[:n_mix_out], refs[n_mix_out:]
    _pre_kernel(False, *pre_in, *pre_out)
    _ctx_mix_kernel(n_chunks, *mix_in, *mix_out, *scratch)


def _pre_lat_mix_ctx(l, xs2d, mod, lw, tabs, ctx_mixer_inputs):
    p_in, p_args, p_shape, p_specs = _pre_parts(l, False, xs2d, mod, lw, tabs)
    m_in, m_args, m_shape, m_specs = _ctx_mix_parts(l, *ctx_mixer_inputs, lw)
    steps = xs2d.shape[0] // TM
    assert steps == ctx_mixer_inputs[1].shape[0] // SEQ
    nc = SEQ // C_CHUNK
    outs = pl.pallas_call(
        functools.partial(_pre_lat_mix_ctx_kernel, nc, len(p_in), len(m_in), len(p_shape), len(m_shape)),
        out_shape=tuple(p_shape + m_shape),
        grid=(steps,),
        in_specs=p_in + m_in,
        out_specs=tuple(p_specs + m_specs),
        scratch_shapes=_ssd_scratch(nc),
        compiler_params=_cparams(("parallel",)),
        name="pre_lat_mix_ctx",
    )(*p_args, *m_args)
    return outs[:len(p_shape)], outs[len(p_shape):]


def _post_kernel(x_ref, mod_ref, oa_ref, ob_ref, y_ref, z_ref, gssm_ref, wout_ref, gmix_ref, gffn_ref,
                 gffo_ref, w1_ref, w2_ref, o_ref):
    x = x_ref[...]
    gate1 = mod_ref[:, 2 * D_MODEL:3 * D_MODEL]
    shift2 = mod_ref[:, 3 * D_MODEL:4 * D_MODEL]
    scale2 = mod_ref[:, 4 * D_MODEL:5 * D_MODEL]
    gate2 = mod_ref[:, 5 * D_MODEL:6 * D_MODEL]
    z = z_ref[...]
    oc = _rms(y_ref[...] * (z * jax.nn.sigmoid(z)), gssm_ref[...])
    mix = jnp.concatenate([oa_ref[...], ob_ref[...], oc.astype(BF16)], axis=1)
    out = jnp.dot(mix, wout_ref[...], preferred_element_type=F32)
    x = x + gate1 * _rms(out, gmix_ref[...])
    h = _rms(x, gffn_ref[...]) * (1.0 + scale2) + shift2
    u = jnp.maximum(_bdot(h, w1_ref[...]), 0.0)
    f = _bdot(u * u, w2_ref[...])
    o_ref[...] = x + gate2 * _rms(f, gffo_ref[...])


def _post(l, is_ctx, x2d, mod, oa, ob, y, z, lw):
    t = x2d.shape[0]
    per_seq = DEC_SEQ // TM
    mod_map = (lambda i: (l, 0, 0, 0)) if is_ctx else (lambda i: (l, 1 + i // per_seq, 0, 0))
    tok = lambda w: pl.BlockSpec((TM, w), lambda i: (i, 0))
    const = lambda shape: pl.BlockSpec((None,) + shape, lambda i: (l,) + (0,) * len(shape),
                                       pipeline_mode=pl.Buffered(1))
    return pl.pallas_call(
        _post_kernel,
        out_shape=jax.ShapeDtypeStruct((t, D_MODEL), F32),
        grid=(t // TM,),
        in_specs=[
            tok(D_MODEL),
            pl.BlockSpec((None, None, 1, 6 * D_MODEL), mod_map),
            tok(A_Q), tok(B_HEADS * B_V), tok(C_INNER), tok(C_INNER),
            const((1, C_INNER)), const((D_MODEL, D_MODEL)), const((1, D_MODEL)), const((1, D_MODEL)),
            const((1, D_MODEL)), const((D_MODEL, D_FF)), const((D_FF, D_MODEL)),
        ],
        out_specs=tok(D_MODEL),
        compiler_params=_cparams(("parallel",)),
        name="post_ctx" if is_ctx else "post_lat",
    )(x2d, mod, oa, ob, y, z, lw["g_ssm"], lw["w_out"], lw["g_mix"], lw["g_ffn"], lw["g_ffo"],
      lw["w1"], lw["w2"])


def _rope_tables():
    pos = np.arange(DEC_SEQ)
    axis_pos = np.stack([(pos // GRID_W), (pos % GRID_W)], axis=0).astype(np.float32)

    def pattern(rot_dim):
        half = rot_dim // 2
        quarter = half // 2
        inv = (1.0 / (np.float32(ROPE_THETA) ** (np.arange(0, half, 2, dtype=np.float32) / np.float32(half))))
        inv = inv.astype(np.float32)
        dd = np.arange(rot_dim)
        ang = (axis_pos[dd // half].T * inv[dd % quarter][None, :]).astype(np.float32)
        sign = np.where((dd % half) < quarter, -1.0, 1.0).astype(np.float32)
        return np.cos(ang).astype(np.float32), (np.sin(ang) * sign).astype(np.float32)

    ca, sa = pattern(HEAD_DIM)
    ca = np.tile(ca, (1, LANES // HEAD_DIM))
    sa = np.tile(sa, (1, LANES // HEAD_DIM))
    cb32, sb32 = pattern(B_ROPE)
    tail = LANES - KPE_LANE - B_ROPE
    cb = np.concatenate([np.ones((DEC_SEQ, KPE_LANE), np.float32), cb32, np.ones((DEC_SEQ, tail), np.float32)], 1)
    sb = np.concatenate([np.zeros((DEC_SEQ, KPE_LANE), np.float32), sb32, np.zeros((DEC_SEQ, tail), np.float32)], 1)
    ident_c = np.ones((TM, LANES), np.float32)
    ident_s = np.zeros((TM, LANES), np.float32)
    return tuple(jnp.asarray(np.concatenate([tb, idt], axis=0))
                 for tb, idt in ((ca, ident_c), (sa, ident_s), (cb, ident_c), (sb, ident_s)))


REPACK_ROWS = 256
W_IN_COLS = A_Q + 2 * A_KV + B_Q_RANK + B_KV_RANK + B_ROPE + C_INNER + C_CONV_CH + 2 * C_HEADS


def _repack_kernel(w_ref, o_ref):
    o_kpe = OFF_KVC + B_KV_RANK
    o_z = o_kpe + B_ROPE
    o_dt = o_z + C_INNER + C_CONV_CH
    o_ref[:, 0:o_kpe] = w_ref[:, 0:o_kpe].astype(BF16)
    o_ref[:, OFF_Z:OFF_MISC] = w_ref[:, o_z:o_dt].astype(BF16)
    lane = _lane_iota((REPACK_ROWS, LANES))
    kpe = pltpu.roll(w_ref[:, o_kpe:o_kpe + LANES], KPE_LANE, axis=1)
    dt = pltpu.roll(w_ref[:, W_IN_COLS - LANES:W_IN_COLS], DT_LANE + 2 * C_HEADS, axis=1)
    misc = jnp.where((lane >= KPE_LANE) & (lane < DT_LANE), kpe,
                     jnp.where((lane >= DT_LANE) & (lane < DT_LANE + 2 * C_HEADS), dt, 0.0))
    o_ref[:, OFF_MISC:IN_COLS_P] = misc.astype(BF16)


def _repack_w_in(w_in):
    return pl.pallas_call(
        _repack_kernel,
        out_shape=jax.ShapeDtypeStruct((DEPTH, D_MODEL, IN_COLS_P), BF16),
        grid=(DEPTH, D_MODEL // REPACK_ROWS),
        in_specs=[pl.BlockSpec((None, REPACK_ROWS, W_IN_COLS), lambda l, i: (l, i, 0))],
        out_specs=pl.BlockSpec((None, REPACK_ROWS, IN_COLS_P), lambda l, i: (l, i, 0)),
        compiler_params=_cparams(("parallel", "parallel")),
        name="repack_w_in",
    )(w_in)


def _prep_weights(p):
    w_in_p = _repack_w_in(p["w_in"])

    w_qb = p["mla_w_qb"].reshape(DEPTH, B_Q_RANK, B_HEADS, B_NOPE + B_ROPE)
    w_qb_p = jnp.pad(w_qb, ((0, 0),) * 3 + ((0, LANES - B_NOPE - B_ROPE),)).reshape(DEPTH, B_Q_RANK, QB_W)
    w_kvb = p["mla_w_kvb"].reshape(DEPTH, B_KV_RANK, B_HEADS, B_NOPE + B_V)
    w_k = jnp.pad(w_kvb[..., :B_NOPE], ((0, 0),) * 3 + ((0, LANES - B_NOPE),)).reshape(DEPTH, B_KV_RANK, QB_W)
    w_v = w_kvb[..., B_NOPE:].reshape(DEPTH, B_KV_RANK, B_HEADS * B_V)

    w_out = p["w_out"]
    order = [r + g * A_REP for r in range(A_REP) for g in range(A_KV_HEADS)]
    w_out_p = jnp.concatenate([w_out[:, hd * HEAD_DIM:(hd + 1) * HEAD_DIM] for hd in order] + [w_out[:, A_Q:]],
                              axis=1).astype(BF16)

    row = lambda v: v.reshape(DEPTH, 1, -1)
    pair_gain = lambda g: row(jnp.tile(g, (1, LANES // HEAD_DIM)))
    lane_vec = lambda v: jnp.pad(v.reshape(DEPTH, 1, -1), ((0, 0), (0, 0), (DT_LANE, LANES - DT_LANE - 2 * C_HEADS)))
    return {
        "g_pre": row(p["norm_mix_pre"]),
        "w_in": w_in_p,
        "g_q": pair_gain(p["attn_q_norm"]),
        "g_k": pair_gain(p["attn_k_norm"]),
        "g_qc": row(p["mla_q_norm"]),
        "w_qb": w_qb_p.astype(BF16),
        "g_kv": row(p["mla_kv_norm"]),
        "w_k": w_k.astype(BF16),
        "w_v": w_v.astype(BF16),
        "conv_w": jnp.swapaxes(p["ssm_conv_w"], 1, 2),
        "conv_b": row(p["ssm_conv_b"]),
        "dt_bias": lane_vec(p["ssm_dt_bias"]),
        "a_log": lane_vec(p["ssm_a_log"]),
        "d_vec": row(jnp.repeat(p["ssm_d"], C_HEAD_DIM, axis=1)),
        "g_ssm": row(p["ssm_norm"]),
        "w_out": w_out_p,
        "g_mix": row(p["norm_mix_post"]),
        "g_ffn": row(p["norm_ffn_pre"]),
        "g_ffo": row(p["norm_ffn_post"]),
        "w1": p["w_ffn1"].astype(BF16),
        "w2": p["w_ffn2"].astype(BF16),
    }


def _state_to_pairs(h):
    ht = jnp.swapaxes(h, -1, -2)
    zero = jnp.zeros_like(ht[..., 0, :, :])
    pairs = []
    for pr in range(N_PAIRS):
        heads = (2 * pr, 2 * pr + 1)
        row_blocks = [jnp.concatenate([ht[..., hd, :, :] if hd // HEADS_PER_GROUP == g else zero for hd in heads],
                                      axis=-1) for g in range(C_GROUPS)]
        pairs.append(jnp.concatenate(row_blocks, axis=-2))
    return jnp.stack(pairs, axis=-3)


def kernel(x_prompt, x_sample, cache_attn_k, cache_attn_v, cache_mla_ckv, cache_mla_kpe, state_ssm, c, c_ctx, norm_mix_pre, norm_mix_post, norm_ffn_pre, norm_ffn_post, w_mod, b_mod, w_in, attn_q_norm, attn_k_norm, mla_q_norm, mla_w_qb, mla_kv_norm, mla_w_kvb, ssm_conv_w, ssm_conv_b, ssm_dt_bias, ssm_a_log, ssm_d, ssm_norm, w_out, w_ffn1, w_ffn2):
    p = dict(norm_mix_pre=norm_mix_pre, norm_mix_post=norm_mix_post, norm_ffn_pre=norm_ffn_pre,
             norm_ffn_post=norm_ffn_post, w_in=w_in, attn_q_norm=attn_q_norm, attn_k_norm=attn_k_norm,
             mla_q_norm=mla_q_norm, mla_w_qb=mla_w_qb, mla_kv_norm=mla_kv_norm, mla_w_kvb=mla_w_kvb,
             ssm_conv_w=ssm_conv_w, ssm_conv_b=ssm_conv_b, ssm_dt_bias=ssm_dt_bias, ssm_a_log=ssm_a_log,
             ssm_d=ssm_d, ssm_norm=ssm_norm, w_out=w_out, w_ffn1=w_ffn1, w_ffn2=w_ffn2)
    lw = _prep_weights(p)
    tabs = _rope_tables()

    cvec = jnp.concatenate([c_ctx[None, :], c, jnp.zeros((MOD_ROWS - 1 - DEC_BATCH, D_MODEL), F32)], axis=0)
    mod = _modulation(cvec, w_mod, b_mod).reshape(DEPTH, MOD_ROWS, 1, 6 * D_MODEL)

    kpe_p = jnp.pad(cache_mla_kpe, ((0, 0),) * 3 + ((KPE_LANE, LANES - KPE_LANE - B_ROPE),))
    kc, vct, kbc, vbct = _cache_prep(
        cache_attn_k.reshape(DEC_BATCH, DEPTH, PAST_LEN, A_KV), cache_attn_v.reshape(DEC_BATCH, DEPTH, PAST_LEN, A_KV),
        cache_mla_ckv, kpe_p, lw["w_k"], lw["w_v"])
    h0_lat = _state_to_pairs(state_ssm)

    xp = x_prompt.reshape(BATCH * SEQ, D_MODEL)
    xs = x_sample.reshape(DEC_BATCH * DEC_SEQ, D_MODEL)
    new_k, new_v, new_ckv, new_kpe, new_ssm = [], [], [], [], []
    for l in range(DEPTH):
        qat, ka, vat, qbt, kb, vbt, z, xbc, misc, kf, vf, ckvf, kpef = _pre_ctx(l, xp, mod, lw, tabs)
        lat_pre, (oa, ob, y, hfin) = _pre_lat_mix_ctx(l, xs, mod, lw, tabs, (qat, ka, vat, qbt, kb, vbt, xbc, misc))
        xp = _post(l, True, xp, mod, oa, ob, y, z, lw)
        new_k.append(kf.reshape(BATCH, SEQ, A_KV_HEADS, HEAD_DIM))
        new_v.append(vf.reshape(BATCH, SEQ, A_KV_HEADS, HEAD_DIM))
        new_ckv.append(ckvf.reshape(BATCH, SEQ, B_KV_RANK))
        new_kpe.append(kpef.reshape(BATCH, SEQ, B_ROPE))
        new_ssm.append(hfin)
        qat, ka, vat, qbt, kb, vbt, z, xbc, misc = lat_pre
        oa = _attn_lat(l, True, qat, ka, vat, kc, vct)
        ob = _attn_lat(l, False, qbt, kb, vbt, kbc, vbct)
        y, _ = _ssd_lat(l, xbc, misc, lw, h0_lat)
        xs = _post(l, False, xs, mod, oa, ob, y, z, lw)
    return (xp.reshape(BATCH, SEQ, D_MODEL), xs.reshape(DEC_BATCH, DEC_SEQ, D_MODEL),
            jnp.stack(new_k, axis=1), jnp.stack(new_v, axis=1), jnp.stack(new_ckv, axis=1),
            jnp.stack(new_kpe, axis=1), jnp.stack(new_ssm, axis=1))
```

```python
import functools
import math

import numpy as np
import jax
import jax.numpy as jnp
from jax import lax
from jax.experimental import pallas as pl
from jax.experimental.pallas import tpu as pltpu

F32 = jnp.float32
BF16 = jnp.bfloat16

D_MODEL = 1024
BATCH = 16
SEQ = 256
DEPTH = 4
DEC_BATCH = 2
DEC_SEQ = 4096
PAST_LEN = 256
GRID_W = 64
HEAD_DIM = 64
A_HEADS = 6
A_KV_HEADS = 2
A_REP = A_HEADS // A_KV_HEADS
B_HEADS = 4
B_Q_RANK = 256
B_KV_RANK = 128
B_NOPE = 64
B_ROPE = 32
B_V = 64
C_HEADS = 6
C_HEAD_DIM = 64
C_INNER = C_HEADS * C_HEAD_DIM
C_GROUPS = 2
C_STATE = 64
C_CONV = 5
C_CHUNK = 128
C_CONV_CH = C_INNER + 2 * C_GROUPS * C_STATE
D_FF = 4 * D_MODEL
A_Q = A_HEADS * HEAD_DIM
A_KV = A_KV_HEADS * HEAD_DIM
ROPE_THETA = 10000.0
EPS = 1e-6
LOG2E = math.log2(math.e)

LANES = 128
SUBLANES = 8
VMEM_LIMIT = 56 * 1024 * 1024

OFF_QA = 0
OFF_KA = OFF_QA + A_Q
OFF_VA = OFF_KA + A_KV
OFF_QC = OFF_VA + A_KV
OFF_KVC = OFF_QC + B_Q_RANK
OFF_Z = OFF_KVC + B_KV_RANK
OFF_XBC = OFF_Z + C_INNER
OFF_MISC = OFF_XBC + C_CONV_CH
IN_COLS_P = OFF_MISC + LANES
KPE_LANE = 64
DT_LANE = KPE_LANE + B_ROPE
QB_W = B_HEADS * LANES

TM = 512
TQ = 512
TK = 1024
SEG_MAX = TK + LANES


def _cparams(sem):
    return pltpu.CompilerParams(dimension_semantics=sem, vmem_limit_bytes=VMEM_LIMIT)


def _rms(x, g):
    return x * lax.rsqrt(jnp.mean(x * x, axis=-1, keepdims=True) + EPS) * g


def _bdot(a, b):
    return jnp.dot(a.astype(BF16), b.astype(BF16), preferred_element_type=F32)


def _lane_iota(shape):
    return lax.broadcasted_iota(jnp.int32, shape, len(shape) - 1)


def _swap_lanes(x, dist):
    lane = _lane_iota(x.shape)
    fwd = pltpu.roll(x, LANES - dist, axis=1)
    bwd = pltpu.roll(x, dist, axis=1)
    return jnp.where((lane % (2 * dist)) < dist, fwd, bwd)


def _rope(x, cos, sin, dist):
    return x * cos + _swap_lanes(x, dist) * sin


def _head_pair_rms(col, gain):
    lo = _lane_iota(col.shape) < HEAD_DIM
    c2 = col * col
    s_lo = jnp.sum(jnp.where(lo, c2, 0.0), axis=-1, keepdims=True)
    s_hi = jnp.sum(jnp.where(lo, 0.0, c2), axis=-1, keepdims=True)
    return col * lax.rsqrt(jnp.where(lo, s_lo, s_hi) * (1.0 / HEAD_DIM) + EPS) * gain


MOD_ROWS = SUBLANES
MOD_TN = 1536


def _mod_kernel(c_ref, w_ref, b_ref, o_ref):
    c = c_ref[...]
    o_ref[...] = _bdot(c * jax.nn.sigmoid(c), w_ref[...]) + b_ref[...]


def _modulation(cvec, w_mod, b_mod):
    return pl.pallas_call(
        _mod_kernel,
        out_shape=jax.ShapeDtypeStruct((DEPTH, MOD_ROWS, 6 * D_MODEL), F32),
        grid=(DEPTH, 6 * D_MODEL // MOD_TN),
        in_specs=[
            pl.BlockSpec((MOD_ROWS, D_MODEL), lambda l, j: (0, 0)),
            pl.BlockSpec((None, D_MODEL, MOD_TN), lambda l, j: (l, 0, j)),
            pl.BlockSpec((None, 1, MOD_TN), lambda l, j: (l, 0, j)),
        ],
        out_specs=pl.BlockSpec((None, MOD_ROWS, MOD_TN), lambda l, j: (l, 0, j)),
        compiler_params=_cparams(("parallel", "parallel")),
        name="modulation",
    )(cvec, w_mod, b_mod.reshape(DEPTH, 1, 6 * D_MODEL))


def _cache_kernel(k_ref, v_ref, ckv_ref, kpe_ref, wk_ref, wv_ref, ko_ref, vt_ref, kb_ref, vbt_ref):
    ko_ref[...] = k_ref[...].astype(BF16)
    vt_ref[...] = v_ref[...].T.astype(BF16)
    ckv = ckv_ref[...]
    kn = _bdot(ckv, wk_ref[...])
    kpe = kpe_ref[...]
    for h in range(B_HEADS):
        kb_ref[:, h * LANES:(h + 1) * LANES] = (kn[:, h * LANES:(h + 1) * LANES] + kpe).astype(BF16)
    vbt_ref[...] = _bdot(ckv, wv_ref[...]).T.astype(BF16)


def _cache_prep(cache_k, cache_v, cache_ckv, cache_kpe_p, wk_p, wv_p):
    spec_tok = lambda w: pl.BlockSpec((None, None, PAST_LEN, w), lambda l, b: (b, l, 0, 0))
    spec_t = lambda w: pl.BlockSpec((None, None, w, PAST_LEN), lambda l, b: (l, b, 0, 0))
    spec_o = lambda w: pl.BlockSpec((None, None, PAST_LEN, w), lambda l, b: (l, b, 0, 0))
    return pl.pallas_call(
        _cache_kernel,
        out_shape=(
            jax.ShapeDtypeStruct((DEPTH, DEC_BATCH, PAST_LEN, A_KV), BF16),
            jax.ShapeDtypeStruct((DEPTH, DEC_BATCH, A_KV, PAST_LEN), BF16),
            jax.ShapeDtypeStruct((DEPTH, DEC_BATCH, PAST_LEN, QB_W), BF16),
            jax.ShapeDtypeStruct((DEPTH, DEC_BATCH, B_HEADS * B_V, PAST_LEN), BF16),
        ),
        grid=(DEPTH, DEC_BATCH),
        in_specs=[
            spec_tok(A_KV), spec_tok(A_KV), spec_tok(B_KV_RANK), spec_tok(LANES),
            pl.BlockSpec((None, B_KV_RANK, QB_W), lambda l, b: (l, 0, 0)),
            pl.BlockSpec((None, B_KV_RANK, B_HEADS * B_V), lambda l, b: (l, 0, 0)),
        ],
        out_specs=(spec_o(A_KV), spec_t(A_KV), spec_o(QB_W), spec_t(B_HEADS * B_V)),
        compiler_params=_cparams(("parallel", "parallel")),
        name="cache_prep",
    )(cache_k, cache_v, cache_ckv, cache_kpe_p, wk_p, wv_p)


PRE_SUB = 2


def _pre_kernel(is_ctx, *refs):
    for sub in range(PRE_SUB):
        _pre_rows(is_ctx, slice(sub * (TM // PRE_SUB), (sub + 1) * (TM // PRE_SUB)), *refs)


def _pre_rows(is_ctx, rows, x_ref, mod_ref, gpre_ref, win_ref, gq_ref, gk_ref, gqc_ref, wqb_ref, gkv_ref,
              wk_ref, wv_ref, dtb_ref, ca_ref, sa_ref, cb_ref, sb_ref, *outs):
    if is_ctx:
        (qat_ref, ka_ref, vat_ref, qbt_ref, kb_ref, vbt_ref, z_ref, xbc_ref, misc_ref,
         kf_ref, vf_ref, ckvf_ref, kpef_ref) = outs
    else:
        qat_ref, ka_ref, vat_ref, qbt_ref, kb_ref, vbt_ref, z_ref, xbc_ref, misc_ref = outs
    x = x_ref[rows, :]
    shift1 = mod_ref[:, 0:D_MODEL]
    scale1 = mod_ref[:, D_MODEL:2 * D_MODEL]
    h = _rms(x, gpre_ref[...]) * (1.0 + scale1) + shift1
    proj = _bdot(h, win_ref[...])

    ca, sa, cb, sb = ca_ref[rows, :], sa_ref[rows, :], cb_ref[rows, :], sb_ref[rows, :]
    lane = _lane_iota((x.shape[0], LANES))

    gq = gq_ref[...]
    for cidx in range(A_Q // LANES):
        col = proj[:, OFF_QA + cidx * LANES:OFF_QA + (cidx + 1) * LANES]
        qn = _rope(_head_pair_rms(col, gq), ca, sa, HEAD_DIM // 4) * (HEAD_DIM ** -0.5 * LOG2E)
        qat_ref[cidx * LANES:(cidx + 1) * LANES, rows] = qn.T.astype(BF16)

    kn = _head_pair_rms(proj[:, OFF_KA:OFF_KA + A_KV], gk_ref[...])
    vcol = proj[:, OFF_VA:OFF_VA + A_KV]
    if is_ctx:
        kf_ref[rows, :] = kn
        vf_ref[rows, :] = vcol
    ka_ref[rows, :] = _rope(kn, ca, sa, HEAD_DIM // 4).astype(BF16)
    vat_ref[:, rows] = vcol.T.astype(BF16)

    qc = _rms(proj[:, OFF_QC:OFF_QC + B_Q_RANK], gqc_ref[...])
    qb = _bdot(qc, wqb_ref[...])
    for hd in range(B_HEADS):
        col = qb[:, hd * LANES:(hd + 1) * LANES]
        col = _rope(col, cb, sb, B_ROPE // 4) * ((B_NOPE + B_ROPE) ** -0.5 * LOG2E)
        qbt_ref[hd * LANES:(hd + 1) * LANES, rows] = col.T.astype(BF16)

    ckv = _rms(proj[:, OFF_KVC:OFF_KVC + B_KV_RANK], gkv_ref[...])
    if is_ctx:
        ckvf_ref[rows, :] = ckv
        kpef_ref[rows, :] = proj[:, OFF_MISC + KPE_LANE:OFF_MISC + KPE_LANE + B_ROPE]
    knope = _bdot(ckv, wk_ref[...])
    misc = proj[:, OFF_MISC:OFF_MISC + LANES]
    kpe = jnp.where((lane >= KPE_LANE) & (lane < KPE_LANE + B_ROPE), _rope(misc, cb, sb, B_ROPE // 4), 0.0)
    for hd in range(B_HEADS):
        kb_ref[rows, hd * LANES:(hd + 1) * LANES] = (knope[:, hd * LANES:(hd + 1) * LANES] + kpe).astype(BF16)
    vbt_ref[:, rows] = _bdot(ckv, wv_ref[...]).T.astype(BF16)

    x_dt = misc + dtb_ref[...]
    dtv = jnp.maximum(x_dt, 0.0) + jnp.log(1.0 + jnp.exp(-jnp.abs(x_dt)))
    misc_ref[rows, :] = jnp.where(lane >= DT_LANE, dtv, misc)

    z_ref[rows, :] = proj[:, OFF_Z:OFF_Z + C_INNER]
    xbc_ref[rows, :] = proj[:, OFF_XBC:OFF_XBC + C_CONV_CH]


def _pre_parts(l, is_ctx, x2d, mod, lw, tabs):
    t = x2d.shape[0]
    per_seq = DEC_SEQ // TM
    if is_ctx:
        mod_map = lambda i: (l, 0, 0, 0)
        tab_map = lambda i: (per_seq, 0)
    else:
        mod_map = lambda i: (l, 1 + i // per_seq, 0, 0)
        tab_map = lambda i: (i % per_seq, 0)
    const = lambda shape: pl.BlockSpec((None,) + shape, lambda i: (l,) + (0,) * len(shape))
    tab_spec = pl.BlockSpec((TM, LANES), tab_map)
    tok = lambda w: pl.BlockSpec((TM, w), lambda i: (i, 0))
    tok_t = lambda w: pl.BlockSpec((w, TM), lambda i: (0, i))
    out_shape = [
        jax.ShapeDtypeStruct((A_Q, t), BF16),
        jax.ShapeDtypeStruct((t, A_KV), BF16),
        jax.ShapeDtypeStruct((A_KV, t), BF16),
        jax.ShapeDtypeStruct((QB_W, t), BF16),
        jax.ShapeDtypeStruct((t, QB_W), BF16),
        jax.ShapeDtypeStruct((B_HEADS * B_V, t), BF16),
        jax.ShapeDtypeStruct((t, C_INNER), F32),
        jax.ShapeDtypeStruct((t, C_CONV_CH), F32),
        jax.ShapeDtypeStruct((t, LANES), F32),
    ]
    out_specs = [tok_t(A_Q), tok(A_KV), tok_t(A_KV), tok_t(QB_W), tok(QB_W), tok_t(B_HEADS * B_V),
                 tok(C_INNER), tok(C_CONV_CH), tok(LANES)]
    if is_ctx:
        out_shape += ([jax.ShapeDtypeStruct((t, A_KV), F32)] * 2 + [jax.ShapeDtypeStruct((t, B_KV_RANK), F32)]
                      + [jax.ShapeDtypeStruct((t, B_ROPE), F32)])
        out_specs += [tok(A_KV), tok(A_KV), tok(B_KV_RANK), tok(B_ROPE)]
    in_specs = [
        tok(D_MODEL),
        pl.BlockSpec((None, None, 1, 6 * D_MODEL), mod_map),
        const((1, D_MODEL)),
        const((D_MODEL, IN_COLS_P)),
        const((1, LANES)), const((1, LANES)), const((1, B_Q_RANK)),
        const((B_Q_RANK, QB_W)), const((1, B_KV_RANK)),
        const((B_KV_RANK, QB_W)), const((B_KV_RANK, B_HEADS * B_V)), const((1, LANES)),
        tab_spec, tab_spec, tab_spec, tab_spec,
    ]
    args = [x2d, mod, lw["g_pre"], lw["w_in"], lw["g_q"], lw["g_k"], lw["g_qc"], lw["w_qb"], lw["g_kv"],
            lw["w_k"], lw["w_v"], lw["dt_bias"], *tabs]
    return in_specs, args, out_shape, out_specs


def _pre_ctx(l, x2d, mod, lw, tabs):
    in_specs, args, out_shape, out_specs = _pre_parts(l, True, x2d, mod, lw, tabs)
    return pl.pallas_call(
        functools.partial(_pre_kernel, True),
        out_shape=tuple(out_shape),
        grid=(x2d.shape[0] // TM,),
        in_specs=in_specs,
        out_specs=tuple(out_specs),
        compiler_params=_cparams(("parallel",)),
        name="pre_ctx",
    )(*args)


def _merge_halves(lo_part, hi_part):
    return jnp.where(_lane_iota(lo_part.shape) < HEAD_DIM, lo_part, hi_part)


def _gqa_query_slot(qat_ref, hd):
    q = qat_ref[hd * HEAD_DIM:(hd + 1) * HEAD_DIM, :]
    zero = jnp.zeros_like(q)
    return jnp.concatenate([q, zero] if hd // A_REP == 0 else [zero, q], axis=0)


ONES_ROWS = 16
ACC_ROWS = LANES + ONES_ROWS


def _with_ones(v_t):
    return jnp.concatenate([v_t, jnp.ones((ONES_ROWS, v_t.shape[1]), BF16)], axis=0)


def _first_tile(k, q_t, v_t):
    return _first_scores(jnp.dot(k, q_t, preferred_element_type=F32), v_t)


def _first_scores(s, v_t):
    m = jnp.max(s, axis=0, keepdims=True)
    p = jnp.exp2(s - m)
    return m, jnp.dot(_with_ones(v_t), p.astype(BF16), preferred_element_type=F32)


def _next_tile(s, m, acc, v_t):
    m_new = jnp.maximum(m, jnp.max(s, axis=0, keepdims=True))
    p = jnp.exp2(s - m_new)
    acc = jnp.exp2(m - m_new) * acc + jnp.dot(_with_ones(v_t), p.astype(BF16), preferred_element_type=F32)
    return m_new, acc


def _normalised(acc):
    return (acc[0:LANES, :] / acc[LANES:LANES + 1, :]).T


def _write_heads(outs_a, outs_b, oa_ref, ob_ref):
    for r in range(A_REP):
        oa_ref[:, r * LANES:(r + 1) * LANES] = _merge_halves(outs_a[r], outs_a[A_REP + r]).astype(BF16)
    for pr in range(B_HEADS // 2):
        ob_ref[:, pr * LANES:(pr + 1) * LANES] = _merge_halves(outs_b[2 * pr], outs_b[2 * pr + 1]).astype(BF16)


def _attn_ctx_kernel(qat_ref, ka_ref, vat_ref, qbt_ref, kb_ref, vbt_ref, oa_ref, ob_ref):
    def split_heads(acc, n):
        o = acc[0:LANES, :] / acc[LANES:LANES + 1, :]
        return [o[:, i * SEQ:(i + 1) * SEQ].T for i in range(n)]

    q_all = jnp.concatenate([_gqa_query_slot(qat_ref, hd) for hd in range(A_HEADS)], axis=1)
    outs_a = split_heads(_first_tile(ka_ref[...], q_all, vat_ref[...])[1], A_HEADS)
    outs_b = []
    for pr in range(B_HEADS // 2):
        s = jnp.concatenate([jnp.dot(kb_ref[:, hd * LANES:(hd + 1) * LANES], qbt_ref[hd * LANES:(hd + 1) * LANES, :],
                                     preferred_element_type=F32) for hd in (2 * pr, 2 * pr + 1)], axis=1)
        outs_b += split_heads(_first_scores(s, vbt_ref[pr * LANES:(pr + 1) * LANES, :])[1], 2)
    _write_heads(outs_a, outs_b, oa_ref, ob_ref)


def _attn_lat_kernel(is_gqa, *refs):
    if is_gqa:
        q0_ref, q1_ref, k_ref, vt_ref, kc_ref, vct_ref, o_ref, s_ref, acc_ref = refs
        zero = jnp.zeros((HEAD_DIM, TQ), BF16)
        q_slots = (jnp.concatenate([q0_ref[...], zero], axis=0), jnp.concatenate([zero, q1_ref[...]], axis=0))
        key_cols = (0, 0)
    else:
        q_ref, k_ref, vt_ref, kc_ref, vct_ref, o_ref, s_ref, acc_ref = refs
        q_slots = (q_ref[0:LANES, :], q_ref[LANES:2 * LANES, :])
        key_cols = (0, LANES)
    cache = ("cache", 0, PAST_LEN)
    rest = SEG_MAX - PAST_LEN
    tiles = (DEC_SEQ - rest - SEG_MAX) // TK
    plans = ([[cache, ("lat", 0, rest)], [("lat", rest, SEG_MAX)]]
             + [[("lat", rest + SEG_MAX + j * TK, TK)] for j in range(tiles)],
             [[("lat", j * TK, TK)] for j in range(tiles)]
             + [[("lat", tiles * TK, SEG_MAX)], [("lat", DEC_SEQ - rest, rest), cache]])

    def seg_len(seg):
        return sum(n for _, _, n in seg)

    def keys(seg, c0):
        parts = [(kc_ref if src == "cache" else k_ref)[k0:k0 + n, c0:c0 + LANES] for src, k0, n in seg]
        return parts[0] if len(parts) == 1 else jnp.concatenate(parts, axis=0)

    def vals(seg):
        parts = [(vct_ref if src == "cache" else vt_ref)[:, k0:k0 + n] for src, k0, n in seg]
        return parts[0] if len(parts) == 1 else jnp.concatenate(parts, axis=1)

    row_max = [None, None]
    for hh in range(2):
        seg = plans[hh][0]
        s_ref[hh, 0, 0:seg_len(seg), :] = jnp.dot(keys(seg, key_cols[hh]), q_slots[hh], preferred_element_type=F32)
    for j in range(max(len(plan) for plan in plans)):
        for hh in range(2):
            plan = plans[hh]
            if j >= len(plan):
                continue
            if j + 1 < len(plan):
                nxt = plan[j + 1]
                s_ref[hh, (j + 1) % 2, 0:seg_len(nxt), :] = jnp.dot(keys(nxt, key_cols[hh]), q_slots[hh],
                                                              preferred_element_type=F32)
            seg = plan[j]
            s = s_ref[hh, j % 2, 0:seg_len(seg), :]
            if j == 0:
                row_max[hh], acc = _first_scores(s, vals(seg))
            else:
                row_max[hh], acc = _next_tile(s, row_max[hh], acc_ref[hh], vals(seg))
            acc_ref[hh] = acc
    o_ref[...] = _merge_halves(_normalised(acc_ref[0]), _normalised(acc_ref[1])).astype(BF16)


def _attn_lat(l, is_gqa, q_t, k, v_t, kc, vc_t):
    t = k.shape[0]
    nq = DEC_SEQ // TQ
    n_pairs = A_REP if is_gqa else B_HEADS // 2
    kw = A_KV if is_gqa else 2 * LANES
    if is_gqa:
        q_specs = [pl.BlockSpec((HEAD_DIM, TQ), lambda b, r, i: (r, b * nq + i)),
                   pl.BlockSpec((HEAD_DIM, TQ), lambda b, r, i: (A_REP + r, b * nq + i))]
        q_args = [q_t, q_t]
        pair_col = lambda r: 0
    else:
        q_specs = [pl.BlockSpec((2 * LANES, TQ), lambda b, r, i: (r, b * nq + i))]
        q_args = [q_t]
        pair_col = lambda r: r
    return pl.pallas_call(
        functools.partial(_attn_lat_kernel, is_gqa),
        out_shape=jax.ShapeDtypeStruct((t, n_pairs * LANES), BF16),
        grid=(DEC_BATCH, n_pairs, nq),
        in_specs=q_specs + [
            pl.BlockSpec((DEC_SEQ, kw), lambda b, r, i: (b, pair_col(r))),
            pl.BlockSpec((LANES, DEC_SEQ), lambda b, r, i: (pair_col(r), b)),
            pl.BlockSpec((None, None, PAST_LEN, kw), lambda b, r, i: (l, b, 0, pair_col(r))),
            pl.BlockSpec((None, None, LANES, PAST_LEN), lambda b, r, i: (l, b, pair_col(r), 0)),
        ],
        out_specs=pl.BlockSpec((TQ, LANES), lambda b, r, i: (b * nq + i, r)),
        scratch_shapes=[pltpu.VMEM((2, 2, SEG_MAX, TQ), F32), pltpu.VMEM((2, ACC_ROWS, TQ), F32)],
        compiler_params=_cparams(("parallel", "parallel", "parallel")),
        name="attn_lat_gqa" if is_gqa else "attn_lat_mla",
    )(*q_args, k, v_t, kc, vc_t)


HALO = SUBLANES
N_PAIRS = C_HEADS // 2
HEADS_PER_GROUP = C_HEADS // C_GROUPS


def _ssd_kernel(n_chunks, has_h0, xbc_ref, misc_ref, cw_ref, cb_ref, alog_ref, dvec_ref, *rest):
    if has_h0:
        h0_ref, y_ref, hout_ref, stf_ref, stb_ref, newb_ref, cm_ref, ecx_ref = rest
        stf_ref[...] = h0_ref[0]
        stb_ref[...] = h0_ref[1]
    else:
        y_ref, hout_ref, stf_ref, stb_ref, newb_ref, cm_ref, ecx_ref = rest
        stf_ref[...] = jnp.zeros_like(stf_ref)
        stb_ref[...] = jnp.zeros_like(stb_ref)
    seq_len = n_chunks * C_CHUNK
    row = lax.broadcasted_iota(jnp.int32, (C_CHUNK, LANES), 0)
    lane = _lane_iota((C_CHUNK, LANES))
    lo = lane < C_HEAD_DIM
    lane_group = (lane >= C_STATE).astype(jnp.int32)
    row_group = (row >= C_STATE).astype(jnp.int32)
    causal = (row >= lane, row <= lane)
    neg_a = -jnp.exp(alog_ref[...]) * LOG2E
    dt_lanes = (lane >= DT_LANE) & (lane < DT_LANE + 2 * C_HEADS)

    def lane_of(direction, hd):
        return DT_LANE + C_HEADS * direction + hd

    def pair_cols(v, direction, heads):
        j0, j1 = lane_of(direction, heads[0]), lane_of(direction, heads[1])
        return jnp.where(lo[0:v.shape[0], :], v[:, j0:j0 + 1], v[:, j1:j1 + 1])

    def pass1(c, carry):
        start = pl.multiple_of(c * C_CHUNK, C_CHUNK)
        prev0 = pl.multiple_of(jnp.maximum(start - HALO, 0), HALO)
        next0 = pl.multiple_of(jnp.minimum(start + C_CHUNK, seq_len - HALO), HALO)
        keep_prev = jnp.where(c > 0, 1.0, 0.0).astype(F32)
        keep_next = jnp.where(c < n_chunks - 1, 1.0, 0.0).astype(F32)
        rows_c = pl.ds(start, C_CHUNK)

        ue = jnp.concatenate([xbc_ref[pl.ds(prev0, HALO), :] * keep_prev, xbc_ref[rows_c, :],
                              xbc_ref[pl.ds(next0, HALO), :] * keep_next], axis=0)
        acc = jnp.broadcast_to(cb_ref[...], (C_CHUNK, C_CONV_CH))
        for k in range(C_CONV):
            sh = (C_CONV // 2 - k) % ue.shape[0]
            r = ue if sh == 0 else pltpu.roll(ue, sh, axis=0)
            acc = acc + r[HALO:HALO + C_CHUNK, :] * cw_ref[k:k + 1, :]
        xc = acc * jax.nn.sigmoid(acc)
        xs = xc[:, 0:C_INNER]
        bm = xc[:, C_INNER:C_INNER + LANES]
        cm = xc[:, C_INNER + LANES:C_INNER + 2 * LANES]
        bmt = bm.T.astype(BF16)
        cm16 = cm.astype(BF16)
        g_mats = [jnp.dot(jnp.where(lane_group == g, cm, 0.0).astype(BF16), bmt, preferred_element_type=F32)
                  for g in range(C_GROUPS)]

        dtv = jnp.where(dt_lanes, misc_ref[rows_c, :], 0.0)
        dta = dtv * neg_a
        cum = dta
        step = 1
        while step < C_CHUNK:
            cum = cum + jnp.where(row >= step, pltpu.roll(cum, step, axis=0), 0.0)
            step *= 2
        tot = cum[C_CHUNK - 1:C_CHUNK, :]
        cxs = (cum, tot - cum + dta)
        cxts = (cxs[0].T, cxs[1].T)
        dtt = dtv.T
        wsts = tuple(jnp.exp2(tot - cx) * dtv for cx in cxs)
        ecx_f = jnp.exp2(cxs[0])
        cdec = jnp.exp2(tot)
        cm_ref[c] = cm16
        ecx_ref[c, 0:C_CHUNK, :] = jnp.exp2(cxs[1])
        ecx_ref[c, C_CHUNK:C_CHUNK + SUBLANES, :] = jnp.broadcast_to(cdec, (SUBLANES, LANES))

        for pr in range(N_PAIRS):
            heads = (2 * pr, 2 * pr + 1)
            groups = tuple(hd // HEADS_PER_GROUP for hd in heads)
            xs_pair = xs[:, pr * LANES:(pr + 1) * LANES]
            xs16 = xs_pair.astype(BF16)
            own = row_group == jnp.where(lo, groups[0], groups[1])
            y = xs_pair * dvec_ref[:, pr * LANES:(pr + 1) * LANES]
            for direction in range(2):
                yd = []
                for hd in heads:
                    j = lane_of(direction, hd)
                    seg = jnp.where(causal[direction], cxs[direction][:, j:j + 1] - cxts[direction][j:j + 1, :],
                                    -jnp.inf)
                    sc = g_mats[hd // HEADS_PER_GROUP] * jnp.exp2(seg) * dtt[j:j + 1, :]
                    yd.append(jnp.dot(sc.astype(BF16), xs16, preferred_element_type=F32))
                y = y + _merge_halves(yd[0], yd[1])
                xw = xs_pair * pair_cols(wsts[direction], direction, heads)
                new = jnp.where(own, jnp.dot(bmt, xw.astype(BF16), preferred_element_type=F32), 0.0)
                if direction == 0:
                    st = stf_ref[pr]
                    y_off = jnp.dot(cm16, st.astype(BF16), preferred_element_type=F32)
                    y = y + y_off * pair_cols(ecx_f, 0, heads)
                    stf_ref[pr] = st * pair_cols(cdec, 0, heads) + new
                else:
                    newb_ref[c, pr] = new
            y_ref[rows_c, pr * LANES:(pr + 1) * LANES] = y
        return carry

    def pass2(i, carry):
        c = n_chunks - 1 - i
        rows_c = pl.ds(pl.multiple_of(c * C_CHUNK, C_CHUNK), C_CHUNK)
        cm16 = cm_ref[c]
        ecx_b = ecx_ref[c, 0:C_CHUNK, :]
        cdec = ecx_ref[c, C_CHUNK:C_CHUNK + 1, :]
        for pr in range(N_PAIRS):
            heads = (2 * pr, 2 * pr + 1)
            st = stb_ref[pr]
            y_off = jnp.dot(cm16, st.astype(BF16), preferred_element_type=F32)
            cols = pl.ds(pr * LANES, LANES)
            y_ref[rows_c, cols] = y_ref[rows_c, cols] + y_off * pair_cols(ecx_b, 1, heads)
            stb_ref[pr] = st * pair_cols(cdec, 1, heads) + newb_ref[c, pr]
        return carry

    unroll = True if n_chunks <= 2 else 4
    lax.fori_loop(0, n_chunks, pass1, 0, unroll=unroll)
    lax.fori_loop(0, n_chunks, pass2, 0, unroll=unroll)

    for direction, st_ref in enumerate((stf_ref, stb_ref)):
        for pr in range(N_PAIRS):
            st_t = st_ref[pr].T
            for hh, hd in enumerate((2 * pr, 2 * pr + 1)):
                g = hd // HEADS_PER_GROUP
                hout_ref[direction, hd] = st_t[hh * C_HEAD_DIM:(hh + 1) * C_HEAD_DIM,
                                               g * C_STATE:(g + 1) * C_STATE]


def _ssd_scratch(nc):
    return [pltpu.VMEM((N_PAIRS, LANES, LANES), F32), pltpu.VMEM((N_PAIRS, LANES, LANES), F32),
            pltpu.VMEM((nc, N_PAIRS, LANES, LANES), F32), pltpu.VMEM((nc, C_CHUNK, LANES), BF16),
            pltpu.VMEM((nc, C_CHUNK + SUBLANES, LANES), F32)]


def _ssd_lat(l, xbc, misc, lw, h0):
    t = xbc.shape[0]
    nb = t // DEC_SEQ
    nc = DEC_SEQ // C_CHUNK
    const = lambda shape: pl.BlockSpec((None,) + shape, lambda b: (l,) + (0,) * len(shape))
    return pl.pallas_call(
        functools.partial(_ssd_kernel, nc, True),
        out_shape=(jax.ShapeDtypeStruct((t, C_INNER), F32),
                   jax.ShapeDtypeStruct((nb, 2, C_HEADS, C_HEAD_DIM, C_STATE), F32)),
        grid=(nb,),
        in_specs=[
            pl.BlockSpec((DEC_SEQ, C_CONV_CH), lambda b: (b, 0)),
            pl.BlockSpec((DEC_SEQ, LANES), lambda b: (b, 0)),
            const((C_CONV, C_CONV_CH)), const((1, C_CONV_CH)), const((1, LANES)), const((1, C_INNER)),
            pl.BlockSpec((None, None, 2, N_PAIRS, LANES, LANES), lambda b: (b, l, 0, 0, 0, 0)),
        ],
        out_specs=(pl.BlockSpec((DEC_SEQ, C_INNER), lambda b: (b, 0)),
                   pl.BlockSpec((None, 2, C_HEADS, C_HEAD_DIM, C_STATE), lambda b: (b, 0, 0, 0, 0))),
        scratch_shapes=_ssd_scratch(nc),
        compiler_params=_cparams(("parallel",)),
        name="ssd_lat",
    )(xbc, misc, lw["conv_w"], lw["conv_b"], lw["a_log"], lw["d_vec"], h0)


def _ctx_mix_kernel(n_chunks, qat_ref, ka_ref, vat_ref, qbt_ref, kb_ref, vbt_ref, xbc_ref, misc_ref, cw_ref, cb_ref,
                    alog_ref, dvec_ref, oa_ref, ob_ref, y_ref, hout_ref, *scratch):
    _attn_ctx_kernel(qat_ref, ka_ref, vat_ref, qbt_ref, kb_ref, vbt_ref, oa_ref, ob_ref)
    _ssd_kernel(n_chunks, False, xbc_ref, misc_ref, cw_ref, cb_ref, alog_ref, dvec_ref, y_ref, hout_ref, *scratch)


def _ctx_mix_parts(l, qat, ka, vat, qbt, kb, vbt, xbc, misc, lw):
    t = ka.shape[0]
    tok = lambda w: pl.BlockSpec((SEQ, w), lambda b: (b, 0))
    tok_t = lambda w: pl.BlockSpec((w, SEQ), lambda b: (0, b))
    const = lambda shape: pl.BlockSpec((None,) + shape, lambda b: (l,) + (0,) * len(shape))
    in_specs = [tok_t(A_Q), tok(A_KV), tok_t(A_KV), tok_t(QB_W), tok(QB_W), tok_t(B_HEADS * B_V),
                tok(C_CONV_CH), tok(LANES),
                const((C_CONV, C_CONV_CH)), const((1, C_CONV_CH)), const((1, LANES)), const((1, C_INNER))]
    args = [qat, ka, vat, qbt, kb, vbt, xbc, misc, lw["conv_w"], lw["conv_b"], lw["a_log"], lw["d_vec"]]
    out_shape = [jax.ShapeDtypeStruct((t, A_Q), BF16), jax.ShapeDtypeStruct((t, B_HEADS * B_V), BF16),
                 jax.ShapeDtypeStruct((t, C_INNER), F32),
                 jax.ShapeDtypeStruct((t // SEQ, 2, C_HEADS, C_HEAD_DIM, C_STATE), F32)]
    out_specs = [tok(A_Q), tok(B_HEADS * B_V), tok(C_INNER),
                 pl.BlockSpec((None, 2, C_HEADS, C_HEAD_DIM, C_STATE), lambda b: (b, 0, 0, 0, 0))]
    return in_specs, args, out_shape, out_specs


def _pre_lat_mix_ctx_kernel(n_chunks, n_pre_in, n_mix_in, n_pre_out, n_mix_out, *refs):
    pre_in, refs = refs[:n_pre_in], refs[n_pre_in:]
    mix_in, refs = refs[:n_mix_in], refs[n_mix_in:]
    pre_out, refs = refs[:n_pre_out], refs[n_pre_out:]
    mix_out, scratch = refs[:n_mix_out], refs[n_mix_out:]
    _pre_kernel(False, *pre_in, *pre_out)
    _ctx_mix_kernel(n_chunks, *mix_in, *mix_out, *scratch)


def _pre_lat_mix_ctx(l, xs2d, mod, lw, tabs, ctx_mixer_inputs):
    p_in, p_args, p_shape, p_specs = _pre_parts(l, False, xs2d, mod, lw, tabs)
    m_in, m_args, m_shape, m_specs = _ctx_mix_parts(l, *ctx_mixer_inputs, lw)
    steps = xs2d.shape[0] // TM
    assert steps == ctx_mixer_inputs[1].shape[0] // SEQ
    nc = SEQ // C_CHUNK
    outs = pl.pallas_call(
        functools.partial(_pre_lat_mix_ctx_kernel, nc, len(p_in), len(m_in), len(p_shape), len(m_shape)),
        out_shape=tuple(p_shape + m_shape),
        grid=(steps,),
        in_specs=p_in + m_in,
        out_specs=tuple(p_specs + m_specs),
        scratch_shapes=_ssd_scratch(nc),
        compiler_params=_cparams(("parallel",)),
        name="pre_lat_mix_ctx",
    )(*p_args, *m_args)
    return outs[:len(p_shape)], outs[len(p_shape):]


def _post_kernel(x_ref, mod_ref, oa_ref, ob_ref, y_ref, z_ref, gssm_ref, wout_ref, gmix_ref, gffn_ref,
                 gffo_ref, w1_ref, w2_ref, o_ref):
    x = x_ref[...]
    gate1 = mod_ref[:, 2 * D_MODEL:3 * D_MODEL]
    shift2 = mod_ref[:, 3 * D_MODEL:4 * D_MODEL]
    scale2 = mod_ref[:, 4 * D_MODEL:5 * D_MODEL]
    gate2 = mod_ref[:, 5 * D_MODEL:6 * D_MODEL]
    z = z_ref[...]
    oc = _rms(y_ref[...] * (z * jax.nn.sigmoid(z)), gssm_ref[...])
    mix = jnp.concatenate([oa_ref[...], ob_ref[...], oc.astype(BF16)], axis=1)
    out = jnp.dot(mix, wout_ref[...], preferred_element_type=F32)
    x = x + gate1 * _rms(out, gmix_ref[...])
    h = _rms(x, gffn_ref[...]) * (1.0 + scale2) + shift2
    u = jnp.maximum(_bdot(h, w1_ref[...]), 0.0)
    f = _bdot(u * u, w2_ref[...])
    o_ref[...] = x + gate2 * _rms(f, gffo_ref[...])


def _post(l, is_ctx, x2d, mod, oa, ob, y, z, lw):
    t = x2d.shape[0]
    per_seq = DEC_SEQ // TM
    mod_map = (lambda i: (l, 0, 0, 0)) if is_ctx else (lambda i: (l, 1 + i // per_seq, 0, 0))
    tok = lambda w: pl.BlockSpec((TM, w), lambda i: (i, 0))
    const = lambda shape: pl.BlockSpec((None,) + shape, lambda i: (l,) + (0,) * len(shape),
                                       pipeline_mode=pl.Buffered(1))
    return pl.pallas_call(
        _post_kernel,
        out_shape=jax.ShapeDtypeStruct((t, D_MODEL), F32),
        grid=(t // TM,),
        in_specs=[
            tok(D_MODEL),
            pl.BlockSpec((None, None, 1, 6 * D_MODEL), mod_map),
            tok(A_Q), tok(B_HEADS * B_V), tok(C_INNER), tok(C_INNER),
            const((1, C_INNER)), const((D_MODEL, D_MODEL)), const((1, D_MODEL)), const((1, D_MODEL)),
            const((1, D_MODEL)), const((D_MODEL, D_FF)), const((D_FF, D_MODEL)),
        ],
        out_specs=tok(D_MODEL),
        compiler_params=_cparams(("parallel",)),
        name="post_ctx" if is_ctx else "post_lat",
    )(x2d, mod, oa, ob, y, z, lw["g_ssm"], lw["w_out"], lw["g_mix"], lw["g_ffn"], lw["g_ffo"],
      lw["w1"], lw["w2"])


def _rope_tables():
    pos = np.arange(DEC_SEQ)
    axis_pos = np.stack([(pos // GRID_W), (pos % GRID_W)], axis=0).astype(np.float32)

    def pattern(rot_dim):
        half = rot_dim // 2
        quarter = half // 2
        inv = (1.0 / (np.float32(ROPE_THETA) ** (np.arange(0, half, 2, dtype=np.float32) / np.float32(half))))
        inv = inv.astype(np.float32)
        dd = np.arange(rot_dim)
        ang = (axis_pos[dd // half].T * inv[dd % quarter][None, :]).astype(np.float32)
        sign = np.where((dd % half) < quarter, -1.0, 1.0).astype(np.float32)
        return np.cos(ang).astype(np.float32), (np.sin(ang) * sign).astype(np.float32)

    ca, sa = pattern(HEAD_DIM)
    ca = np.tile(ca, (1, LANES // HEAD_DIM))
    sa = np.tile(sa, (1, LANES // HEAD_DIM))
    cb32, sb32 = pattern(B_ROPE)
    tail = LANES - KPE_LANE - B_ROPE
    cb = np.concatenate([np.ones((DEC_SEQ, KPE_LANE), np.float32), cb32, np.ones((DEC_SEQ, tail), np.float32)], 1)
    sb = np.concatenate([np.zeros((DEC_SEQ, KPE_LANE), np.float32), sb32, np.zeros((DEC_SEQ, tail), np.float32)], 1)
    ident_c = np.ones((TM, LANES), np.float32)
    ident_s = np.zeros((TM, LANES), np.float32)
    return tuple(jnp.asarray(np.concatenate([tb, idt], axis=0))
                 for tb, idt in ((ca, ident_c), (sa, ident_s), (cb, ident_c), (sb, ident_s)))


REPACK_ROWS = 256
W_IN_COLS = A_Q + 2 * A_KV + B_Q_RANK + B_KV_RANK + B_ROPE + C_INNER + C_CONV_CH + 2 * C_HEADS


def _repack_kernel(w_ref, o_ref):
    o_kpe = OFF_KVC + B_KV_RANK
    o_z = o_kpe + B_ROPE
    o_dt = o_z + C_INNER + C_CONV_CH
    o_ref[:, 0:o_kpe] = w_ref[:, 0:o_kpe].astype(BF16)
    o_ref[:, OFF_Z:OFF_MISC] = w_ref[:, o_z:o_dt].astype(BF16)
    lane = _lane_iota((REPACK_ROWS, LANES))
    kpe = pltpu.roll(w_ref[:, o_kpe:o_kpe + LANES], KPE_LANE, axis=1)
    dt = pltpu.roll(w_ref[:, W_IN_COLS - LANES:W_IN_COLS], DT_LANE + 2 * C_HEADS, axis=1)
    misc = jnp.where((lane >= KPE_LANE) & (lane < DT_LANE), kpe,
                     jnp.where((lane >= DT_LANE) & (lane < DT_LANE + 2 * C_HEADS), dt, 0.0))
    o_ref[:, OFF_MISC:IN_COLS_P] = misc.astype(BF16)


def _repack_w_in(w_in):
    return pl.pallas_call(
        _repack_kernel,
        out_shape=jax.ShapeDtypeStruct((DEPTH, D_MODEL, IN_COLS_P), BF16),
        grid=(DEPTH, D_MODEL // REPACK_ROWS),
        in_specs=[pl.BlockSpec((None, REPACK_ROWS, W_IN_COLS), lambda l, i: (l, i, 0))],
        out_specs=pl.BlockSpec((None, REPACK_ROWS, IN_COLS_P), lambda l, i: (l, i, 0)),
        compiler_params=_cparams(("parallel", "parallel")),
        name="repack_w_in",
    )(w_in)


def _prep_weights(p):
    w_in_p = _repack_w_in(p["w_in"])

    w_qb = p["mla_w_qb"].reshape(DEPTH, B_Q_RANK, B_HEADS, B_NOPE + B_ROPE)
    w_qb_p = jnp.pad(w_qb, ((0, 0),) * 3 + ((0, LANES - B_NOPE - B_ROPE),)).reshape(DEPTH, B_Q_RANK, QB_W)
    w_kvb = p["mla_w_kvb"].reshape(DEPTH, B_KV_RANK, B_HEADS, B_NOPE + B_V)
    w_k = jnp.pad(w_kvb[..., :B_NOPE], ((0, 0),) * 3 + ((0, LANES - B_NOPE),)).reshape(DEPTH, B_KV_RANK, QB_W)
    w_v = w_kvb[..., B_NOPE:].reshape(DEPTH, B_KV_RANK, B_HEADS * B_V)

    w_out = p["w_out"]
    order = [r + g * A_REP for r in range(A_REP) for g in range(A_KV_HEADS)]
    w_out_p = jnp.concatenate([w_out[:, hd * HEAD_DIM:(hd + 1) * HEAD_DIM] for hd in order] + [w_out[:, A_Q:]],
                              axis=1).astype(BF16)

    row = lambda v: v.reshape(DEPTH, 1, -1)
    pair_gain = lambda g: row(jnp.tile(g, (1, LANES // HEAD_DIM)))
    lane_vec = lambda v: jnp.pad(v.reshape(DEPTH, 1, -1), ((0, 0), (0, 0), (DT_LANE, LANES - DT_LANE - 2 * C_HEADS)))
    return {
        "g_pre": row(p["norm_mix_pre"]),
        "w_in": w_in_p,
        "g_q": pair_gain(p["attn_q_norm"]),
        "g_k": pair_gain(p["attn_k_norm"]),
        "g_qc": row(p["mla_q_norm"]),
        "w_qb": w_qb_p.astype(BF16),
        "g_kv": row(p["mla_kv_norm"]),
        "w_k": w_k.astype(BF16),
        "w_v": w_v.astype(BF16),
        "conv_w": jnp.swapaxes(p["ssm_conv_w"], 1, 2),
        "conv_b": row(p["ssm_conv_b"]),
        "dt_bias": lane_vec(p["ssm_dt_bias"]),
        "a_log": lane_vec(p["ssm_a_log"]),
        "d_vec": row(jnp.repeat(p["ssm_d"], C_HEAD_DIM, axis=1)),
        "g_ssm": row(p["ssm_norm"]),
        "w_out": w_out_p,
        "g_mix": row(p["norm_mix_post"]),
        "g_ffn": row(p["norm_ffn_pre"]),
        "g_ffo": row(p["norm_ffn_post"]),
        "w1": p["w_ffn1"].astype(BF16),
        "w2": p["w_ffn2"].astype(BF16),
    }


def _state_to_pairs(h):
    ht = jnp.swapaxes(h, -1, -2)
    zero = jnp.zeros_like(ht[..., 0, :, :])
    pairs = []
    for pr in range(N_PAIRS):
        heads = (2 * pr, 2 * pr + 1)
        row_blocks = [jnp.concatenate([ht[..., hd, :, :] if hd // HEADS_PER_GROUP == g else zero for hd in heads],
                                      axis=-1) for g in range(C_GROUPS)]
        pairs.append(jnp.concatenate(row_blocks, axis=-2))
    return jnp.stack(pairs, axis=-3)


def kernel(x_prompt, x_sample, cache_attn_k, cache_attn_v, cache_mla_ckv, cache_mla_kpe, state_ssm, c, c_ctx, norm_mix_pre, norm_mix_post, norm_ffn_pre, norm_ffn_post, w_mod, b_mod, w_in, attn_q_norm, attn_k_norm, mla_q_norm, mla_w_qb, mla_kv_norm, mla_w_kvb, ssm_conv_w, ssm_conv_b, ssm_dt_bias, ssm_a_log, ssm_d, ssm_norm, w_out, w_ffn1, w_ffn2):
    p = dict(norm_mix_pre=norm_mix_pre, norm_mix_post=norm_mix_post, norm_ffn_pre=norm_ffn_pre,
             norm_ffn_post=norm_ffn_post, w_in=w_in, attn_q_norm=attn_q_norm, attn_k_norm=attn_k_norm,
             mla_q_norm=mla_q_norm, mla_w_qb=mla_w_qb, mla_kv_norm=mla_kv_norm, mla_w_kvb=mla_w_kvb,
             ssm_conv_w=ssm_conv_w, ssm_conv_b=ssm_conv_b, ssm_dt_bias=ssm_dt_bias, ssm_a_log=ssm_a_log,
             ssm_d=ssm_d, ssm_norm=ssm_norm, w_out=w_out, w_ffn1=w_ffn1, w_ffn2=w_ffn2)
    lw = _prep_weights(p)
    tabs = _rope_tables()

    cvec = jnp.concatenate([c_ctx[None, :], c, jnp.zeros((MOD_ROWS - 1 - DEC_BATCH, D_MODEL), F32)], axis=0)
    mod = _modulation(cvec, w_mod, b_mod).reshape(DEPTH, MOD_ROWS, 1, 6 * D_MODEL)

    kpe_p = jnp.pad(cache_mla_kpe, ((0, 0),) * 3 + ((KPE_LANE, LANES - KPE_LANE - B_ROPE),))
    kc, vct, kbc, vbct = _cache_prep(
        cache_attn_k.reshape(DEC_BATCH, DEPTH, PAST_LEN, A_KV), cache_attn_v.reshape(DEC_BATCH, DEPTH, PAST_LEN, A_KV),
        cache_mla_ckv, kpe_p, lw["w_k"], lw["w_v"])
    h0_lat = _state_to_pairs(state_ssm)

    xp = x_prompt.reshape(BATCH * SEQ, D_MODEL)
    xs = x_sample.reshape(DEC_BATCH * DEC_SEQ, D_MODEL)
    new_k, new_v, new_ckv, new_kpe, new_ssm = [], [], [], [], []
    for l in range(DEPTH):
        qat, ka, vat, qbt, kb, vbt, z, xbc, misc, kf, vf, ckvf, kpef = _pre_ctx(l, xp, mod, lw, tabs)
        lat_pre, (oa, ob, y, hfin) = _pre_lat_mix_ctx(l, xs, mod, lw, tabs, (qat, ka, vat, qbt, kb, vbt, xbc, misc))
        xp = _post(l, True, xp, mod, oa, ob, y, z, lw)
        new_k.append(kf.reshape(BATCH, SEQ, A_KV_HEADS, HEAD_DIM))
        new_v.append(vf.reshape(BATCH, SEQ, A_KV_HEADS, HEAD_DIM))
        new_ckv.append(ckvf.reshape(BATCH, SEQ, B_KV_RANK))
        new_kpe.append(kpef.reshape(BATCH, SEQ, B_ROPE))
        new_ssm.append(hfin)
        qat, ka, vat, qbt, kb, vbt, z, xbc, misc = lat_pre
        oa = _attn_lat(l, True, qat, ka, vat, kc, vct)
        ob = _attn_lat(l, False, qbt, kb, vbt, kbc, vbct)
        y, _ = _ssd_lat(l, xbc, misc, lw, h0_lat)
        xs = _post(l, False, xs, mod, oa, ob, y, z, lw)
    return (xp.reshape(BATCH, SEQ, D_MODEL), xs.reshape(DEC_BATCH, DEC_SEQ, D_MODEL),
            jnp.stack(new_k, axis=1), jnp.stack(new_v, axis=1), jnp.stack(new_ckv, axis=1),
            jnp.stack(new_kpe, axis=1), jnp.stack(new_ssm, axis=1))
```

```python
import functools
import math

import numpy as np
import jax
import jax.numpy as jnp
from jax import lax
from jax.experimental import pallas as pl
from jax.experimental.pallas import tpu as pltpu

F32 = jnp.float32
BF16 = jnp.bfloat16

D_MODEL = 1024
BATCH = 16
SEQ = 256
DEPTH = 4
DEC_BATCH = 2
DEC_SEQ = 4096
PAST_LEN = 256
GRID_W = 64
HEAD_DIM = 64
A_HEADS = 6
A_KV_HEADS = 2
A_REP = A_HEADS // A_KV_HEADS
B_HEADS = 4
B_Q_RANK = 256
B_KV_RANK = 128
B_NOPE = 64
B_ROPE = 32
B_V = 64
C_HEADS = 6
C_HEAD_DIM = 64
C_INNER = C_HEADS * C_HEAD_DIM
C_GROUPS = 2
C_STATE = 64
C_CONV = 5
C_CHUNK = 128
C_CONV_CH = C_INNER + 2 * C_GROUPS * C_STATE
D_FF = 4 * D_MODEL
A_Q = A_HEADS * HEAD_DIM
A_KV = A_KV_HEADS * HEAD_DIM
ROPE_THETA = 10000.0
EPS = 1e-6
LOG2E = math.log2(math.e)

LANES = 128
SUBLANES = 8
VMEM_LIMIT = 56 * 1024 * 1024

OFF_QA = 0
OFF_KA = OFF_QA + A_Q
OFF_VA = OFF_KA + A_KV
OFF_QC = OFF_VA + A_KV
OFF_KVC = OFF_QC + B_Q_RANK
OFF_Z = OFF_KVC + B_KV_RANK
OFF_XBC = OFF_Z + C_INNER
OFF_MISC = OFF_XBC + C_CONV_CH
IN_COLS_P = OFF_MISC + LANES
KPE_LANE = 64
DT_LANE = KPE_LANE + B_ROPE
QB_W = B_HEADS * LANES

TM = 512
TQ = 512
TK = 1024
SEG_MAX = TK + LANES


def _cparams(sem):
    return pltpu.CompilerParams(dimension_semantics=sem, vmem_limit_bytes=VMEM_LIMIT)


def _rms(x, g):
    return x * lax.rsqrt(jnp.mean(x * x, axis=-1, keepdims=True) + EPS) * g


def _bdot(a, b):
    return jnp.dot(a.astype(BF16), b.astype(BF16), preferred_element_type=F32)


def _lane_iota(shape):
    return lax.broadcasted_iota(jnp.int32, shape, len(shape) - 1)


def _swap_lanes(x, dist):
    lane = _lane_iota(x.shape)
    fwd = pltpu.roll(x, LANES - dist, axis=1)
    bwd = pltpu.roll(x, dist, axis=1)
    return jnp.where((lane % (2 * dist)) < dist, fwd, bwd)


def _rope(x, cos, sin, dist):
    return x * cos + _swap_lanes(x, dist) * sin


def _head_pair_rms(col, gain):
    lo = _lane_iota(col.shape) < HEAD_DIM
    c2 = col * col
    s_lo = jnp.sum(jnp.where(lo, c2, 0.0), axis=-1, keepdims=True)
    s_hi = jnp.sum(jnp.where(lo, 0.0, c2), axis=-1, keepdims=True)
    return col * lax.rsqrt(jnp.where(lo, s_lo, s_hi) * (1.0 / HEAD_DIM) + EPS) * gain


MOD_ROWS = SUBLANES
MOD_TN = 1536


def _mod_kernel(c_ref, w_ref, b_ref, o_ref):
    c = c_ref[...]
    o_ref[...] = _bdot(c * jax.nn.sigmoid(c), w_ref[...]) + b_ref[...]


def _modulation(cvec, w_mod, b_mod):
    return pl.pallas_call(
        _mod_kernel,
        out_shape=jax.ShapeDtypeStruct((DEPTH, MOD_ROWS, 6 * D_MODEL), F32),
        grid=(DEPTH, 6 * D_MODEL // MOD_TN),
        in_specs=[
            pl.BlockSpec((MOD_ROWS, D_MODEL), lambda l, j: (0, 0)),
            pl.BlockSpec((None, D_MODEL, MOD_TN), lambda l, j: (l, 0, j)),
            pl.BlockSpec((None, 1, MOD_TN), lambda l, j: (l, 0, j)),
        ],
        out_specs=pl.BlockSpec((None, MOD_ROWS, MOD_TN), lambda l, j: (l, 0, j)),
        compiler_params=_cparams(("parallel", "parallel")),
        name="modulation",
    )(cvec, w_mod, b_mod.reshape(DEPTH, 1, 6 * D_MODEL))


def _cache_kernel(k_ref, v_ref, ckv_ref, kpe_ref, wk_ref, wv_ref, ko_ref, vt_ref, kb_ref, vbt_ref):
    ko_ref[...] = k_ref[...].astype(BF16)
    vt_ref[...] = v_ref[...].T.astype(BF16)
    ckv = ckv_ref[...]
    kn = _bdot(ckv, wk_ref[...])
    kpe = kpe_ref[...]
    for h in range(B_HEADS):
        kb_ref[:, h * LANES:(h + 1) * LANES] = (kn[:, h * LANES:(h + 1) * LANES] + kpe).astype(BF16)
    vbt_ref[...] = _bdot(ckv, wv_ref[...]).T.astype(BF16)


def _cache_prep(cache_k, cache_v, cache_ckv, cache_kpe_p, wk_p, wv_p):
    spec_tok = lambda w: pl.BlockSpec((None, None, PAST_LEN, w), lambda l, b: (b, l, 0, 0))
    spec_t = lambda w: pl.BlockSpec((None, None, w, PAST_LEN), lambda l, b: (l, b, 0, 0))
    spec_o = lambda w: pl.BlockSpec((None, None, PAST_LEN, w), lambda l, b: (l, b, 0, 0))
    return pl.pallas_call(
        _cache_kernel,
        out_shape=(
            jax.ShapeDtypeStruct((DEPTH, DEC_BATCH, PAST_LEN, A_KV), BF16),
            jax.ShapeDtypeStruct((DEPTH, DEC_BATCH, A_KV, PAST_LEN), BF16),
            jax.ShapeDtypeStruct((DEPTH, DEC_BATCH, PAST_LEN, QB_W), BF16),
            jax.ShapeDtypeStruct((DEPTH, DEC_BATCH, B_HEADS * B_V, PAST_LEN), BF16),
        ),
        grid=(DEPTH, DEC_BATCH),
        in_specs=[
            spec_tok(A_KV), spec_tok(A_KV), spec_tok(B_KV_RANK), spec_tok(LANES),
            pl.BlockSpec((None, B_KV_RANK, QB_W), lambda l, b: (l, 0, 0)),
            pl.BlockSpec((None, B_KV_RANK, B_HEADS * B_V), lambda l, b: (l, 0, 0)),
        ],
        out_specs=(spec_o(A_KV), spec_t(A_KV), spec_o(QB_W), spec_t(B_HEADS * B_V)),
        compiler_params=_cparams(("parallel", "parallel")),
        name="cache_prep",
    )(cache_k, cache_v, cache_ckv, cache_kpe_p, wk_p, wv_p)


PRE_SUB = 2


def _pre_kernel(is_ctx, *refs):
    for sub in range(PRE_SUB):
        _pre_rows(is_ctx, slice(sub * (TM // PRE_SUB), (sub + 1) * (TM // PRE_SUB)), *refs)


def _pre_rows(is_ctx, rows, x_ref, mod_ref, gpre_ref, win_ref, gq_ref, gk_ref, gqc_ref, wqb_ref, gkv_ref,
              wk_ref, wv_ref, dtb_ref, ca_ref, sa_ref, cb_ref, sb_ref, *outs):
    if is_ctx:
        (qat_ref, ka_ref, vat_ref, qbt_ref, kb_ref, vbt_ref, z_ref, xbc_ref, misc_ref,
         kf_ref, vf_ref, ckvf_ref, kpef_ref) = outs
    else:
        qat_ref, ka_ref, vat_ref, qbt_ref, kb_ref, vbt_ref, z_ref, xbc_ref, misc_ref = outs
    x = x_ref[rows, :]
    shift1 = mod_ref[:, 0:D_MODEL]
    scale1 = mod_ref[:, D_MODEL:2 * D_MODEL]
    h = _rms(x, gpre_ref[...]) * (1.0 + scale1) + shift1
    proj = _bdot(h, win_ref[...])

    ca, sa, cb, sb = ca_ref[rows, :], sa_ref[rows, :], cb_ref[rows, :], sb_ref[rows, :]
    lane = _lane_iota((x.shape[0], LANES))

    gq = gq_ref[...]
    for cidx in range(A_Q // LANES):
        col = proj[:, OFF_QA + cidx * LANES:OFF_QA + (cidx + 1) * LANES]
        qn = _rope(_head_pair_rms(col, gq), ca, sa, HEAD_DIM // 4) * (HEAD_DIM ** -0.5 * LOG2E)
        qat_ref[cidx * LANES:(cidx + 1) * LANES, rows] = qn.T.astype(BF16)

    kn = _head_pair_rms(proj[:, OFF_KA:OFF_KA + A_KV], gk_ref[...])
    vcol = proj[:, OFF_VA:OFF_VA + A_KV]
    if is_ctx:
        kf_ref[rows, :] = kn
        vf_ref[rows, :] = vcol
    ka_ref[rows, :] = _rope(kn, ca, sa, HEAD_DIM // 4).astype(BF16)
    vat_ref[:, rows] = vcol.T.astype(BF16)

    qc = _rms(proj[:, OFF_QC:OFF_QC + B_Q_RANK], gqc_ref[...])
    qb = _bdot(qc, wqb_ref[...])
    for hd in range(B_HEADS):
        col = qb[:, hd * LANES:(hd + 1) * LANES]
        col = _rope(col, cb, sb, B_ROPE // 4) * ((B_NOPE + B_ROPE) ** -0.5 * LOG2E)
        qbt_ref[hd * LANES:(hd + 1) * LANES, rows] = col.T.astype(BF16)

    ckv = _rms(proj[:, OFF_KVC:OFF_KVC + B_KV_RANK], gkv_ref[...])
    if is_ctx:
        ckvf_ref[rows, :] = ckv
        kpef_ref[rows, :] = proj[:, OFF_MISC + KPE_LANE:OFF_MISC + KPE_LANE + B_ROPE]
    knope = _bdot(ckv, wk_ref[...])
    misc = proj[:, OFF_MISC:OFF_MISC + LANES]
    kpe = jnp.where((lane >= KPE_LANE) & (lane < KPE_LANE + B_ROPE), _rope(misc, cb, sb, B_ROPE // 4), 0.0)
    for hd in range(B_HEADS):
        kb_ref[rows, hd * LANES:(hd + 1) * LANES] = (knope[:, hd * LANES:(hd + 1) * LANES] + kpe).astype(BF16)
    vbt_ref[:, rows] = _bdot(ckv, wv_ref[...]).T.astype(BF16)

    x_dt = misc + dtb_ref[...]
    dtv = jnp.maximum(x_dt, 0.0) + jnp.log(1.0 + jnp.exp(-jnp.abs(x_dt)))
    misc_ref[rows, :] = jnp.where(lane >= DT_LANE, dtv, misc)

    z_ref[rows, :] = proj[:, OFF_Z:OFF_Z + C_INNER]
    xbc_ref[rows, :] = proj[:, OFF_XBC:OFF_XBC + C_CONV_CH]


def _pre_parts(l, is_ctx, x2d, mod, lw, tabs):
    t = x2d.shape[0]
    per_seq = DEC_SEQ // TM
    if is_ctx:
        mod_map = lambda i: (l, 0, 0, 0)
        tab_map = lambda i: (per_seq, 0)
    else:
        mod_map = lambda i: (l, 1 + i // per_seq, 0, 0)
        tab_map = lambda i: (i % per_seq, 0)
    const = lambda shape: pl.BlockSpec((None,) + shape, lambda i: (l,) + (0,) * len(shape))
    tab_spec = pl.BlockSpec((TM, LANES), tab_map)
    tok = lambda w: pl.BlockSpec((TM, w), lambda i: (i, 0))
    tok_t = lambda w: pl.BlockSpec((w, TM), lambda i: (0, i))
    out_shape = [
        jax.ShapeDtypeStruct((A_Q, t), BF16),
        jax.ShapeDtypeStruct((t, A_KV), BF16),
        jax.ShapeDtypeStruct((A_KV, t), BF16),
        jax.ShapeDtypeStruct((QB_W, t), BF16),
        jax.ShapeDtypeStruct((t, QB_W), BF16),
        jax.ShapeDtypeStruct((B_HEADS * B_V, t), BF16),
        jax.ShapeDtypeStruct((t, C_INNER), F32),
        jax.ShapeDtypeStruct((t, C_CONV_CH), F32),
        jax.ShapeDtypeStruct((t, LANES), F32),
    ]
    out_specs = [tok_t(A_Q), tok(A_KV), tok_t(A_KV), tok_t(QB_W), tok(QB_W), tok_t(B_HEADS * B_V),
                 tok(C_INNER), tok(C_CONV_CH), tok(LANES)]
    if is_ctx:
        out_shape += ([jax.ShapeDtypeStruct((t, A_KV), F32)] * 2 + [jax.ShapeDtypeStruct((t, B_KV_RANK), F32)]
                      + [jax.ShapeDtypeStruct((t, B_ROPE), F32)])
        out_specs += [tok(A_KV), tok(A_KV), tok(B_KV_RANK), tok(B_ROPE)]
    in_specs = [
        tok(D_MODEL),
        pl.BlockSpec((None, None, 1, 6 * D_MODEL), mod_map),
        const((1, D_MODEL)),
        const((D_MODEL, IN_COLS_P)),
        const((1, LANES)), const((1, LANES)), const((1, B_Q_RANK)),
        const((B_Q_RANK, QB_W)), const((1, B_KV_RANK)),
        const((B_KV_RANK, QB_W)), const((B_KV_RANK, B_HEADS * B_V)), const((1, LANES)),
        tab_spec, tab_spec, tab_spec, tab_spec,
    ]
    args = [x2d, mod, lw["g_pre"], lw["w_in"], lw["g_q"], lw["g_k"], lw["g_qc"], lw["w_qb"], lw["g_kv"],
            lw["w_k"], lw["w_v"], lw["dt_bias"], *tabs]
    return in_specs, args, out_shape, out_specs


def _pre_ctx(l, x2d, mod, lw, tabs):
    in_specs, args, out_shape, out_specs = _pre_parts(l, True, x2d, mod, lw, tabs)
    return pl.pallas_call(
        functools.partial(_pre_kernel, True),
        out_shape=tuple(out_shape),
        grid=(x2d.shape[0] // TM,),
        in_specs=in_specs,
        out_specs=tuple(out_specs),
        compiler_params=_cparams(("parallel",)),
        name="pre_ctx",
    )(*args)


def _merge_halves(lo_part, hi_part):
    return jnp.where(_lane_iota(lo_part.shape) < HEAD_DIM, lo_part, hi_part)


def _gqa_query_slot(qat_ref, hd):
    q = qat_ref[hd * HEAD_DIM:(hd + 1) * HEAD_DIM, :]
    zero = jnp.zeros_like(q)
    return jnp.concatenate([q, zero] if hd // A_REP == 0 else [zero, q], axis=0)


ONES_ROWS = 16
ACC_ROWS = LANES + ONES_ROWS


def _with_ones(v_t):
    return jnp.concatenate([v_t, jnp.ones((ONES_ROWS, v_t.shape[1]), BF16)], axis=0)


def _first_tile(k, q_t, v_t):
    return _first_scores(jnp.dot(k, q_t, preferred_element_type=F32), v_t)


def _first_scores(s, v_t):
    m = jnp.max(s, axis=0, keepdims=True)
    p = jnp.exp2(s - m)
    return m, jnp.dot(_with_ones(v_t), p.astype(BF16), preferred_element_type=F32)


def _next_tile(s, m, acc, v_t):
    m_new = jnp.maximum(m, jnp.max(s, axis=0, keepdims=True))
    p = jnp.exp2(s - m_new)
    acc = jnp.exp2(m - m_new) * acc + jnp.dot(_with_ones(v_t), p.astype(BF16), preferred_element_type=F32)
    return m_new, acc


def _normalised(acc):
    return (acc[0:LANES, :] / acc[LANES:LANES + 1, :]).T


def _write_heads(outs_a, outs_b, oa_ref, ob_ref):
    for r in range(A_REP):
        oa_ref[:, r * LANES:(r + 1) * LANES] = _merge_halves(outs_a[r], outs_a[A_REP + r]).astype(BF16)
    for pr in range(B_HEADS // 2):
        ob_ref[:, pr * LANES:(pr + 1) * LANES] = _merge_halves(outs_b[2 * pr], outs_b[2 * pr + 1]).astype(BF16)


def _attn_ctx_kernel(qat_ref, ka_ref, vat_ref, qbt_ref, kb_ref, vbt_ref, oa_ref, ob_ref):
    def split_heads(acc, n):
        o = acc[0:LANES, :] / acc[LANES:LANES + 1, :]
        return [o[:, i * SEQ:(i + 1) * SEQ].T for i in range(n)]

    q_all = jnp.concatenate([_gqa_query_slot(qat_ref, hd) for hd in range(A_HEADS)], axis=1)
    outs_a = split_heads(_first_tile(ka_ref[...], q_all, vat_ref[...])[1], A_HEADS)
    outs_b = []
    for pr in range(B_HEADS // 2):
        s = jnp.concatenate([jnp.dot(kb_ref[:, hd * LANES:(hd + 1) * LANES], qbt_ref[hd * LANES:(hd + 1) * LANES, :],
                                     preferred_element_type=F32) for hd in (2 * pr, 2 * pr + 1)], axis=1)
        outs_b += split_heads(_first_scores(s, vbt_ref[pr * LANES:(pr + 1) * LANES, :])[1], 2)
    _write_heads(outs_a, outs_b, oa_ref, ob_ref)


def _attn_lat_kernel(is_gqa, *refs):
    if is_gqa:
        q0_ref, q1_ref, k_ref, vt_ref, kc_ref, vct_ref, o_ref, s_ref, acc_ref = refs
        zero = jnp.zeros((HEAD_DIM, TQ), BF16)
        q_slots = (jnp.concatenate([q0_ref[...], zero], axis=0), jnp.concatenate([zero, q1_ref[...]], axis=0))
        key_cols = (0, 0)
    else:
        q_ref, k_ref, vt_ref, kc_ref, vct_ref, o_ref, s_ref, acc_ref = refs
        q_slots = (q_ref[0:LANES, :], q_ref[LANES:2 * LANES, :])
        key_cols = (0, LANES)
    cache = ("cache", 0, PAST_LEN)
    rest = SEG_MAX - PAST_LEN
    tiles = (DEC_SEQ - rest - SEG_MAX) // TK
    plans = ([[cache, ("lat", 0, rest)], [("lat", rest, SEG_MAX)]]
             + [[("lat", rest + SEG_MAX + j * TK, TK)] for j in range(tiles)],
             [[("lat", j * TK, TK)] for j in range(tiles)]
             + [[("lat", tiles * TK, SEG_MAX)], [("lat", DEC_SEQ - rest, rest), cache]])

    def seg_len(seg):
        return sum(n for _, _, n in seg)

    def keys(seg, c0):
        parts = [(kc_ref if src == "cache" else k_ref)[k0:k0 + n, c0:c0 + LANES] for src, k0, n in seg]
        return parts[0] if len(parts) == 1 else jnp.concatenate(parts, axis=0)

    def vals(seg):
        parts = [(vct_ref if src == "cache" else vt_ref)[:, k0:k0 + n] for src, k0, n in seg]
        return parts[0] if len(parts) == 1 else jnp.concatenate(parts, axis=1)

    row_max = [None, None]
    for hh in range(2):
        seg = plans[hh][0]
        s_ref[hh, 0, 0:seg_len(seg), :] = jnp.dot(keys(seg, key_cols[hh]), q_slots[hh], preferred_element_type=F32)
    for j in range(max(len(plan) for plan in plans)):
        for hh in range(2):
            plan = plans[hh]
            if j >= len(plan):
                continue
            if j + 1 < len(plan):
                nxt = plan[j + 1]
                s_ref[hh, (j + 1) % 2, 0:seg_len(nxt), :] = jnp.dot(keys(nxt, key_cols[hh]), q_slots[hh],
                                                              preferred_element_type=F32)
            seg = plan[j]
            s = s_ref[hh, j % 2, 0:seg_len(seg), :]
            if j == 0:
                row_max[hh], acc = _first_scores(s, vals(seg))
            else:
                row_max[hh], acc = _next_tile(s, row_max[hh], acc_ref[hh], vals(seg))
            acc_ref[hh] = acc
    o_ref[...] = _merge_halves(_normalised(acc_ref[0]), _normalised(acc_ref[1])).astype(BF16)


def _attn_lat(l, is_gqa, q_t, k, v_t, kc, vc_t):
    t = k.shape[0]
    nq = DEC_SEQ // TQ
    n_pairs = A_REP if is_gqa else B_HEADS // 2
    kw = A_KV if is_gqa else 2 * LANES
    if is_gqa:
        q_specs = [pl.BlockSpec((HEAD_DIM, TQ), lambda b, r, i: (r, b * nq + i)),
                   pl.BlockSpec((HEAD_DIM, TQ), lambda b, r, i: (A_REP + r, b * nq + i))]
        q_args = [q_t, q_t]
        pair_col = lambda r: 0
    else:
        q_specs = [pl.BlockSpec((2 * LANES, TQ), lambda b, r, i: (r, b * nq + i))]
        q_args = [q_t]
        pair_col = lambda r: r
    return pl.pallas_call(
        functools.partial(_attn_lat_kernel, is_gqa),
        out_shape=jax.ShapeDtypeStruct((t, n_pairs * LANES), BF16),
        grid=(DEC_BATCH, n_pairs, nq),
        in_specs=q_specs + [
            pl.BlockSpec((DEC_SEQ, kw), lambda b, r, i: (b, pair_col(r))),
            pl.BlockSpec((LANES, DEC_SEQ), lambda b, r, i: (pair_col(r), b)),
            pl.BlockSpec((None, None, PAST_LEN, kw), lambda b, r, i: (l, b, 0, pair_col(r))),
            pl.BlockSpec((None, None, LANES, PAST_LEN), lambda b, r, i: (l, b, pair_col(r), 0)),
        ],
        out_specs=pl.BlockSpec((TQ, LANES), lambda b, r, i: (b * nq + i, r)),
        scratch_shapes=[pltpu.VMEM((2, 2, SEG_MAX, TQ), F32), pltpu.VMEM((2, ACC_ROWS, TQ), F32)],
        compiler_params=_cparams(("parallel", "parallel", "parallel")),
        name="attn_lat_gqa" if is_gqa else "attn_lat_mla",
    )(*q_args, k, v_t, kc, vc_t)


HALO = SUBLANES
N_PAIRS = C_HEADS // 2
HEADS_PER_GROUP = C_HEADS // C_GROUPS


def _ssd_kernel(n_chunks, has_h0, xbc_ref, misc_ref, cw_ref, cb_ref, alog_ref, dvec_ref, *rest):
    if has_h0:
        h0_ref, y_ref, hout_ref, stf_ref, stb_ref, newb_ref, cm_ref, ecx_ref = rest
        stf_ref[...] = h0_ref[0]
        stb_ref[...] = h0_ref[1]
    else:
        y_ref, hout_ref, stf_ref, stb_ref, newb_ref, cm_ref, ecx_ref = rest
        stf_ref[...] = jnp.zeros_like(stf_ref)
        stb_ref[...] = jnp.zeros_like(stb_ref)
    seq_len = n_chunks * C_CHUNK
    row = lax.broadcasted_iota(jnp.int32, (C_CHUNK, LANES), 0)
    lane = _lane_iota((C_CHUNK, LANES))
    lo = lane < C_HEAD_DIM
    lane_group = (lane >= C_STATE).astype(jnp.int32)
    row_group = (row >= C_STATE).astype(jnp.int32)
    causal = (row >= lane, row <= lane)
    neg_a = -jnp.exp(alog_ref[...]) * LOG2E
    dt_lanes = (lane >= DT_LANE) & (lane < DT_LANE + 2 * C_HEADS)

    def lane_of(direction, hd):
        return DT_LANE + C_HEADS * direction + hd

    def pair_cols(v, direction, heads):
        j0, j1 = lane_of(direction, heads[0]), lane_of(direction, heads[1])
        return jnp.where(lo[0:v.shape[0], :], v[:, j0:j0 + 1], v[:, j1:j1 + 1])

    def pass1(c, carry):
        start = pl.multiple_of(c * C_CHUNK, C_CHUNK)
        prev0 = pl.multiple_of(jnp.maximum(start - HALO, 0), HALO)
        next0 = pl.multiple_of(jnp.minimum(start + C_CHUNK, seq_len - HALO), HALO)
        keep_prev = jnp.where(c > 0, 1.0, 0.0).astype(F32)
        keep_next = jnp.where(c < n_chunks - 1, 1.0, 0.0).astype(F32)
        rows_c = pl.ds(start, C_CHUNK)

        ue = jnp.concatenate([xbc_ref[pl.ds(prev0, HALO), :] * keep_prev, xbc_ref[rows_c, :],
                              xbc_ref[pl.ds(next0, HALO), :] * keep_next], axis=0)
        acc = jnp.broadcast_to(cb_ref[...], (C_CHUNK, C_CONV_CH))
        for k in range(C_CONV):
            sh = (C_CONV // 2 - k) % ue.shape[0]
            r = ue if sh == 0 else pltpu.roll(ue, sh, axis=0)
            acc = acc + r[HALO:HALO + C_CHUNK, :] * cw_ref[k:k + 1, :]
        xc = acc * jax.nn.sigmoid(acc)
        xs = xc[:, 0:C_INNER]
        bm = xc[:, C_INNER:C_INNER + LANES]
        cm = xc[:, C_INNER + LANES:C_INNER + 2 * LANES]
        bmt = bm.T.astype(BF16)
        cm16 = cm.astype(BF16)
        g_mats = [jnp.dot(jnp.where(lane_group == g, cm, 0.0).astype(BF16), bmt, preferred_element_type=F32)
                  for g in range(C_GROUPS)]

        dtv = jnp.where(dt_lanes, misc_ref[rows_c, :], 0.0)
        dta = dtv * neg_a
        cum = dta
        step = 1
        while step < C_CHUNK:
            cum = cum + jnp.where(row >= step, pltpu.roll(cum, step, axis=0), 0.0)
            step *= 2
        tot = cum[C_CHUNK - 1:C_CHUNK, :]
        cxs = (cum, tot - cum + dta)
        cxts = (cxs[0].T, cxs[1].T)
        dtt = dtv.T
        wsts = tuple(jnp.exp2(tot - cx) * dtv for cx in cxs)
        ecx_f = jnp.exp2(cxs[0])
        cdec = jnp.exp2(tot)
        cm_ref[c] = cm16
        ecx_ref[c, 0:C_CHUNK, :] = jnp.exp2(cxs[1])
        ecx_ref[c, C_CHUNK:C_CHUNK + SUBLANES, :] = jnp.broadcast_to(cdec, (SUBLANES, LANES))

        for pr in range(N_PAIRS):
            heads = (2 * pr, 2 * pr + 1)
            groups = tuple(hd // HEADS_PER_GROUP for hd in heads)
            xs_pair = xs[:, pr * LANES:(pr + 1) * LANES]
            xs16 = xs_pair.astype(BF16)
            own = row_group == jnp.where(lo, groups[0], groups[1])
            y = xs_pair * dvec_ref[:, pr * LANES:(pr + 1) * LANES]
            for direction in range(2):
                yd = []
                for hd in heads:
                    j = lane_of(direction, hd)
                    seg = jnp.where(causal[direction], cxs[direction][:, j:j + 1] - cxts[direction][j:j + 1, :],
                                    -jnp.inf)
                    sc = g_mats[hd // HEADS_PER_GROUP] * jnp.exp2(seg) * dtt[j:j + 1, :]
                    yd.append(jnp.dot(sc.astype(BF16), xs16, preferred_element_type=F32))
                y = y + _merge_halves(yd[0], yd[1])
                xw = xs_pair * pair_cols(wsts[direction], direction, heads)
                new = jnp.where(own, jnp.dot(bmt, xw.astype(BF16), preferred_element_type=F32), 0.0)
                if direction == 0:
                    st = stf_ref[pr]
                    y_off = jnp.dot(cm16, st.astype(BF16), preferred_element_type=F32)
                    y = y + y_off * pair_cols(ecx_f, 0, heads)
                    stf_ref[pr] = st * pair_cols(cdec, 0, heads) + new
                else:
                    newb_ref[c, pr] = new
            y_ref[rows_c, pr * LANES:(pr + 1) * LANES] = y
        return carry

    def pass2(i, carry):
        c = n_chunks - 1 - i
        rows_c = pl.ds(pl.multiple_of(c * C_CHUNK, C_CHUNK), C_CHUNK)
        cm16 = cm_ref[c]
        ecx_b = ecx_ref[c, 0:C_CHUNK, :]
        cdec = ecx_ref[c, C_CHUNK:C_CHUNK + 1, :]
        for pr in range(N_PAIRS):
            heads = (2 * pr, 2 * pr + 1)
            st = stb_ref[pr]
            y_off = jnp.dot(cm16, st.astype(BF16), preferred_element_type=F32)
            cols = pl.ds(pr * LANES, LANES)
            y_ref[rows_c, cols] = y_ref[rows_c, cols] + y_off * pair_cols(ecx_b, 1, heads)
            stb_ref[pr] = st * pair_cols(cdec, 1, heads) + newb_ref[c, pr]
        return carry

    unroll = True if n_chunks <= 2 else 4
    lax.fori_loop(0, n_chunks, pass1, 0, unroll=unroll)
    lax.fori_loop(0, n_chunks, pass2, 0, unroll=unroll)

    for direction, st_ref in enumerate((stf_ref, stb_ref)):
        for pr in range(N_PAIRS):
            st_t = st_ref[pr].T
            for hh, hd in enumerate((2 * pr, 2 * pr + 1)):
                g = hd // HEADS_PER_GROUP
                hout_ref[direction, hd] = st_t[hh * C_HEAD_DIM:(hh + 1) * C_HEAD_DIM,
                                               g * C_STATE:(g + 1) * C_STATE]


def _ssd_scratch(nc):
    return [pltpu.VMEM((N_PAIRS, LANES, LANES), F32), pltpu.VMEM((N_PAIRS, LANES, LANES), F32),
            pltpu.VMEM((nc, N_PAIRS, LANES, LANES), F32), pltpu.VMEM((nc, C_CHUNK, LANES), BF16),
            pltpu.VMEM((nc, C_CHUNK + SUBLANES, LANES), F32)]


def _ssd_lat(l, xbc, misc, lw, h0):
    t = xbc.shape[0]
    nb = t // DEC_SEQ
    nc = DEC_SEQ // C_CHUNK
    const = lambda shape: pl.BlockSpec((None,) + shape, lambda b: (l,) + (0,) * len(shape))
    return pl.pallas_call(
        functools.partial(_ssd_kernel, nc, True),
        out_shape=(jax.ShapeDtypeStruct((t, C_INNER), F32),
                   jax.ShapeDtypeStruct((nb, 2, C_HEADS, C_HEAD_DIM, C_STATE), F32)),
        grid=(nb,),
        in_specs=[
            pl.BlockSpec((DEC_SEQ, C_CONV_CH), lambda b: (b, 0)),
            pl.BlockSpec((DEC_SEQ, LANES), lambda b: (b, 0)),
            const((C_CONV, C_CONV_CH)), const((1, C_CONV_CH)), const((1, LANES)), const((1, C_INNER)),
            pl.BlockSpec((None, None, 2, N_PAIRS, LANES, LANES), lambda b: (b, l, 0, 0, 0, 0)),
        ],
        out_specs=(pl.BlockSpec((DEC_SEQ, C_INNER), lambda b: (b, 0)),
                   pl.BlockSpec((None, 2, C_HEADS, C_HEAD_DIM, C_STATE), lambda b: (b, 0, 0, 0, 0))),
        scratch_shapes=_ssd_scratch(nc),
        compiler_params=_cparams(("parallel",)),
        name="ssd_lat",
    )(xbc, misc, lw["conv_w"], lw["conv_b"], lw["a_log"], lw["d_vec"], h0)


def _ctx_mix_kernel(n_chunks, qat_ref, ka_ref, vat_ref, qbt_ref, kb_ref, vbt_ref, xbc_ref, misc_ref, cw_ref, cb_ref,
                    alog_ref, dvec_ref, oa_ref, ob_ref, y_ref, hout_ref, *scratch):
    _attn_ctx_kernel(qat_ref, ka_ref, vat_ref, qbt_ref, kb_ref, vbt_ref, oa_ref, ob_ref)
    _ssd_kernel(n_chunks, False, xbc_ref, misc_ref, cw_ref, cb_ref, alog_ref, dvec_ref, y_ref, hout_ref, *scratch)


def _ctx_mix_parts(l, qat, ka, vat, qbt, kb, vbt, xbc, misc, lw):
    t = ka.shape[0]
    tok = lambda w: pl.BlockSpec((SEQ, w), lambda b: (b, 0))
    tok_t = lambda w: pl.BlockSpec((w, SEQ), lambda b: (0, b))
    const = lambda shape: pl.BlockSpec((None,) + shape, lambda b: (l,) + (0,) * len(shape))
    in_specs = [tok_t(A_Q), tok(A_KV), tok_t(A_KV), tok_t(QB_W), tok(QB_W), tok_t(B_HEADS * B_V),
                tok(C_CONV_CH), tok(LANES),
                const((C_CONV, C_CONV_CH)), const((1, C_CONV_CH)), const((1, LANES)), const((1, C_INNER))]
    args = [qat, ka, vat, qbt, kb, vbt, xbc, misc, lw["conv_w"], lw["conv_b"], lw["a_log"], lw["d_vec"]]
    out_shape = [jax.ShapeDtypeStruct((t, A_Q), BF16), jax.ShapeDtypeStruct((t, B_HEADS * B_V), BF16),
                 jax.ShapeDtypeStruct((t, C_INNER), F32),
                 jax.ShapeDtypeStruct((t // SEQ, 2, C_HEADS, C_HEAD_DIM, C_STATE), F32)]
    out_specs = [tok(A_Q), tok(B_HEADS * B_V), tok(C_INNER),
                 pl.BlockSpec((None, 2, C_HEADS, C_HEAD_DIM, C_STATE), lambda b: (b, 0, 0, 0, 0))]
    return in_specs, args, out_shape, out_specs


def _pre_lat_mix_ctx_kernel(n_chunks, n_pre_in, n_mix_in, n_pre_out, n_mix_out, *refs):
    pre_in, refs = refs[:n_pre_in], refs[n_pre_in:]
    mix_in, refs = refs[:n_mix_in], refs[n_mix_in:]
    pre_out, refs = refs[:n_pre_out], refs[n_pre_out:]
    mix_out, scratch = refs[:n_mix_out], refs[n_mix_out:]
    _pre_kernel(False, *pre_in, *pre_out)
    _ctx_mix_kernel(n_chunks, *mix_in, *mix_out, *scratch)


def _pre_lat_mix_ctx(l, xs2d, mod, lw, tabs, ctx_mixer_inputs):
    p_in, p_args, p_shape, p_specs = _pre_parts(l, False, xs2d, mod, lw, tabs)
    m_in, m_args, m_shape, m_specs = _ctx_mix_parts(l, *ctx_mixer_inputs, lw)
    steps = xs2d.shape[0] // TM
    assert steps == ctx_mixer_inputs[1].shape[0] // SEQ
    nc = SEQ // C_CHUNK
    outs = pl.pallas_call(
        functools.partial(_pre_lat_mix_ctx_kernel, nc, len(p_in), len(m_in), len(p_shape), len(m_shape)),
        out_shape=tuple(p_shape + m_shape),
        grid=(steps,),
        in_specs=p_in + m_in,
        out_specs=tuple(p_specs + m_specs),
        scratch_shapes=_ssd_scratch(nc),
        compiler_params=_cparams(("parallel",)),
        name="pre_lat_mix_ctx",
    )(*p_args, *m_args)
    return outs[:len(p_shape)], outs[len(p_shape):]


def _post_kernel(x_ref, mod_ref, oa_ref, ob_ref, y_ref, z_ref, gssm_ref, wout_ref, gmix_ref, gffn_ref,
                 gffo_ref, w1_ref, w2_ref, o_ref):
    x = x_ref[...]
    gate1 = mod_ref[:, 2 * D_MODEL:3 * D_MODEL]
    shift2 = mod_ref[:, 3 * D_MODEL:4 * D_MODEL]
    scale2 = mod_ref[:, 4 * D_MODEL:5 * D_MODEL]
    gate2 = mod_ref[:, 5 * D_MODEL:6 * D_MODEL]
    z = z_ref[...]
    oc = _rms(y_ref[...] * (z * jax.nn.sigmoid(z)), gssm_ref[...])
    mix = jnp.concatenate([oa_ref[...], ob_ref[...], oc.astype(BF16)], axis=1)
    out = jnp.dot(mix, wout_ref[...], preferred_element_type=F32)
    x = x + gate1 * _rms(out, gmix_ref[...])
    h = _rms(x, gffn_ref[...]) * (1.0 + scale2) + shift2
    u = jnp.maximum(_bdot(h, w1_ref[...]), 0.0)
    f = _bdot(u * u, w2_ref[...])
    o_ref[...] = x + gate2 * _rms(f, gffo_ref[...])


N_POST_TOK = 6


def _post_both_kernel(n_ctx, *refs):
    ctx_in, lat_in = refs[:N_POST_TOK], refs[N_POST_TOK:2 * N_POST_TOK]
    weights = refs[2 * N_POST_TOK:-2]
    out_ctx, out_lat = refs[-2:]
    i = pl.program_id(0)

    @pl.when(i < n_ctx)
    def _():
        _post_kernel(*ctx_in, *weights, out_ctx)

    @pl.when(i >= n_ctx)
    def _():
        _post_kernel(*lat_in, *weights, out_lat)


def _post_both(l, xp, xs, mod, ctx_mix, lat_mix, lw):
    n_ctx, n_lat = xp.shape[0] // TM, xs.shape[0] // TM
    per_seq = DEC_SEQ // TM
    ctx_i = lambda i: jnp.minimum(i, n_ctx - 1)
    lat_i = lambda i: jnp.maximum(i - n_ctx, 0)
    tok_c = lambda w: pl.BlockSpec((TM, w), lambda i: (ctx_i(i), 0))
    tok_l = lambda w: pl.BlockSpec((TM, w), lambda i: (lat_i(i), 0))
    const = lambda shape: pl.BlockSpec((None,) + shape, lambda i: (l,) + (0,) * len(shape),
                                       pipeline_mode=pl.Buffered(1))
    widths = (A_Q, B_HEADS * B_V, C_INNER, C_INNER)
    in_specs = ([tok_c(D_MODEL), pl.BlockSpec((None, None, 1, 6 * D_MODEL), lambda i: (l, 0, 0, 0))]
                + [tok_c(w) for w in widths]
                + [tok_l(D_MODEL),
                   pl.BlockSpec((None, None, 1, 6 * D_MODEL), lambda i: (l, 1 + lat_i(i) // per_seq, 0, 0))]
                + [tok_l(w) for w in widths]
                + [const((1, C_INNER)), const((D_MODEL, D_MODEL)), const((1, D_MODEL)), const((1, D_MODEL)),
                   const((1, D_MODEL)), const((D_MODEL, D_FF)), const((D_FF, D_MODEL))])
    return pl.pallas_call(
        functools.partial(_post_both_kernel, n_ctx),
        out_shape=(jax.ShapeDtypeStruct(xp.shape, F32), jax.ShapeDtypeStruct(xs.shape, F32)),
        grid=(n_ctx + n_lat,),
        in_specs=in_specs,
        out_specs=(tok_c(D_MODEL), tok_l(D_MODEL)),
        compiler_params=_cparams(("arbitrary",)),
        name="post",
    )(xp, mod, *ctx_mix, xs, mod, *lat_mix, lw["g_ssm"], lw["w_out"], lw["g_mix"], lw["g_ffn"], lw["g_ffo"],
      lw["w1"], lw["w2"])


def _rope_tables():
    pos = np.arange(DEC_SEQ)
    axis_pos = np.stack([(pos // GRID_W), (pos % GRID_W)], axis=0).astype(np.float32)

    def pattern(rot_dim):
        half = rot_dim // 2
        quarter = half // 2
        inv = (1.0 / (np.float32(ROPE_THETA) ** (np.arange(0, half, 2, dtype=np.float32) / np.float32(half))))
        inv = inv.astype(np.float32)
        dd = np.arange(rot_dim)
        ang = (axis_pos[dd // half].T * inv[dd % quarter][None, :]).astype(np.float32)
        sign = np.where((dd % half) < quarter, -1.0, 1.0).astype(np.float32)
        return np.cos(ang).astype(np.float32), (np.sin(ang) * sign).astype(np.float32)

    ca, sa = pattern(HEAD_DIM)
    ca = np.tile(ca, (1, LANES // HEAD_DIM))
    sa = np.tile(sa, (1, LANES // HEAD_DIM))
    cb32, sb32 = pattern(B_ROPE)
    tail = LANES - KPE_LANE - B_ROPE
    cb = np.concatenate([np.ones((DEC_SEQ, KPE_LANE), np.float32), cb32, np.ones((DEC_SEQ, tail), np.float32)], 1)
    sb = np.concatenate([np.zeros((DEC_SEQ, KPE_LANE), np.float32), sb32, np.zeros((DEC_SEQ, tail), np.float32)], 1)
    ident_c = np.ones((TM, LANES), np.float32)
    ident_s = np.zeros((TM, LANES), np.float32)
    return tuple(jnp.asarray(np.concatenate([tb, idt], axis=0))
                 for tb, idt in ((ca, ident_c), (sa, ident_s), (cb, ident_c), (sb, ident_s)))


REPACK_ROWS = 256
W_IN_COLS = A_Q + 2 * A_KV + B_Q_RANK + B_KV_RANK + B_ROPE + C_INNER + C_CONV_CH + 2 * C_HEADS


def _repack_kernel(w_ref, o_ref):
    o_kpe = OFF_KVC + B_KV_RANK
    o_z = o_kpe + B_ROPE
    o_dt = o_z + C_INNER + C_CONV_CH
    o_ref[:, 0:o_kpe] = w_ref[:, 0:o_kpe].astype(BF16)
    o_ref[:, OFF_Z:OFF_MISC] = w_ref[:, o_z:o_dt].astype(BF16)
    lane = _lane_iota((REPACK_ROWS, LANES))
    kpe = pltpu.roll(w_ref[:, o_kpe:o_kpe + LANES], KPE_LANE, axis=1)
    dt = pltpu.roll(w_ref[:, W_IN_COLS - LANES:W_IN_COLS], DT_LANE + 2 * C_HEADS, axis=1)
    misc = jnp.where((lane >= KPE_LANE) & (lane < DT_LANE), kpe,
                     jnp.where((lane >= DT_LANE) & (lane < DT_LANE + 2 * C_HEADS), dt, 0.0))
    o_ref[:, OFF_MISC:IN_COLS_P] = misc.astype(BF16)


def _repack_w_in(w_in):
    return pl.pallas_call(
        _repack_kernel,
        out_shape=jax.ShapeDtypeStruct((DEPTH, D_MODEL, IN_COLS_P), BF16),
        grid=(DEPTH, D_MODEL // REPACK_ROWS),
        in_specs=[pl.BlockSpec((None, REPACK_ROWS, W_IN_COLS), lambda l, i: (l, i, 0))],
        out_specs=pl.BlockSpec((None, REPACK_ROWS, IN_COLS_P), lambda l, i: (l, i, 0)),
        compiler_params=_cparams(("parallel", "parallel")),
        name="repack_w_in",
    )(w_in)


def _prep_weights(p):
    w_in_p = _repack_w_in(p["w_in"])

    w_qb = p["mla_w_qb"].reshape(DEPTH, B_Q_RANK, B_HEADS, B_NOPE + B_ROPE)
    w_qb_p = jnp.pad(w_qb, ((0, 0),) * 3 + ((0, LANES - B_NOPE - B_ROPE),)).reshape(DEPTH, B_Q_RANK, QB_W)
    w_kvb = p["mla_w_kvb"].reshape(DEPTH, B_KV_RANK, B_HEADS, B_NOPE + B_V)
    w_k = jnp.pad(w_kvb[..., :B_NOPE], ((0, 0),) * 3 + ((0, LANES - B_NOPE),)).reshape(DEPTH, B_KV_RANK, QB_W)
    w_v = w_kvb[..., B_NOPE:].reshape(DEPTH, B_KV_RANK, B_HEADS * B_V)

    w_out = p["w_out"]
    order = [r + g * A_REP for r in range(A_REP) for g in range(A_KV_HEADS)]
    w_out_p = jnp.concatenate([w_out[:, hd * HEAD_DIM:(hd + 1) * HEAD_DIM] for hd in order] + [w_out[:, A_Q:]],
                              axis=1).astype(BF16)

    row = lambda v: v.reshape(DEPTH, 1, -1)
    pair_gain = lambda g: row(jnp.tile(g, (1, LANES // HEAD_DIM)))
    lane_vec = lambda v: jnp.pad(v.reshape(DEPTH, 1, -1), ((0, 0), (0, 0), (DT_LANE, LANES - DT_LANE - 2 * C_HEADS)))
    return {
        "g_pre": row(p["norm_mix_pre"]),
        "w_in": w_in_p,
        "g_q": pair_gain(p["attn_q_norm"]),
        "g_k": pair_gain(p["attn_k_norm"]),
        "g_qc": row(p["mla_q_norm"]),
        "w_qb": w_qb_p.astype(BF16),
        "g_kv": row(p["mla_kv_norm"]),
        "w_k": w_k.astype(BF16),
        "w_v": w_v.astype(BF16),
        "conv_w": jnp.swapaxes(p["ssm_conv_w"], 1, 2),
        "conv_b": row(p["ssm_conv_b"]),
        "dt_bias": lane_vec(p["ssm_dt_bias"]),
        "a_log": lane_vec(p["ssm_a_log"]),
        "d_vec": row(jnp.repeat(p["ssm_d"], C_HEAD_DIM, axis=1)),
        "g_ssm": row(p["ssm_norm"]),
        "w_out": w_out_p,
        "g_mix": row(p["norm_mix_post"]),
        "g_ffn": row(p["norm_ffn_pre"]),
        "g_ffo": row(p["norm_ffn_post"]),
        "w1": p["w_ffn1"].astype(BF16),
        "w2": p["w_ffn2"].astype(BF16),
    }


def _state_to_pairs(h):
    ht = jnp.swapaxes(h, -1, -2)
    zero = jnp.zeros_like(ht[..., 0, :, :])
    pairs = []
    for pr in range(N_PAIRS):
        heads = (2 * pr, 2 * pr + 1)
        row_blocks = [jnp.concatenate([ht[..., hd, :, :] if hd // HEADS_PER_GROUP == g else zero for hd in heads],
                                      axis=-1) for g in range(C_GROUPS)]
        pairs.append(jnp.concatenate(row_blocks, axis=-2))
    return jnp.stack(pairs, axis=-3)


def kernel(x_prompt, x_sample, cache_attn_k, cache_attn_v, cache_mla_ckv, cache_mla_kpe, state_ssm, c, c_ctx, norm_mix_pre, norm_mix_post, norm_ffn_pre, norm_ffn_post, w_mod, b_mod, w_in, attn_q_norm, attn_k_norm, mla_q_norm, mla_w_qb, mla_kv_norm, mla_w_kvb, ssm_conv_w, ssm_conv_b, ssm_dt_bias, ssm_a_log, ssm_d, ssm_norm, w_out, w_ffn1, w_ffn2):
    p = dict(norm_mix_pre=norm_mix_pre, norm_mix_post=norm_mix_post, norm_ffn_pre=norm_ffn_pre,
             norm_ffn_post=norm_ffn_post, w_in=w_in, attn_q_norm=attn_q_norm, attn_k_norm=attn_k_norm,
             mla_q_norm=mla_q_norm, mla_w_qb=mla_w_qb, mla_kv_norm=mla_kv_norm, mla_w_kvb=mla_w_kvb,
             ssm_conv_w=ssm_conv_w, ssm_conv_b=ssm_conv_b, ssm_dt_bias=ssm_dt_bias, ssm_a_log=ssm_a_log,
             ssm_d=ssm_d, ssm_norm=ssm_norm, w_out=w_out, w_ffn1=w_ffn1, w_ffn2=w_ffn2)
    lw = _prep_weights(p)
    tabs = _rope_tables()

    cvec = jnp.concatenate([c_ctx[None, :], c, jnp.zeros((MOD_ROWS - 1 - DEC_BATCH, D_MODEL), F32)], axis=0)
    mod = _modulation(cvec, w_mod, b_mod).reshape(DEPTH, MOD_ROWS, 1, 6 * D_MODEL)

    kpe_p = jnp.pad(cache_mla_kpe, ((0, 0),) * 3 + ((KPE_LANE, LANES - KPE_LANE - B_ROPE),))
    kc, vct, kbc, vbct = _cache_prep(
        cache_attn_k.reshape(DEC_BATCH, DEPTH, PAST_LEN, A_KV), cache_attn_v.reshape(DEC_BATCH, DEPTH, PAST_LEN, A_KV),
        cache_mla_ckv, kpe_p, lw["w_k"], lw["w_v"])
    h0_lat = _state_to_pairs(state_ssm)

    xp = x_prompt.reshape(BATCH * SEQ, D_MODEL)
    xs = x_sample.reshape(DEC_BATCH * DEC_SEQ, D_MODEL)
    new_k, new_v, new_ckv, new_kpe, new_ssm = [], [], [], [], []
    for l in range(DEPTH):
        qat, ka, vat, qbt, kb, vbt, z, xbc, misc, kf, vf, ckvf, kpef = _pre_ctx(l, xp, mod, lw, tabs)
        lat_pre, (oa, ob, y, hfin) = _pre_lat_mix_ctx(l, xs, mod, lw, tabs, (qat, ka, vat, qbt, kb, vbt, xbc, misc))
        ctx_mix = (oa, ob, y, z)
        new_k.append(kf.reshape(BATCH, SEQ, A_KV_HEADS, HEAD_DIM))
        new_v.append(vf.reshape(BATCH, SEQ, A_KV_HEADS, HEAD_DIM))
        new_ckv.append(ckvf.reshape(BATCH, SEQ, B_KV_RANK))
        new_kpe.append(kpef.reshape(BATCH, SEQ, B_ROPE))
        new_ssm.append(hfin)
        qat, ka, vat, qbt, kb, vbt, z, xbc, misc = lat_pre
        oa = _attn_lat(l, True, qat, ka, vat, kc, vct)
        ob = _attn_lat(l, False, qbt, kb, vbt, kbc, vbct)
        y, _ = _ssd_lat(l, xbc, misc, lw, h0_lat)
        xp, xs = _post_both(l, xp, xs, mod, ctx_mix, (oa, ob, y, z), lw)
    return (xp.reshape(BATCH, SEQ, D_MODEL), xs.reshape(DEC_BATCH, DEC_SEQ, D_MODEL),
            jnp.stack(new_k, axis=1), jnp.stack(new_v, axis=1), jnp.stack(new_ckv, axis=1),
            jnp.stack(new_kpe, axis=1), jnp.stack(new_ssm, axis=1))
```
